```python
import math
import jax, jax.numpy as jnp
from jax import lax
import numpy as np

D_MODEL = 2048
BATCH = 4
SEQ = 2048
DEPTH = 2

HEAD_DIM = 64
GROUP_WIDTH = D_MODEL // 4
CONV_WIDTH = GROUP_WIDTH
CONV_K = 3
SWA_HEADS = GROUP_WIDTH // HEAD_DIM
SWA_KV_HEADS = max(1, SWA_HEADS // 4)
SWA_GROUP = SWA_HEADS // SWA_KV_HEADS
SWA_WIDTH = SWA_HEADS * HEAD_DIM
SWA_KV_WIDTH = SWA_KV_HEADS * HEAD_DIM
SWA_WINDOW = 128
DIL_HEADS = GROUP_WIDTH // HEAD_DIM
DIL_WIDTH = DIL_HEADS * HEAD_DIM
DIL_PAIRS = ((128, 1), (512, 4), (2048, 16))
RWKV_HEADS = GROUP_WIDTH // HEAD_DIM
RWKV_WIDTH = RWKV_HEADS * HEAD_DIM
DECAY_LORA = 64
ICLR_LORA = 64
VRES_LORA = 32
GATE_LORA = 128
RWKV_IN_WIDTH = 3 * RWKV_WIDTH + DECAY_LORA + ICLR_LORA + GATE_LORA
MIX_WIDTH = CONV_WIDTH + SWA_WIDTH + DIL_WIDTH + RWKV_WIDTH
IN_SPLITS = (CONV_WIDTH, CONV_WIDTH, CONV_WIDTH,
             SWA_WIDTH, SWA_KV_WIDTH, SWA_KV_WIDTH,
             DIL_WIDTH, DIL_WIDTH, DIL_WIDTH,
             RWKV_IN_WIDTH)
IN_WIDTH = sum(IN_SPLITS)
D_FF = 4 * D_MODEL
BLK = 128
NUM_BUCKETS = 32
BUCKET_MAX_DIST = 128
N_ATTN_HEADS = SWA_HEADS + DIL_HEADS
RMS_EPS = 1e-6
LN_X_EPS = 64e-5
NEG = -1e30

kernel_name = 'hybrid_parallel_heads_block'


def split_cols(t, sizes):
    out, start = [], 0
    for s in sizes:
        out.append(t[..., start:start + s])
        start += s
    return out


def rms_norm(x, g, eps=RMS_EPS):
    xf = x.astype(jnp.float32)
    y = xf * lax.rsqrt(jnp.mean(xf * xf, axis=-1, keepdims=True) + eps)
    return y.astype(x.dtype) * g


def t5_bucket(dist):
    dist = jnp.maximum(dist, 0)
    max_exact = NUM_BUCKETS // 2
    scaled = jnp.log(jnp.maximum(dist, 1).astype(jnp.float32) / max_exact) / math.log(BUCKET_MAX_DIST / max_exact)
    large = max_exact + (scaled * (NUM_BUCKETS - max_exact)).astype(jnp.int32)
    large = jnp.minimum(large, NUM_BUCKETS - 1)
    return jnp.where(dist < max_exact, dist, large)


def block_rel_bias(table_cols, stride):
    dist = BLK + jnp.arange(BLK)[:, None] - jnp.arange(2 * BLK)[None, :]
    bucket = t5_bucket(dist * stride)
    return jnp.transpose(table_cols[bucket].astype(jnp.float32), (2, 0, 1))


def banded_attention(q, k, v, bias, max_dist, sink=None):
    n, hk, g, seq, dh = q.shape
    nb = -(-seq // BLK)
    pad = nb * BLK - seq
    qb = jnp.pad(q, ((0, 0), (0, 0), (0, 0), (0, pad), (0, 0))).reshape(n, hk, g, nb, BLK, dh)

    def windows(t):
        tb = jnp.pad(t, ((0, 0), (0, 0), (BLK, pad), (0, 0))).reshape(n, hk, nb + 1, BLK, dh)
        return jnp.concatenate([tb[:, :, :-1], tb[:, :, 1:]], axis=3)

    kw, vw = windows(k), windows(v)
    s = jnp.einsum('nhgiqd,nhikd->nhgiqk', qb, kw, preferred_element_type=jnp.float32) * (dh ** -0.5)
    s = s + bias[None, :, :, None]
    dist = BLK + jnp.arange(BLK)[:, None] - jnp.arange(2 * BLK)[None, :]
    kpos = (jnp.arange(nb)[:, None] - 1) * BLK + jnp.arange(2 * BLK)[None, :]
    valid = ((dist >= 0) & (dist <= max_dist))[None] & (kpos >= 0)[:, None, :]
    s = jnp.where(valid, s, NEG)
    m = jnp.max(s, axis=-1, keepdims=True)
    if sink is not None:
        sk = sink.astype(jnp.float32)[None, :, :, None, None, None]
        m = jnp.maximum(m, sk)
    p = jnp.exp(s - m)
    den = jnp.sum(p, axis=-1, keepdims=True)
    if sink is not None:
        den = den + jnp.exp(sk - m)
    o = jnp.einsum('nhgiqk,nhikd->nhgiqd', p, vw.astype(jnp.float32)) / den
    lse = (m + jnp.log(den))[..., 0]
    o = o.reshape(n, hk, g, nb * BLK, dh)[:, :, :, :seq].astype(q.dtype)
    lse = lse.reshape(n, hk, g, nb * BLK)[..., :seq]
    return o, lse


def dilate(t, r):
    b, h, s, d = t.shape
    return t.reshape(b, h, s // r, r, d).transpose(0, 3, 1, 2, 4).reshape(b * r, h, s // r, d)


def undilate(t, r):
    br, h, l, d = t.shape
    return t.reshape(br // r, r, h, l, d).transpose(0, 2, 3, 1, 4).reshape(br // r, h, l * r, d)


def short_conv_mixer(gate_b, gate_c, u, conv_w):
    z = gate_c * u
    z = lax.conv_general_dilated(z, conv_w[:, None, :], (1,), [(CONV_K - 1, 0)],
                                 dimension_numbers=('NWC', 'WIO', 'NWC'),
                                 feature_group_count=CONV_WIDTH)
    return gate_b * z


def swa_mixer(q, k, v, q_gain, k_gain, sink, bias):
    bsz, seq, _ = q.shape
    q = rms_norm(q.reshape(bsz, seq, SWA_HEADS, HEAD_DIM), q_gain)
    k = rms_norm(k.reshape(bsz, seq, SWA_KV_HEADS, HEAD_DIM), k_gain)
    v = v.reshape(bsz, seq, SWA_KV_HEADS, HEAD_DIM)
    q = q.transpose(0, 2, 1, 3).reshape(bsz, SWA_KV_HEADS, SWA_GROUP, seq, HEAD_DIM)
    o, _ = banded_attention(q, k.transpose(0, 2, 1, 3), v.transpose(0, 2, 1, 3), bias,
                            SWA_WINDOW - 1, sink.reshape(SWA_KV_HEADS, SWA_GROUP))
    return o.reshape(bsz, SWA_HEADS, seq, HEAD_DIM).transpose(0, 2, 1, 3).reshape(bsz, seq, SWA_WIDTH)


def dilated_mixer(q, k, v, q_gain, k_gain, biases):
    bsz, seq, _ = q.shape
    q = rms_norm(q.reshape(bsz, seq, DIL_HEADS, HEAD_DIM), q_gain).transpose(0, 2, 1, 3)
    k = rms_norm(k.reshape(bsz, seq, DIL_HEADS, HEAD_DIM), k_gain).transpose(0, 2, 1, 3)
    v = v.reshape(bsz, seq, DIL_HEADS, HEAD_DIM).transpose(0, 2, 1, 3)
    outs, lses = [], []
    for (window, r), bias in zip(DIL_PAIRS, biases):
        o, lse = banded_attention(dilate(q, r)[:, :, None], dilate(k, r), dilate(v, r), bias, window // r)
        outs.append(undilate(o[:, :, 0], r))
        lses.append(undilate(lse[:, :, 0, :, None], r))
    wts = jax.nn.softmax(jnp.stack(lses), axis=0)
    o = jnp.sum(wts * jnp.stack(outs).astype(jnp.float32), axis=0).astype(q.dtype)
    return o.transpose(0, 2, 1, 3).reshape(bsz, seq, DIL_WIDTH)


def wkv7_scan(r, w, k, v, a, b):
    bsz, seq, nh, n = r.shape

    def step(state, inp):
        r_t, w_t, k_t, v_t, a_t, b_t = inp
        sa = jnp.einsum('bhvk,bhk->bhv', state, a_t)
        state = (state * w_t[:, :, None, :] + sa[..., None] * b_t[:, :, None, :]
                 + v_t[..., None] * k_t[:, :, None, :])
        return state, jnp.einsum('bhvk,bhk->bhv', state, r_t)

    xs = tuple(jnp.swapaxes(t, 0, 1) for t in (r, w, k, v, a, b))
    state0 = jnp.zeros((bsz, nh, n, n), jnp.float32)
    _, y = lax.scan(step, state0, xs)
    return jnp.swapaxes(y, 0, 1)


def rwkv7_mixer(p, mu, w0, w2, a0, a2, g2, k_k, k_a, r_k, ln_g, ln_b, v_first, vres):
    bsz, seq, _ = p.shape
    prev = jnp.pad(p, ((0, 0), (1, 0), (0, 0)))[:, :seq]
    p = p + (prev - p) * mu
    r, k, v, wd, ad, gd = split_cols(p, (RWKV_WIDTH, RWKV_WIDTH, RWKV_WIDTH, DECAY_LORA, ICLR_LORA, GATE_LORA))
    logw = -jax.nn.softplus(-(w0 + jnp.tanh(wd) @ w2)) - 0.5
    decay = jnp.exp(-jnp.exp(logw.astype(jnp.float32)))
    a = jax.nn.sigmoid(a0 + ad @ a2)
    g = jax.nn.sigmoid(gd) @ g2
    if vres is None:
        v_first = v
    else:
        v0, v1, v2 = vres
        v = v + (v_first - v) * jax.nn.sigmoid(v0 + (v @ v1) @ v2)

    def heads(t):
        return t.reshape(bsz, seq, RWKV_HEADS, HEAD_DIM).astype(jnp.float32)

    kk = heads(k * k_k)
    kk = kk * lax.rsqrt(jnp.maximum(jnp.sum(kk * kk, axis=-1, keepdims=True), 1e-24))
    k = k * (1 + (a - 1) * k_a)
    rh, kh, vh, ah, wh = heads(r), heads(k), heads(v), heads(a), heads(decay)
    y = wkv7_scan(rh, wh, kh, vh, -kk, kk * ah)
    mean = jnp.mean(y, axis=-1, keepdims=True)
    var = jnp.mean(jnp.square(y - mean), axis=-1, keepdims=True)
    y = (y - mean) * lax.rsqrt(var + LN_X_EPS)
    y = y.reshape(bsz, seq, RWKV_WIDTH) * ln_g + ln_b
    bonus = (jnp.sum(rh * kh * r_k.astype(jnp.float32), axis=-1, keepdims=True) * vh).reshape(bsz, seq, RWKV_WIDTH)
    y = (y + bonus).astype(p.dtype) * g
    return y, v_first


def setup_inputs(seed: int = 0) -> dict:
    key = jax.random.key(seed)
    ks = iter(jax.random.split(key, 40))

    def nrm(shape, scale):
        return jax.random.normal(next(ks), shape, jnp.float32) * scale

    def unif(shape, lo, hi):
        return jax.random.uniform(next(ks), shape, jnp.float32, minval=lo, maxval=hi)

    L, W = DEPTH, RWKV_WIDTH
    return {
        'x': nrm((BATCH, SEQ, D_MODEL), 1.0),
        'norm_mix': 1.0 + nrm((L, D_MODEL), 0.02),
        'w_in': nrm((L, D_MODEL, IN_WIDTH), D_MODEL ** -0.5),
        'conv_w': nrm((L, CONV_K, CONV_WIDTH), CONV_K ** -0.5),
        'swa_q_norm': 1.0 + nrm((L, HEAD_DIM), 0.02),
        'swa_k_norm': 1.0 + nrm((L, HEAD_DIM), 0.02),
        'swa_sink': nrm((L, SWA_HEADS), 0.5),
        'dil_q_norm': 1.0 + nrm((L, HEAD_DIM), 0.02),
        'dil_k_norm': 1.0 + nrm((L, HEAD_DIM), 0.02),
        'rwkv_mu': unif((L, RWKV_IN_WIDTH), 0.0, 1.0),
        'decay_w0': unif((L, W), -3.0, 1.0),
        'decay_w2': nrm((L, DECAY_LORA, W), 0.1),
        'iclr_a0': nrm((L, W), 0.1),
        'iclr_a2': nrm((L, ICLR_LORA, W), 0.1),
        'gate_g2': nrm((L, GATE_LORA, W), GATE_LORA ** -0.5),
        'k_k': 0.85 + nrm((L, W), 0.02),
        'k_a': 1.0 + nrm((L, W), 0.02),
        'r_k': nrm((L, RWKV_HEADS, HEAD_DIM), 0.1),
        'ln_x_g': 1.0 + nrm((L, W), 0.02),
        'ln_x_b': nrm((L, W), 0.02),
        'vres_v0': nrm((L - 1, W), 0.1),
        'vres_v1': nrm((L - 1, W, VRES_LORA), W ** -0.5),
        'vres_v2': nrm((L - 1, VRES_LORA, W), 0.1),
        'w_out': nrm((L, MIX_WIDTH, D_MODEL), MIX_WIDTH ** -0.5),
        'norm_ffn': 1.0 + nrm((L, D_MODEL), 0.02),
        'w_up': nrm((L, D_MODEL, D_FF), D_MODEL ** -0.5),
        'w_down': nrm((L, D_FF, D_MODEL), D_FF ** -0.5),
        'rel_bias': nrm((NUM_BUCKETS, N_ATTN_HEADS), 0.3),
    }


def reference(x, norm_mix, w_in, conv_w, swa_q_norm, swa_k_norm, swa_sink, dil_q_norm, dil_k_norm,
              rwkv_mu, decay_w0, decay_w2, iclr_a0, iclr_a2, gate_g2, k_k, k_a, r_k, ln_x_g, ln_x_b,
              vres_v0, vres_v1, vres_v2, w_out, norm_ffn, w_up, w_down, rel_bias):
    swa_bias = block_rel_bias(rel_bias[:, :SWA_HEADS], 1).reshape(SWA_KV_HEADS, SWA_GROUP, BLK, 2 * BLK)
    dil_biases = [block_rel_bias(rel_bias[:, SWA_HEADS:], r)[:, None] for _, r in DIL_PAIRS]
    v_first = None
    for layer in range(DEPTH):
        h = rms_norm(x, norm_mix[layer])
        proj = h @ w_in[layer]
        c_b, c_c, c_u, s_q, s_k, s_v, d_q, d_k, d_v, rw = split_cols(proj, IN_SPLITS)
        y_conv = short_conv_mixer(c_b, c_c, c_u, conv_w[layer])
        y_swa = swa_mixer(s_q, s_k, s_v, swa_q_norm[layer], swa_k_norm[layer], swa_sink[layer], swa_bias)
        y_dil = dilated_mixer(d_q, d_k, d_v, dil_q_norm[layer], dil_k_norm[layer], dil_biases)
        vres = None if layer == 0 else (vres_v0[layer - 1], vres_v1[layer - 1], vres_v2[layer - 1])
        y_rwkv, v_first = rwkv7_mixer(rw, rwkv_mu[layer], decay_w0[layer], decay_w2[layer], iclr_a0[layer],
                                      iclr_a2[layer], gate_g2[layer], k_k[layer], k_a[layer], r_k[layer],
                                      ln_x_g[layer], ln_x_b[layer], v_first, vres)
        x = x + jnp.concatenate([y_conv, y_swa, y_dil, y_rwkv], axis=-1) @ w_out[layer]
        h = rms_norm(x, norm_ffn[layer])
        x = x + jnp.square(jax.nn.relu(h @ w_up[layer])) @ w_down[layer]
    return x
```

```python
import functools
import math

import jax
import jax.numpy as jnp
from jax import lax
from jax.experimental import pallas as pl
from jax.experimental.pallas import tpu as pltpu

F32 = jnp.float32
BF16 = jnp.bfloat16
HIGHEST = lax.Precision.HIGHEST

D_MODEL = 2048
HEAD_DIM = 64
GROUP_WIDTH = 512
N_HEADS = GROUP_WIDTH // HEAD_DIM
SWA_KV_HEADS = 2
SWA_GROUP = N_HEADS // SWA_KV_HEADS
SWA_WINDOW = 128
DIL_PAIRS = ((128, 1), (512, 4), (2048, 16))
DECAY_LORA = 64
ICLR_LORA = 64
GATE_LORA = 128
RWKV_IN_WIDTH = 3 * GROUP_WIDTH + DECAY_LORA + ICLR_LORA + GATE_LORA
BLK = 128
NUM_BUCKETS = 32
BUCKET_MAX_DIST = 128
RMS_EPS = 1e-6
LN_X_EPS = 64e-5
NEG = -1e30
WKV_CHUNK = 64

OFF_RW = 0
OFF_SWA_K = OFF_RW + RWKV_IN_WIDTH
OFF_SWA_V = OFF_SWA_K + SWA_KV_HEADS * HEAD_DIM
OFF_SWA_Q = OFF_SWA_V + SWA_KV_HEADS * HEAD_DIM
OFF_CONV = OFF_SWA_Q + GROUP_WIDTH
OFF_DIL = OFF_CONV + 3 * GROUP_WIDTH
IN_WIDTH = OFF_DIL + 3 * GROUP_WIDTH

VMEM_LIMIT = 48 * 1024 * 1024


def _params(*sem):
    return pltpu.CompilerParams(dimension_semantics=sem, vmem_limit_bytes=VMEM_LIMIT)


def _dot_hi(a, b):
    return jnp.dot(a, b, precision=HIGHEST, preferred_element_type=F32)


def _dot_nt_hi(a, b):
    return lax.dot_general(a, b, (((1,), (1,)), ((), ())), precision=HIGHEST,
                           preferred_element_type=F32)


def _dot_tn_hi(a, b):
    return lax.dot_general(a, b, (((0,), (0,)), ((), ())), precision=HIGHEST,
                           preferred_element_type=F32)


def _head_blockdiag(width):
    r = lax.broadcasted_iota(jnp.int32, (width, width), 0) >> 6
    c = lax.broadcasted_iota(jnp.int32, (width, width), 1) >> 6
    return (r == c).astype(F32)


def _sigmoid(z):
    return 1.0 / (1.0 + jnp.exp(-z))


def _norm_matmul_kernel(x_ref, g_ref, w_ref, o_ref, h_ref):
    @pl.when(pl.program_id(1) == 0)
    def _():
        x = x_ref[...]
        ms = jnp.mean(x * x, axis=-1, keepdims=True)
        h_ref[...] = (x * lax.rsqrt(ms + RMS_EPS) * g_ref[...]).astype(BF16)

    o_ref[...] = jnp.dot(h_ref[...], w_ref[...], preferred_element_type=F32)


def _norm_matmul(x2d, gain, w_bf16, tm=1024, tn=512):
    m, k = x2d.shape
    n = w_bf16.shape[1]
    return pl.pallas_call(
        _norm_matmul_kernel,
        grid=(m // tm, n // tn),
        in_specs=[
            pl.BlockSpec((tm, k), lambda i, j: (i, 0)),
            pl.BlockSpec((1, k), lambda i, j: (0, 0)),
            pl.BlockSpec((k, tn), lambda i, j: (0, j)),
        ],
        out_specs=pl.BlockSpec((tm, tn), lambda i, j: (i, j)),
        out_shape=jax.ShapeDtypeStruct((m, n), F32),
        scratch_shapes=[pltpu.VMEM((tm, k), BF16)],
        compiler_params=_params("parallel", "arbitrary"),
        name="norm_matmul",
    )(x2d, gain.reshape(1, k), w_bf16)


def _t5_bucket(dist):
    dist = jnp.maximum(dist, 0)
    max_exact = NUM_BUCKETS // 2
    scaled = (jnp.log(jnp.maximum(dist, 1).astype(F32) / max_exact)
              / math.log(BUCKET_MAX_DIST / max_exact))
    large = max_exact + (scaled * (NUM_BUCKETS - max_exact)).astype(jnp.int32)
    large = jnp.minimum(large, NUM_BUCKETS - 1)
    return jnp.where(dist < max_exact, dist, large)


def _bias_kernel(bucket_ref, table_ref, o_ref):
    h = pl.program_id(1)
    bucket = bucket_ref[0]
    acc = jnp.zeros(bucket.shape, F32)
    for b in range(NUM_BUCKETS):
        acc = jnp.where(bucket == b, table_ref[b, h], acc)
    o_ref[0, 0] = acc


def _bias_tiles(rel_bias):
    dist = BLK + jnp.arange(BLK)[:, None] - jnp.arange(2 * BLK)[None, :]
    buckets = jnp.stack([_t5_bucket(dist * r) for _, r in DIL_PAIRS]).astype(jnp.int32)
    nh = rel_bias.shape[1]
    return pl.pallas_call(
        _bias_kernel,
        grid=(len(DIL_PAIRS), nh),
        in_specs=[
            pl.BlockSpec((1, BLK, 2 * BLK), lambda s, h: (s, 0, 0)),
            pl.BlockSpec(memory_space=pltpu.SMEM),
        ],
        out_specs=pl.BlockSpec((1, 1, BLK, 2 * BLK), lambda s, h: (s, h, 0, 0)),
        out_shape=jax.ShapeDtypeStruct((len(DIL_PAIRS), nh, BLK, 2 * BLK), F32),
        compiler_params=_params("arbitrary", "arbitrary"),
        name="bias_tiles",
    )(buckets, rel_bias)


def _conv_kernel(b_ref, c_ref, u_ref, w_ref, o_ref):
    z = c_ref[0] * u_ref[0]
    row = lax.broadcasted_iota(jnp.int32, z.shape, 0)
    z1 = jnp.where(row >= 1, pltpu.roll(z, 1, axis=0), 0.0)
    z2 = jnp.where(row >= 2, pltpu.roll(z, 2, axis=0), 0.0)
    w = w_ref[...]
    y = z2 * w[0:1, :] + z1 * w[1:2, :] + z * w[2:3, :]
    o_ref[0] = (b_ref[0] * y).astype(o_ref.dtype)


def _conv_mixer(proj, conv_w):
    bsz, seq, _ = proj.shape
    lanes = 128
    nblk = GROUP_WIDTH // lanes
    base = OFF_CONV // lanes

    def col(seg):
        return pl.BlockSpec((1, seq, lanes), lambda b, j: (b, 0, base + seg * nblk + j))

    return pl.pallas_call(
        _conv_kernel,
        grid=(bsz, nblk),
        in_specs=[col(0), col(1), col(2), pl.BlockSpec((3, lanes), lambda b, j: (0, j))],
        out_specs=pl.BlockSpec((1, seq, lanes), lambda b, j: (b, 0, j)),
        out_shape=jax.ShapeDtypeStruct((bsz, seq, GROUP_WIDTH), BF16),
        compiler_params=_params("parallel", "parallel"),
        name="conv_mixer",
    )(proj, proj, proj, conv_w)


def _band_mask(max_dist):
    a = lax.broadcasted_iota(jnp.int32, (BLK, 2 * BLK), 0)
    b = lax.broadcasted_iota(jnp.int32, (BLK, 2 * BLK), 1)
    dist = BLK + a - b
    return (dist >= 0) & (dist <= max_dist), b


def _head_rms(x, bd, gain):
    ms = _dot_hi(x * x, bd) * (1.0 / HEAD_DIM)
    return x * lax.rsqrt(ms + RMS_EPS) * gain


def _swa_kernel(q_ref, k_ref, v_ref, qg_ref, kg_ref, sink_ref, bias_ref, o_ref,
                qn_ref, kn_ref, vb_ref):
    seq = q_ref.shape[1]
    nb = seq // BLK
    kvw = SWA_KV_HEADS * HEAD_DIM
    bd_q = _head_blockdiag(GROUP_WIDTH)
    bd_k = _head_blockdiag(kvw)
    scale = HEAD_DIM ** -0.5

    kn_ref[0:BLK, :] = jnp.zeros((BLK, kvw), BF16)
    vb_ref[0:BLK, :] = jnp.zeros((BLK, kvw), BF16)

    def prep(i, carry):
        r0 = pl.multiple_of(i * BLK, BLK)
        q = q_ref[0, pl.ds(r0, BLK), :]
        qn_ref[pl.ds(r0, BLK), :] = (_head_rms(q, bd_q, qg_ref[...]) * scale).astype(BF16)
        k = k_ref[0, pl.ds(r0, BLK), :]
        kn_ref[pl.ds(r0 + BLK, BLK), :] = _head_rms(k, bd_k, kg_ref[...]).astype(BF16)
        vb_ref[pl.ds(r0 + BLK, BLK), :] = v_ref[0, pl.ds(r0, BLK), :].astype(BF16)
        return carry

    lax.fori_loop(0, nb, prep, 0)

    band, kcol = _band_mask(SWA_WINDOW - 1)

    def block(i, carry):
        r0 = pl.multiple_of(i * BLK, BLK)
        valid = band & ((i > 0) | (kcol >= BLK))
        outs = []
        for hk in range(SWA_KV_HEADS):
            kw = kn_ref[pl.ds(r0, 2 * BLK), hk * HEAD_DIM:(hk + 1) * HEAD_DIM]
            vw = vb_ref[pl.ds(r0, 2 * BLK), hk * HEAD_DIM:(hk + 1) * HEAD_DIM]
            for g in range(SWA_GROUP):
                h = hk * SWA_GROUP + g
                qh = qn_ref[pl.ds(r0, BLK), h * HEAD_DIM:(h + 1) * HEAD_DIM]
                s = lax.dot_general(qh, kw, (((1,), (1,)), ((), ())),
                                    preferred_element_type=F32)
                s = jnp.where(valid, s + bias_ref[h], NEG)
                sink = sink_ref[h]
                m = jnp.maximum(jnp.max(s, axis=-1, keepdims=True), sink)
                p = jnp.exp(s - m)
                den = jnp.sum(p, axis=-1, keepdims=True) + jnp.exp(sink - m)
                o = jnp.dot(p.astype(BF16), vw, preferred_element_type=F32) / den
                outs.append(o)
        o_ref[0, pl.ds(r0, BLK), :] = jnp.concatenate(outs, axis=-1).astype(o_ref.dtype)
        return carry

    lax.fori_loop(0, nb, block, 0)


def _swa_mixer(proj, q_gain, k_gain, sink, bias):
    bsz, seq, _ = proj.shape
    kvw = SWA_KV_HEADS * HEAD_DIM
    q_gain_t = jnp.tile(q_gain, N_HEADS).reshape(1, GROUP_WIDTH)
    k_gain_t = jnp.tile(k_gain, SWA_KV_HEADS).reshape(1, kvw)
    return pl.pallas_call(
        _swa_kernel,
        grid=(bsz,),
        in_specs=[
            pl.BlockSpec((1, seq, GROUP_WIDTH), lambda b: (b, 0, OFF_SWA_Q // GROUP_WIDTH)),
            pl.BlockSpec((1, seq, kvw), lambda b: (b, 0, OFF_SWA_K // kvw)),
            pl.BlockSpec((1, seq, kvw), lambda b: (b, 0, OFF_SWA_V // kvw)),
            pl.BlockSpec((1, GROUP_WIDTH), lambda b: (0, 0)),
            pl.BlockSpec((1, kvw), lambda b: (0, 0)),
            pl.BlockSpec(memory_space=pltpu.SMEM),
            pl.BlockSpec((N_HEADS, BLK, 2 * BLK), lambda b: (0, 0, 0)),
        ],
        out_specs=pl.BlockSpec((1, seq, GROUP_WIDTH), lambda b: (b, 0, 0)),
        out_shape=jax.ShapeDtypeStruct((bsz, seq, GROUP_WIDTH), BF16),
        scratch_shapes=[
            pltpu.VMEM((seq, GROUP_WIDTH), BF16),
            pltpu.VMEM((seq + BLK, kvw), BF16),
            pltpu.VMEM((seq + BLK, kvw), BF16),
        ],
        compiler_params=_params("parallel"),
        name="swa_mixer",
    )(proj, proj, proj, q_gain_t, k_gain_t, sink, bias)


def _dil_kernel(q_ref, k_ref, v_ref, qg_ref, kg_ref, bias_ref, o_ref,
                qn_ref, kn_ref, ob_ref, lb_ref):
    seq = q_ref.shape[1]
    lanes = q_ref.shape[2]
    heads = lanes // HEAD_DIM
    bd = _head_blockdiag(lanes)
    scale = HEAD_DIM ** -0.5

    def prep(i, carry):
        r0 = pl.multiple_of(i * BLK, BLK)
        qn_ref[pl.ds(r0, BLK), :] = _head_rms(q_ref[0, pl.ds(r0, BLK), :], bd, qg_ref[...]) * scale
        kn_ref[pl.ds(r0, BLK), :] = _head_rms(k_ref[0, pl.ds(r0, BLK), :], bd, kg_ref[...])
        return carry

    lax.fori_loop(0, seq // BLK, prep, 0)

    for br, (window, r) in enumerate(DIL_PAIRS):
        nb = seq // r // BLK
        band, kcol = _band_mask(window // r)

        def block(t, carry, br=br, r=r, nb=nb, band=band, kcol=kcol):
            c = t // nb
            i = t - c * nb
            cur = c + i * (BLK * r)
            prev = jnp.maximum(cur - BLK * r, c)
            valid = band & ((i > 0) | (kcol >= BLK))

            def rows(ref, start):
                if r == 1:
                    return ref[pl.ds(start, BLK), :]
                return ref[pl.ds(start, BLK, stride=r), :]

            q = rows(qn_ref, cur).astype(BF16)
            kw = jnp.concatenate([rows(kn_ref, prev), rows(kn_ref, cur)], axis=0).astype(BF16)
            vw = jnp.concatenate([rows(v_ref.at[0], prev), rows(v_ref.at[0], cur)],
                                 axis=0).astype(BF16)
            outs, lses = [], []
            for h in range(heads):
                sl = slice(h * HEAD_DIM, (h + 1) * HEAD_DIM)
                s = lax.dot_general(q[:, sl], kw[:, sl], (((1,), (1,)), ((), ())),
                                    preferred_element_type=F32)
                s = jnp.where(valid, s + bias_ref[br, h], NEG)
                m = jnp.max(s, axis=-1, keepdims=True)
                p = jnp.exp(s - m)
                den = jnp.sum(p, axis=-1, keepdims=True)
                o = jnp.dot(p.astype(BF16), vw[:, sl], preferred_element_type=F32) / den
                outs.append(o)
                lses.append(jnp.broadcast_to(m + jnp.log(den), (BLK, HEAD_DIM)))
            o_all = jnp.concatenate(outs, axis=-1)
            l_all = jnp.concatenate(lses, axis=-1)
            if r == 1:
                ob_ref[br, pl.ds(cur, BLK), :] = o_all
                lb_ref[br, pl.ds(cur, BLK), :] = l_all
            else:
                ob_ref[br, pl.ds(cur, BLK, stride=r), :] = o_all
                lb_ref[br, pl.ds(cur, BLK, stride=r), :] = l_all
            return carry

        lax.fori_loop(0, r * nb, block, 0)

    def combine(i, carry):
        r0 = pl.multiple_of(i * BLK, BLK)
        l0 = lb_ref[0, pl.ds(r0, BLK), :]
        l1 = lb_ref[1, pl.ds(r0, BLK), :]
        l2 = lb_ref[2, pl.ds(r0, BLK), :]
        m = jnp.maximum(jnp.maximum(l0, l1), l2)
        e0, e1, e2 = jnp.exp(l0 - m), jnp.exp(l1 - m), jnp.exp(l2 - m)
        tot = e0 + e1 + e2
        o = ((e0 / tot) * ob_ref[0, pl.ds(r0, BLK), :]
             + (e1 / tot) * ob_ref[1, pl.ds(r0, BLK), :]
             + (e2 / tot) * ob_ref[2, pl.ds(r0, BLK), :])
        o_ref[0, pl.ds(r0, BLK), :] = o.astype(o_ref.dtype)
        return carry

    lax.fori_loop(0, seq // BLK, combine, 0)


def _dil_mixer(proj, q_gain, k_gain, bias):
    bsz, seq, _ = proj.shape
    lanes = 128
    heads = lanes // HEAD_DIM
    nblk = GROUP_WIDTH // lanes
    base = OFF_DIL // lanes
    gq = jnp.tile(q_gain, heads).reshape(1, lanes)
    gk = jnp.tile(k_gain, heads).reshape(1, lanes)

    def col(seg):
        return pl.BlockSpec((1, seq, lanes), lambda b, j: (b, 0, base + seg * nblk + j))

    return pl.pallas_call(
        _dil_kernel,
        grid=(bsz, nblk),
        in_specs=[
            col(0), col(1), col(2),
            pl.BlockSpec((1, lanes), lambda b, j: (0, 0)),
            pl.BlockSpec((1, lanes), lambda b, j: (0, 0)),
            pl.BlockSpec((len(DIL_PAIRS), heads, BLK, 2 * BLK), lambda b, j: (0, j, 0, 0)),
        ],
        out_specs=pl.BlockSpec((1, seq, lanes), lambda b, j: (b, 0, j)),
        out_shape=jax.ShapeDtypeStruct((bsz, seq, GROUP_WIDTH), BF16),
        scratch_shapes=[
            pltpu.VMEM((seq, lanes), F32),
            pltpu.VMEM((seq, lanes), F32),
            pltpu.VMEM((len(DIL_PAIRS), seq, lanes), F32),
            pltpu.VMEM((len(DIL_PAIRS), seq, lanes), F32),
        ],
        compiler_params=_params("parallel", "parallel"),
        name="dil_mixer",
    )(proj, proj, proj, gq, gk, bias)


def _rwkv_prep_kernel(*refs, has_vres):
    if has_vres:
        (p_ref, pp_ref, mu_ref, w0_ref, w2_ref, a0_ref, a2_ref, g2_ref, kk_ref, ka_ref,
         vf_ref, v0_ref, v1_ref, v2_ref,
         r_out, lw_out, k_out, v_out, kk_out, b_out, g_out) = refs
    else:
        (p_ref, pp_ref, mu_ref, w0_ref, w2_ref, a0_ref, a2_ref, g2_ref, kk_ref, ka_ref,
         r_out, lw_out, k_out, v_out, kk_out, b_out, g_out) = refs
    w = GROUP_WIDTH
    p = p_ref[0]
    row = lax.broadcasted_iota(jnp.int32, p.shape, 0)
    last_prev = pp_ref[0, 7:8, :]
    last_prev = jnp.where(pl.program_id(1) > 0, last_prev, 0.0)
    prev = jnp.where(row >= 1, pltpu.roll(p, 1, axis=0), last_prev)
    xs = p + (prev - p) * mu_ref[...]
    r = xs[:, 0:w]
    k = xs[:, w:2 * w]
    v = xs[:, 2 * w:3 * w]
    o = 3 * w
    wd = xs[:, o:o + DECAY_LORA]
    ad = xs[:, o + DECAY_LORA:o + DECAY_LORA + ICLR_LORA]
    gd = xs[:, o + DECAY_LORA + ICLR_LORA:]

    z = -(w0_ref[...] + _dot_hi(jnp.tanh(wd), w2_ref[...]))
    softplus = jnp.maximum(z, 0.0) + jnp.log(1.0 + jnp.exp(-jnp.abs(z)))
    logw = -softplus - 0.5
    lw_out[0] = -jnp.exp(logw)
    a = _sigmoid(a0_ref[...] + _dot_hi(ad, a2_ref[...]))
    g_out[0] = _dot_hi(_sigmoid(gd), g2_ref[...])
    if has_vres:
        mix = _sigmoid(v0_ref[...] + _dot_hi(_dot_hi(v, v1_ref[...]), v2_ref[...]))
        v = v + (vf_ref[0] - v) * mix
    kk = k * kk_ref[...]
    ss = _dot_hi(kk * kk, _head_blockdiag(w))
    kk = kk * lax.rsqrt(jnp.maximum(ss, 1e-24))
    r_out[0] = r
    k_out[0] = k * (1.0 + (a - 1.0) * ka_ref[...])
    v_out[0] = v
    kk_out[0] = kk
    b_out[0] = kk * a


def _rwkv_prep(proj, mu, w0, w2, a0, a2, g2, k_k, k_a, v_first, vres, ts=256):
    bsz, seq, _ = proj.shape
    w = GROUP_WIDTH
    nt = seq // ts
    has_vres = vres is not None

    def full(shape):
        return pl.BlockSpec(shape, lambda b, i: (0,) * len(shape))

    row = lambda a: a.reshape(1, -1)
    tile = pl.BlockSpec((1, ts, w), lambda b, i: (b, i, 0))
    in_specs = [
        pl.BlockSpec((1, ts, RWKV_IN_WIDTH), lambda b, i: (b, i, OFF_RW // RWKV_IN_WIDTH)),
        pl.BlockSpec((1, 8, RWKV_IN_WIDTH),
                     lambda b, i: (b, jnp.maximum(i * (ts // 8) - 1, 0), OFF_RW // RWKV_IN_WIDTH)),
        full((1, RWKV_IN_WIDTH)), full((1, w)), full((DECAY_LORA, w)), full((1, w)),
        full((ICLR_LORA, w)), full((GATE_LORA, w)), full((1, w)), full((1, w)),
    ]
    args = [proj, proj, row(mu), row(w0), w2, row(a0), a2, g2, row(k_k), row(k_a)]
    if has_vres:
        v0, v1, v2 = vres
        in_specs += [tile, full((1, w)), full(v1.shape), full(v2.shape)]
        args += [v_first, row(v0), v1, v2]
    out = jax.ShapeDtypeStruct((bsz, seq, w), F32)
    return pl.pallas_call(
        functools.partial(_rwkv_prep_kernel, has_vres=has_vres),
        grid=(bsz, nt),
        in_specs=in_specs,
        out_specs=[tile] * 7,
        out_shape=[out] * 7,
        compiler_params=_params("parallel", "parallel"),
        name="rwkv_prep",
    )(*args)


def _wkv_kernel(r_ref, lw_ref, k_ref, v_ref, kk_ref, b_ref, g_ref, rk_ref, lng_ref, lnb_ref,
                o_ref, st_ref):
    c = WKV_CHUNK
    n = HEAD_DIM

    @pl.when(pl.program_id(1) == 0)
    def _():
        st_ref[...] = jnp.zeros(st_ref.shape, F32)

    r = r_ref[0]
    lw = lw_ref[0]
    k = k_ref[0]
    v = v_ref[0]
    row = lax.broadcasted_iota(jnp.int32, (c, c), 0)
    col = lax.broadcasted_iota(jnp.int32, (c, c), 1)
    lower = row >= col
    strict = row > col
    eye = (row == col).astype(F32)

    cum = _dot_hi(lower.astype(F32), lw)
    e_pos = jnp.exp(cum)
    e_neg = jnp.exp(-cum)
    a_t = -kk_ref[0] * jnp.exp(cum - lw)
    r_t = r * e_pos
    b_t = b_ref[0] * e_neg
    k_t = k * e_neg
    chunk_decay = e_pos[c - 1:c, :]

    ys = []
    for h in range(N_HEADS):
        sl = slice(h * n, (h + 1) * n)
        ar = jnp.concatenate([a_t[:, sl], r_t[:, sl]], axis=0)
        bk = jnp.concatenate([b_t[:, sl], k_t[:, sl]], axis=0)
        vh = v[:, sl]
        aa = _dot_nt_hi(ar, bk)
        n_ab = jnp.where(strict, aa[:c, :c], 0.0)
        a_ak = jnp.where(strict, aa[:c, c:], 0.0)
        a_rb = jnp.where(lower, aa[c:, :c], 0.0)
        a_rk = jnp.where(lower, aa[c:, c:], 0.0)
        t = eye + n_ab
        pw = n_ab
        for _ in range(int(math.log2(c)) - 1):
            pw = _dot_hi(pw, pw)
            t = t + _dot_hi(pw, t)
        s0 = st_ref[h]
        ars = _dot_nt_hi(ar, s0)
        u = _dot_hi(t, ars[:c] + _dot_hi(a_ak, vh))
        ys.append(ars[c:] + _dot_hi(a_rb, u) + _dot_hi(a_rk, vh))
        uv = jnp.concatenate([u, vh], axis=0)
        st_ref[h] = (s0 + _dot_tn_hi(uv, bk)) * chunk_decay[:, sl]

    y = jnp.concatenate(ys, axis=-1)
    bd = _head_blockdiag(GROUP_WIDTH)
    mean = _dot_hi(y, bd) * (1.0 / n)
    yc = y - mean
    var = _dot_hi(yc * yc, bd) * (1.0 / n)
    yn = yc * lax.rsqrt(var + LN_X_EPS) * lng_ref[...] + lnb_ref[...]
    bonus = _dot_hi(r * k * rk_ref[...], bd) * v
    o_ref[0] = ((yn + bonus) * g_ref[0]).astype(o_ref.dtype)


def _wkv(r, lw, k, v, kk, b, g, r_k, ln_g, ln_b):
    bsz, seq, w = r.shape
    c = WKV_CHUNK
    tile = pl.BlockSpec((1, c, w), lambda bi, ci: (bi, ci, 0))
    vec = pl.BlockSpec((1, w), lambda bi, ci: (0, 0))
    return pl.pallas_call(
        _wkv_kernel,
        grid=(bsz, seq // c),
        in_specs=[tile] * 7 + [vec] * 3,
        out_specs=tile,
        out_shape=jax.ShapeDtypeStruct((bsz, seq, w), BF16),
        scratch_shapes=[pltpu.VMEM((N_HEADS, HEAD_DIM, HEAD_DIM), F32)],
        compiler_params=_params("parallel", "arbitrary"),
        name="wkv_scan",
    )(r, lw, k, v, kk, b, g, r_k.reshape(1, w), ln_g.reshape(1, w), ln_b.reshape(1, w))


def _wout_kernel(x_ref, y0_ref, y1_ref, y2_ref, y3_ref, w_ref, o_ref):
    acc = x_ref[...]
    for idx, y_ref in enumerate((y0_ref, y1_ref, y2_ref, y3_ref)):
        acc = acc + jnp.dot(y_ref[...], w_ref[idx * GROUP_WIDTH:(idx + 1) * GROUP_WIDTH, :],
                            preferred_element_type=F32)
    o_ref[...] = acc


def _wout(x2d, ys, w_bf16, tm=512):
    m, d = x2d.shape
    ytile = pl.BlockSpec((tm, GROUP_WIDTH), lambda i: (i, 0))
    xtile = pl.BlockSpec((tm, d), lambda i: (i, 0))
    return pl.pallas_call(
        _wout_kernel,
        grid=(m // tm,),
        in_specs=[xtile] + [ytile] * 4 + [pl.BlockSpec(w_bf16.shape, lambda i: (0, 0))],
        out_specs=xtile,
        out_shape=jax.ShapeDtypeStruct((m, d), F32),
        compiler_params=_params("parallel"),
        name="wout_residual",
    )(x2d, *ys, w_bf16)


def _ffn_kernel(x_ref, g_ref, wu_ref, wd_ref, o_ref, h_ref):
    @pl.when(pl.program_id(1) == 0)
    def _():
        x = x_ref[...]
        ms = jnp.mean(x * x, axis=-1, keepdims=True)
        h_ref[...] = (x * lax.rsqrt(ms + RMS_EPS) * g_ref[...]).astype(BF16)
        o_ref[...] = x

    u = jnp.dot(h_ref[...], wu_ref[...], preferred_element_type=F32)
    act = jnp.square(jnp.maximum(u, 0.0)).astype(BF16)
    o_ref[...] += jnp.dot(act, wd_ref[...], preferred_element_type=F32)


def _ffn(x2d, gain, wu_bf16, wd_bf16, tm=512, tf=512):
    m, d = x2d.shape
    f = wu_bf16.shape[1]
    xtile = pl.BlockSpec((tm, d), lambda i, j: (i, 0))
    return pl.pallas_call(
        _ffn_kernel,
        grid=(m // tm, f // tf),
        in_specs=[
            xtile,
            pl.BlockSpec((1, d), lambda i, j: (0, 0)),
            pl.BlockSpec((d, tf), lambda i, j: (0, j)),
            pl.BlockSpec((tf, d), lambda i, j: (j, 0)),
        ],
        out_specs=xtile,
        out_shape=jax.ShapeDtypeStruct((m, d), F32),
        scratch_shapes=[pltpu.VMEM((tm, d), BF16)],
        compiler_params=_params("parallel", "arbitrary"),
        name="ffn",
    )(x2d, gain.reshape(1, d), wu_bf16, wd_bf16)


def _permute_w_in(w):
    gw = GROUP_WIDTH
    conv = w[:, 0:3 * gw]
    swa_q = w[:, 3 * gw:4 * gw]
    swa_kv = w[:, 4 * gw:4 * gw + 2 * SWA_KV_HEADS * HEAD_DIM]
    dil0 = 4 * gw + 2 * SWA_KV_HEADS * HEAD_DIM
    dil = w[:, dil0:dil0 + 3 * gw]
    rw = w[:, dil0 + 3 * gw:]
    return jnp.concatenate([rw, swa_kv, swa_q, conv, dil], axis=1)


def kernel(x, norm_mix, w_in, conv_w, swa_q_norm, swa_k_norm, swa_sink, dil_q_norm, dil_k_norm,
           rwkv_mu, decay_w0, decay_w2, iclr_a0, iclr_a2, gate_g2, k_k, k_a, r_k, ln_x_g, ln_x_b,
           vres_v0, vres_v1, vres_v2, w_out, norm_ffn, w_up, w_down, rel_bias):
    bsz, seq, d = x.shape
    depth = w_in.shape[0]
    bias = _bias_tiles(rel_bias)
    swa_bias = bias[0, :N_HEADS]
    dil_bias = bias[:, N_HEADS:]
    x2d = x.reshape(bsz * seq, d)
    v_first = None
    for layer in range(depth):
        w_in_p = _permute_w_in(w_in[layer]).astype(BF16)
        proj = _norm_matmul(x2d, norm_mix[layer], w_in_p).reshape(bsz, seq, IN_WIDTH)
        y_conv = _conv_mixer(proj, conv_w[layer])
        y_swa = _swa_mixer(proj, swa_q_norm[layer], swa_k_norm[layer], swa_sink[layer], swa_bias)
        y_dil = _dil_mixer(proj, dil_q_norm[layer], dil_k_norm[layer], dil_bias)
        vres = None if layer == 0 else (vres_v0[layer - 1], vres_v1[layer - 1], vres_v2[layer - 1])
        r, lw, k, v, kk, b, g = _rwkv_prep(
            proj, rwkv_mu[layer], decay_w0[layer], decay_w2[layer], iclr_a0[layer], iclr_a2[layer],
            gate_g2[layer], k_k[layer], k_a[layer], v_first, vres)
        if layer == 0:
            v_first = v
        y_rwkv = _wkv(r, lw, k, v, kk, b, g, r_k[layer].reshape(-1), ln_x_g[layer], ln_x_b[layer])
        ys = [y.reshape(bsz * seq, GROUP_WIDTH) for y in (y_conv, y_swa, y_dil, y_rwkv)]
        x2d = _wout(x2d, ys, w_out[layer].astype(BF16))
        x2d = _ffn(x2d, norm_ffn[layer], w_up[layer].astype(BF16), w_down[layer].astype(BF16))
    return x2d.reshape(bsz, seq, d)
```

```python
import functools
import math

import jax
import jax.numpy as jnp
from jax import lax
from jax.experimental import pallas as pl
from jax.experimental.pallas import tpu as pltpu

F32 = jnp.float32
BF16 = jnp.bfloat16
HIGHEST = lax.Precision.HIGHEST

D_MODEL = 2048
HEAD_DIM = 64
GROUP_WIDTH = 512
N_HEADS = GROUP_WIDTH // HEAD_DIM
SWA_KV_HEADS = 2
SWA_GROUP = N_HEADS // SWA_KV_HEADS
SWA_WINDOW = 128
DIL_PAIRS = ((128, 1), (512, 4), (2048, 16))
DECAY_LORA = 64
ICLR_LORA = 64
GATE_LORA = 128
RWKV_IN_WIDTH = 3 * GROUP_WIDTH + DECAY_LORA + ICLR_LORA + GATE_LORA
BLK = 128
NUM_BUCKETS = 32
BUCKET_MAX_DIST = 128
RMS_EPS = 1e-6
LN_X_EPS = 64e-5
NEG = -1e30
WKV_CHUNK = 64

OFF_RW = 0
OFF_SWA_K = OFF_RW + RWKV_IN_WIDTH
OFF_SWA_V = OFF_SWA_K + SWA_KV_HEADS * HEAD_DIM
OFF_SWA_Q = OFF_SWA_V + SWA_KV_HEADS * HEAD_DIM
OFF_CONV = OFF_SWA_Q + GROUP_WIDTH
OFF_DIL = OFF_CONV + 3 * GROUP_WIDTH
IN_WIDTH = OFF_DIL + 3 * GROUP_WIDTH

VMEM_LIMIT = 48 * 1024 * 1024


def _params(*sem):
    return pltpu.CompilerParams(dimension_semantics=sem, vmem_limit_bytes=VMEM_LIMIT)


def _dot_hi(a, b):
    return jnp.dot(a, b, precision=HIGHEST, preferred_element_type=F32)


def _dot_nt_hi(a, b):
    return lax.dot_general(a, b, (((1,), (1,)), ((), ())), precision=HIGHEST,
                           preferred_element_type=F32)


def _dot_tn_hi(a, b):
    return lax.dot_general(a, b, (((0,), (0,)), ((), ())), precision=HIGHEST,
                           preferred_element_type=F32)


def _head_blockdiag(width):
    r = lax.broadcasted_iota(jnp.int32, (width, width), 0) >> 6
    c = lax.broadcasted_iota(jnp.int32, (width, width), 1) >> 6
    return (r == c).astype(F32)


def _sigmoid(z):
    return 1.0 / (1.0 + jnp.exp(-z))


def _norm_matmul_kernel(x_ref, g_ref, w_ref, o_ref, h_ref):
    @pl.when(pl.program_id(1) == 0)
    def _():
        x = x_ref[...]
        ms = jnp.mean(x * x, axis=-1, keepdims=True)
        h_ref[...] = (x * lax.rsqrt(ms + RMS_EPS) * g_ref[...]).astype(BF16)

    o_ref[...] = jnp.dot(h_ref[...], w_ref[...], preferred_element_type=F32)


def _norm_matmul(x2d, gain, w_bf16, tm=1024, tn=512):
    m, k = x2d.shape
    n = w_bf16.shape[1]
    return pl.pallas_call(
        _norm_matmul_kernel,
        grid=(m // tm, n // tn),
        in_specs=[
            pl.BlockSpec((tm, k), lambda i, j: (i, 0)),
            pl.BlockSpec((1, k), lambda i, j: (0, 0)),
            pl.BlockSpec((k, tn), lambda i, j: (0, j)),
        ],
        out_specs=pl.BlockSpec((tm, tn), lambda i, j: (i, j)),
        out_shape=jax.ShapeDtypeStruct((m, n), F32),
        scratch_shapes=[pltpu.VMEM((tm, k), BF16)],
        compiler_params=_params("parallel", "arbitrary"),
        name="norm_matmul",
    )(x2d, gain.reshape(1, k), w_bf16)


def _t5_bucket(dist):
    dist = jnp.maximum(dist, 0)
    max_exact = NUM_BUCKETS // 2
    scaled = (jnp.log(jnp.maximum(dist, 1).astype(F32) / max_exact)
              / math.log(BUCKET_MAX_DIST / max_exact))
    large = max_exact + (scaled * (NUM_BUCKETS - max_exact)).astype(jnp.int32)
    large = jnp.minimum(large, NUM_BUCKETS - 1)
    return jnp.where(dist < max_exact, dist, large)


def _bias_kernel(bucket_ref, table_ref, o_ref):
    h = pl.program_id(1)
    bucket = bucket_ref[0]
    acc = jnp.zeros(bucket.shape, F32)
    for b in range(NUM_BUCKETS):
        acc = jnp.where(bucket == b, table_ref[b, h], acc)
    o_ref[0, 0] = acc


def _bias_tiles(rel_bias):
    dist = BLK + jnp.arange(BLK)[:, None] - jnp.arange(2 * BLK)[None, :]
    buckets = jnp.stack([_t5_bucket(dist * r) for _, r in DIL_PAIRS]).astype(jnp.int32)
    nh = rel_bias.shape[1]
    return pl.pallas_call(
        _bias_kernel,
        grid=(len(DIL_PAIRS), nh),
        in_specs=[
            pl.BlockSpec((1, BLK, 2 * BLK), lambda s, h: (s, 0, 0)),
            pl.BlockSpec(memory_space=pltpu.SMEM),
        ],
        out_specs=pl.BlockSpec((1, 1, BLK, 2 * BLK), lambda s, h: (s, h, 0, 0)),
        out_shape=jax.ShapeDtypeStruct((len(DIL_PAIRS), nh, BLK, 2 * BLK), F32),
        compiler_params=_params("arbitrary", "arbitrary"),
        name="bias_tiles",
    )(buckets, rel_bias)


def _conv_kernel(b_ref, c_ref, u_ref, w_ref, o_ref):
    z = c_ref[0] * u_ref[0]
    row = lax.broadcasted_iota(jnp.int32, z.shape, 0)
    z1 = jnp.where(row >= 1, pltpu.roll(z, 1, axis=0), 0.0)
    z2 = jnp.where(row >= 2, pltpu.roll(z, 2, axis=0), 0.0)
    w = w_ref[...]
    y = z2 * w[0:1, :] + z1 * w[1:2, :] + z * w[2:3, :]
    o_ref[0] = (b_ref[0] * y).astype(o_ref.dtype)


def _conv_mixer(proj, conv_w):
    bsz, seq, _ = proj.shape
    lanes = 128
    nblk = GROUP_WIDTH // lanes
    base = OFF_CONV // lanes

    def col(seg):
        return pl.BlockSpec((1, seq, lanes), lambda b, j: (b, 0, base + seg * nblk + j))

    return pl.pallas_call(
        _conv_kernel,
        grid=(bsz, nblk),
        in_specs=[col(0), col(1), col(2), pl.BlockSpec((3, lanes), lambda b, j: (0, j))],
        out_specs=pl.BlockSpec((1, seq, lanes), lambda b, j: (b, 0, j)),
        out_shape=jax.ShapeDtypeStruct((bsz, seq, GROUP_WIDTH), BF16),
        compiler_params=_params("parallel", "parallel"),
        name="conv_mixer",
    )(proj, proj, proj, conv_w)


def _band_mask(max_dist):
    a = lax.broadcasted_iota(jnp.int32, (BLK, 2 * BLK), 0)
    b = lax.broadcasted_iota(jnp.int32, (BLK, 2 * BLK), 1)
    dist = BLK + a - b
    return (dist >= 0) & (dist <= max_dist), b


def _head_rms(x, bd, gain):
    ms = _dot_hi(x * x, bd) * (1.0 / HEAD_DIM)
    return x * lax.rsqrt(ms + RMS_EPS) * gain


def _swa_kernel(q_ref, k_ref, v_ref, qg_ref, kg_ref, sink_ref, bias_ref, o_ref,
                qn_ref, kn_ref, vb_ref):
    seq = q_ref.shape[1]
    nb = seq // BLK
    kvw = SWA_KV_HEADS * HEAD_DIM
    bd_q = _head_blockdiag(GROUP_WIDTH)
    bd_k = _head_blockdiag(kvw)
    scale = HEAD_DIM ** -0.5

    kn_ref[0:BLK, :] = jnp.zeros((BLK, kvw), BF16)
    vb_ref[0:BLK, :] = jnp.zeros((BLK, kvw), BF16)

    def prep(i, carry):
        r0 = pl.multiple_of(i * BLK, BLK)
        q = q_ref[0, pl.ds(r0, BLK), :]
        qn_ref[pl.ds(r0, BLK), :] = (_head_rms(q, bd_q, qg_ref[...]) * scale).astype(BF16)
        k = k_ref[0, pl.ds(r0, BLK), :]
        kn_ref[pl.ds(r0 + BLK, BLK), :] = _head_rms(k, bd_k, kg_ref[...]).astype(BF16)
        vb_ref[pl.ds(r0 + BLK, BLK), :] = v_ref[0, pl.ds(r0, BLK), :].astype(BF16)
        return carry

    lax.fori_loop(0, nb, prep, 0)

    band, kcol = _band_mask(SWA_WINDOW - 1)

    def block(i, carry):
        r0 = pl.multiple_of(i * BLK, BLK)
        valid = band & ((i > 0) | (kcol >= BLK))
        outs = []
        for hk in range(SWA_KV_HEADS):
            kw = kn_ref[pl.ds(r0, 2 * BLK), hk * HEAD_DIM:(hk + 1) * HEAD_DIM]
            vw = vb_ref[pl.ds(r0, 2 * BLK), hk * HEAD_DIM:(hk + 1) * HEAD_DIM]
            for g in range(SWA_GROUP):
                h = hk * SWA_GROUP + g
                qh = qn_ref[pl.ds(r0, BLK), h * HEAD_DIM:(h + 1) * HEAD_DIM]
                s = lax.dot_general(qh, kw, (((1,), (1,)), ((), ())),
                                    preferred_element_type=F32)
                s = jnp.where(valid, s + bias_ref[h], NEG)
                sink = sink_ref[h]
                m = jnp.maximum(jnp.max(s, axis=-1, keepdims=True), sink)
                p = jnp.exp(s - m)
                den = jnp.sum(p, axis=-1, keepdims=True) + jnp.exp(sink - m)
                o = jnp.dot(p.astype(BF16), vw, preferred_element_type=F32) / den
                outs.append(o)
        o_ref[0, pl.ds(r0, BLK), :] = jnp.concatenate(outs, axis=-1).astype(o_ref.dtype)
        return carry

    lax.fori_loop(0, nb, block, 0)


def _swa_mixer(proj, q_gain, k_gain, sink, bias):
    bsz, seq, _ = proj.shape
    kvw = SWA_KV_HEADS * HEAD_DIM
    q_gain_t = jnp.tile(q_gain, N_HEADS).reshape(1, GROUP_WIDTH)
    k_gain_t = jnp.tile(k_gain, SWA_KV_HEADS).reshape(1, kvw)
    return pl.pallas_call(
        _swa_kernel,
        grid=(bsz,),
        in_specs=[
            pl.BlockSpec((1, seq, GROUP_WIDTH), lambda b: (b, 0, OFF_SWA_Q // GROUP_WIDTH)),
            pl.BlockSpec((1, seq, kvw), lambda b: (b, 0, OFF_SWA_K // kvw)),
            pl.BlockSpec((1, seq, kvw), lambda b: (b, 0, OFF_SWA_V // kvw)),
            pl.BlockSpec((1, GROUP_WIDTH), lambda b: (0, 0)),
            pl.BlockSpec((1, kvw), lambda b: (0, 0)),
            pl.BlockSpec(memory_space=pltpu.SMEM),
            pl.BlockSpec((N_HEADS, BLK, 2 * BLK), lambda b: (0, 0, 0)),
        ],
        out_specs=pl.BlockSpec((1, seq, GROUP_WIDTH), lambda b: (b, 0, 0)),
        out_shape=jax.ShapeDtypeStruct((bsz, seq, GROUP_WIDTH), BF16),
        scratch_shapes=[
            pltpu.VMEM((seq, GROUP_WIDTH), BF16),
            pltpu.VMEM((seq + BLK, kvw), BF16),
            pltpu.VMEM((seq + BLK, kvw), BF16),
        ],
        compiler_params=_params("parallel"),
        name="swa_mixer",
    )(proj, proj, proj, q_gain_t, k_gain_t, sink, bias)


def _dil_kernel(q_ref, k_ref, v_ref, qg_ref, kg_ref, bias_ref, o_ref,
                qn_ref, kn_ref, ob_ref, lb_ref):
    seq = q_ref.shape[1]
    lanes = q_ref.shape[2]
    heads = lanes // HEAD_DIM
    bd = _head_blockdiag(lanes)
    scale = HEAD_DIM ** -0.5

    def prep(i, carry):
        r0 = pl.multiple_of(i * BLK, BLK)
        qn_ref[pl.ds(r0, BLK), :] = _head_rms(q_ref[0, pl.ds(r0, BLK), :], bd, qg_ref[...]) * scale
        kn_ref[pl.ds(r0, BLK), :] = _head_rms(k_ref[0, pl.ds(r0, BLK), :], bd, kg_ref[...])
        return carry

    lax.fori_loop(0, seq // BLK, prep, 0)

    for br, (window, r) in enumerate(DIL_PAIRS):
        nb = seq // r // BLK
        band, kcol = _band_mask(window // r)

        def block(t, carry, br=br, r=r, nb=nb, band=band, kcol=kcol):
            c = t // nb
            i = t - c * nb
            cur = c + i * (BLK * r)
            prev = jnp.maximum(cur - BLK * r, c)
            valid = band & ((i > 0) | (kcol >= BLK))

            def rows(ref, start):
                if r == 1:
                    return ref[pl.ds(start, BLK), :]
                return ref[pl.ds(start, BLK, stride=r), :]

            q = rows(qn_ref, cur).astype(BF16)
            kw = jnp.concatenate([rows(kn_ref, prev), rows(kn_ref, cur)], axis=0).astype(BF16)
            vw = jnp.concatenate([rows(v_ref.at[0], prev), rows(v_ref.at[0], cur)],
                                 axis=0).astype(BF16)
            outs, lses = [], []
            for h in range(heads):
                sl = slice(h * HEAD_DIM, (h + 1) * HEAD_DIM)
                s = lax.dot_general(q[:, sl], kw[:, sl], (((1,), (1,)), ((), ())),
                                    preferred_element_type=F32)
                s = jnp.where(valid, s + bias_ref[br, h], NEG)
                m = jnp.max(s, axis=-1, keepdims=True)
                p = jnp.exp(s - m)
                den = jnp.sum(p, axis=-1, keepdims=True)
                o = jnp.dot(p.astype(BF16), vw[:, sl], preferred_element_type=F32) / den
                outs.append(o)
                lses.append(jnp.broadcast_to(m + jnp.log(den), (BLK, HEAD_DIM)))
            o_all = jnp.concatenate(outs, axis=-1)
            l_all = jnp.concatenate(lses, axis=-1)
            if r == 1:
                ob_ref[br, pl.ds(cur, BLK), :] = o_all
                lb_ref[br, pl.ds(cur, BLK), :] = l_all
            else:
                ob_ref[br, pl.ds(cur, BLK, stride=r), :] = o_all
                lb_ref[br, pl.ds(cur, BLK, stride=r), :] = l_all
            return carry

        lax.fori_loop(0, r * nb, block, 0)

    def combine(i, carry):
        r0 = pl.multiple_of(i * BLK, BLK)
        l0 = lb_ref[0, pl.ds(r0, BLK), :]
        l1 = lb_ref[1, pl.ds(r0, BLK), :]
        l2 = lb_ref[2, pl.ds(r0, BLK), :]
        m = jnp.maximum(jnp.maximum(l0, l1), l2)
        e0, e1, e2 = jnp.exp(l0 - m), jnp.exp(l1 - m), jnp.exp(l2 - m)
        tot = e0 + e1 + e2
        o = ((e0 / tot) * ob_ref[0, pl.ds(r0, BLK), :]
             + (e1 / tot) * ob_ref[1, pl.ds(r0, BLK), :]
             + (e2 / tot) * ob_ref[2, pl.ds(r0, BLK), :])
        o_ref[0, pl.ds(r0, BLK), :] = o.astype(o_ref.dtype)
        return carry

    lax.fori_loop(0, seq // BLK, combine, 0)


def _dil_mixer(proj, q_gain, k_gain, bias):
    bsz, seq, _ = proj.shape
    lanes = 128
    heads = lanes // HEAD_DIM
    nblk = GROUP_WIDTH // lanes
    base = OFF_DIL // lanes
    gq = jnp.tile(q_gain, heads).reshape(1, lanes)
    gk = jnp.tile(k_gain, heads).reshape(1, lanes)

    def col(seg):
        return pl.BlockSpec((1, seq, lanes), lambda b, j: (b, 0, base + seg * nblk + j))

    return pl.pallas_call(
        _dil_kernel,
        grid=(bsz, nblk),
        in_specs=[
            col(0), col(1), col(2),
            pl.BlockSpec((1, lanes), lambda b, j: (0, 0)),
            pl.BlockSpec((1, lanes), lambda b, j: (0, 0)),
            pl.BlockSpec((len(DIL_PAIRS), heads, BLK, 2 * BLK), lambda b, j: (0, j, 0, 0)),
        ],
        out_specs=pl.BlockSpec((1, seq, lanes), lambda b, j: (b, 0, j)),
        out_shape=jax.ShapeDtypeStruct((bsz, seq, GROUP_WIDTH), BF16),
        scratch_shapes=[
            pltpu.VMEM((seq, lanes), F32),
            pltpu.VMEM((seq, lanes), F32),
            pltpu.VMEM((len(DIL_PAIRS), seq, lanes), F32),
            pltpu.VMEM((len(DIL_PAIRS), seq, lanes), F32),
        ],
        compiler_params=_params("parallel", "parallel"),
        name="dil_mixer",
    )(proj, proj, proj, gq, gk, bias)


def _rwkv_prep_kernel(*refs, has_vres):
    if has_vres:
        (p_ref, pp_ref, mu_ref, w0_ref, w2_ref, a0_ref, a2_ref, g2_ref, kk_ref, ka_ref,
         vf_ref, v0_ref, v1_ref, v2_ref,
         r_out, lw_out, k_out, v_out, kk_out, b_out, g_out) = refs
    else:
        (p_ref, pp_ref, mu_ref, w0_ref, w2_ref, a0_ref, a2_ref, g2_ref, kk_ref, ka_ref,
         r_out, lw_out, k_out, v_out, kk_out, b_out, g_out) = refs
    w = GROUP_WIDTH
    p = p_ref[0]
    row = lax.broadcasted_iota(jnp.int32, p.shape, 0)
    last_prev = pp_ref[0, 7:8, :]
    last_prev = jnp.where(pl.program_id(1) > 0, last_prev, 0.0)
    prev = jnp.where(row >= 1, pltpu.roll(p, 1, axis=0), last_prev)
    xs = p + (prev - p) * mu_ref[...]
    r = xs[:, 0:w]
    k = xs[:, w:2 * w]
    v = xs[:, 2 * w:3 * w]
    o = 3 * w
    wd = xs[:, o:o + DECAY_LORA]
    ad = xs[:, o + DECAY_LORA:o + DECAY_LORA + ICLR_LORA]
    gd = xs[:, o + DECAY_LORA + ICLR_LORA:]

    z = -(w0_ref[...] + _dot_hi(jnp.tanh(wd), w2_ref[...]))
    softplus = jnp.maximum(z, 0.0) + jnp.log(1.0 + jnp.exp(-jnp.abs(z)))
    logw = -softplus - 0.5
    lw_out[0] = -jnp.exp(logw)
    a = _sigmoid(a0_ref[...] + _dot_hi(ad, a2_ref[...]))
    g_out[0] = _dot_hi(_sigmoid(gd), g2_ref[...])
    if has_vres:
        mix = _sigmoid(v0_ref[...] + _dot_hi(_dot_hi(v, v1_ref[...]), v2_ref[...]))
        v = v + (vf_ref[0] - v) * mix
    kk = k * kk_ref[...]
    ss = _dot_hi(kk * kk, _head_blockdiag(w))
    kk = kk * lax.rsqrt(jnp.maximum(ss, 1e-24))
    r_out[0] = r
    k_out[0] = k * (1.0 + (a - 1.0) * ka_ref[...])
    v_out[0] = v
    kk_out[0] = kk
    b_out[0] = kk * a


def _rwkv_prep(proj, mu, w0, w2, a0, a2, g2, k_k, k_a, v_first, vres, ts=256):
    bsz, seq, _ = proj.shape
    w = GROUP_WIDTH
    nt = seq // ts
    has_vres = vres is not None

    def full(shape):
        return pl.BlockSpec(shape, lambda b, i: (0,) * len(shape))

    row = lambda a: a.reshape(1, -1)
    tile = pl.BlockSpec((1, ts, w), lambda b, i: (b, i, 0))
    in_specs = [
        pl.BlockSpec((1, ts, RWKV_IN_WIDTH), lambda b, i: (b, i, OFF_RW // RWKV_IN_WIDTH)),
        pl.BlockSpec((1, 8, RWKV_IN_WIDTH),
                     lambda b, i: (b, jnp.maximum(i * (ts // 8) - 1, 0), OFF_RW // RWKV_IN_WIDTH)),
        full((1, RWKV_IN_WIDTH)), full((1, w)), full((DECAY_LORA, w)), full((1, w)),
        full((ICLR_LORA, w)), full((GATE_LORA, w)), full((1, w)), full((1, w)),
    ]
    args = [proj, proj, row(mu), row(w0), w2, row(a0), a2, g2, row(k_k), row(k_a)]
    if has_vres:
        v0, v1, v2 = vres
        in_specs += [tile, full((1, w)), full(v1.shape), full(v2.shape)]
        args += [v_first, row(v0), v1, v2]
    out = jax.ShapeDtypeStruct((bsz, seq, w), F32)
    return pl.pallas_call(
        functools.partial(_rwkv_prep_kernel, has_vres=has_vres),
        grid=(bsz, nt),
        in_specs=in_specs,
        out_specs=[tile] * 7,
        out_shape=[out] * 7,
        compiler_params=_params("parallel", "parallel"),
        name="rwkv_prep",
    )(*args)


def _mm(a, b):
    return jnp.dot(a.astype(BF16), b.astype(BF16), preferred_element_type=F32)


def _mm_nt(a, b):
    return lax.dot_general(a.astype(BF16), b.astype(BF16), (((1,), (1,)), ((), ())),
                           preferred_element_type=F32)


def _mm_tn(a, b):
    return lax.dot_general(a.astype(BF16), b.astype(BF16), (((0,), (0,)), ((), ())),
                           preferred_element_type=F32)


def _split_bf16(x):
    hi = x.astype(BF16)
    lo = (x - hi.astype(F32)).astype(BF16)
    return hi, lo


def _wkv_maps_kernel(r_ref, lw_ref, k_ref, v_ref, kk_ref, b_ref,
                     q_out, y1_out, m_out, g_out,
                     at_ref, rt_ref, bt_ref, kt_ref, dec_ref):
    c = WKV_CHUNK
    n = HEAD_DIM
    rows = r_ref.shape[1]
    nchunk = rows // c

    row = lax.broadcasted_iota(jnp.int32, (rows, rows), 0)
    col = lax.broadcasted_iota(jnp.int32, (rows, rows), 1)
    tri = ((row >= col) & ((row // c) == (col // c))).astype(BF16)
    lw = lw_ref[0]
    lw_hi, lw_lo = _split_bf16(lw)
    cum = (jnp.dot(tri, lw_hi, preferred_element_type=F32)
           + jnp.dot(tri, lw_lo, preferred_element_type=F32))
    e_pos = jnp.exp(cum)
    e_neg = jnp.exp(-cum)
    at_ref[...] = -kk_ref[0] * jnp.exp(cum - lw)
    rt_ref[...] = r_ref[0] * e_pos
    bt_ref[...] = b_ref[0] * e_neg
    kt_ref[...] = k_ref[0] * e_neg
    dec_ref[...] = e_pos

    crow = lax.broadcasted_iota(jnp.int32, (c, c), 0)
    ccol = lax.broadcasted_iota(jnp.int32, (c, c), 1)
    lower = crow >= ccol
    strict = crow > ccol
    eye = (crow == ccol).astype(F32)

    pairs = [(j, h) for j in range(nchunk) for h in range(N_HEADS)]

    def tile(ref, j, h, bf16=True):
        x = ref[j * c:(j + 1) * c, h * n:(h + 1) * n]
        return x.astype(BF16) if bf16 else x

    ah = [tile(at_ref, j, h) for j, h in pairs]
    rh = [tile(rt_ref, j, h, bf16=False) for j, h in pairs]
    bh = [tile(bt_ref, j, h) for j, h in pairs]
    kh = [tile(kt_ref, j, h) for j, h in pairs]
    vh = [v_ref[0, j * c:(j + 1) * c, h * n:(h + 1) * n].astype(BF16) for j, h in pairs]
    dh = [dec_ref[(j + 1) * c - 1:(j + 1) * c, h * n:(h + 1) * n] for j, h in pairs]
    idx = range(len(pairs))
    ar = [jnp.concatenate([ah[i], rh[i].astype(BF16)], axis=0) for i in idx]
    bk = [jnp.concatenate([bh[i], kh[i]], axis=0) for i in idx]
    aa = [_mm_nt(ar[i], bk[i]) for i in idx]
    n_ab = [jnp.where(strict, aa[i][:c, :c], 0.0) for i in idx]
    a_kr = [jnp.concatenate([jnp.where(strict, aa[i][:c, c:], 0.0),
                             jnp.where(lower, aa[i][c:, c:], 0.0)], axis=0).astype(BF16)
            for i in idx]
    a_rb = [jnp.where(lower, aa[i][c:, :c], 0.0).astype(BF16) for i in idx]
    t = [eye + n_ab[i] for i in idx]
    pw = n_ab
    for _ in range(int(math.log2(c)) - 1):
        pw = [_mm(pw[i], pw[i]) for i in idx]
        t = [t[i] + _mm(pw[i], t[i]) for i in idx]
    av = [_mm(a_kr[i], vh[i]) for i in idx]
    wu = [_mm(t[i], jnp.concatenate([ah[i], av[i][:c].astype(BF16)], axis=1)).astype(BF16)
          for i in idx]
    aw = [_mm(a_rb[i], wu[i]) for i in idx]
    q = [rh[i] + aw[i][:, :n] for i in idx]
    y1 = [aw[i][:, n:] + av[i][c:] for i in idx]
    m = [(eye + _mm_tn(wu[i][:, :n], bh[i])) * dh[i] for i in idx]
    g = [_mm_tn(jnp.concatenate([wu[i][:, n:], vh[i]], axis=0), bk[i]) * dh[i] for i in idx]
    for j in range(nchunk):
        sel = slice(j * N_HEADS, (j + 1) * N_HEADS)
        q_out[0, j * c:(j + 1) * c, :] = jnp.concatenate(q[sel], axis=-1).astype(q_out.dtype)
        y1_out[0, j * c:(j + 1) * c, :] = jnp.concatenate(y1[sel], axis=-1)
        m_out[0, j * c:(j + 1) * c, :] = jnp.concatenate(m[sel], axis=-1).astype(m_out.dtype)
        g_out[0, j * c:(j + 1) * c, :] = jnp.concatenate(g[sel], axis=-1)


def _wkv_maps(r, lw, k, v, kk, b, rows=256):
    bsz, seq, w = r.shape
    tile = pl.BlockSpec((1, rows, w), lambda bi, ti: (bi, ti, 0))
    f32 = jax.ShapeDtypeStruct((bsz, seq, w), F32)
    bf16 = jax.ShapeDtypeStruct((bsz, seq, w), BF16)
    return pl.pallas_call(
        _wkv_maps_kernel,
        grid=(bsz, seq // rows),
        in_specs=[tile] * 6,
        out_specs=[tile] * 4,
        out_shape=[bf16, f32, bf16, f32],
        scratch_shapes=[pltpu.VMEM((rows, w), F32)] * 5,
        compiler_params=_params("parallel", "parallel"),
        name="wkv_chunk_maps",
    )(r, lw, k, v, kk, b)


def _wkv_scan_kernel(q_ref, y1_ref, m_ref, gm_ref, r_ref, k_ref, v_ref, g_ref,
                     rk_ref, lng_ref, lnb_ref, o_ref, st_ref):
    n = HEAD_DIM
    bsz = q_ref.shape[0]

    @pl.when(pl.program_id(0) == 0)
    def _():
        st_ref[...] = jnp.zeros(st_ref.shape, F32)

    bd = _head_blockdiag(GROUP_WIDTH).astype(BF16)

    def head_sum(x):
        hi, lo = _split_bf16(x)
        return (jnp.dot(hi, bd, preferred_element_type=F32)
                + jnp.dot(lo, bd, preferred_element_type=F32))

    for bi in range(bsz):
        ys = []
        for h in range(N_HEADS):
            sl = slice(h * n, (h + 1) * n)
            s0 = st_ref[bi * N_HEADS + h].astype(BF16)
            ys.append(_mm_nt(q_ref[bi, :, sl], s0) + y1_ref[bi, :, sl])
            st_ref[bi * N_HEADS + h] = _mm(s0, m_ref[bi, :, sl]) + gm_ref[bi, :, sl]
        y = jnp.concatenate(ys, axis=-1)
        mean = head_sum(y) * (1.0 / n)
        yc = y - mean
        var = head_sum(yc * yc) * (1.0 / n)
        yn = yc * lax.rsqrt(var + LN_X_EPS) * lng_ref[...] + lnb_ref[...]
        r = r_ref[bi]
        v = v_ref[bi]
        bonus = head_sum(r * k_ref[bi] * rk_ref[...]) * v
        o_ref[bi] = ((yn + bonus) * g_ref[bi]).astype(o_ref.dtype)


def _wkv(r, lw, k, v, kk, b, g, r_k, ln_g, ln_b):
    bsz, seq, w = r.shape
    c = WKV_CHUNK
    q, y1, m, gm = _wkv_maps(r, lw, k, v, kk, b)
    tile = pl.BlockSpec((bsz, c, w), lambda ci: (0, ci, 0))
    vec = pl.BlockSpec((1, w), lambda ci: (0, 0))
    return pl.pallas_call(
        _wkv_scan_kernel,
        grid=(seq // c,),
        in_specs=[tile] * 8 + [vec] * 3,
        out_specs=tile,
        out_shape=jax.ShapeDtypeStruct((bsz, seq, w), BF16),
        scratch_shapes=[pltpu.VMEM((bsz * N_HEADS, HEAD_DIM, HEAD_DIM), F32)],
        compiler_params=_params("arbitrary"),
        name="wkv_state_scan",
    )(q, y1, m, gm, r, k, v, g, r_k.reshape(1, w), ln_g.reshape(1, w), ln_b.reshape(1, w))


def _wout_kernel(x_ref, y0_ref, y1_ref, y2_ref, y3_ref, w_ref, o_ref):
    acc = x_ref[...]
    for idx, y_ref in enumerate((y0_ref, y1_ref, y2_ref, y3_ref)):
        acc = acc + jnp.dot(y_ref[...], w_ref[idx * GROUP_WIDTH:(idx + 1) * GROUP_WIDTH, :],
                            preferred_element_type=F32)
    o_ref[...] = acc


def _wout(x2d, ys, w_bf16, tm=512):
    m, d = x2d.shape
    ytile = pl.BlockSpec((tm, GROUP_WIDTH), lambda i: (i, 0))
    xtile = pl.BlockSpec((tm, d), lambda i: (i, 0))
    return pl.pallas_call(
        _wout_kernel,
        grid=(m // tm,),
        in_specs=[xtile] + [ytile] * 4 + [pl.BlockSpec(w_bf16.shape, lambda i: (0, 0))],
        out_specs=xtile,
        out_shape=jax.ShapeDtypeStruct((m, d), F32),
        compiler_params=_params("parallel"),
        name="wout_residual",
    )(x2d, *ys, w_bf16)


def _ffn_kernel(x_ref, g_ref, wu_ref, wd_ref, o_ref, h_ref):
    @pl.when(pl.program_id(1) == 0)
    def _():
        x = x_ref[...]
        ms = jnp.mean(x * x, axis=-1, keepdims=True)
        h_ref[...] = (x * lax.rsqrt(ms + RMS_EPS) * g_ref[...]).astype(BF16)
        o_ref[...] = x

    u = jnp.dot(h_ref[...], wu_ref[...], preferred_element_type=F32)
    act = jnp.square(jnp.maximum(u, 0.0)).astype(BF16)
    o_ref[...] += jnp.dot(act, wd_ref[...], preferred_element_type=F32)


def _ffn(x2d, gain, wu_bf16, wd_bf16, tm=512, tf=512):
    m, d = x2d.shape
    f = wu_bf16.shape[1]
    xtile = pl.BlockSpec((tm, d), lambda i, j: (i, 0))
    return pl.pallas_call(
        _ffn_kernel,
        grid=(m // tm, f // tf),
        in_specs=[
            xtile,
            pl.BlockSpec((1, d), lambda i, j: (0, 0)),
            pl.BlockSpec((d, tf), lambda i, j: (0, j)),
            pl.BlockSpec((tf, d), lambda i, j: (j, 0)),
        ],
        out_specs=xtile,
        out_shape=jax.ShapeDtypeStruct((m, d), F32),
        scratch_shapes=[pltpu.VMEM((tm, d), BF16)],
        compiler_params=_params("parallel", "arbitrary"),
        name="ffn",
    )(x2d, gain.reshape(1, d), wu_bf16, wd_bf16)


def _permute_w_in(w):
    gw = GROUP_WIDTH
    conv = w[:, 0:3 * gw]
    swa_q = w[:, 3 * gw:4 * gw]
    swa_kv = w[:, 4 * gw:4 * gw + 2 * SWA_KV_HEADS * HEAD_DIM]
    dil0 = 4 * gw + 2 * SWA_KV_HEADS * HEAD_DIM
    dil = w[:, dil0:dil0 + 3 * gw]
    rw = w[:, dil0 + 3 * gw:]
    return jnp.concatenate([rw, swa_kv, swa_q, conv, dil], axis=1)


def kernel(x, norm_mix, w_in, conv_w, swa_q_norm, swa_k_norm, swa_sink, dil_q_norm, dil_k_norm,
           rwkv_mu, decay_w0, decay_w2, iclr_a0, iclr_a2, gate_g2, k_k, k_a, r_k, ln_x_g, ln_x_b,
           vres_v0, vres_v1, vres_v2, w_out, norm_ffn, w_up, w_down, rel_bias):
    bsz, seq, d = x.shape
    depth = w_in.shape[0]
    bias = _bias_tiles(rel_bias)
    swa_bias = bias[0, :N_HEADS]
    dil_bias = bias[:, N_HEADS:]
    x2d = x.reshape(bsz * seq, d)
    v_first = None
    for layer in range(depth):
        w_in_p = _permute_w_in(w_in[layer]).astype(BF16)
        proj = _norm_matmul(x2d, norm_mix[layer], w_in_p).reshape(bsz, seq, IN_WIDTH)
        y_conv = _conv_mixer(proj, conv_w[layer])
        y_swa = _swa_mixer(proj, swa_q_norm[layer], swa_k_norm[layer], swa_sink[layer], swa_bias)
        y_dil = _dil_mixer(proj, dil_q_norm[layer], dil_k_norm[layer], dil_bias)
        vres = None if layer == 0 else (vres_v0[layer - 1], vres_v1[layer - 1], vres_v2[layer - 1])
        r, lw, k, v, kk, b, g = _rwkv_prep(
            proj, rwkv_mu[layer], decay_w0[layer], decay_w2[layer], iclr_a0[layer], iclr_a2[layer],
            gate_g2[layer], k_k[layer], k_a[layer], v_first, vres)
        if layer == 0:
            v_first = v
        y_rwkv = _wkv(r, lw, k, v, kk, b, g, r_k[layer].reshape(-1), ln_x_g[layer], ln_x_b[layer])
        ys = [y.reshape(bsz * seq, GROUP_WIDTH) for y in (y_conv, y_swa, y_dil, y_rwkv)]
        x2d = _wout(x2d, ys, w_out[layer].astype(BF16))
        x2d = _ffn(x2d, norm_ffn[layer], w_up[layer].astype(BF16), w_down[layer].astype(BF16))
    return x2d.reshape(bsz, seq, d)
```

```python
import functools
import math

import jax
import jax.numpy as jnp
from jax import lax
from jax.experimental import pallas as pl
from jax.experimental.pallas import tpu as pltpu

F32 = jnp.float32
BF16 = jnp.bfloat16

D_MODEL = 2048
HEAD_DIM = 64
GROUP_WIDTH = 512
N_HEADS = GROUP_WIDTH // HEAD_DIM
SWA_KV_HEADS = 2
SWA_GROUP = N_HEADS // SWA_KV_HEADS
SWA_WINDOW = 128
DIL_PAIRS = ((128, 1), (512, 4), (2048, 16))
DECAY_LORA = 64
ICLR_LORA = 64
GATE_LORA = 128
RWKV_IN_WIDTH = 3 * GROUP_WIDTH + DECAY_LORA + ICLR_LORA + GATE_LORA
BLK = 128
NUM_BUCKETS = 32
BUCKET_MAX_DIST = 128
RMS_EPS = 1e-6
LN_X_EPS = 64e-5
NEG = -1e30
WKV_CHUNK = 64
DIL_TILES = 4

OFF_RW = 0
OFF_SWA_K = OFF_RW + RWKV_IN_WIDTH
OFF_SWA_V = OFF_SWA_K + SWA_KV_HEADS * HEAD_DIM
OFF_SWA_Q = OFF_SWA_V + SWA_KV_HEADS * HEAD_DIM
OFF_CONV = OFF_SWA_Q + GROUP_WIDTH
OFF_DIL = OFF_CONV + 3 * GROUP_WIDTH
IN_WIDTH = OFF_DIL + 3 * GROUP_WIDTH

VMEM_LIMIT = 48 * 1024 * 1024


def _params(*sem):
    return pltpu.CompilerParams(dimension_semantics=sem, vmem_limit_bytes=VMEM_LIMIT)


def _mm(a, b):
    return jnp.dot(a.astype(BF16), b.astype(BF16), preferred_element_type=F32)


def _mm_nt(a, b):
    return lax.dot_general(a.astype(BF16), b.astype(BF16), (((1,), (1,)), ((), ())),
                           preferred_element_type=F32)


def _mm_tn(a, b):
    return lax.dot_general(a.astype(BF16), b.astype(BF16), (((0,), (0,)), ((), ())),
                           preferred_element_type=F32)


def _split_bf16(x):
    hi = x.astype(BF16)
    lo = (x - hi.astype(F32)).astype(BF16)
    return hi, lo


def _head_sums(x):
    lanes = 128
    r = lax.broadcasted_iota(jnp.int32, (lanes, lanes), 0) // HEAD_DIM
    c = lax.broadcasted_iota(jnp.int32, (lanes, lanes), 1) // HEAD_DIM
    bd = (r == c).astype(BF16)
    hi, lo = _split_bf16(x)
    cols = []
    for j in range(x.shape[-1] // lanes):
        sl = slice(j * lanes, (j + 1) * lanes)
        cols.append(jnp.dot(hi[:, sl], bd, preferred_element_type=F32)
                    + jnp.dot(lo[:, sl], bd, preferred_element_type=F32))
    return cols[0] if len(cols) == 1 else jnp.concatenate(cols, axis=-1)


def _sigmoid(z):
    return 1.0 / (1.0 + jnp.exp(-z))


def _norm_matmul_kernel(x_ref, g_ref, w_ref, o_ref, h_ref):
    @pl.when(pl.program_id(1) == 0)
    def _():
        x = x_ref[...]
        ms = jnp.mean(x * x, axis=-1, keepdims=True)
        h_ref[...] = (x * lax.rsqrt(ms + RMS_EPS) * g_ref[...]).astype(BF16)

    o_ref[...] = jnp.dot(h_ref[...], w_ref[...], preferred_element_type=F32)


def _norm_matmul(x2d, gain, w_bf16, tm=1024, tn=512):
    m, k = x2d.shape
    n = w_bf16.shape[1]
    return pl.pallas_call(
        _norm_matmul_kernel,
        grid=(m // tm, n // tn),
        in_specs=[
            pl.BlockSpec((tm, k), lambda i, j: (i, 0)),
            pl.BlockSpec((1, k), lambda i, j: (0, 0)),
            pl.BlockSpec((k, tn), lambda i, j: (0, j)),
        ],
        out_specs=pl.BlockSpec((tm, tn), lambda i, j: (i, j)),
        out_shape=jax.ShapeDtypeStruct((m, n), F32),
        scratch_shapes=[pltpu.VMEM((tm, k), BF16)],
        compiler_params=_params("parallel", "arbitrary"),
        name="norm_matmul",
    )(x2d, gain.reshape(1, k), w_bf16)


def _t5_bucket(dist):
    dist = jnp.maximum(dist, 0)
    max_exact = NUM_BUCKETS // 2
    scaled = (jnp.log(jnp.maximum(dist, 1).astype(F32) / max_exact)
              / math.log(BUCKET_MAX_DIST / max_exact))
    large = max_exact + (scaled * (NUM_BUCKETS - max_exact)).astype(jnp.int32)
    large = jnp.minimum(large, NUM_BUCKETS - 1)
    return jnp.where(dist < max_exact, dist, large)


def _bias_kernel(bucket_ref, table_ref, o_ref):
    h = pl.program_id(1)
    bucket = bucket_ref[0]
    acc = jnp.zeros(bucket.shape, F32)
    for b in range(NUM_BUCKETS):
        acc = jnp.where(bucket == b, table_ref[b, h], acc)
    o_ref[0, 0] = acc


def _bias_tiles(rel_bias):
    dist = BLK + jnp.arange(BLK)[:, None] - jnp.arange(2 * BLK)[None, :]
    buckets = jnp.stack([_t5_bucket(dist * r) for _, r in DIL_PAIRS]).astype(jnp.int32)
    nh = rel_bias.shape[1]
    return pl.pallas_call(
        _bias_kernel,
        grid=(len(DIL_PAIRS), nh),
        in_specs=[
            pl.BlockSpec((1, BLK, 2 * BLK), lambda s, h: (s, 0, 0)),
            pl.BlockSpec(memory_space=pltpu.SMEM),
        ],
        out_specs=pl.BlockSpec((1, 1, BLK, 2 * BLK), lambda s, h: (s, h, 0, 0)),
        out_shape=jax.ShapeDtypeStruct((len(DIL_PAIRS), nh, BLK, 2 * BLK), F32),
        compiler_params=_params("arbitrary", "arbitrary"),
        name="bias_tiles",
    )(buckets, rel_bias)


def _conv_kernel(b_ref, c_ref, u_ref, w_ref, o_ref):
    z = c_ref[0] * u_ref[0]
    row = lax.broadcasted_iota(jnp.int32, z.shape, 0)
    z1 = jnp.where(row >= 1, pltpu.roll(z, 1, axis=0), 0.0)
    z2 = jnp.where(row >= 2, pltpu.roll(z, 2, axis=0), 0.0)
    w = w_ref[...]
    y = z2 * w[0:1, :] + z1 * w[1:2, :] + z * w[2:3, :]
    o_ref[0] = (b_ref[0] * y).astype(o_ref.dtype)


def _conv_mixer(proj, conv_w):
    bsz, seq, _ = proj.shape
    lanes = 128
    nblk = GROUP_WIDTH // lanes
    base = OFF_CONV // lanes

    def col(seg):
        return pl.BlockSpec((1, seq, lanes), lambda b, j: (b, 0, base + seg * nblk + j))

    return pl.pallas_call(
        _conv_kernel,
        grid=(bsz, nblk),
        in_specs=[col(0), col(1), col(2), pl.BlockSpec((3, lanes), lambda b, j: (0, j))],
        out_specs=pl.BlockSpec((1, seq, lanes), lambda b, j: (b, 0, j)),
        out_shape=jax.ShapeDtypeStruct((bsz, seq, GROUP_WIDTH), BF16),
        compiler_params=_params("parallel", "parallel"),
        name="conv_mixer",
    )(proj, proj, proj, conv_w)


def _band_mask(max_dist):
    a = lax.broadcasted_iota(jnp.int32, (BLK, 2 * BLK), 0)
    b = lax.broadcasted_iota(jnp.int32, (BLK, 2 * BLK), 1)
    dist = BLK + a - b
    return (dist >= 0) & (dist <= max_dist), b


def _attend(qs, kws, vws, biases, sinks=None):
    idx = range(len(qs))
    s = [lax.dot_general(qs[i], kws[i], (((1,), (1,)), ((), ())), preferred_element_type=F32)
         + biases[i] for i in idx]
    m = [jnp.max(s[i], axis=-1, keepdims=True) for i in idx]
    if sinks is not None:
        m = [jnp.maximum(m[i], sinks[i]) for i in idx]
    p = [jnp.exp(s[i] - m[i]) for i in idx]
    den = [jnp.sum(p[i], axis=-1, keepdims=True) for i in idx]
    if sinks is not None:
        den = [den[i] + jnp.exp(sinks[i] - m[i]) for i in idx]
    o = [jnp.dot(p[i].astype(BF16), vws[i], preferred_element_type=F32) / den[i] for i in idx]
    return o, m, den


def _head_rms(x, gain):
    ms = _head_sums(x * x) * (1.0 / HEAD_DIM)
    return x * lax.rsqrt(ms + RMS_EPS) * gain


def _swa_kernel(q_ref, k_ref, v_ref, qg_ref, kg_ref, sink_ref, bias_ref, o_ref,
                qn_ref, kn_ref, vb_ref, bm_ref):
    seq = q_ref.shape[1]
    nb = seq // BLK
    kvw = SWA_KV_HEADS * HEAD_DIM
    scale = HEAD_DIM ** -0.5

    kn_ref[0:BLK, :] = jnp.zeros((BLK, kvw), BF16)
    vb_ref[0:BLK, :] = jnp.zeros((BLK, kvw), BF16)

    def prep(i, carry):
        r0 = pl.multiple_of(i * BLK, BLK)
        q = q_ref[0, pl.ds(r0, BLK), :]
        qn_ref[pl.ds(r0, BLK), :] = (_head_rms(q, qg_ref[...]) * scale).astype(BF16)
        k = k_ref[0, pl.ds(r0, BLK), :]
        kn_ref[pl.ds(r0 + BLK, BLK), :] = _head_rms(k, kg_ref[...]).astype(BF16)
        vb_ref[pl.ds(r0 + BLK, BLK), :] = v_ref[0, pl.ds(r0, BLK), :].astype(BF16)
        return carry

    lax.fori_loop(0, nb, prep, 0)

    band, kcol = _band_mask(SWA_WINDOW - 1)
    for h in range(N_HEADS):
        bm_ref[h] = jnp.where(band & (kcol >= BLK), bias_ref[h], NEG)
        bm_ref[N_HEADS + h] = jnp.where(band, bias_ref[h], NEG)
    sinks = [sink_ref[h] for h in range(N_HEADS)]

    def block(i, carry):
        r0 = pl.multiple_of(i * BLK, BLK)
        later = jnp.minimum(i, 1) * N_HEADS
        qs, kws, vws, bms = [], [], [], []
        for hk in range(SWA_KV_HEADS):
            kw = kn_ref[pl.ds(r0, 2 * BLK), hk * HEAD_DIM:(hk + 1) * HEAD_DIM]
            vw = vb_ref[pl.ds(r0, 2 * BLK), hk * HEAD_DIM:(hk + 1) * HEAD_DIM]
            for g in range(SWA_GROUP):
                h = hk * SWA_GROUP + g
                qs.append(qn_ref[pl.ds(r0, BLK), h * HEAD_DIM:(h + 1) * HEAD_DIM])
                kws.append(kw)
                vws.append(vw)
                bms.append(bm_ref[later + h])
        outs, _, _ = _attend(qs, kws, vws, bms, sinks)
        o_ref[0, pl.ds(r0, BLK), :] = jnp.concatenate(outs, axis=-1).astype(o_ref.dtype)
        return carry

    lax.fori_loop(0, nb, block, 0)


def _swa_mixer(proj, q_gain, k_gain, sink, bias):
    bsz, seq, _ = proj.shape
    kvw = SWA_KV_HEADS * HEAD_DIM
    q_gain_t = jnp.tile(q_gain, N_HEADS).reshape(1, GROUP_WIDTH)
    k_gain_t = jnp.tile(k_gain, SWA_KV_HEADS).reshape(1, kvw)
    return pl.pallas_call(
        _swa_kernel,
        grid=(bsz,),
        in_specs=[
            pl.BlockSpec((1, seq, GROUP_WIDTH), lambda b: (b, 0, OFF_SWA_Q // GROUP_WIDTH)),
            pl.BlockSpec((1, seq, kvw), lambda b: (b, 0, OFF_SWA_K // kvw)),
            pl.BlockSpec((1, seq, kvw), lambda b: (b, 0, OFF_SWA_V // kvw)),
            pl.BlockSpec((1, GROUP_WIDTH), lambda b: (0, 0)),
            pl.BlockSpec((1, kvw), lambda b: (0, 0)),
            pl.BlockSpec(memory_space=pltpu.SMEM),
            pl.BlockSpec((N_HEADS, BLK, 2 * BLK), lambda b: (0, 0, 0)),
        ],
        out_specs=pl.BlockSpec((1, seq, GROUP_WIDTH), lambda b: (b, 0, 0)),
        out_shape=jax.ShapeDtypeStruct((bsz, seq, GROUP_WIDTH), BF16),
        scratch_shapes=[
            pltpu.VMEM((seq, GROUP_WIDTH), BF16),
            pltpu.VMEM((seq + BLK, kvw), BF16),
            pltpu.VMEM((seq + BLK, kvw), BF16),
            pltpu.VMEM((2 * N_HEADS, BLK, 2 * BLK), F32),
        ],
        compiler_params=_params("parallel"),
        name="swa_mixer",
    )(proj, proj, proj, q_gain_t, k_gain_t, sink, bias)


def _dil_kernel(q_ref, k_ref, v_ref, qg_ref, kg_ref, bias_ref, o_ref,
                qn_ref, kn_ref, ob_ref, lb_ref, bm_ref):
    seq = q_ref.shape[1]
    lanes = q_ref.shape[2]
    heads = lanes // HEAD_DIM
    scale = HEAD_DIM ** -0.5

    def prep(i, carry):
        r0 = pl.multiple_of(i * BLK, BLK)
        qn_ref[pl.ds(r0, BLK), :] = _head_rms(q_ref[0, pl.ds(r0, BLK), :], qg_ref[...]) * scale
        kn_ref[pl.ds(r0, BLK), :] = _head_rms(k_ref[0, pl.ds(r0, BLK), :], kg_ref[...])
        return carry

    lax.fori_loop(0, seq // BLK, prep, 0)

    for br, (window, r) in enumerate(DIL_PAIRS):
        band, kcol = _band_mask(window // r)
        for h in range(heads):
            bm_ref[(br * heads + h) * 2] = jnp.where(band & (kcol >= BLK), bias_ref[br, h], NEG)
            bm_ref[(br * heads + h) * 2 + 1] = jnp.where(band, bias_ref[br, h], NEG)

    for br, (window, r) in enumerate(DIL_PAIRS):
        nb = seq // r // BLK

        def blocks(it, carry, br=br, r=r, nb=nb):
            qs, kws, vws, bms, curs = [], [], [], [], []
            for u in range(DIL_TILES):
                t = it * DIL_TILES + u
                c = t // nb
                i = t - c * nb
                cur = c + i * (BLK * r)
                prev = jnp.maximum(cur - BLK * r, c)
                later = jnp.minimum(i, 1)

                def rows(ref, start):
                    if r == 1:
                        return ref[pl.ds(start, BLK), :]
                    return ref[pl.ds(start, BLK, stride=r), :]

                q = rows(qn_ref, cur).astype(BF16)
                kw = jnp.concatenate([rows(kn_ref, prev), rows(kn_ref, cur)], axis=0).astype(BF16)
                vw = jnp.concatenate([rows(v_ref.at[0], prev), rows(v_ref.at[0], cur)],
                                     axis=0).astype(BF16)
                curs.append(cur)
                for h in range(heads):
                    sl = slice(h * HEAD_DIM, (h + 1) * HEAD_DIM)
                    qs.append(q[:, sl])
                    kws.append(kw[:, sl])
                    vws.append(vw[:, sl])
                    bms.append(bm_ref[(br * heads + h) * 2 + later])
            outs, ms, dens = _attend(qs, kws, vws, bms)
            for u in range(DIL_TILES):
                sel = slice(u * heads, (u + 1) * heads)
                o_all = jnp.concatenate(outs[sel], axis=-1)
                l_all = jnp.concatenate(
                    [jnp.broadcast_to(m + jnp.log(d), (BLK, HEAD_DIM))
                     for m, d in zip(ms[sel], dens[sel])], axis=-1)
                if r == 1:
                    ob_ref[br, pl.ds(curs[u], BLK), :] = o_all
                    lb_ref[br, pl.ds(curs[u], BLK), :] = l_all
                else:
                    ob_ref[br, pl.ds(curs[u], BLK, stride=r), :] = o_all
                    lb_ref[br, pl.ds(curs[u], BLK, stride=r), :] = l_all
            return carry

        lax.fori_loop(0, r * nb // DIL_TILES, blocks, 0)

    def combine(i, carry):
        r0 = pl.multiple_of(i * BLK, BLK)
        l0 = lb_ref[0, pl.ds(r0, BLK), :]
        l1 = lb_ref[1, pl.ds(r0, BLK), :]
        l2 = lb_ref[2, pl.ds(r0, BLK), :]
        m = jnp.maximum(jnp.maximum(l0, l1), l2)
        e0, e1, e2 = jnp.exp(l0 - m), jnp.exp(l1 - m), jnp.exp(l2 - m)
        tot = e0 + e1 + e2
        o = ((e0 / tot) * ob_ref[0, pl.ds(r0, BLK), :]
             + (e1 / tot) * ob_ref[1, pl.ds(r0, BLK), :]
             + (e2 / tot) * ob_ref[2, pl.ds(r0, BLK), :])
        o_ref[0, pl.ds(r0, BLK), :] = o.astype(o_ref.dtype)
        return carry

    lax.fori_loop(0, seq // BLK, combine, 0)


def _dil_mixer(proj, q_gain, k_gain, bias):
    bsz, seq, _ = proj.shape
    lanes = 128
    heads = lanes // HEAD_DIM
    nblk = GROUP_WIDTH // lanes
    base = OFF_DIL // lanes
    gq = jnp.tile(q_gain, heads).reshape(1, lanes)
    gk = jnp.tile(k_gain, heads).reshape(1, lanes)

    def col(seg):
        return pl.BlockSpec((1, seq, lanes), lambda b, j: (b, 0, base + seg * nblk + j))

    return pl.pallas_call(
        _dil_kernel,
        grid=(bsz, nblk),
        in_specs=[
            col(0), col(1), col(2),
            pl.BlockSpec((1, lanes), lambda b, j: (0, 0)),
            pl.BlockSpec((1, lanes), lambda b, j: (0, 0)),
            pl.BlockSpec((len(DIL_PAIRS), heads, BLK, 2 * BLK), lambda b, j: (0, j, 0, 0)),
        ],
        out_specs=pl.BlockSpec((1, seq, lanes), lambda b, j: (b, 0, j)),
        out_shape=jax.ShapeDtypeStruct((bsz, seq, GROUP_WIDTH), BF16),
        scratch_shapes=[
            pltpu.VMEM((seq, lanes), F32),
            pltpu.VMEM((seq, lanes), F32),
            pltpu.VMEM((len(DIL_PAIRS), seq, lanes), F32),
            pltpu.VMEM((len(DIL_PAIRS), seq, lanes), F32),
            pltpu.VMEM((len(DIL_PAIRS) * heads * 2, BLK, 2 * BLK), F32),
        ],
        compiler_params=_params("parallel", "parallel"),
        name="dil_mixer",
    )(proj, proj, proj, gq, gk, bias)


def _rwkv_prep_kernel(*refs, has_vres):
    if has_vres:
        (p_ref, pp_ref, mu_ref, w0_ref, w2_ref, a0_ref, a2_ref, g2_ref, kk_ref, ka_ref,
         vf_ref, v0_ref, v1_ref, v2_ref,
         r_out, lw_out, k_out, v_out, kk_out, b_out, g_out) = refs
    else:
        (p_ref, pp_ref, mu_ref, w0_ref, w2_ref, a0_ref, a2_ref, g2_ref, kk_ref, ka_ref,
         r_out, lw_out, k_out, v_out, kk_out, b_out, g_out) = refs
    w = GROUP_WIDTH
    p = p_ref[0]
    row = lax.broadcasted_iota(jnp.int32, p.shape, 0)
    last_prev = pp_ref[0, 7:8, :]
    last_prev = jnp.where(pl.program_id(1) > 0, last_prev, 0.0)
    prev = jnp.where(row >= 1, pltpu.roll(p, 1, axis=0), last_prev)
    xs = p + (prev - p) * mu_ref[...]
    r = xs[:, 0:w]
    k = xs[:, w:2 * w]
    v = xs[:, 2 * w:3 * w]
    o = 3 * w
    wd = xs[:, o:o + DECAY_LORA]
    ad = xs[:, o + DECAY_LORA:o + DECAY_LORA + ICLR_LORA]
    gd = xs[:, o + DECAY_LORA + ICLR_LORA:]

    z = -(w0_ref[...] + _mm(jnp.tanh(wd), w2_ref[...]))
    softplus = jnp.maximum(z, 0.0) + jnp.log(1.0 + jnp.exp(-jnp.abs(z)))
    logw = -softplus - 0.5
    lw_out[0] = -jnp.exp(logw)
    a = _sigmoid(a0_ref[...] + _mm(ad, a2_ref[...]))
    g_out[0] = _mm(_sigmoid(gd), g2_ref[...])
    if has_vres:
        mix = _sigmoid(v0_ref[...] + _mm(_mm(v, v1_ref[...]), v2_ref[...]))
        v = v + (vf_ref[0] - v) * mix
    kk = k * kk_ref[...]
    ss = _head_sums(kk * kk)
    kk = kk * lax.rsqrt(jnp.maximum(ss, 1e-24))
    r_out[0] = r
    k_out[0] = k * (1.0 + (a - 1.0) * ka_ref[...])
    v_out[0] = v
    kk_out[0] = kk
    b_out[0] = kk * a


def _rwkv_prep(proj, mu, w0, w2, a0, a2, g2, k_k, k_a, v_first, vres, ts=256):
    bsz, seq, _ = proj.shape
    w = GROUP_WIDTH
    nt = seq // ts
    has_vres = vres is not None

    def full(shape):
        return pl.BlockSpec(shape, lambda b, i: (0,) * len(shape))

    row = lambda a: a.reshape(1, -1)
    tile = pl.BlockSpec((1, ts, w), lambda b, i: (b, i, 0))
    in_specs = [
        pl.BlockSpec((1, ts, RWKV_IN_WIDTH), lambda b, i: (b, i, OFF_RW // RWKV_IN_WIDTH)),
        pl.BlockSpec((1, 8, RWKV_IN_WIDTH),
                     lambda b, i: (b, jnp.maximum(i * (ts // 8) - 1, 0), OFF_RW // RWKV_IN_WIDTH)),
        full((1, RWKV_IN_WIDTH)), full((1, w)), full((DECAY_LORA, w)), full((1, w)),
        full((ICLR_LORA, w)), full((GATE_LORA, w)), full((1, w)), full((1, w)),
    ]
    args = [proj, proj, row(mu), row(w0), w2, row(a0), a2, g2, row(k_k), row(k_a)]
    if has_vres:
        v0, v1, v2 = vres
        in_specs += [tile, full((1, w)), full(v1.shape), full(v2.shape)]
        args += [v_first, row(v0), v1, v2]
    out = jax.ShapeDtypeStruct((bsz, seq, w), F32)
    return pl.pallas_call(
        functools.partial(_rwkv_prep_kernel, has_vres=has_vres),
        grid=(bsz, nt),
        in_specs=in_specs,
        out_specs=[tile] * 7,
        out_shape=[out] * 7,
        compiler_params=_params("parallel", "parallel"),
        name="rwkv_prep",
    )(*args)


def _wkv_maps_kernel(r_ref, lw_ref, k_ref, v_ref, kk_ref, b_ref,
                     q_out, y1_out, m_out, g_out,
                     at_ref, rt_ref, bt_ref, kt_ref, dec_ref):
    c = WKV_CHUNK
    n = HEAD_DIM
    rows = r_ref.shape[1]
    nchunk = rows // c

    row = lax.broadcasted_iota(jnp.int32, (rows, rows), 0)
    col = lax.broadcasted_iota(jnp.int32, (rows, rows), 1)
    tri = ((row >= col) & ((row // c) == (col // c))).astype(BF16)
    lw = lw_ref[0]
    lw_hi, lw_lo = _split_bf16(lw)
    cum = (jnp.dot(tri, lw_hi, preferred_element_type=F32)
           + jnp.dot(tri, lw_lo, preferred_element_type=F32))
    e_pos = jnp.exp(cum)
    e_neg = jnp.exp(-cum)
    at_ref[...] = -kk_ref[0] * jnp.exp(cum - lw)
    rt_ref[...] = r_ref[0] * e_pos
    bt_ref[...] = b_ref[0] * e_neg
    kt_ref[...] = k_ref[0] * e_neg
    dec_ref[...] = e_pos

    crow = lax.broadcasted_iota(jnp.int32, (c, c), 0)
    ccol = lax.broadcasted_iota(jnp.int32, (c, c), 1)
    lower = crow >= ccol
    strict = crow > ccol
    eye = (crow == ccol).astype(F32)

    pairs = [(j, h) for j in range(nchunk) for h in range(N_HEADS)]

    def tile(ref, j, h, bf16=True):
        x = ref[j * c:(j + 1) * c, h * n:(h + 1) * n]
        return x.astype(BF16) if bf16 else x

    ah = [tile(at_ref, j, h) for j, h in pairs]
    rh = [tile(rt_ref, j, h, bf16=False) for j, h in pairs]
    bh = [tile(bt_ref, j, h) for j, h in pairs]
    kh = [tile(kt_ref, j, h) for j, h in pairs]
    vh = [v_ref[0, j * c:(j + 1) * c, h * n:(h + 1) * n].astype(BF16) for j, h in pairs]
    dh = [dec_ref[(j + 1) * c - 1:(j + 1) * c, h * n:(h + 1) * n] for j, h in pairs]
    idx = range(len(pairs))
    ar = [jnp.concatenate([ah[i], rh[i].astype(BF16)], axis=0) for i in idx]
    bk = [jnp.concatenate([bh[i], kh[i]], axis=0) for i in idx]
    aa = [_mm_nt(ar[i], bk[i]) for i in idx]
    n_ab = [jnp.where(strict, aa[i][:c, :c], 0.0) for i in idx]
    a_kr = [jnp.concatenate([jnp.where(strict, aa[i][:c, c:], 0.0),
                             jnp.where(lower, aa[i][c:, c:], 0.0)], axis=0).astype(BF16)
            for i in idx]
    a_rb = [jnp.where(lower, aa[i][c:, :c], 0.0).astype(BF16) for i in idx]
    t = [eye + n_ab[i] for i in idx]
    pw = n_ab
    for _ in range(int(math.log2(c)) - 1):
        pw = [_mm(pw[i], pw[i]) for i in idx]
        t = [t[i] + _mm(pw[i], t[i]) for i in idx]
    av = [_mm(a_kr[i], vh[i]) for i in idx]
    wu = [_mm(t[i], jnp.concatenate([ah[i], av[i][:c].astype(BF16)], axis=1)).astype(BF16)
          for i in idx]
    aw = [_mm(a_rb[i], wu[i]) for i in idx]
    q = [rh[i] + aw[i][:, :n] for i in idx]
    y1 = [aw[i][:, n:] + av[i][c:] for i in idx]
    m = [(eye + _mm_tn(wu[i][:, :n], bh[i])) * dh[i] for i in idx]
    g = [_mm_tn(jnp.concatenate([wu[i][:, n:], vh[i]], axis=0), bk[i]) * dh[i] for i in idx]
    for j in range(nchunk):
        sel = slice(j * N_HEADS, (j + 1) * N_HEADS)
        q_out[0, j * c:(j + 1) * c, :] = jnp.concatenate(q[sel], axis=-1).astype(q_out.dtype)
        y1_out[0, j * c:(j + 1) * c, :] = jnp.concatenate(y1[sel], axis=-1)
        m_out[0, j * c:(j + 1) * c, :] = jnp.concatenate(m[sel], axis=-1).astype(m_out.dtype)
        g_out[0, j * c:(j + 1) * c, :] = jnp.concatenate(g[sel], axis=-1)


def _wkv_maps(r, lw, k, v, kk, b, rows=256):
    bsz, seq, w = r.shape
    tile = pl.BlockSpec((1, rows, w), lambda bi, ti: (bi, ti, 0))
    f32 = jax.ShapeDtypeStruct((bsz, seq, w), F32)
    bf16 = jax.ShapeDtypeStruct((bsz, seq, w), BF16)
    return pl.pallas_call(
        _wkv_maps_kernel,
        grid=(bsz, seq // rows),
        in_specs=[tile] * 6,
        out_specs=[tile] * 4,
        out_shape=[bf16, f32, bf16, f32],
        scratch_shapes=[pltpu.VMEM((rows, w), F32)] * 5,
        compiler_params=_params("parallel", "parallel"),
        name="wkv_chunk_maps",
    )(r, lw, k, v, kk, b)


def _wkv_scan_kernel(q_ref, y1_ref, m_ref, gm_ref, r_ref, k_ref, v_ref, g_ref,
                     rk_ref, lng_ref, lnb_ref, o_ref, st_ref):
    n = HEAD_DIM
    bsz = q_ref.shape[0]

    @pl.when(pl.program_id(0) == 0)
    def _():
        st_ref[...] = jnp.zeros(st_ref.shape, F32)

    for bi in range(bsz):
        ys = []
        for h in range(N_HEADS):
            sl = slice(h * n, (h + 1) * n)
            s0 = st_ref[bi * N_HEADS + h].astype(BF16)
            ys.append(_mm_nt(q_ref[bi, :, sl], s0) + y1_ref[bi, :, sl])
            st_ref[bi * N_HEADS + h] = _mm(s0, m_ref[bi, :, sl]) + gm_ref[bi, :, sl]
        y = jnp.concatenate(ys, axis=-1)
        mean = _head_sums(y) * (1.0 / n)
        yc = y - mean
        var = _head_sums(yc * yc) * (1.0 / n)
        yn = yc * lax.rsqrt(var + LN_X_EPS) * lng_ref[...] + lnb_ref[...]
        r = r_ref[bi]
        v = v_ref[bi]
        bonus = _head_sums(r * k_ref[bi] * rk_ref[...]) * v
        o_ref[bi] = ((yn + bonus) * g_ref[bi]).astype(o_ref.dtype)


def _wkv(r, lw, k, v, kk, b, g, r_k, ln_g, ln_b):
    bsz, seq, w = r.shape
    c = WKV_CHUNK
    q, y1, m, gm = _wkv_maps(r, lw, k, v, kk, b)
    tile = pl.BlockSpec((bsz, c, w), lambda ci: (0, ci, 0))
    vec = pl.BlockSpec((1, w), lambda ci: (0, 0))
    return pl.pallas_call(
        _wkv_scan_kernel,
        grid=(seq // c,),
        in_specs=[tile] * 8 + [vec] * 3,
        out_specs=tile,
        out_shape=jax.ShapeDtypeStruct((bsz, seq, w), BF16),
        scratch_shapes=[pltpu.VMEM((bsz * N_HEADS, HEAD_DIM, HEAD_DIM), F32)],
        compiler_params=_params("arbitrary"),
        name="wkv_state_scan",
    )(q, y1, m, gm, r, k, v, g, r_k.reshape(1, w), ln_g.reshape(1, w), ln_b.reshape(1, w))


def _wout_kernel(x_ref, y0_ref, y1_ref, y2_ref, y3_ref, w_ref, o_ref):
    acc = x_ref[...]
    for idx, y_ref in enumerate((y0_ref, y1_ref, y2_ref, y3_ref)):
        acc = acc + jnp.dot(y_ref[...], w_ref[idx * GROUP_WIDTH:(idx + 1) * GROUP_WIDTH, :],
                            preferred_element_type=F32)
    o_ref[...] = acc


def _wout(x2d, ys, w_bf16, tm=512):
    m, d = x2d.shape
    ytile = pl.BlockSpec((tm, GROUP_WIDTH), lambda i: (i, 0))
    xtile = pl.BlockSpec((tm, d), lambda i: (i, 0))
    return pl.pallas_call(
        _wout_kernel,
        grid=(m // tm,),
        in_specs=[xtile] + [ytile] * 4 + [pl.BlockSpec(w_bf16.shape, lambda i: (0, 0))],
        out_specs=xtile,
        out_shape=jax.ShapeDtypeStruct((m, d), F32),
        compiler_params=_params("parallel"),
        name="wout_residual",
    )(x2d, *ys, w_bf16)


def _ffn_kernel(x_ref, g_ref, wu_ref, wd_ref, o_ref, h_ref):
    @pl.when(pl.program_id(1) == 0)
    def _():
        x = x_ref[...]
        ms = jnp.mean(x * x, axis=-1, keepdims=True)
        h_ref[...] = (x * lax.rsqrt(ms + RMS_EPS) * g_ref[...]).astype(BF16)
        o_ref[...] = x

    u = jnp.dot(h_ref[...], wu_ref[...], preferred_element_type=F32)
    act = jnp.square(jnp.maximum(u, 0.0)).astype(BF16)
    o_ref[...] += jnp.dot(act, wd_ref[...], preferred_element_type=F32)


def _ffn(x2d, gain, wu_bf16, wd_bf16, tm=512, tf=512):
    m, d = x2d.shape
    f = wu_bf16.shape[1]
    xtile = pl.BlockSpec((tm, d), lambda i, j: (i, 0))
    return pl.pallas_call(
        _ffn_kernel,
        grid=(m // tm, f // tf),
        in_specs=[
            xtile,
            pl.BlockSpec((1, d), lambda i, j: (0, 0)),
            pl.BlockSpec((d, tf), lambda i, j: (0, j)),
            pl.BlockSpec((tf, d), lambda i, j: (j, 0)),
        ],
        out_specs=xtile,
        out_shape=jax.ShapeDtypeStruct((m, d), F32),
        scratch_shapes=[pltpu.VMEM((tm, d), BF16)],
        compiler_params=_params("parallel", "arbitrary"),
        name="ffn",
    )(x2d, gain.reshape(1, d), wu_bf16, wd_bf16)


def _permute_w_in(w):
    gw = GROUP_WIDTH
    conv = w[:, 0:3 * gw]
    swa_q = w[:, 3 * gw:4 * gw]
    swa_kv = w[:, 4 * gw:4 * gw + 2 * SWA_KV_HEADS * HEAD_DIM]
    dil0 = 4 * gw + 2 * SWA_KV_HEADS * HEAD_DIM
    dil = w[:, dil0:dil0 + 3 * gw]
    rw = w[:, dil0 + 3 * gw:]
    return jnp.concatenate([rw, swa_kv, swa_q, conv, dil], axis=1)


def kernel(x, norm_mix, w_in, conv_w, swa_q_norm, swa_k_norm, swa_sink, dil_q_norm, dil_k_norm,
           rwkv_mu, decay_w0, decay_w2, iclr_a0, iclr_a2, gate_g2, k_k, k_a, r_k, ln_x_g, ln_x_b,
           vres_v0, vres_v1, vres_v2, w_out, norm_ffn, w_up, w_down, rel_bias):
    bsz, seq, d = x.shape
    depth = w_in.shape[0]
    bias = _bias_tiles(rel_bias)
    swa_bias = bias[0, :N_HEADS]
    dil_bias = bias[:, N_HEADS:]
    x2d = x.reshape(bsz * seq, d)
    v_first = None
    for layer in range(depth):
        w_in_p = _permute_w_in(w_in[layer]).astype(BF16)
        proj = _norm_matmul(x2d, norm_mix[layer], w_in_p).reshape(bsz, seq, IN_WIDTH)
        y_conv = _conv_mixer(proj, conv_w[layer])
        y_swa = _swa_mixer(proj, swa_q_norm[layer], swa_k_norm[layer], swa_sink[layer], swa_bias)
        y_dil = _dil_mixer(proj, dil_q_norm[layer], dil_k_norm[layer], dil_bias)
        vres = None if layer == 0 else (vres_v0[layer - 1], vres_v1[layer - 1], vres_v2[layer - 1])
        r, lw, k, v, kk, b, g = _rwkv_prep(
            proj, rwkv_mu[layer], decay_w0[layer], decay_w2[layer], iclr_a0[layer], iclr_a2[layer],
            gate_g2[layer], k_k[layer], k_a[layer], v_first, vres)
        if layer == 0:
            v_first = v
        y_rwkv = _wkv(r, lw, k, v, kk, b, g, r_k[layer].reshape(-1), ln_x_g[layer], ln_x_b[layer])
        ys = [y.reshape(bsz * seq, GROUP_WIDTH) for y in (y_conv, y_swa, y_dil, y_rwkv)]
        x2d = _wout(x2d, ys, w_out[layer].astype(BF16))
        x2d = _ffn(x2d, norm_ffn[layer], w_up[layer].astype(BF16), w_down[layer].astype(BF16))
    return x2d.reshape(bsz, seq, d)
```

```python
import functools
import math

import jax
import jax.numpy as jnp
from jax import lax
from jax.experimental import pallas as pl
from jax.experimental.pallas import tpu as pltpu

F32 = jnp.float32
BF16 = jnp.bfloat16

D_MODEL = 2048
HEAD_DIM = 64
GROUP_WIDTH = 512
N_HEADS = GROUP_WIDTH // HEAD_DIM
SWA_KV_HEADS = 2
SWA_GROUP = N_HEADS // SWA_KV_HEADS
SWA_WINDOW = 128
DIL_PAIRS = ((128, 1), (512, 4), (2048, 16))
DECAY_LORA = 64
ICLR_LORA = 64
GATE_LORA = 128
RWKV_IN_WIDTH = 3 * GROUP_WIDTH + DECAY_LORA + ICLR_LORA + GATE_LORA
BLK = 128
NUM_BUCKETS = 32
BUCKET_MAX_DIST = 128
RMS_EPS = 1e-6
LN_X_EPS = 64e-5
NEG = -1e30
WKV_CHUNK = 64
DIL_TILES = 4

OFF_CONV = 0
OFF_SWA_Q = OFF_CONV + 3 * GROUP_WIDTH
OFF_SWA_K = OFF_SWA_Q + GROUP_WIDTH
OFF_SWA_V = OFF_SWA_K + SWA_KV_HEADS * HEAD_DIM
OFF_DIL = OFF_SWA_V + SWA_KV_HEADS * HEAD_DIM
OFF_RW = OFF_DIL + 3 * GROUP_WIDTH
IN_WIDTH = OFF_RW + RWKV_IN_WIDTH

V7X_VMEM_BYTES = 64 * 1024 * 1024
VMEM_LIMIT = 48 * 1024 * 1024
FFN_VMEM_LIMIT = V7X_VMEM_BYTES - 6 * 1024 * 1024


def _params(*sem, vmem_limit=VMEM_LIMIT):
    return pltpu.CompilerParams(dimension_semantics=sem, vmem_limit_bytes=vmem_limit)


def _mm(a, b):
    return jnp.dot(a.astype(BF16), b.astype(BF16), preferred_element_type=F32)


def _mm_nt(a, b):
    return lax.dot_general(a.astype(BF16), b.astype(BF16), (((1,), (1,)), ((), ())),
                           preferred_element_type=F32)


def _mm_tn(a, b):
    return lax.dot_general(a.astype(BF16), b.astype(BF16), (((0,), (0,)), ((), ())),
                           preferred_element_type=F32)


def _split_bf16(x):
    hi = x.astype(BF16)
    lo = (x - hi.astype(F32)).astype(BF16)
    return hi, lo


def _head_sums(x):
    lanes = 128
    r = lax.broadcasted_iota(jnp.int32, (lanes, lanes), 0) // HEAD_DIM
    c = lax.broadcasted_iota(jnp.int32, (lanes, lanes), 1) // HEAD_DIM
    bd = (r == c).astype(BF16)
    hi, lo = _split_bf16(x)
    cols = []
    for j in range(x.shape[-1] // lanes):
        sl = slice(j * lanes, (j + 1) * lanes)
        cols.append(jnp.dot(hi[:, sl], bd, preferred_element_type=F32)
                    + jnp.dot(lo[:, sl], bd, preferred_element_type=F32))
    return cols[0] if len(cols) == 1 else jnp.concatenate(cols, axis=-1)


def _sigmoid(z):
    return 1.0 / (1.0 + jnp.exp(-z))


def _norm_matmul_kernel(x_ref, g_ref, w_ref, o_ref, h_ref):
    @pl.when(pl.program_id(1) == 0)
    def _():
        x = x_ref[...]
        ms = jnp.mean(x * x, axis=-1, keepdims=True)
        h_ref[...] = (x * lax.rsqrt(ms + RMS_EPS) * g_ref[0]).astype(BF16)

    o_ref[...] = jnp.dot(h_ref[...], w_ref[0].astype(BF16), preferred_element_type=F32)


def _norm_matmul(x2d, gains, w_stack, layer, tm=1024, tn=512):
    m, k = x2d.shape
    n = w_stack.shape[2]
    return pl.pallas_call(
        _norm_matmul_kernel,
        grid=(m // tm, n // tn),
        in_specs=[
            pl.BlockSpec((tm, k), lambda i, j: (i, 0)),
            pl.BlockSpec((1, 1, k), lambda i, j: (layer, 0, 0)),
            pl.BlockSpec((1, k, tn), lambda i, j: (layer, 0, j)),
        ],
        out_specs=pl.BlockSpec((tm, tn), lambda i, j: (i, j)),
        out_shape=jax.ShapeDtypeStruct((m, n), F32),
        scratch_shapes=[pltpu.VMEM((tm, k), BF16)],
        compiler_params=_params("parallel", "arbitrary"),
        name="norm_matmul",
    )(x2d, gains.reshape(-1, 1, k), w_stack)


def _t5_bucket(dist):
    dist = jnp.maximum(dist, 0)
    max_exact = NUM_BUCKETS // 2
    scaled = (jnp.log(jnp.maximum(dist, 1).astype(F32) / max_exact)
              / math.log(BUCKET_MAX_DIST / max_exact))
    large = max_exact + (scaled * (NUM_BUCKETS - max_exact)).astype(jnp.int32)
    large = jnp.minimum(large, NUM_BUCKETS - 1)
    return jnp.where(dist < max_exact, dist, large)


def _bias_kernel(bucket_ref, table_ref, o_ref):
    h = pl.program_id(1)
    bucket = bucket_ref[0]
    acc = jnp.zeros(bucket.shape, F32)
    for b in range(NUM_BUCKETS):
        acc = jnp.where(bucket == b, table_ref[b, h], acc)
    o_ref[0, 0] = acc


def _bias_tiles(rel_bias):
    dist = BLK + jnp.arange(BLK)[:, None] - jnp.arange(2 * BLK)[None, :]
    buckets = jnp.stack([_t5_bucket(dist * r) for _, r in DIL_PAIRS]).astype(jnp.int32)
    nh = rel_bias.shape[1]
    return pl.pallas_call(
        _bias_kernel,
        grid=(len(DIL_PAIRS), nh),
        in_specs=[
            pl.BlockSpec((1, BLK, 2 * BLK), lambda s, h: (s, 0, 0)),
            pl.BlockSpec(memory_space=pltpu.SMEM),
        ],
        out_specs=pl.BlockSpec((1, 1, BLK, 2 * BLK), lambda s, h: (s, h, 0, 0)),
        out_shape=jax.ShapeDtypeStruct((len(DIL_PAIRS), nh, BLK, 2 * BLK), F32),
        compiler_params=_params("arbitrary", "arbitrary"),
        name="bias_tiles",
    )(buckets, rel_bias)


def _conv_kernel(b_ref, c_ref, u_ref, w_ref, o_ref):
    z = c_ref[0] * u_ref[0]
    row = lax.broadcasted_iota(jnp.int32, z.shape, 0)
    z1 = jnp.where(row >= 1, pltpu.roll(z, 1, axis=0), 0.0)
    z2 = jnp.where(row >= 2, pltpu.roll(z, 2, axis=0), 0.0)
    w = w_ref[...]
    y = z2 * w[0:1, :] + z1 * w[1:2, :] + z * w[2:3, :]
    o_ref[0] = (b_ref[0] * y).astype(o_ref.dtype)


def _conv_mixer(proj, conv_w):
    bsz, seq, _ = proj.shape
    lanes = 128
    nblk = GROUP_WIDTH // lanes
    base = OFF_CONV // lanes

    def col(seg):
        return pl.BlockSpec((1, seq, lanes), lambda b, j: (b, 0, base + seg * nblk + j))

    return pl.pallas_call(
        _conv_kernel,
        grid=(bsz, nblk),
        in_specs=[col(0), col(1), col(2), pl.BlockSpec((3, lanes), lambda b, j: (0, j))],
        out_specs=pl.BlockSpec((1, seq, lanes), lambda b, j: (b, 0, j)),
        out_shape=jax.ShapeDtypeStruct((bsz, seq, GROUP_WIDTH), BF16),
        compiler_params=_params("parallel", "parallel"),
        name="conv_mixer",
    )(proj, proj, proj, conv_w)


def _band_mask(max_dist):
    a = lax.broadcasted_iota(jnp.int32, (BLK, 2 * BLK), 0)
    b = lax.broadcasted_iota(jnp.int32, (BLK, 2 * BLK), 1)
    dist = BLK + a - b
    return (dist >= 0) & (dist <= max_dist), b


def _attend(qs, kws, vws, biases, sinks=None):
    idx = range(len(qs))
    s = [lax.dot_general(qs[i], kws[i], (((1,), (1,)), ((), ())), preferred_element_type=F32)
         + biases[i] for i in idx]
    m = [jnp.max(s[i], axis=-1, keepdims=True) for i in idx]
    if sinks is not None:
        m = [jnp.maximum(m[i], sinks[i]) for i in idx]
    p = [jnp.exp(s[i] - m[i]) for i in idx]
    den = [jnp.sum(p[i], axis=-1, keepdims=True) for i in idx]
    if sinks is not None:
        den = [den[i] + jnp.exp(sinks[i] - m[i]) for i in idx]
    o = [jnp.dot(p[i].astype(BF16), vws[i], preferred_element_type=F32) / den[i] for i in idx]
    return o, m, den


def _head_rms(x, gain):
    ms = _head_sums(x * x) * (1.0 / HEAD_DIM)
    return x * lax.rsqrt(ms + RMS_EPS) * gain


def _swa_kernel(q_ref, k_ref, v_ref, qg_ref, kg_ref, sink_ref, bias_ref, o_ref,
                qn_ref, kn_ref, vb_ref, bm_ref):
    seq = q_ref.shape[1]
    nb = seq // BLK
    kvw = SWA_KV_HEADS * HEAD_DIM
    scale = HEAD_DIM ** -0.5

    kn_ref[0:BLK, :] = jnp.zeros((BLK, 2 * kvw), BF16)
    vb_ref[0:BLK, :] = jnp.zeros((BLK, 2 * kvw), BF16)
    lane_half = lax.broadcasted_iota(jnp.int32, (1, kvw), 1) // HEAD_DIM

    def both_halves(x):
        xr = pltpu.roll(x, HEAD_DIM, axis=1)
        return jnp.concatenate([jnp.where(lane_half == 0, x, xr),
                                jnp.where(lane_half == 0, xr, x)], axis=-1)

    def prep(i, carry):
        r0 = pl.multiple_of(i * BLK, BLK)
        q = q_ref[0, pl.ds(r0, BLK), :]
        qn_ref[pl.ds(r0, BLK), :] = (_head_rms(q, qg_ref[...]) * scale).astype(BF16)
        k = _head_rms(k_ref[0, pl.ds(r0, BLK), :], kg_ref[...])
        kn_ref[pl.ds(r0 + BLK, BLK), :] = both_halves(k).astype(BF16)
        vb_ref[pl.ds(r0 + BLK, BLK), :] = both_halves(v_ref[0, pl.ds(r0, BLK), :]).astype(BF16)
        return carry

    lax.fori_loop(0, nb, prep, 0)

    band, kcol = _band_mask(SWA_WINDOW - 1)
    for h in range(N_HEADS):
        bm_ref[h] = jnp.where(band & (kcol >= BLK), bias_ref[h], NEG)
        bm_ref[N_HEADS + h] = jnp.where(band, bias_ref[h], NEG)
    sinks = [sink_ref[h] for h in range(N_HEADS)]

    def block(i, carry):
        r0 = pl.multiple_of(i * BLK, BLK)
        later = jnp.minimum(i, 1) * N_HEADS
        qs, kws, vws, bms = [], [], [], []
        for hk in range(SWA_KV_HEADS):
            kw = kn_ref[pl.ds(r0, 2 * BLK), hk * kvw:(hk + 1) * kvw]
            vw = vb_ref[pl.ds(r0, 2 * BLK), hk * kvw:(hk + 1) * kvw]
            for g in range(SWA_GROUP):
                h = hk * SWA_GROUP + g
                q2 = qn_ref[pl.ds(r0, BLK), (h // 2) * kvw:(h // 2 + 1) * kvw]
                qs.append(jnp.where(lane_half == h % 2, q2, jnp.zeros_like(q2)))
                kws.append(kw)
                vws.append(vw)
                bms.append(bm_ref[later + h])
        outs, _, _ = _attend(qs, kws, vws, bms, sinks)
        pairs = [jnp.where(lane_half == 0, outs[h], outs[h + 1]) for h in range(0, N_HEADS, 2)]
        o_ref[0, pl.ds(r0, BLK), :] = jnp.concatenate(pairs, axis=-1).astype(o_ref.dtype)
        return carry

    lax.fori_loop(0, nb, block, 0)


def _swa_mixer(proj, q_gain, k_gain, sink, bias):
    bsz, seq, _ = proj.shape
    kvw = SWA_KV_HEADS * HEAD_DIM
    q_gain_t = jnp.tile(q_gain, N_HEADS).reshape(1, GROUP_WIDTH)
    k_gain_t = jnp.tile(k_gain, SWA_KV_HEADS).reshape(1, kvw)
    return pl.pallas_call(
        _swa_kernel,
        grid=(bsz,),
        in_specs=[
            pl.BlockSpec((1, seq, GROUP_WIDTH), lambda b: (b, 0, OFF_SWA_Q // GROUP_WIDTH)),
            pl.BlockSpec((1, seq, kvw), lambda b: (b, 0, OFF_SWA_K // kvw)),
            pl.BlockSpec((1, seq, kvw), lambda b: (b, 0, OFF_SWA_V // kvw)),
            pl.BlockSpec((1, GROUP_WIDTH), lambda b: (0, 0)),
            pl.BlockSpec((1, kvw), lambda b: (0, 0)),
            pl.BlockSpec(memory_space=pltpu.SMEM),
            pl.BlockSpec((N_HEADS, BLK, 2 * BLK), lambda b: (0, 0, 0)),
        ],
        out_specs=pl.BlockSpec((1, seq, GROUP_WIDTH), lambda b: (b, 0, 0)),
        out_shape=jax.ShapeDtypeStruct((bsz, seq, GROUP_WIDTH), BF16),
        scratch_shapes=[
            pltpu.VMEM((seq, GROUP_WIDTH), BF16),
            pltpu.VMEM((seq + BLK, 2 * kvw), BF16),
            pltpu.VMEM((seq + BLK, 2 * kvw), BF16),
            pltpu.VMEM((2 * N_HEADS, BLK, 2 * BLK), F32),
        ],
        compiler_params=_params("parallel"),
        name="swa_mixer",
    )(proj, proj, proj, q_gain_t, k_gain_t, sink, bias)


def _dil_kernel(q_ref, k_ref, v_ref, qg_ref, kg_ref, bias_ref, o_ref,
                qn_ref, kn_ref, ob_ref, lb_ref, bm_ref):
    seq = q_ref.shape[1]
    lanes = q_ref.shape[2]
    heads = lanes // HEAD_DIM
    scale = HEAD_DIM ** -0.5

    def prep(i, carry):
        r0 = pl.multiple_of(i * BLK, BLK)
        qn_ref[pl.ds(r0, BLK), :] = _head_rms(q_ref[0, pl.ds(r0, BLK), :], qg_ref[...]) * scale
        kn_ref[pl.ds(r0, BLK), :] = _head_rms(k_ref[0, pl.ds(r0, BLK), :], kg_ref[...])
        return carry

    lax.fori_loop(0, seq // BLK, prep, 0)

    for br, (window, r) in enumerate(DIL_PAIRS):
        band, kcol = _band_mask(window // r)
        for h in range(heads):
            bm_ref[(br * heads + h) * 2] = jnp.where(band & (kcol >= BLK), bias_ref[br, h], NEG)
            bm_ref[(br * heads + h) * 2 + 1] = jnp.where(band, bias_ref[br, h], NEG)

    lane_head = lax.broadcasted_iota(jnp.int32, (1, lanes), 1) // HEAD_DIM

    for br, (window, r) in enumerate(DIL_PAIRS):
        nb = seq // r // BLK

        def blocks(it, carry, br=br, r=r, nb=nb):
            qs, kws, vws, bms, curs = [], [], [], [], []
            for u in range(DIL_TILES):
                t = it * DIL_TILES + u
                c = t // nb
                i = t - c * nb
                cur = c + i * (BLK * r)
                prev = jnp.maximum(cur - BLK * r, c)
                later = jnp.minimum(i, 1)

                def rows(ref, start):
                    if r == 1:
                        return ref[pl.ds(start, BLK), :]
                    return ref[pl.ds(start, BLK, stride=r), :]

                q = rows(qn_ref, cur).astype(BF16)
                kw = jnp.concatenate([rows(kn_ref, prev), rows(kn_ref, cur)], axis=0).astype(BF16)
                vw = jnp.concatenate([rows(v_ref.at[0], prev), rows(v_ref.at[0], cur)],
                                     axis=0).astype(BF16)
                curs.append(cur)
                for h in range(heads):
                    qs.append(jnp.where(lane_head == h, q, jnp.zeros_like(q)))
                    kws.append(kw)
                    vws.append(vw)
                    bms.append(bm_ref[(br * heads + h) * 2 + later])
            outs, ms, dens = _attend(qs, kws, vws, bms)
            for u in range(DIL_TILES):
                o_all = outs[u * heads]
                l_all = ms[u * heads] + jnp.log(dens[u * heads])
                for h in range(1, heads):
                    o_all = jnp.where(lane_head == h, outs[u * heads + h], o_all)
                    l_all = jnp.where(lane_head == h,
                                      ms[u * heads + h] + jnp.log(dens[u * heads + h]), l_all)
                if r == 1:
                    ob_ref[br, pl.ds(curs[u], BLK), :] = o_all
                    lb_ref[br, pl.ds(curs[u], BLK), :] = l_all
                else:
                    ob_ref[br, pl.ds(curs[u], BLK, stride=r), :] = o_all
                    lb_ref[br, pl.ds(curs[u], BLK, stride=r), :] = l_all
            return carry

        lax.fori_loop(0, r * nb // DIL_TILES, blocks, 0)

    def combine(i, carry):
        r0 = pl.multiple_of(i * BLK, BLK)
        l0 = lb_ref[0, pl.ds(r0, BLK), :]
        l1 = lb_ref[1, pl.ds(r0, BLK), :]
        l2 = lb_ref[2, pl.ds(r0, BLK), :]
        m = jnp.maximum(jnp.maximum(l0, l1), l2)
        e0, e1, e2 = jnp.exp(l0 - m), jnp.exp(l1 - m), jnp.exp(l2 - m)
        tot = e0 + e1 + e2
        o = ((e0 / tot) * ob_ref[0, pl.ds(r0, BLK), :]
             + (e1 / tot) * ob_ref[1, pl.ds(r0, BLK), :]
             + (e2 / tot) * ob_ref[2, pl.ds(r0, BLK), :])
        o_ref[0, pl.ds(r0, BLK), :] = o.astype(o_ref.dtype)
        return carry

    lax.fori_loop(0, seq // BLK, combine, 0)


def _dil_mixer(proj, q_gain, k_gain, bias):
    bsz, seq, _ = proj.shape
    lanes = 128
    heads = lanes // HEAD_DIM
    nblk = GROUP_WIDTH // lanes
    base = OFF_DIL // lanes
    gq = jnp.tile(q_gain, heads).reshape(1, lanes)
    gk = jnp.tile(k_gain, heads).reshape(1, lanes)

    def col(seg):
        return pl.BlockSpec((1, seq, lanes), lambda b, j: (b, 0, base + seg * nblk + j))

    return pl.pallas_call(
        _dil_kernel,
        grid=(bsz, nblk),
        in_specs=[
            col(0), col(1), col(2),
            pl.BlockSpec((1, lanes), lambda b, j: (0, 0)),
            pl.BlockSpec((1, lanes), lambda b, j: (0, 0)),
            pl.BlockSpec((len(DIL_PAIRS), heads, BLK, 2 * BLK), lambda b, j: (0, j, 0, 0)),
        ],
        out_specs=pl.BlockSpec((1, seq, lanes), lambda b, j: (b, 0, j)),
        out_shape=jax.ShapeDtypeStruct((bsz, seq, GROUP_WIDTH), BF16),
        scratch_shapes=[
            pltpu.VMEM((seq, lanes), F32),
            pltpu.VMEM((seq, lanes), F32),
            pltpu.VMEM((len(DIL_PAIRS), seq, lanes), F32),
            pltpu.VMEM((len(DIL_PAIRS), seq, lanes), F32),
            pltpu.VMEM((len(DIL_PAIRS) * heads * 2, BLK, 2 * BLK), F32),
        ],
        compiler_params=_params("parallel", "parallel"),
        name="dil_mixer",
    )(proj, proj, proj, gq, gk, bias)


def _rwkv_prep_kernel(*refs, has_vres):
    if has_vres:
        (p_ref, pp_ref, mu_ref, w0_ref, w2_ref, a0_ref, a2_ref, g2_ref, kk_ref, ka_ref,
         vf_ref, v0_ref, v1_ref, v2_ref,
         r_out, lw_out, k_out, v_out, kk_out, b_out, g_out) = refs
    else:
        (p_ref, pp_ref, mu_ref, w0_ref, w2_ref, a0_ref, a2_ref, g2_ref, kk_ref, ka_ref,
         r_out, lw_out, k_out, v_out, kk_out, b_out, g_out) = refs
    w = GROUP_WIDTH
    p = p_ref[0]
    row = lax.broadcasted_iota(jnp.int32, p.shape, 0)
    last_prev = pp_ref[0, 7:8, :]
    last_prev = jnp.where(pl.program_id(1) > 0, last_prev, 0.0)
    prev = jnp.where(row >= 1, pltpu.roll(p, 1, axis=0), last_prev)
    xs = p + (prev - p) * mu_ref[...]
    r = xs[:, 0:w]
    k = xs[:, w:2 * w]
    v = xs[:, 2 * w:3 * w]
    o = 3 * w
    wd = xs[:, o:o + DECAY_LORA]
    ad = xs[:, o + DECAY_LORA:o + DECAY_LORA + ICLR_LORA]
    gd = xs[:, o + DECAY_LORA + ICLR_LORA:]

    z = -(w0_ref[...] + _mm(jnp.tanh(wd), w2_ref[...]))
    softplus = jnp.maximum(z, 0.0) + jnp.log(1.0 + jnp.exp(-jnp.abs(z)))
    logw = -softplus - 0.5
    lw_out[0] = -jnp.exp(logw)
    a = _sigmoid(a0_ref[...] + _mm(ad, a2_ref[...]))
    g_out[0] = _mm(_sigmoid(gd), g2_ref[...])
    if has_vres:
        mix = _sigmoid(v0_ref[...] + _mm(_mm(v, v1_ref[...]), v2_ref[...]))
        v = v + (vf_ref[0] - v) * mix
    kk = k * kk_ref[...]
    ss = _head_sums(kk * kk)
    kk = kk * lax.rsqrt(jnp.maximum(ss, 1e-24))
    r_out[0] = r
    k_out[0] = k * (1.0 + (a - 1.0) * ka_ref[...])
    v_out[0] = v
    kk_out[0] = kk
    b_out[0] = kk * a


def _rwkv_prep(proj, mu, w0, w2, a0, a2, g2, k_k, k_a, v_first, vres, ts=256):
    bsz, seq, _ = proj.shape
    w = GROUP_WIDTH
    nt = seq // ts
    has_vres = vres is not None

    def full(shape):
        return pl.BlockSpec(shape, lambda b, i: (0,) * len(shape))

    row = lambda a: a.reshape(1, -1)
    tile = pl.BlockSpec((1, ts, w), lambda b, i: (b, i, 0))
    in_specs = [
        pl.BlockSpec((pl.Element(1), pl.Element(ts), pl.Element(RWKV_IN_WIDTH)),
                     lambda b, i: (b, i * ts, OFF_RW)),
        pl.BlockSpec((pl.Element(1), pl.Element(8), pl.Element(RWKV_IN_WIDTH)),
                     lambda b, i: (b, jnp.maximum(i * (ts // 8) - 1, 0) * 8, OFF_RW)),
        full((1, RWKV_IN_WIDTH)), full((1, w)), full((DECAY_LORA, w)), full((1, w)),
        full((ICLR_LORA, w)), full((GATE_LORA, w)), full((1, w)), full((1, w)),
    ]
    args = [proj, proj, row(mu), row(w0), w2, row(a0), a2, g2, row(k_k), row(k_a)]
    if has_vres:
        v0, v1, v2 = vres
        in_specs += [tile, full((1, w)), full(v1.shape), full(v2.shape)]
        args += [v_first, row(v0), v1, v2]
    out = jax.ShapeDtypeStruct((bsz, seq, w), F32)
    return pl.pallas_call(
        functools.partial(_rwkv_prep_kernel, has_vres=has_vres),
        grid=(bsz, nt),
        in_specs=in_specs,
        out_specs=[tile] * 7,
        out_shape=[out] * 7,
        compiler_params=_params("parallel", "parallel"),
        name="rwkv_prep",
    )(*args)


def _wkv_maps_kernel(r_ref, lw_ref, k_ref, v_ref, kk_ref, b_ref,
                     q_out, y1_out, m_out, g_out,
                     at_ref, rt_ref, bt_ref, kt_ref, dec_ref):
    c = WKV_CHUNK
    n = HEAD_DIM
    rows = r_ref.shape[1]
    nchunk = rows // c

    row = lax.broadcasted_iota(jnp.int32, (rows, rows), 0)
    col = lax.broadcasted_iota(jnp.int32, (rows, rows), 1)
    tri = ((row >= col) & ((row // c) == (col // c))).astype(BF16)
    lw = lw_ref[0]
    lw_hi, lw_lo = _split_bf16(lw)
    cum = (jnp.dot(tri, lw_hi, preferred_element_type=F32)
           + jnp.dot(tri, lw_lo, preferred_element_type=F32))
    e_pos = jnp.exp(cum)
    e_neg = jnp.exp(-cum)
    at_ref[...] = -kk_ref[0] * jnp.exp(cum - lw)
    rt_ref[...] = r_ref[0] * e_pos
    bt_ref[...] = b_ref[0] * e_neg
    kt_ref[...] = k_ref[0] * e_neg
    dec_ref[...] = e_pos

    crow = lax.broadcasted_iota(jnp.int32, (c, c), 0)
    ccol = lax.broadcasted_iota(jnp.int32, (c, c), 1)
    lower = crow >= ccol
    strict = crow > ccol
    eye = (crow == ccol).astype(F32)

    pairs = [(j, h) for j in range(nchunk) for h in range(N_HEADS)]

    def tile(ref, j, h, bf16=True):
        x = ref[j * c:(j + 1) * c, h * n:(h + 1) * n]
        return x.astype(BF16) if bf16 else x

    ah = [tile(at_ref, j, h) for j, h in pairs]
    rh = [tile(rt_ref, j, h, bf16=False) for j, h in pairs]
    bh = [tile(bt_ref, j, h) for j, h in pairs]
    kh = [tile(kt_ref, j, h) for j, h in pairs]
    vh = [v_ref[0, j * c:(j + 1) * c, h * n:(h + 1) * n].astype(BF16) for j, h in pairs]
    dh = [dec_ref[(j + 1) * c - 1:(j + 1) * c, h * n:(h + 1) * n] for j, h in pairs]
    idx = range(len(pairs))
    ar = [jnp.concatenate([ah[i], rh[i].astype(BF16)], axis=0) for i in idx]
    bk = [jnp.concatenate([bh[i], kh[i]], axis=0) for i in idx]
    aa = [_mm_nt(ar[i], bk[i]) for i in idx]
    n_ab = [jnp.where(strict, aa[i][:c, :c], 0.0) for i in idx]
    a_kr = [jnp.concatenate([jnp.where(strict, aa[i][:c, c:], 0.0),
                             jnp.where(lower, aa[i][c:, c:], 0.0)], axis=0).astype(BF16)
            for i in idx]
    a_rb = [jnp.where(lower, aa[i][c:, :c], 0.0).astype(BF16) for i in idx]
    t = [eye + n_ab[i] for i in idx]
    pw = n_ab
    for _ in range(int(math.log2(c)) - 1):
        pw = [_mm(pw[i], pw[i]) for i in idx]
        t = [t[i] + _mm(pw[i], t[i]) for i in idx]
    av = [_mm(a_kr[i], vh[i]) for i in idx]
    wu = [_mm(t[i], jnp.concatenate([ah[i], av[i][:c].astype(BF16)], axis=1)).astype(BF16)
          for i in idx]
    aw = [_mm(a_rb[i], wu[i]) for i in idx]
    q = [rh[i] + aw[i][:, :n] for i in idx]
    y1 = [aw[i][:, n:] + av[i][c:] for i in idx]
    m = [(eye + _mm_tn(wu[i][:, :n], bh[i])) * dh[i] for i in idx]
    g = [_mm_tn(jnp.concatenate([wu[i][:, n:], vh[i]], axis=0), bk[i]) * dh[i] for i in idx]
    for j in range(nchunk):
        sel = slice(j * N_HEADS, (j + 1) * N_HEADS)
        q_out[0, j * c:(j + 1) * c, :] = jnp.concatenate(q[sel], axis=-1).astype(q_out.dtype)
        y1_out[0, j * c:(j + 1) * c, :] = jnp.concatenate(y1[sel], axis=-1)
        m_out[0, j * c:(j + 1) * c, :] = jnp.concatenate(m[sel], axis=-1).astype(m_out.dtype)
        g_out[0, j * c:(j + 1) * c, :] = jnp.concatenate(g[sel], axis=-1)


def _wkv_maps(r, lw, k, v, kk, b, rows=256):
    bsz, seq, w = r.shape
    tile = pl.BlockSpec((1, rows, w), lambda bi, ti: (bi, ti, 0))
    f32 = jax.ShapeDtypeStruct((bsz, seq, w), F32)
    bf16 = jax.ShapeDtypeStruct((bsz, seq, w), BF16)
    return pl.pallas_call(
        _wkv_maps_kernel,
        grid=(bsz, seq // rows),
        in_specs=[tile] * 6,
        out_specs=[tile] * 4,
        out_shape=[bf16, f32, bf16, f32],
        scratch_shapes=[pltpu.VMEM((rows, w), F32)] * 5,
        compiler_params=_params("parallel", "parallel"),
        name="wkv_chunk_maps",
    )(r, lw, k, v, kk, b)


def _wkv_scan_kernel(q_ref, y1_ref, m_ref, gm_ref, r_ref, k_ref, v_ref, g_ref,
                     rk_ref, lng_ref, lnb_ref, o_ref, st_ref):
    n = HEAD_DIM
    bsz = q_ref.shape[0]

    @pl.when(pl.program_id(0) == 0)
    def _():
        st_ref[...] = jnp.zeros(st_ref.shape, F32)

    for bi in range(bsz):
        ys = []
        for h in range(N_HEADS):
            sl = slice(h * n, (h + 1) * n)
            s0 = st_ref[bi * N_HEADS + h].astype(BF16)
            ys.append(_mm_nt(q_ref[bi, :, sl], s0) + y1_ref[bi, :, sl])
            st_ref[bi * N_HEADS + h] = _mm(s0, m_ref[bi, :, sl]) + gm_ref[bi, :, sl]
        y = jnp.concatenate(ys, axis=-1)
        mean = _head_sums(y) * (1.0 / n)
        yc = y - mean
        var = _head_sums(yc * yc) * (1.0 / n)
        yn = yc * lax.rsqrt(var + LN_X_EPS) * lng_ref[...] + lnb_ref[...]
        r = r_ref[bi]
        v = v_ref[bi]
        bonus = _head_sums(r * k_ref[bi] * rk_ref[...]) * v
        o_ref[bi] = ((yn + bonus) * g_ref[bi]).astype(o_ref.dtype)


def _wkv(r, lw, k, v, kk, b, g, r_k, ln_g, ln_b):
    bsz, seq, w = r.shape
    c = WKV_CHUNK
    q, y1, m, gm = _wkv_maps(r, lw, k, v, kk, b)
    tile = pl.BlockSpec((bsz, c, w), lambda ci: (0, ci, 0))
    vec = pl.BlockSpec((1, w), lambda ci: (0, 0))
    return pl.pallas_call(
        _wkv_scan_kernel,
        grid=(seq // c,),
        in_specs=[tile] * 8 + [vec] * 3,
        out_specs=tile,
        out_shape=jax.ShapeDtypeStruct((bsz, seq, w), BF16),
        scratch_shapes=[pltpu.VMEM((bsz * N_HEADS, HEAD_DIM, HEAD_DIM), F32)],
        compiler_params=_params("arbitrary"),
        name="wkv_state_scan",
    )(q, y1, m, gm, r, k, v, g, r_k.reshape(1, w), ln_g.reshape(1, w), ln_b.reshape(1, w))


def _wout_kernel(x_ref, y0_ref, y1_ref, y2_ref, y3_ref, w_ref, o_ref):
    acc = x_ref[...]
    for idx, y_ref in enumerate((y0_ref, y1_ref, y2_ref, y3_ref)):
        acc = acc + jnp.dot(y_ref[...], w_ref[idx * GROUP_WIDTH:(idx + 1) * GROUP_WIDTH, :],
                            preferred_element_type=F32)
    o_ref[...] = acc


def _wout(x2d, ys, w_bf16, tm=512):
    m, d = x2d.shape
    ytile = pl.BlockSpec((tm, GROUP_WIDTH), lambda i: (i, 0))
    xtile = pl.BlockSpec((tm, d), lambda i: (i, 0))
    return pl.pallas_call(
        _wout_kernel,
        grid=(m // tm,),
        in_specs=[xtile] + [ytile] * 4 + [pl.BlockSpec(w_bf16.shape, lambda i: (0, 0))],
        out_specs=xtile,
        out_shape=jax.ShapeDtypeStruct((m, d), F32),
        compiler_params=_params("parallel"),
        name="wout_residual",
    )(x2d, *ys, w_bf16)


def _ffn_kernel(x_ref, g_ref, wu_ref, wd_ref, o_ref, h_ref):
    @pl.when(pl.program_id(1) == 0)
    def _():
        x = x_ref[...]
        ms = jnp.mean(x * x, axis=-1, keepdims=True)
        h_ref[...] = (x * lax.rsqrt(ms + RMS_EPS) * g_ref[0]).astype(BF16)
        o_ref[...] = x

    u = jnp.dot(h_ref[...], wu_ref[0].astype(BF16), preferred_element_type=F32)
    act = jnp.square(jnp.maximum(u, 0.0)).astype(BF16)
    o_ref[...] += jnp.dot(act, wd_ref[0].astype(BF16), preferred_element_type=F32)


def _ffn(x2d, gains, wu_stack, wd_stack, layer, tm=1024, tf=512):
    m, d = x2d.shape
    f = wu_stack.shape[2]
    xtile = pl.BlockSpec((tm, d), lambda i, j: (i, 0))
    return pl.pallas_call(
        _ffn_kernel,
        grid=(m // tm, f // tf),
        in_specs=[
            xtile,
            pl.BlockSpec((1, 1, d), lambda i, j: (layer, 0, 0)),
            pl.BlockSpec((1, d, tf), lambda i, j: (layer, 0, j)),
            pl.BlockSpec((1, tf, d), lambda i, j: (layer, j, 0)),
        ],
        out_specs=xtile,
        out_shape=jax.ShapeDtypeStruct((m, d), F32),
        scratch_shapes=[pltpu.VMEM((tm, d), BF16)],
        compiler_params=_params("parallel", "arbitrary", vmem_limit=FFN_VMEM_LIMIT),
        name="ffn",
    )(x2d, gains.reshape(-1, 1, d), wu_stack, wd_stack)


def kernel(x, norm_mix, w_in, conv_w, swa_q_norm, swa_k_norm, swa_sink, dil_q_norm, dil_k_norm,
           rwkv_mu, decay_w0, decay_w2, iclr_a0, iclr_a2, gate_g2, k_k, k_a, r_k, ln_x_g, ln_x_b,
           vres_v0, vres_v1, vres_v2, w_out, norm_ffn, w_up, w_down, rel_bias):
    bsz, seq, d = x.shape
    depth = w_in.shape[0]
    bias = _bias_tiles(rel_bias)
    swa_bias = bias[0, :N_HEADS]
    dil_bias = bias[:, N_HEADS:]
    x2d = x.reshape(bsz * seq, d)
    v_first = None
    for layer in range(depth):
        proj = _norm_matmul(x2d, norm_mix, w_in, layer).reshape(bsz, seq, IN_WIDTH)
        y_conv = _conv_mixer(proj, conv_w[layer])
        y_swa = _swa_mixer(proj, swa_q_norm[layer], swa_k_norm[layer], swa_sink[layer], swa_bias)
        y_dil = _dil_mixer(proj, dil_q_norm[layer], dil_k_norm[layer], dil_bias)
        vres = None if layer == 0 else (vres_v0[layer - 1], vres_v1[layer - 1], vres_v2[layer - 1])
        r, lw, k, v, kk, b, g = _rwkv_prep(
            proj, rwkv_mu[layer], decay_w0[layer], decay_w2[layer], iclr_a0[layer], iclr_a2[layer],
            gate_g2[layer], k_k[layer], k_a[layer], v_first, vres)
        if layer == 0:
            v_first = v
        y_rwkv = _wkv(r, lw, k, v, kk, b, g, r_k[layer].reshape(-1), ln_x_g[layer], ln_x_b[layer])
        ys = [y.reshape(bsz * seq, GROUP_WIDTH) for y in (y_conv, y_swa, y_dil, y_rwkv)]
        x2d = _wout(x2d, ys, w_out[layer].astype(BF16))
        x2d = _ffn(x2d, norm_ffn, w_up, w_down, layer)
    return x2d.reshape(bsz, seq, d)
```

```python
import functools
import math

import jax
import jax.numpy as jnp
from jax import lax
from jax.experimental import pallas as pl
from jax.experimental.pallas import tpu as pltpu

F32 = jnp.float32
BF16 = jnp.bfloat16

D_MODEL = 2048
HEAD_DIM = 64
GROUP_WIDTH = 512
N_HEADS = GROUP_WIDTH // HEAD_DIM
SWA_KV_HEADS = 2
SWA_GROUP = N_HEADS // SWA_KV_HEADS
SWA_WINDOW = 128
DIL_PAIRS = ((128, 1), (512, 4), (2048, 16))
DECAY_LORA = 64
ICLR_LORA = 64
GATE_LORA = 128
RWKV_IN_WIDTH = 3 * GROUP_WIDTH + DECAY_LORA + ICLR_LORA + GATE_LORA
BLK = 128
NUM_BUCKETS = 32
BUCKET_MAX_DIST = 128
RMS_EPS = 1e-6
LN_X_EPS = 64e-5
NEG = -1e30
WKV_CHUNK = 64
DIL_TILES = 4

OFF_CONV = 0
OFF_SWA_Q = OFF_CONV + 3 * GROUP_WIDTH
OFF_SWA_K = OFF_SWA_Q + GROUP_WIDTH
OFF_SWA_V = OFF_SWA_K + SWA_KV_HEADS * HEAD_DIM
OFF_DIL = OFF_SWA_V + SWA_KV_HEADS * HEAD_DIM
OFF_RW = OFF_DIL + 3 * GROUP_WIDTH
IN_WIDTH = OFF_RW + RWKV_IN_WIDTH

V7X_VMEM_BYTES = 64 * 1024 * 1024
VMEM_LIMIT = 48 * 1024 * 1024
FFN_VMEM_LIMIT = V7X_VMEM_BYTES - 6 * 1024 * 1024


def _params(*sem, vmem_limit=VMEM_LIMIT):
    return pltpu.CompilerParams(dimension_semantics=sem, vmem_limit_bytes=vmem_limit)


def _mm(a, b):
    return jnp.dot(a.astype(BF16), b.astype(BF16), preferred_element_type=F32)


def _mm_nt(a, b):
    return lax.dot_general(a.astype(BF16), b.astype(BF16), (((1,), (1,)), ((), ())),
                           preferred_element_type=F32)


def _mm_tn(a, b):
    return lax.dot_general(a.astype(BF16), b.astype(BF16), (((0,), (0,)), ((), ())),
                           preferred_element_type=F32)


def _split_bf16(x):
    hi = x.astype(BF16)
    lo = (x - hi.astype(F32)).astype(BF16)
    return hi, lo


def _head_sums(x):
    lanes = 128
    r = lax.broadcasted_iota(jnp.int32, (lanes, lanes), 0) // HEAD_DIM
    c = lax.broadcasted_iota(jnp.int32, (lanes, lanes), 1) // HEAD_DIM
    bd = (r == c).astype(BF16)
    hi, lo = _split_bf16(x)
    cols = []
    for j in range(x.shape[-1] // lanes):
        sl = slice(j * lanes, (j + 1) * lanes)
        cols.append(jnp.dot(hi[:, sl], bd, preferred_element_type=F32)
                    + jnp.dot(lo[:, sl], bd, preferred_element_type=F32))
    return cols[0] if len(cols) == 1 else jnp.concatenate(cols, axis=-1)


def _sigmoid(z):
    return 1.0 / (1.0 + jnp.exp(-z))


def _norm_matmul_kernel(x_ref, g_ref, w_ref, o_ref, h_ref):
    @pl.when(pl.program_id(1) == 0)
    def _():
        x = x_ref[...]
        ms = jnp.mean(x * x, axis=-1, keepdims=True)
        h_ref[...] = (x * lax.rsqrt(ms + RMS_EPS) * g_ref[0]).astype(BF16)

    o_ref[...] = jnp.dot(h_ref[...], w_ref[0].astype(BF16), preferred_element_type=F32)


def _norm_matmul(x2d, gains, w_stack, layer, tm=1024, tn=512):
    m, k = x2d.shape
    n = w_stack.shape[2]
    return pl.pallas_call(
        _norm_matmul_kernel,
        grid=(m // tm, n // tn),
        in_specs=[
            pl.BlockSpec((tm, k), lambda i, j: (i, 0)),
            pl.BlockSpec((1, 1, k), lambda i, j: (layer, 0, 0)),
            pl.BlockSpec((1, k, tn), lambda i, j: (layer, 0, j)),
        ],
        out_specs=pl.BlockSpec((tm, tn), lambda i, j: (i, j)),
        out_shape=jax.ShapeDtypeStruct((m, n), F32),
        scratch_shapes=[pltpu.VMEM((tm, k), BF16)],
        compiler_params=_params("parallel", "arbitrary"),
        name="norm_matmul",
    )(x2d, gains.reshape(-1, 1, k), w_stack)


def _t5_bucket(dist):
    dist = jnp.maximum(dist, 0)
    max_exact = NUM_BUCKETS // 2
    scaled = (jnp.log(jnp.maximum(dist, 1).astype(F32) / max_exact)
              / math.log(BUCKET_MAX_DIST / max_exact))
    large = max_exact + (scaled * (NUM_BUCKETS - max_exact)).astype(jnp.int32)
    large = jnp.minimum(large, NUM_BUCKETS - 1)
    return jnp.where(dist < max_exact, dist, large)


def _bias_kernel(bucket_ref, table_ref, o_ref):
    h = pl.program_id(1)
    bucket = bucket_ref[0]
    acc = jnp.zeros(bucket.shape, F32)
    for b in range(NUM_BUCKETS):
        acc = jnp.where(bucket == b, table_ref[b, h], acc)
    o_ref[0, 0] = acc


def _bias_tiles(rel_bias):
    dist = BLK + jnp.arange(BLK)[:, None] - jnp.arange(2 * BLK)[None, :]
    buckets = jnp.stack([_t5_bucket(dist * r) for _, r in DIL_PAIRS]).astype(jnp.int32)
    nh = rel_bias.shape[1]
    return pl.pallas_call(
        _bias_kernel,
        grid=(len(DIL_PAIRS), nh),
        in_specs=[
            pl.BlockSpec((1, BLK, 2 * BLK), lambda s, h: (s, 0, 0)),
            pl.BlockSpec(memory_space=pltpu.SMEM),
        ],
        out_specs=pl.BlockSpec((1, 1, BLK, 2 * BLK), lambda s, h: (s, h, 0, 0)),
        out_shape=jax.ShapeDtypeStruct((len(DIL_PAIRS), nh, BLK, 2 * BLK), F32),
        compiler_params=_params("arbitrary", "arbitrary"),
        name="bias_tiles",
    )(buckets, rel_bias)


def _conv_kernel(b_ref, c_ref, u_ref, w_ref, o_ref):
    z = c_ref[0] * u_ref[0]
    row = lax.broadcasted_iota(jnp.int32, z.shape, 0)
    z1 = jnp.where(row >= 1, pltpu.roll(z, 1, axis=0), 0.0)
    z2 = jnp.where(row >= 2, pltpu.roll(z, 2, axis=0), 0.0)
    w = w_ref[...]
    y = z2 * w[0:1, :] + z1 * w[1:2, :] + z * w[2:3, :]
    o_ref[0] = (b_ref[0] * y).astype(o_ref.dtype)


def _conv_mixer(proj, conv_w):
    bsz, seq, _ = proj.shape
    lanes = 128
    nblk = GROUP_WIDTH // lanes
    base = OFF_CONV // lanes

    def col(seg):
        return pl.BlockSpec((1, seq, lanes), lambda b, j: (b, 0, base + seg * nblk + j))

    return pl.pallas_call(
        _conv_kernel,
        grid=(bsz, nblk),
        in_specs=[col(0), col(1), col(2), pl.BlockSpec((3, lanes), lambda b, j: (0, j))],
        out_specs=pl.BlockSpec((1, seq, lanes), lambda b, j: (b, 0, j)),
        out_shape=jax.ShapeDtypeStruct((bsz, seq, GROUP_WIDTH), BF16),
        compiler_params=_params("parallel", "parallel"),
        name="conv_mixer",
    )(proj, proj, proj, conv_w)


def _band_mask(max_dist):
    a = lax.broadcasted_iota(jnp.int32, (BLK, 2 * BLK), 0)
    b = lax.broadcasted_iota(jnp.int32, (BLK, 2 * BLK), 1)
    dist = BLK + a - b
    return (dist >= 0) & (dist <= max_dist), b


def _attend(qs, kws, vws, biases, sinks=None):
    idx = range(len(qs))
    s = [lax.dot_general(qs[i], kws[i], (((1,), (1,)), ((), ())), preferred_element_type=F32)
         + biases[i] for i in idx]
    m = [jnp.max(s[i], axis=-1, keepdims=True) for i in idx]
    if sinks is not None:
        m = [jnp.maximum(m[i], sinks[i]) for i in idx]
    p = [jnp.exp(s[i] - m[i]) for i in idx]
    den = [jnp.sum(p[i], axis=-1, keepdims=True) for i in idx]
    if sinks is not None:
        den = [den[i] + jnp.exp(sinks[i] - m[i]) for i in idx]
    o = [jnp.dot(p[i].astype(BF16), vws[i], preferred_element_type=F32) / den[i] for i in idx]
    return o, m, den


def _head_rms(x, gain):
    ms = _head_sums(x * x) * (1.0 / HEAD_DIM)
    return x * lax.rsqrt(ms + RMS_EPS) * gain


def _swa_kernel(q_ref, k_ref, v_ref, qg_ref, kg_ref, sink_ref, bias_ref, o_ref,
                qn_ref, kn_ref, vb_ref, bm_ref):
    seq = q_ref.shape[1]
    nb = seq // BLK
    kvw = SWA_KV_HEADS * HEAD_DIM
    scale = HEAD_DIM ** -0.5

    kn_ref[0:BLK, :] = jnp.zeros((BLK, 2 * kvw), BF16)
    vb_ref[0:BLK, :] = jnp.zeros((BLK, 2 * kvw), BF16)
    lane_half = lax.broadcasted_iota(jnp.int32, (1, kvw), 1) // HEAD_DIM

    def both_halves(x):
        xr = pltpu.roll(x, HEAD_DIM, axis=1)
        return jnp.concatenate([jnp.where(lane_half == 0, x, xr),
                                jnp.where(lane_half == 0, xr, x)], axis=-1)

    def prep(i, carry):
        r0 = pl.multiple_of(i * BLK, BLK)
        q = q_ref[0, pl.ds(r0, BLK), :]
        qn_ref[pl.ds(r0, BLK), :] = (_head_rms(q, qg_ref[...]) * scale).astype(BF16)
        k = _head_rms(k_ref[0, pl.ds(r0, BLK), :], kg_ref[...])
        kn_ref[pl.ds(r0 + BLK, BLK), :] = both_halves(k).astype(BF16)
        vb_ref[pl.ds(r0 + BLK, BLK), :] = both_halves(v_ref[0, pl.ds(r0, BLK), :]).astype(BF16)
        return carry

    lax.fori_loop(0, nb, prep, 0)

    band, kcol = _band_mask(SWA_WINDOW - 1)
    for h in range(N_HEADS):
        bm_ref[h] = jnp.where(band & (kcol >= BLK), bias_ref[h], NEG)
        bm_ref[N_HEADS + h] = jnp.where(band, bias_ref[h], NEG)
    sinks = [sink_ref[h] for h in range(N_HEADS)]

    def block(i, carry):
        r0 = pl.multiple_of(i * BLK, BLK)
        later = jnp.minimum(i, 1) * N_HEADS
        qs, kws, vws, bms = [], [], [], []
        for hk in range(SWA_KV_HEADS):
            kw = kn_ref[pl.ds(r0, 2 * BLK), hk * kvw:(hk + 1) * kvw]
            vw = vb_ref[pl.ds(r0, 2 * BLK), hk * kvw:(hk + 1) * kvw]
            for g in range(SWA_GROUP):
                h = hk * SWA_GROUP + g
                q2 = qn_ref[pl.ds(r0, BLK), (h // 2) * kvw:(h // 2 + 1) * kvw]
                qs.append(jnp.where(lane_half == h % 2, q2, jnp.zeros_like(q2)))
                kws.append(kw)
                vws.append(vw)
                bms.append(bm_ref[later + h])
        outs, _, _ = _attend(qs, kws, vws, bms, sinks)
        pairs = [jnp.where(lane_half == 0, outs[h], outs[h + 1]) for h in range(0, N_HEADS, 2)]
        o_ref[0, pl.ds(r0, BLK), :] = jnp.concatenate(pairs, axis=-1).astype(o_ref.dtype)
        return carry

    lax.fori_loop(0, nb, block, 0)


def _swa_mixer(proj, q_gain, k_gain, sink, bias):
    bsz, seq, _ = proj.shape
    kvw = SWA_KV_HEADS * HEAD_DIM
    q_gain_t = jnp.tile(q_gain, N_HEADS).reshape(1, GROUP_WIDTH)
    k_gain_t = jnp.tile(k_gain, SWA_KV_HEADS).reshape(1, kvw)
    return pl.pallas_call(
        _swa_kernel,
        grid=(bsz,),
        in_specs=[
            pl.BlockSpec((1, seq, GROUP_WIDTH), lambda b: (b, 0, OFF_SWA_Q // GROUP_WIDTH)),
            pl.BlockSpec((1, seq, kvw), lambda b: (b, 0, OFF_SWA_K // kvw)),
            pl.BlockSpec((1, seq, kvw), lambda b: (b, 0, OFF_SWA_V // kvw)),
            pl.BlockSpec((1, GROUP_WIDTH), lambda b: (0, 0)),
            pl.BlockSpec((1, kvw), lambda b: (0, 0)),
            pl.BlockSpec(memory_space=pltpu.SMEM),
            pl.BlockSpec((N_HEADS, BLK, 2 * BLK), lambda b: (0, 0, 0)),
        ],
        out_specs=pl.BlockSpec((1, seq, GROUP_WIDTH), lambda b: (b, 0, 0)),
        out_shape=jax.ShapeDtypeStruct((bsz, seq, GROUP_WIDTH), BF16),
        scratch_shapes=[
            pltpu.VMEM((seq, GROUP_WIDTH), BF16),
            pltpu.VMEM((seq + BLK, 2 * kvw), BF16),
            pltpu.VMEM((seq + BLK, 2 * kvw), BF16),
            pltpu.VMEM((2 * N_HEADS, BLK, 2 * BLK), F32),
        ],
        compiler_params=_params("parallel"),
        name="swa_mixer",
    )(proj, proj, proj, q_gain_t, k_gain_t, sink, bias)


def _dil_kernel(q_ref, k_ref, v_ref, qg_ref, kg_ref, bias_ref, o_ref,
                qn_ref, kn_ref, ob_ref, lb_ref, bm_ref):
    seq = q_ref.shape[1]
    lanes = q_ref.shape[2]
    heads = lanes // HEAD_DIM
    scale = HEAD_DIM ** -0.5

    def prep(i, carry):
        r0 = pl.multiple_of(i * BLK, BLK)
        qn_ref[pl.ds(r0, BLK), :] = _head_rms(q_ref[0, pl.ds(r0, BLK), :], qg_ref[...]) * scale
        kn_ref[pl.ds(r0, BLK), :] = _head_rms(k_ref[0, pl.ds(r0, BLK), :], kg_ref[...])
        return carry

    lax.fori_loop(0, seq // BLK, prep, 0)

    for br, (window, r) in enumerate(DIL_PAIRS):
        band, kcol = _band_mask(window // r)
        for h in range(heads):
            bm_ref[(br * heads + h) * 2] = jnp.where(band & (kcol >= BLK), bias_ref[br, h], NEG)
            bm_ref[(br * heads + h) * 2 + 1] = jnp.where(band, bias_ref[br, h], NEG)

    lane_head = lax.broadcasted_iota(jnp.int32, (1, lanes), 1) // HEAD_DIM

    for br, (window, r) in enumerate(DIL_PAIRS):
        nb = seq // r // BLK

        def blocks(it, carry, br=br, r=r, nb=nb):
            qs, kws, vws, bms, curs = [], [], [], [], []
            for u in range(DIL_TILES):
                t = it * DIL_TILES + u
                c = t // nb
                i = t - c * nb
                cur = c + i * (BLK * r)
                prev = jnp.maximum(cur - BLK * r, c)
                later = jnp.minimum(i, 1)

                def rows(ref, start):
                    if r == 1:
                        return ref[pl.ds(start, BLK), :]
                    return ref[pl.ds(start, BLK, stride=r), :]

                q = rows(qn_ref, cur).astype(BF16)
                kw = jnp.concatenate([rows(kn_ref, prev), rows(kn_ref, cur)], axis=0).astype(BF16)
                vw = jnp.concatenate([rows(v_ref.at[0], prev), rows(v_ref.at[0], cur)],
                                     axis=0).astype(BF16)
                curs.append(cur)
                for h in range(heads):
                    qs.append(jnp.where(lane_head == h, q, jnp.zeros_like(q)))
                    kws.append(kw)
                    vws.append(vw)
                    bms.append(bm_ref[(br * heads + h) * 2 + later])
            outs, ms, dens = _attend(qs, kws, vws, bms)
            for u in range(DIL_TILES):
                o_all = outs[u * heads]
                l_all = ms[u * heads] + jnp.log(dens[u * heads])
                for h in range(1, heads):
                    o_all = jnp.where(lane_head == h, outs[u * heads + h], o_all)
                    l_all = jnp.where(lane_head == h,
                                      ms[u * heads + h] + jnp.log(dens[u * heads + h]), l_all)
                if r == 1:
                    ob_ref[br, pl.ds(curs[u], BLK), :] = o_all
                    lb_ref[br, pl.ds(curs[u], BLK), :] = l_all
                else:
                    ob_ref[br, pl.ds(curs[u], BLK, stride=r), :] = o_all
                    lb_ref[br, pl.ds(curs[u], BLK, stride=r), :] = l_all
            return carry

        lax.fori_loop(0, r * nb // DIL_TILES, blocks, 0)

    def combine(i, carry):
        r0 = pl.multiple_of(i * BLK, BLK)
        l0 = lb_ref[0, pl.ds(r0, BLK), :]
        l1 = lb_ref[1, pl.ds(r0, BLK), :]
        l2 = lb_ref[2, pl.ds(r0, BLK), :]
        m = jnp.maximum(jnp.maximum(l0, l1), l2)
        e0, e1, e2 = jnp.exp(l0 - m), jnp.exp(l1 - m), jnp.exp(l2 - m)
        tot = e0 + e1 + e2
        o = ((e0 / tot) * ob_ref[0, pl.ds(r0, BLK), :]
             + (e1 / tot) * ob_ref[1, pl.ds(r0, BLK), :]
             + (e2 / tot) * ob_ref[2, pl.ds(r0, BLK), :])
        o_ref[0, pl.ds(r0, BLK), :] = o.astype(o_ref.dtype)
        return carry

    lax.fori_loop(0, seq // BLK, combine, 0)


def _dil_mixer(proj, q_gain, k_gain, bias):
    bsz, seq, _ = proj.shape
    lanes = 128
    heads = lanes // HEAD_DIM
    nblk = GROUP_WIDTH // lanes
    base = OFF_DIL // lanes
    gq = jnp.tile(q_gain, heads).reshape(1, lanes)
    gk = jnp.tile(k_gain, heads).reshape(1, lanes)

    def col(seg):
        return pl.BlockSpec((1, seq, lanes), lambda b, j: (b, 0, base + seg * nblk + j))

    return pl.pallas_call(
        _dil_kernel,
        grid=(bsz, nblk),
        in_specs=[
            col(0), col(1), col(2),
            pl.BlockSpec((1, lanes), lambda b, j: (0, 0)),
            pl.BlockSpec((1, lanes), lambda b, j: (0, 0)),
            pl.BlockSpec((len(DIL_PAIRS), heads, BLK, 2 * BLK), lambda b, j: (0, j, 0, 0)),
        ],
        out_specs=pl.BlockSpec((1, seq, lanes), lambda b, j: (b, 0, j)),
        out_shape=jax.ShapeDtypeStruct((bsz, seq, GROUP_WIDTH), BF16),
        scratch_shapes=[
            pltpu.VMEM((seq, lanes), F32),
            pltpu.VMEM((seq, lanes), F32),
            pltpu.VMEM((len(DIL_PAIRS), seq, lanes), F32),
            pltpu.VMEM((len(DIL_PAIRS), seq, lanes), F32),
            pltpu.VMEM((len(DIL_PAIRS) * heads * 2, BLK, 2 * BLK), F32),
        ],
        compiler_params=_params("parallel", "parallel"),
        name="dil_mixer",
    )(proj, proj, proj, gq, gk, bias)


def _rwkv_maps_kernel(*refs, has_vres):
    if has_vres:
        (p_ref, pp_ref, mu_ref, w0_ref, w2_ref, a0_ref, a2_ref, g2_ref, kk_ref, ka_ref, rk_ref,
         vf_ref, v0_ref, v1_ref, v2_ref,
         q_out, y1_out, m_out, gm_out, bonus_out, gate_out, *scratch) = refs
    else:
        (p_ref, pp_ref, mu_ref, w0_ref, w2_ref, a0_ref, a2_ref, g2_ref, kk_ref, ka_ref, rk_ref,
         q_out, y1_out, m_out, gm_out, bonus_out, gate_out, v_out, *scratch) = refs
    w = GROUP_WIDTH
    p = p_ref[0]
    row = lax.broadcasted_iota(jnp.int32, p.shape, 0)
    last_prev = pp_ref[0, 7:8, :]
    last_prev = jnp.where(pl.program_id(1) > 0, last_prev, 0.0)
    prev = jnp.where(row >= 1, pltpu.roll(p, 1, axis=0), last_prev)
    xs = p + (prev - p) * mu_ref[...]
    r = xs[:, 0:w]
    k = xs[:, w:2 * w]
    v = xs[:, 2 * w:3 * w]
    o = 3 * w
    wd = xs[:, o:o + DECAY_LORA]
    ad = xs[:, o + DECAY_LORA:o + DECAY_LORA + ICLR_LORA]
    gd = xs[:, o + DECAY_LORA + ICLR_LORA:]

    z = -(w0_ref[...] + _mm(jnp.tanh(wd), w2_ref[...]))
    softplus = jnp.maximum(z, 0.0) + jnp.log(1.0 + jnp.exp(-jnp.abs(z)))
    logw = -softplus - 0.5
    lw = -jnp.exp(logw)
    a = _sigmoid(a0_ref[...] + _mm(ad, a2_ref[...]))
    gate_out[0] = _mm(_sigmoid(gd), g2_ref[...])
    if has_vres:
        mix = _sigmoid(v0_ref[...] + _mm(_mm(v, v1_ref[...]), v2_ref[...]))
        v = v + (vf_ref[0] - v) * mix
    else:
        v_out[0] = v
    kk = k * kk_ref[...]
    ss = _head_sums(kk * kk)
    kk = kk * lax.rsqrt(jnp.maximum(ss, 1e-24))
    k = k * (1.0 + (a - 1.0) * ka_ref[...])
    bonus_out[0] = _head_sums(r * k * rk_ref[...]) * v
    _chunk_maps(r, lw, k, v, kk, kk * a, q_out.at[0], y1_out.at[0], m_out.at[0], gm_out.at[0],
                *scratch)


def _rwkv_maps(proj, mu, w0, w2, a0, a2, g2, k_k, k_a, r_k, v_first, vres, ts=256):
    bsz, seq, _ = proj.shape
    w = GROUP_WIDTH
    nt = seq // ts
    has_vres = vres is not None

    def full(shape):
        return pl.BlockSpec(shape, lambda b, i: (0,) * len(shape))

    row = lambda a: a.reshape(1, -1)
    tile = pl.BlockSpec((1, ts, w), lambda b, i: (b, i, 0))
    in_specs = [
        pl.BlockSpec((pl.Element(1), pl.Element(ts), pl.Element(RWKV_IN_WIDTH)),
                     lambda b, i: (b, i * ts, OFF_RW)),
        pl.BlockSpec((pl.Element(1), pl.Element(8), pl.Element(RWKV_IN_WIDTH)),
                     lambda b, i: (b, jnp.maximum(i * (ts // 8) - 1, 0) * 8, OFF_RW)),
        full((1, RWKV_IN_WIDTH)), full((1, w)), full((DECAY_LORA, w)), full((1, w)),
        full((ICLR_LORA, w)), full((GATE_LORA, w)), full((1, w)), full((1, w)), full((1, w)),
    ]
    args = [proj, proj, row(mu), row(w0), w2, row(a0), a2, g2, row(k_k), row(k_a), row(r_k)]
    if has_vres:
        v0, v1, v2 = vres
        in_specs += [tile, full((1, w)), full(v1.shape), full(v2.shape)]
        args += [v_first, row(v0), v1, v2]
    f32 = jax.ShapeDtypeStruct((bsz, seq, w), F32)
    bf16 = jax.ShapeDtypeStruct((bsz, seq, w), BF16)
    out_shape = [bf16, f32, bf16, f32, f32, f32] + ([] if has_vres else [f32])
    return pl.pallas_call(
        functools.partial(_rwkv_maps_kernel, has_vres=has_vres),
        grid=(bsz, nt),
        in_specs=in_specs,
        out_specs=[tile] * len(out_shape),
        out_shape=out_shape,
        scratch_shapes=[pltpu.VMEM((ts, w), F32)] * 5 + [pltpu.VMEM((ts, w), BF16)],
        compiler_params=_params("parallel", "parallel"),
        name="rwkv_chunk_maps",
    )(*args)


def _chunk_maps(r, lw, k, v, kk, b, q_out, y1_out, m_out, g_out,
                at_ref, rt_ref, bt_ref, kt_ref, dec_ref, vb_ref):
    c = WKV_CHUNK
    n = HEAD_DIM
    rows = r.shape[0]
    nchunk = rows // c

    row = lax.broadcasted_iota(jnp.int32, (rows, rows), 0)
    col = lax.broadcasted_iota(jnp.int32, (rows, rows), 1)
    tri = ((row >= col) & ((row // c) == (col // c))).astype(BF16)
    lw_hi, lw_lo = _split_bf16(lw)
    cum = (jnp.dot(tri, lw_hi, preferred_element_type=F32)
           + jnp.dot(tri, lw_lo, preferred_element_type=F32))
    e_pos = jnp.exp(cum)
    e_neg = jnp.exp(-cum)
    at_ref[...] = -kk * jnp.exp(cum - lw)
    rt_ref[...] = r * e_pos
    bt_ref[...] = b * e_neg
    kt_ref[...] = k * e_neg
    dec_ref[...] = e_pos
    vb_ref[...] = v.astype(BF16)

    crow = lax.broadcasted_iota(jnp.int32, (c, c), 0)
    ccol = lax.broadcasted_iota(jnp.int32, (c, c), 1)
    lower = crow >= ccol
    strict = crow > ccol
    eye = (crow == ccol).astype(F32)

    pairs = [(j, h) for j in range(nchunk) for h in range(N_HEADS)]

    def tile(ref, j, h, bf16=True):
        x = ref[j * c:(j + 1) * c, h * n:(h + 1) * n]
        return x.astype(BF16) if bf16 else x

    ah = [tile(at_ref, j, h) for j, h in pairs]
    rh = [tile(rt_ref, j, h, bf16=False) for j, h in pairs]
    bh = [tile(bt_ref, j, h) for j, h in pairs]
    kh = [tile(kt_ref, j, h) for j, h in pairs]
    vh = [vb_ref[j * c:(j + 1) * c, h * n:(h + 1) * n] for j, h in pairs]
    dh = [dec_ref[(j + 1) * c - 1:(j + 1) * c, h * n:(h + 1) * n] for j, h in pairs]
    idx = range(len(pairs))
    ar = [jnp.concatenate([ah[i], rh[i].astype(BF16)], axis=0) for i in idx]
    bk = [jnp.concatenate([bh[i], kh[i]], axis=0) for i in idx]
    aa = [_mm_nt(ar[i], bk[i]) for i in idx]
    n_ab = [jnp.where(strict, aa[i][:c, :c], 0.0) for i in idx]
    a_kr = [jnp.concatenate([jnp.where(strict, aa[i][:c, c:], 0.0),
                             jnp.where(lower, aa[i][c:, c:], 0.0)], axis=0).astype(BF16)
            for i in idx]
    a_rb = [jnp.where(lower, aa[i][c:, :c], 0.0).astype(BF16) for i in idx]
    t = [eye + n_ab[i] for i in idx]
    pw = n_ab
    for _ in range(int(math.log2(c)) - 1):
        pw = [_mm(pw[i], pw[i]) for i in idx]
        t = [t[i] + _mm(pw[i], t[i]) for i in idx]
    av = [_mm(a_kr[i], vh[i]) for i in idx]
    wu = [_mm(t[i], jnp.concatenate([ah[i], av[i][:c].astype(BF16)], axis=1)).astype(BF16)
          for i in idx]
    aw = [_mm(a_rb[i], wu[i]) for i in idx]
    q = [rh[i] + aw[i][:, :n] for i in idx]
    y1 = [aw[i][:, n:] + av[i][c:] for i in idx]
    m = [(eye + _mm_tn(wu[i][:, :n], bh[i])) * dh[i] for i in idx]
    g = [_mm_tn(jnp.concatenate([wu[i][:, n:], vh[i]], axis=0), bk[i]) * dh[i] for i in idx]
    for j in range(nchunk):
        sel = slice(j * N_HEADS, (j + 1) * N_HEADS)
        q_out[j * c:(j + 1) * c, :] = jnp.concatenate(q[sel], axis=-1).astype(q_out.dtype)
        y1_out[j * c:(j + 1) * c, :] = jnp.concatenate(y1[sel], axis=-1)
        m_out[j * c:(j + 1) * c, :] = jnp.concatenate(m[sel], axis=-1).astype(m_out.dtype)
        g_out[j * c:(j + 1) * c, :] = jnp.concatenate(g[sel], axis=-1)


def _wkv_scan_kernel(q_ref, y1_ref, m_ref, gm_ref, bonus_ref, gate_ref, lng_ref, lnb_ref,
                     o_ref, st_ref):
    n = HEAD_DIM
    bsz = q_ref.shape[0]

    @pl.when(pl.program_id(0) == 0)
    def _():
        st_ref[...] = jnp.zeros(st_ref.shape, F32)

    for bi in range(bsz):
        ys = []
        for h in range(N_HEADS):
            sl = slice(h * n, (h + 1) * n)
            s0 = st_ref[bi * N_HEADS + h].astype(BF16)
            ys.append(_mm_nt(q_ref[bi, :, sl], s0) + y1_ref[bi, :, sl])
            st_ref[bi * N_HEADS + h] = _mm(s0, m_ref[bi, :, sl]) + gm_ref[bi, :, sl]
        y = jnp.concatenate(ys, axis=-1)
        mean = _head_sums(y) * (1.0 / n)
        yc = y - mean
        var = _head_sums(yc * yc) * (1.0 / n)
        yn = yc * lax.rsqrt(var + LN_X_EPS) * lng_ref[...] + lnb_ref[...]
        o_ref[bi] = ((yn + bonus_ref[bi]) * gate_ref[bi]).astype(o_ref.dtype)


def _wkv_scan(q, y1, m, gm, bonus, gate, ln_g, ln_b):
    bsz, seq, w = y1.shape
    c = WKV_CHUNK
    tile = pl.BlockSpec((bsz, c, w), lambda ci: (0, ci, 0))
    vec = pl.BlockSpec((1, w), lambda ci: (0, 0))
    return pl.pallas_call(
        _wkv_scan_kernel,
        grid=(seq // c,),
        in_specs=[tile] * 6 + [vec] * 2,
        out_specs=tile,
        out_shape=jax.ShapeDtypeStruct((bsz, seq, w), BF16),
        scratch_shapes=[pltpu.VMEM((bsz * N_HEADS, HEAD_DIM, HEAD_DIM), F32)],
        compiler_params=_params("arbitrary"),
        name="wkv_state_scan",
    )(q, y1, m, gm, bonus, gate, ln_g.reshape(1, w), ln_b.reshape(1, w))


def _wout_kernel(x_ref, y0_ref, y1_ref, y2_ref, y3_ref, w_ref, o_ref):
    acc = x_ref[...]
    for idx, y_ref in enumerate((y0_ref, y1_ref, y2_ref, y3_ref)):
        acc = acc + jnp.dot(y_ref[...], w_ref[idx * GROUP_WIDTH:(idx + 1) * GROUP_WIDTH, :],
                            preferred_element_type=F32)
    o_ref[...] = acc


def _wout(x2d, ys, w_bf16, tm=512):
    m, d = x2d.shape
    ytile = pl.BlockSpec((tm, GROUP_WIDTH), lambda i: (i, 0))
    xtile = pl.BlockSpec((tm, d), lambda i: (i, 0))
    return pl.pallas_call(
        _wout_kernel,
        grid=(m // tm,),
        in_specs=[xtile] + [ytile] * 4 + [pl.BlockSpec(w_bf16.shape, lambda i: (0, 0))],
        out_specs=xtile,
        out_shape=jax.ShapeDtypeStruct((m, d), F32),
        compiler_params=_params("parallel"),
        name="wout_residual",
    )(x2d, *ys, w_bf16)


def _ffn_kernel(x_ref, g_ref, wu_ref, wd_ref, o_ref, h_ref):
    @pl.when(pl.program_id(1) == 0)
    def _():
        x = x_ref[...]
        ms = jnp.mean(x * x, axis=-1, keepdims=True)
        h_ref[...] = (x * lax.rsqrt(ms + RMS_EPS) * g_ref[0]).astype(BF16)
        o_ref[...] = x

    u = jnp.dot(h_ref[...], wu_ref[0].astype(BF16), preferred_element_type=F32)
    act = jnp.square(jnp.maximum(u, 0.0)).astype(BF16)
    o_ref[...] += jnp.dot(act, wd_ref[0].astype(BF16), preferred_element_type=F32)


def _ffn(x2d, gains, wu_stack, wd_stack, layer, tm=1024, tf=512):
    m, d = x2d.shape
    f = wu_stack.shape[2]
    xtile = pl.BlockSpec((tm, d), lambda i, j: (i, 0))
    return pl.pallas_call(
        _ffn_kernel,
        grid=(m // tm, f // tf),
        in_specs=[
            xtile,
            pl.BlockSpec((1, 1, d), lambda i, j: (layer, 0, 0)),
            pl.BlockSpec((1, d, tf), lambda i, j: (layer, 0, j)),
            pl.BlockSpec((1, tf, d), lambda i, j: (layer, j, 0)),
        ],
        out_specs=xtile,
        out_shape=jax.ShapeDtypeStruct((m, d), F32),
        scratch_shapes=[pltpu.VMEM((tm, d), BF16)],
        compiler_params=_params("parallel", "arbitrary", vmem_limit=FFN_VMEM_LIMIT),
        name="ffn",
    )(x2d, gains.reshape(-1, 1, d), wu_stack, wd_stack)


def kernel(x, norm_mix, w_in, conv_w, swa_q_norm, swa_k_norm, swa_sink, dil_q_norm, dil_k_norm,
           rwkv_mu, decay_w0, decay_w2, iclr_a0, iclr_a2, gate_g2, k_k, k_a, r_k, ln_x_g, ln_x_b,
           vres_v0, vres_v1, vres_v2, w_out, norm_ffn, w_up, w_down, rel_bias):
    bsz, seq, d = x.shape
    depth = w_in.shape[0]
    bias = _bias_tiles(rel_bias)
    swa_bias = bias[0, :N_HEADS]
    dil_bias = bias[:, N_HEADS:]
    x2d = x.reshape(bsz * seq, d)
    v_first = None
    for layer in range(depth):
        proj = _norm_matmul(x2d, norm_mix, w_in, layer).reshape(bsz, seq, IN_WIDTH)
        y_conv = _conv_mixer(proj, conv_w[layer])
        y_swa = _swa_mixer(proj, swa_q_norm[layer], swa_k_norm[layer], swa_sink[layer], swa_bias)
        y_dil = _dil_mixer(proj, dil_q_norm[layer], dil_k_norm[layer], dil_bias)
        vres = None if layer == 0 else (vres_v0[layer - 1], vres_v1[layer - 1], vres_v2[layer - 1])
        maps = _rwkv_maps(
            proj, rwkv_mu[layer], decay_w0[layer], decay_w2[layer], iclr_a0[layer], iclr_a2[layer],
            gate_g2[layer], k_k[layer], k_a[layer], r_k[layer], v_first, vres)
        if layer == 0:
            v_first = maps[6]
        y_rwkv = _wkv_scan(*maps[:6], ln_x_g[layer], ln_x_b[layer])
        ys = [y.reshape(bsz * seq, GROUP_WIDTH) for y in (y_conv, y_swa, y_dil, y_rwkv)]
        x2d = _wout(x2d, ys, w_out[layer].astype(BF16))
        x2d = _ffn(x2d, norm_ffn, w_up, w_down, layer)
    return x2d.reshape(bsz, seq, d)
```

```python
import functools
import math

import jax
import jax.numpy as jnp
from jax import lax
from jax.experimental import pallas as pl
from jax.experimental.pallas import tpu as pltpu

F32 = jnp.float32
BF16 = jnp.bfloat16

D_MODEL = 2048
HEAD_DIM = 64
GROUP_WIDTH = 512
N_HEADS = GROUP_WIDTH // HEAD_DIM
SWA_KV_HEADS = 2
SWA_GROUP = N_HEADS // SWA_KV_HEADS
SWA_WINDOW = 128
DIL_PAIRS = ((128, 1), (512, 4), (2048, 16))
DECAY_LORA = 64
ICLR_LORA = 64
GATE_LORA = 128
RWKV_IN_WIDTH = 3 * GROUP_WIDTH + DECAY_LORA + ICLR_LORA + GATE_LORA
BLK = 128
NUM_BUCKETS = 32
BUCKET_MAX_DIST = 128
RMS_EPS = 1e-6
LN_X_EPS = 64e-5
NEG = -1e30
LOG2E = math.log2(math.e)
WKV_CHUNK = 64
WKV_GROUP = 2
DIL_TILES = 4
PREP_UNROLL = 4

OFF_CONV = 0
OFF_SWA_Q = OFF_CONV + 3 * GROUP_WIDTH
OFF_SWA_K = OFF_SWA_Q + GROUP_WIDTH
OFF_SWA_V = OFF_SWA_K + SWA_KV_HEADS * HEAD_DIM
OFF_DIL = OFF_SWA_V + SWA_KV_HEADS * HEAD_DIM
OFF_RW = OFF_DIL + 3 * GROUP_WIDTH
IN_WIDTH = OFF_RW + RWKV_IN_WIDTH

V7X_VMEM_BYTES = 64 * 1024 * 1024
VMEM_LIMIT = 48 * 1024 * 1024
FFN_VMEM_LIMIT = V7X_VMEM_BYTES - 6 * 1024 * 1024


def _params(*sem, vmem_limit=VMEM_LIMIT):
    return pltpu.CompilerParams(dimension_semantics=sem, vmem_limit_bytes=vmem_limit)


def _mm(a, b):
    return jnp.dot(a.astype(BF16), b.astype(BF16), preferred_element_type=F32)


def _mm_nt(a, b):
    return lax.dot_general(a.astype(BF16), b.astype(BF16), (((1,), (1,)), ((), ())),
                           preferred_element_type=F32)


def _mm_tn(a, b):
    return lax.dot_general(a.astype(BF16), b.astype(BF16), (((0,), (0,)), ((), ())),
                           preferred_element_type=F32)


def _split_bf16(x):
    hi = x.astype(BF16)
    lo = (x - hi.astype(F32)).astype(BF16)
    return hi, lo


def _head_sums(x):
    lanes = 128
    r = lax.broadcasted_iota(jnp.int32, (lanes, lanes), 0) // HEAD_DIM
    c = lax.broadcasted_iota(jnp.int32, (lanes, lanes), 1) // HEAD_DIM
    bd = (r == c).astype(BF16)
    hi, lo = _split_bf16(x)
    cols = []
    for j in range(x.shape[-1] // lanes):
        sl = slice(j * lanes, (j + 1) * lanes)
        cols.append(jnp.dot(hi[:, sl], bd, preferred_element_type=F32)
                    + jnp.dot(lo[:, sl], bd, preferred_element_type=F32))
    return cols[0] if len(cols) == 1 else jnp.concatenate(cols, axis=-1)


def _sigmoid(z):
    return 1.0 / (1.0 + jnp.exp(-z))


def _norm_matmul_kernel(x_ref, g_ref, w_ref, o_ref, h_ref):
    @pl.when(pl.program_id(1) == 0)
    def _():
        x = x_ref[...]
        ms = jnp.mean(x * x, axis=-1, keepdims=True)
        h_ref[...] = (x * lax.rsqrt(ms + RMS_EPS) * g_ref[0]).astype(BF16)

    o_ref[...] = jnp.dot(h_ref[...], w_ref[0].astype(BF16), preferred_element_type=F32)


def _norm_matmul(x2d, gains, w_stack, layer, tm=1024, tn=512):
    m, k = x2d.shape
    n = w_stack.shape[2]
    return pl.pallas_call(
        _norm_matmul_kernel,
        grid=(m // tm, n // tn),
        in_specs=[
            pl.BlockSpec((tm, k), lambda i, j: (i, 0)),
            pl.BlockSpec((1, 1, k), lambda i, j: (layer, 0, 0)),
            pl.BlockSpec((1, k, tn), lambda i, j: (layer, 0, j)),
        ],
        out_specs=pl.BlockSpec((tm, tn), lambda i, j: (i, j)),
        out_shape=jax.ShapeDtypeStruct((m, n), F32),
        scratch_shapes=[pltpu.VMEM((tm, k), BF16)],
        compiler_params=_params("parallel", "arbitrary"),
        name="norm_matmul",
    )(x2d, gains.reshape(-1, 1, k), w_stack)


def _t5_bucket(dist):
    dist = jnp.maximum(dist, 0)
    max_exact = NUM_BUCKETS // 2
    scaled = (jnp.log(jnp.maximum(dist, 1).astype(F32) / max_exact)
              / math.log(BUCKET_MAX_DIST / max_exact))
    large = max_exact + (scaled * (NUM_BUCKETS - max_exact)).astype(jnp.int32)
    large = jnp.minimum(large, NUM_BUCKETS - 1)
    return jnp.where(dist < max_exact, dist, large)


def _bias_kernel(bucket_ref, table_ref, o_ref):
    h = pl.program_id(1)
    bucket = bucket_ref[0]
    acc = jnp.zeros(bucket.shape, F32)
    for b in range(NUM_BUCKETS):
        acc = jnp.where(bucket == b, table_ref[b, h], acc)
    o_ref[0, 0] = acc


def _bias_tiles(rel_bias):
    dist = BLK + jnp.arange(BLK)[:, None] - jnp.arange(2 * BLK)[None, :]
    buckets = jnp.stack([_t5_bucket(dist * r) for _, r in DIL_PAIRS]).astype(jnp.int32)
    nh = rel_bias.shape[1]
    return pl.pallas_call(
        _bias_kernel,
        grid=(len(DIL_PAIRS), nh),
        in_specs=[
            pl.BlockSpec((1, BLK, 2 * BLK), lambda s, h: (s, 0, 0)),
            pl.BlockSpec(memory_space=pltpu.SMEM),
        ],
        out_specs=pl.BlockSpec((1, 1, BLK, 2 * BLK), lambda s, h: (s, h, 0, 0)),
        out_shape=jax.ShapeDtypeStruct((len(DIL_PAIRS), nh, BLK, 2 * BLK), F32),
        compiler_params=_params("arbitrary", "arbitrary"),
        name="bias_tiles",
    )(buckets, rel_bias)


def _conv_kernel(b_ref, c_ref, u_ref, w_ref, o_ref):
    z = c_ref[0] * u_ref[0]
    row = lax.broadcasted_iota(jnp.int32, z.shape, 0)
    z1 = jnp.where(row >= 1, pltpu.roll(z, 1, axis=0), 0.0)
    z2 = jnp.where(row >= 2, pltpu.roll(z, 2, axis=0), 0.0)
    w = w_ref[...]
    y = z2 * w[0:1, :] + z1 * w[1:2, :] + z * w[2:3, :]
    o_ref[0] = (b_ref[0] * y).astype(o_ref.dtype)


def _conv_mixer(proj, conv_w):
    bsz, seq, _ = proj.shape
    lanes = 128
    nblk = GROUP_WIDTH // lanes
    base = OFF_CONV // lanes

    def col(seg):
        return pl.BlockSpec((1, seq, lanes), lambda b, j: (b, 0, base + seg * nblk + j))

    return pl.pallas_call(
        _conv_kernel,
        grid=(bsz, nblk),
        in_specs=[col(0), col(1), col(2), pl.BlockSpec((3, lanes), lambda b, j: (0, j))],
        out_specs=pl.BlockSpec((1, seq, lanes), lambda b, j: (b, 0, j)),
        out_shape=jax.ShapeDtypeStruct((bsz, seq, GROUP_WIDTH), BF16),
        compiler_params=_params("parallel", "parallel"),
        name="conv_mixer",
    )(proj, proj, proj, conv_w)


def _band_mask(max_dist):
    a = lax.broadcasted_iota(jnp.int32, (BLK, 2 * BLK), 0)
    b = lax.broadcasted_iota(jnp.int32, (BLK, 2 * BLK), 1)
    dist = BLK + a - b
    return (dist >= 0) & (dist <= max_dist), b


def _attend(qs, kws, vws, biases, sinks=None):
    idx = range(len(qs))
    s = [lax.dot_general(qs[i], kws[i], (((1,), (1,)), ((), ())), preferred_element_type=F32)
         + biases[i] for i in idx]
    m = [jnp.max(s[i], axis=-1, keepdims=True) for i in idx]
    if sinks is not None:
        m = [jnp.maximum(m[i], sinks[i]) for i in idx]
    p = [jnp.exp2(s[i] - m[i]) for i in idx]
    den = [jnp.sum(p[i], axis=-1, keepdims=True) for i in idx]
    if sinks is not None:
        den = [den[i] + jnp.exp2(sinks[i] - m[i]) for i in idx]
    o = [jnp.dot(p[i].astype(BF16), vws[i], preferred_element_type=F32) / den[i] for i in idx]
    return o, m, den


def _head_rms(x, gain):
    ms = _head_sums(x * x) * (1.0 / HEAD_DIM)
    return x * lax.rsqrt(ms + RMS_EPS) * gain


def _swa_kernel(q_ref, k_ref, v_ref, qg_ref, kg_ref, sink_ref, bias_ref, o_ref,
                qn_ref, kn_ref, vb_ref, bm_ref):
    seq = q_ref.shape[1]
    nb = seq // BLK
    kvw = SWA_KV_HEADS * HEAD_DIM
    scale = HEAD_DIM ** -0.5 * LOG2E

    kn_ref[0:BLK, :] = jnp.zeros((BLK, 2 * kvw), BF16)
    vb_ref[0:BLK, :] = jnp.zeros((BLK, 2 * kvw), BF16)
    lane_half = lax.broadcasted_iota(jnp.int32, (1, kvw), 1) // HEAD_DIM

    def both_halves(x):
        xr = pltpu.roll(x, HEAD_DIM, axis=1)
        return jnp.concatenate([jnp.where(lane_half == 0, x, xr),
                                jnp.where(lane_half == 0, xr, x)], axis=-1)

    def prep(i, carry):
        r0 = pl.multiple_of(i * BLK, BLK)
        q = q_ref[0, pl.ds(r0, BLK), :]
        qn_ref[pl.ds(r0, BLK), :] = (_head_rms(q, qg_ref[...]) * scale).astype(BF16)
        k = _head_rms(k_ref[0, pl.ds(r0, BLK), :], kg_ref[...])
        kn_ref[pl.ds(r0 + BLK, BLK), :] = both_halves(k).astype(BF16)
        vb_ref[pl.ds(r0 + BLK, BLK), :] = both_halves(v_ref[0, pl.ds(r0, BLK), :]).astype(BF16)
        return carry

    lax.fori_loop(0, nb, prep, 0, unroll=PREP_UNROLL)

    band, kcol = _band_mask(SWA_WINDOW - 1)
    for h in range(N_HEADS):
        bm_ref[h] = jnp.where(band & (kcol >= BLK), bias_ref[h] * LOG2E, NEG)
        bm_ref[N_HEADS + h] = jnp.where(band, bias_ref[h] * LOG2E, NEG)
    sinks = [sink_ref[h] * LOG2E for h in range(N_HEADS)]

    def block(i, carry):
        r0 = pl.multiple_of(i * BLK, BLK)
        later = jnp.minimum(i, 1) * N_HEADS
        qs, kws, vws, bms = [], [], [], []
        for hk in range(SWA_KV_HEADS):
            kw = kn_ref[pl.ds(r0, 2 * BLK), hk * kvw:(hk + 1) * kvw]
            vw = vb_ref[pl.ds(r0, 2 * BLK), hk * kvw:(hk + 1) * kvw]
            for g in range(SWA_GROUP):
                h = hk * SWA_GROUP + g
                q2 = qn_ref[pl.ds(r0, BLK), (h // 2) * kvw:(h // 2 + 1) * kvw]
                qs.append(jnp.where(lane_half == h % 2, q2, jnp.zeros_like(q2)))
                kws.append(kw)
                vws.append(vw)
                bms.append(bm_ref[later + h])
        outs, _, _ = _attend(qs, kws, vws, bms, sinks)
        pairs = [jnp.where(lane_half == 0, outs[h], outs[h + 1]) for h in range(0, N_HEADS, 2)]
        o_ref[0, pl.ds(r0, BLK), :] = jnp.concatenate(pairs, axis=-1).astype(o_ref.dtype)
        return carry

    lax.fori_loop(0, nb, block, 0)


def _swa_mixer(proj, q_gain, k_gain, sink, bias):
    bsz, seq, _ = proj.shape
    kvw = SWA_KV_HEADS * HEAD_DIM
    q_gain_t = jnp.tile(q_gain, N_HEADS).reshape(1, GROUP_WIDTH)
    k_gain_t = jnp.tile(k_gain, SWA_KV_HEADS).reshape(1, kvw)
    return pl.pallas_call(
        _swa_kernel,
        grid=(bsz,),
        in_specs=[
            pl.BlockSpec((1, seq, GROUP_WIDTH), lambda b: (b, 0, OFF_SWA_Q // GROUP_WIDTH)),
            pl.BlockSpec((1, seq, kvw), lambda b: (b, 0, OFF_SWA_K // kvw)),
            pl.BlockSpec((1, seq, kvw), lambda b: (b, 0, OFF_SWA_V // kvw)),
            pl.BlockSpec((1, GROUP_WIDTH), lambda b: (0, 0)),
            pl.BlockSpec((1, kvw), lambda b: (0, 0)),
            pl.BlockSpec(memory_space=pltpu.SMEM),
            pl.BlockSpec((N_HEADS, BLK, 2 * BLK), lambda b: (0, 0, 0)),
        ],
        out_specs=pl.BlockSpec((1, seq, GROUP_WIDTH), lambda b: (b, 0, 0)),
        out_shape=jax.ShapeDtypeStruct((bsz, seq, GROUP_WIDTH), BF16),
        scratch_shapes=[
            pltpu.VMEM((seq, GROUP_WIDTH), BF16),
            pltpu.VMEM((seq + BLK, 2 * kvw), BF16),
            pltpu.VMEM((seq + BLK, 2 * kvw), BF16),
            pltpu.VMEM((2 * N_HEADS, BLK, 2 * BLK), F32),
        ],
        compiler_params=_params("parallel"),
        name="swa_mixer",
    )(proj, proj, proj, q_gain_t, k_gain_t, sink, bias)


def _dil_kernel(q_ref, k_ref, v_ref, qg_ref, kg_ref, bias_ref, o_ref,
                qn_ref, kn_ref, ob_ref, lb_ref, bm_ref):
    seq = q_ref.shape[1]
    lanes = q_ref.shape[2]
    heads = lanes // HEAD_DIM
    scale = HEAD_DIM ** -0.5 * LOG2E

    def prep(i, carry):
        r0 = pl.multiple_of(i * BLK, BLK)
        qn_ref[pl.ds(r0, BLK), :] = _head_rms(q_ref[0, pl.ds(r0, BLK), :], qg_ref[...]) * scale
        kn_ref[pl.ds(r0, BLK), :] = _head_rms(k_ref[0, pl.ds(r0, BLK), :], kg_ref[...])
        return carry

    lax.fori_loop(0, seq // BLK, prep, 0, unroll=PREP_UNROLL)

    for br, (window, r) in enumerate(DIL_PAIRS):
        band, kcol = _band_mask(window // r)
        for h in range(heads):
            bias = bias_ref[br, h] * LOG2E
            bm_ref[(br * heads + h) * 2] = jnp.where(band & (kcol >= BLK), bias, NEG)
            bm_ref[(br * heads + h) * 2 + 1] = jnp.where(band, bias, NEG)

    lane_head = lax.broadcasted_iota(jnp.int32, (1, lanes), 1) // HEAD_DIM

    for br, (window, r) in enumerate(DIL_PAIRS):
        nb = seq // r // BLK

        def blocks(it, carry, br=br, r=r, nb=nb):
            qs, kws, vws, bms, curs = [], [], [], [], []
            for u in range(DIL_TILES):
                t = it * DIL_TILES + u
                c = t // nb
                i = t - c * nb
                cur = c + i * (BLK * r)
                prev = jnp.maximum(cur - BLK * r, c)
                later = jnp.minimum(i, 1)

                def rows(ref, start):
                    if r == 1:
                        return ref[pl.ds(start, BLK), :]
                    return ref[pl.ds(start, BLK, stride=r), :]

                q = rows(qn_ref, cur).astype(BF16)
                if nb == 1:
                    kw = rows(kn_ref, cur).astype(BF16)
                    vw = rows(v_ref.at[0], cur).astype(BF16)
                else:
                    kw = jnp.concatenate([rows(kn_ref, prev), rows(kn_ref, cur)],
                                         axis=0).astype(BF16)
                    vw = jnp.concatenate([rows(v_ref.at[0], prev), rows(v_ref.at[0], cur)],
                                         axis=0).astype(BF16)
                curs.append(cur)
                for h in range(heads):
                    qs.append(jnp.where(lane_head == h, q, jnp.zeros_like(q)))
                    kws.append(kw)
                    vws.append(vw)
                    if nb == 1:
                        bms.append(bm_ref[(br * heads + h) * 2, :, BLK:])
                    else:
                        bms.append(bm_ref[(br * heads + h) * 2 + later])
            outs, ms, dens = _attend(qs, kws, vws, bms)
            for u in range(DIL_TILES):
                o_all = outs[u * heads]
                l_all = ms[u * heads] + jnp.log2(dens[u * heads])
                for h in range(1, heads):
                    o_all = jnp.where(lane_head == h, outs[u * heads + h], o_all)
                    l_all = jnp.where(lane_head == h,
                                      ms[u * heads + h] + jnp.log2(dens[u * heads + h]), l_all)
                if r == 1:
                    ob_ref[br, pl.ds(curs[u], BLK), :] = o_all
                    lb_ref[br, pl.ds(curs[u], BLK), :] = l_all
                else:
                    ob_ref[br, pl.ds(curs[u], BLK, stride=r), :] = o_all
                    lb_ref[br, pl.ds(curs[u], BLK, stride=r), :] = l_all
            return carry

        lax.fori_loop(0, r * nb // DIL_TILES, blocks, 0)

    def combine(i, carry):
        r0 = pl.multiple_of(i * BLK, BLK)
        l0 = lb_ref[0, pl.ds(r0, BLK), :]
        l1 = lb_ref[1, pl.ds(r0, BLK), :]
        l2 = lb_ref[2, pl.ds(r0, BLK), :]
        m = jnp.maximum(jnp.maximum(l0, l1), l2)
        e0, e1, e2 = jnp.exp2(l0 - m), jnp.exp2(l1 - m), jnp.exp2(l2 - m)
        tot = e0 + e1 + e2
        o = ((e0 / tot) * ob_ref[0, pl.ds(r0, BLK), :]
             + (e1 / tot) * ob_ref[1, pl.ds(r0, BLK), :]
             + (e2 / tot) * ob_ref[2, pl.ds(r0, BLK), :])
        o_ref[0, pl.ds(r0, BLK), :] = o.astype(o_ref.dtype)
        return carry

    lax.fori_loop(0, seq // BLK, combine, 0, unroll=PREP_UNROLL)


def _dil_mixer(proj, q_gain, k_gain, bias):
    bsz, seq, _ = proj.shape
    lanes = 128
    heads = lanes // HEAD_DIM
    nblk = GROUP_WIDTH // lanes
    base = OFF_DIL // lanes
    gq = jnp.tile(q_gain, heads).reshape(1, lanes)
    gk = jnp.tile(k_gain, heads).reshape(1, lanes)

    def col(seg):
        return pl.BlockSpec((1, seq, lanes), lambda b, j: (b, 0, base + seg * nblk + j))

    return pl.pallas_call(
        _dil_kernel,
        grid=(bsz, nblk),
        in_specs=[
            col(0), col(1), col(2),
            pl.BlockSpec((1, lanes), lambda b, j: (0, 0)),
            pl.BlockSpec((1, lanes), lambda b, j: (0, 0)),
            pl.BlockSpec((len(DIL_PAIRS), heads, BLK, 2 * BLK), lambda b, j: (0, j, 0, 0)),
        ],
        out_specs=pl.BlockSpec((1, seq, lanes), lambda b, j: (b, 0, j)),
        out_shape=jax.ShapeDtypeStruct((bsz, seq, GROUP_WIDTH), BF16),
        scratch_shapes=[
            pltpu.VMEM((seq, lanes), F32),
            pltpu.VMEM((seq, lanes), F32),
            pltpu.VMEM((len(DIL_PAIRS), seq, lanes), F32),
            pltpu.VMEM((len(DIL_PAIRS), seq, lanes), F32),
            pltpu.VMEM((len(DIL_PAIRS) * heads * 2, BLK, 2 * BLK), F32),
        ],
        compiler_params=_params("parallel", "parallel"),
        name="dil_mixer",
    )(proj, proj, proj, gq, gk, bias)


def _rwkv_maps_kernel(*refs, has_vres):
    if has_vres:
        (p_ref, pp_ref, mu_ref, w0_ref, w2_ref, a0_ref, a2_ref, g2_ref, kk_ref, ka_ref, rk_ref,
         vf_ref, v0_ref, v1_ref, v2_ref,
         q_out, y1_out, m_out, gm_out, bonus_out, gate_out, *scratch) = refs
    else:
        (p_ref, pp_ref, mu_ref, w0_ref, w2_ref, a0_ref, a2_ref, g2_ref, kk_ref, ka_ref, rk_ref,
         q_out, y1_out, m_out, gm_out, bonus_out, gate_out, v_out, *scratch) = refs
    w = GROUP_WIDTH
    p = p_ref[0]
    row = lax.broadcasted_iota(jnp.int32, p.shape, 0)
    last_prev = pp_ref[0, 7:8, :]
    last_prev = jnp.where(pl.program_id(1) > 0, last_prev, 0.0)
    prev = jnp.where(row >= 1, pltpu.roll(p, 1, axis=0), last_prev)
    xs = p + (prev - p) * mu_ref[...]
    r = xs[:, 0:w]
    k = xs[:, w:2 * w]
    v = xs[:, 2 * w:3 * w]
    o = 3 * w
    wd = xs[:, o:o + DECAY_LORA]
    ad = xs[:, o + DECAY_LORA:o + DECAY_LORA + ICLR_LORA]
    gd = xs[:, o + DECAY_LORA + ICLR_LORA:]

    z = -(w0_ref[...] + _mm(jnp.tanh(wd), w2_ref[...]))
    softplus = jnp.maximum(z, 0.0) + jnp.log(1.0 + jnp.exp(-jnp.abs(z)))
    logw = -softplus - 0.5
    lw = -jnp.exp(logw)
    a = _sigmoid(a0_ref[...] + _mm(ad, a2_ref[...]))
    gate_out[0] = _mm(_sigmoid(gd), g2_ref[...])
    if has_vres:
        mix = _sigmoid(v0_ref[...] + _mm(_mm(v, v1_ref[...]), v2_ref[...]))
        v = v + (vf_ref[0] - v) * mix
    else:
        v_out[0] = v
    kk = k * kk_ref[...]
    ss = _head_sums(kk * kk)
    kk = kk * lax.rsqrt(jnp.maximum(ss, 1e-24))
    k = k * (1.0 + (a - 1.0) * ka_ref[...])
    bonus_out[0] = _head_sums(r * k * rk_ref[...]) * v
    _chunk_maps(r, lw, k, v, kk, kk * a, q_out.at[0], y1_out.at[0], m_out.at[0], gm_out.at[0],
                *scratch)


def _rwkv_maps(proj, mu, w0, w2, a0, a2, g2, k_k, k_a, r_k, v_first, vres, ts=256):
    bsz, seq, _ = proj.shape
    w = GROUP_WIDTH
    nt = seq // ts
    has_vres = vres is not None

    def full(shape):
        return pl.BlockSpec(shape, lambda b, i: (0,) * len(shape))

    row = lambda a: a.reshape(1, -1)
    tile = pl.BlockSpec((1, ts, w), lambda b, i: (b, i, 0))
    in_specs = [
        pl.BlockSpec((pl.Element(1), pl.Element(ts), pl.Element(RWKV_IN_WIDTH)),
                     lambda b, i: (b, i * ts, OFF_RW)),
        pl.BlockSpec((pl.Element(1), pl.Element(8), pl.Element(RWKV_IN_WIDTH)),
                     lambda b, i: (b, jnp.maximum(i * (ts // 8) - 1, 0) * 8, OFF_RW)),
        full((1, RWKV_IN_WIDTH)), full((1, w)), full((DECAY_LORA, w)), full((1, w)),
        full((ICLR_LORA, w)), full((GATE_LORA, w)), full((1, w)), full((1, w)), full((1, w)),
    ]
    args = [proj, proj, row(mu), row(w0), w2, row(a0), a2, g2, row(k_k), row(k_a), row(r_k)]
    if has_vres:
        v0, v1, v2 = vres
        in_specs += [tile, full((1, w)), full(v1.shape), full(v2.shape)]
        args += [v_first, row(v0), v1, v2]
    f32 = jax.ShapeDtypeStruct((bsz, seq, w), F32)
    bf16 = jax.ShapeDtypeStruct((bsz, seq, w), BF16)
    out_shape = [bf16, f32, bf16, f32, f32, f32] + ([] if has_vres else [f32])
    return pl.pallas_call(
        functools.partial(_rwkv_maps_kernel, has_vres=has_vres),
        grid=(bsz, nt),
        in_specs=in_specs,
        out_specs=[tile] * len(out_shape),
        out_shape=out_shape,
        scratch_shapes=[pltpu.VMEM((ts, w), F32)] * 5 + [pltpu.VMEM((ts, w), BF16)],
        compiler_params=_params("parallel", "parallel"),
        name="rwkv_chunk_maps",
    )(*args)


def _chunk_maps(r, lw, k, v, kk, b, q_out, y1_out, m_out, g_out,
                at_ref, rt_ref, bt_ref, kt_ref, dec_ref, vb_ref):
    c = WKV_CHUNK
    n = HEAD_DIM
    rows = r.shape[0]
    nchunk = rows // c

    row = lax.broadcasted_iota(jnp.int32, (rows, rows), 0)
    col = lax.broadcasted_iota(jnp.int32, (rows, rows), 1)
    tri = ((row >= col) & ((row // c) == (col // c))).astype(BF16)
    lw_hi, lw_lo = _split_bf16(lw)
    cum = (jnp.dot(tri, lw_hi, preferred_element_type=F32)
           + jnp.dot(tri, lw_lo, preferred_element_type=F32))
    e_pos = jnp.exp(cum)
    e_neg = jnp.exp(-cum)
    at_ref[...] = -kk * jnp.exp(cum - lw)
    rt_ref[...] = r * e_pos
    bt_ref[...] = b * e_neg
    kt_ref[...] = k * e_neg
    dec_ref[...] = e_pos
    vb_ref[...] = v.astype(BF16)

    crow = lax.broadcasted_iota(jnp.int32, (c, c), 0)
    ccol = lax.broadcasted_iota(jnp.int32, (c, c), 1)
    lower = crow >= ccol
    strict = crow > ccol
    eye = (crow == ccol).astype(F32)
    wrow = lax.broadcasted_iota(jnp.int32, (c, 2 * c), 0)
    wcol = lax.broadcasted_iota(jnp.int32, (c, 2 * c), 1)
    strict_left = wrow > wcol
    right = wcol >= c
    eye_right = (wcol == wrow + c).astype(F32)

    def tile(ref, j, h, bf16=True):
        x = ref[j * c:(j + 1) * c, h * n:(h + 1) * n]
        return x.astype(BF16) if bf16 else x

    for grp in range(nchunk // WKV_GROUP):
        _chunk_group(range(grp * WKV_GROUP, (grp + 1) * WKV_GROUP), tile, at_ref, rt_ref, bt_ref,
                     kt_ref, dec_ref, vb_ref, q_out, y1_out, m_out, g_out,
                     lower, strict, eye, strict_left, right, eye_right)


def _chunk_group(chunks, tile, at_ref, rt_ref, bt_ref, kt_ref, dec_ref, vb_ref,
                 q_out, y1_out, m_out, g_out, lower, strict, eye, strict_left, right, eye_right):
    c = WKV_CHUNK
    n = HEAD_DIM
    pairs = [(j, h) for j in chunks for h in range(N_HEADS)]
    ah = [tile(at_ref, j, h) for j, h in pairs]
    rh = [tile(rt_ref, j, h, bf16=False) for j, h in pairs]
    bh = [tile(bt_ref, j, h) for j, h in pairs]
    kh = [tile(kt_ref, j, h) for j, h in pairs]
    vh = [vb_ref[j * c:(j + 1) * c, h * n:(h + 1) * n] for j, h in pairs]
    dh = [dec_ref[(j + 1) * c - 1:(j + 1) * c, h * n:(h + 1) * n] for j, h in pairs]
    idx = range(len(pairs))
    ar = [jnp.concatenate([ah[i], rh[i].astype(BF16)], axis=0) for i in idx]
    bk = [jnp.concatenate([bh[i], kh[i]], axis=0) for i in idx]
    aa = [_mm_nt(ar[i], bk[i]) for i in idx]
    a_kr = [jnp.concatenate([jnp.where(strict, aa[i][:c, c:], 0.0),
                             jnp.where(lower, aa[i][c:, c:], 0.0)], axis=0).astype(BF16)
            for i in idx]
    a_rb = [jnp.where(lower, aa[i][c:, :c], 0.0).astype(BF16) for i in idx]
    ps = [jnp.where(strict_left, aa[i][:c, :], 0.0) + eye_right for i in idx]
    for _ in range(int(math.log2(c))):
        ps = [_mm(ps[i][:, :c], ps[i]) + jnp.where(right, ps[i], 0.0) for i in idx]
    t = [ps[i][:, c:].astype(BF16) for i in idx]
    av = [_mm(a_kr[i], vh[i]) for i in idx]
    wu = [_mm(t[i], jnp.concatenate([ah[i], av[i][:c].astype(BF16)], axis=1)).astype(BF16)
          for i in idx]
    aw = [_mm(a_rb[i], wu[i]) for i in idx]
    q = [rh[i] + aw[i][:, :n] for i in idx]
    y1 = [aw[i][:, n:] + av[i][c:] for i in idx]
    zero = jnp.zeros((c, n), BF16)
    mg = [_mm_tn(jnp.concatenate([wu[i], jnp.concatenate([zero, vh[i]], axis=1)], axis=0), bk[i])
          for i in idx]
    m = [(eye + mg[i][:n]) * dh[i] for i in idx]
    g = [mg[i][n:] * dh[i] for i in idx]
    for j in chunks:
        sel = slice((j - chunks[0]) * N_HEADS, (j - chunks[0] + 1) * N_HEADS)
        q_out[j * c:(j + 1) * c, :] = jnp.concatenate(q[sel], axis=-1).astype(q_out.dtype)
        y1_out[j * c:(j + 1) * c, :] = jnp.concatenate(y1[sel], axis=-1)
        m_out[j * c:(j + 1) * c, :] = jnp.concatenate(m[sel], axis=-1).astype(m_out.dtype)
        g_out[j * c:(j + 1) * c, :] = jnp.concatenate(g[sel], axis=-1)


def _wkv_scan_kernel(q_ref, y1_ref, m_ref, gm_ref, bonus_ref, gate_ref, lng_ref, lnb_ref,
                     o_ref, st_ref):
    n = HEAD_DIM
    bsz = q_ref.shape[0]

    @pl.when(pl.program_id(0) == 0)
    def _():
        st_ref[...] = jnp.zeros(st_ref.shape, F32)

    for bi in range(bsz):
        ys = []
        for h in range(N_HEADS):
            sl = slice(h * n, (h + 1) * n)
            s0 = st_ref[bi * N_HEADS + h].astype(BF16)
            ys.append(_mm_nt(q_ref[bi, :, sl], s0) + y1_ref[bi, :, sl])
            st_ref[bi * N_HEADS + h] = _mm(s0, m_ref[bi, :, sl]) + gm_ref[bi, :, sl]
        y = jnp.concatenate(ys, axis=-1)
        mean = _head_sums(y) * (1.0 / n)
        yc = y - mean
        var = _head_sums(yc * yc) * (1.0 / n)
        yn = yc * lax.rsqrt(var + LN_X_EPS) * lng_ref[...] + lnb_ref[...]
        o_ref[bi] = ((yn + bonus_ref[bi]) * gate_ref[bi]).astype(o_ref.dtype)


def _wkv_scan(q, y1, m, gm, bonus, gate, ln_g, ln_b):
    bsz, seq, w = y1.shape
    c = WKV_CHUNK
    tile = pl.BlockSpec((bsz, c, w), lambda ci: (0, ci, 0))
    vec = pl.BlockSpec((1, w), lambda ci: (0, 0))
    return pl.pallas_call(
        _wkv_scan_kernel,
        grid=(seq // c,),
        in_specs=[tile] * 6 + [vec] * 2,
        out_specs=tile,
        out_shape=jax.ShapeDtypeStruct((bsz, seq, w), BF16),
        scratch_shapes=[pltpu.VMEM((bsz * N_HEADS, HEAD_DIM, HEAD_DIM), F32)],
        compiler_params=_params("arbitrary"),
        name="wkv_state_scan",
    )(q, y1, m, gm, bonus, gate, ln_g.reshape(1, w), ln_b.reshape(1, w))


def _wout_kernel(x_ref, y0_ref, y1_ref, y2_ref, y3_ref, w_ref, o_ref):
    acc = x_ref[...]
    for idx, y_ref in enumerate((y0_ref, y1_ref, y2_ref, y3_ref)):
        acc = acc + jnp.dot(y_ref[...], w_ref[idx * GROUP_WIDTH:(idx + 1) * GROUP_WIDTH, :],
                            preferred_element_type=F32)
    o_ref[...] = acc


def _wout(x2d, ys, w_bf16, tm=512):
    m, d = x2d.shape
    ytile = pl.BlockSpec((tm, GROUP_WIDTH), lambda i: (i, 0))
    xtile = pl.BlockSpec((tm, d), lambda i: (i, 0))
    return pl.pallas_call(
        _wout_kernel,
        grid=(m // tm,),
        in_specs=[xtile] + [ytile] * 4 + [pl.BlockSpec(w_bf16.shape, lambda i: (0, 0))],
        out_specs=xtile,
        out_shape=jax.ShapeDtypeStruct((m, d), F32),
        compiler_params=_params("parallel"),
        name="wout_residual",
    )(x2d, *ys, w_bf16)


def _ffn_kernel(x_ref, g_ref, wu_ref, wd_ref, o_ref, h_ref):
    @pl.when(pl.program_id(1) == 0)
    def _():
        x = x_ref[...]
        ms = jnp.mean(x * x, axis=-1, keepdims=True)
        h_ref[...] = (x * lax.rsqrt(ms + RMS_EPS) * g_ref[0]).astype(BF16)
        o_ref[...] = x

    u = jnp.dot(h_ref[...], wu_ref[0].astype(BF16), preferred_element_type=F32)
    act = jnp.square(jnp.maximum(u, 0.0)).astype(BF16)
    o_ref[...] += jnp.dot(act, wd_ref[0].astype(BF16), preferred_element_type=F32)


def _ffn(x2d, gains, wu_stack, wd_stack, layer, tm=1024, tf=512):
    m, d = x2d.shape
    f = wu_stack.shape[2]
    xtile = pl.BlockSpec((tm, d), lambda i, j: (i, 0))
    return pl.pallas_call(
        _ffn_kernel,
        grid=(m // tm, f // tf),
        in_specs=[
            xtile,
            pl.BlockSpec((1, 1, d), lambda i, j: (layer, 0, 0)),
            pl.BlockSpec((1, d, tf), lambda i, j: (layer, 0, j)),
            pl.BlockSpec((1, tf, d), lambda i, j: (layer, j, 0)),
        ],
        out_specs=xtile,
        out_shape=jax.ShapeDtypeStruct((m, d), F32),
        scratch_shapes=[pltpu.VMEM((tm, d), BF16)],
        compiler_params=_params("parallel", "arbitrary", vmem_limit=FFN_VMEM_LIMIT),
        name="ffn",
    )(x2d, gains.reshape(-1, 1, d), wu_stack, wd_stack)


def kernel(x, norm_mix, w_in, conv_w, swa_q_norm, swa_k_norm, swa_sink, dil_q_norm, dil_k_norm,
           rwkv_mu, decay_w0, decay_w2, iclr_a0, iclr_a2, gate_g2, k_k, k_a, r_k, ln_x_g, ln_x_b,
           vres_v0, vres_v1, vres_v2, w_out, norm_ffn, w_up, w_down, rel_bias):
    bsz, seq, d = x.shape
    depth = w_in.shape[0]
    bias = _bias_tiles(rel_bias)
    swa_bias = bias[0, :N_HEADS]
    dil_bias = bias[:, N_HEADS:]
    x2d = x.reshape(bsz * seq, d)
    v_first = None
    for layer in range(depth):
        proj = _norm_matmul(x2d, norm_mix, w_in, layer).reshape(bsz, seq, IN_WIDTH)
        y_conv = _conv_mixer(proj, conv_w[layer])
        y_swa = _swa_mixer(proj, swa_q_norm[layer], swa_k_norm[layer], swa_sink[layer], swa_bias)
        y_dil = _dil_mixer(proj, dil_q_norm[layer], dil_k_norm[layer], dil_bias)
        vres = None if layer == 0 else (vres_v0[layer - 1], vres_v1[layer - 1], vres_v2[layer - 1])
        maps = _rwkv_maps(
            proj, rwkv_mu[layer], decay_w0[layer], decay_w2[layer], iclr_a0[layer], iclr_a2[layer],
            gate_g2[layer], k_k[layer], k_a[layer], r_k[layer], v_first, vres)
        if layer == 0:
            v_first = maps[6]
        y_rwkv = _wkv_scan(*maps[:6], ln_x_g[layer], ln_x_b[layer])
        ys = [y.reshape(bsz * seq, GROUP_WIDTH) for y in (y_conv, y_swa, y_dil, y_rwkv)]
        x2d = _wout(x2d, ys, w_out[layer].astype(BF16))
        x2d = _ffn(x2d, norm_ffn, w_up, w_down, layer)
    return x2d.reshape(bsz, seq, d)
```

```python
import functools
import math

import jax
import jax.numpy as jnp
from jax import lax
from jax.experimental import pallas as pl
from jax.experimental.pallas import tpu as pltpu

F32 = jnp.float32
BF16 = jnp.bfloat16

D_MODEL = 2048
HEAD_DIM = 64
GROUP_WIDTH = 512
N_HEADS = GROUP_WIDTH // HEAD_DIM
SWA_KV_HEADS = 2
SWA_GROUP = N_HEADS // SWA_KV_HEADS
SWA_WINDOW = 128
DIL_PAIRS = ((128, 1), (512, 4), (2048, 16))
DECAY_LORA = 64
ICLR_LORA = 64
GATE_LORA = 128
RWKV_IN_WIDTH = 3 * GROUP_WIDTH + DECAY_LORA + ICLR_LORA + GATE_LORA
BLK = 128
NUM_BUCKETS = 32
BUCKET_MAX_DIST = 128
RMS_EPS = 1e-6
LN_X_EPS = 64e-5
NEG = -1e30
LOG2E = math.log2(math.e)
WKV_CHUNK = 64
WKV_GROUP = 4
DIL_TILES = 4
PREV_ROWS = 16
PREP_UNROLL = 4

OFF_CONV = 0
OFF_SWA_Q = OFF_CONV + 3 * GROUP_WIDTH
OFF_SWA_K = OFF_SWA_Q + GROUP_WIDTH
OFF_SWA_V = OFF_SWA_K + SWA_KV_HEADS * HEAD_DIM
OFF_DIL = OFF_SWA_V + SWA_KV_HEADS * HEAD_DIM
OFF_RW = OFF_DIL + 3 * GROUP_WIDTH
IN_WIDTH = OFF_RW + RWKV_IN_WIDTH

V7X_VMEM_BYTES = 64 * 1024 * 1024
VMEM_LIMIT = 48 * 1024 * 1024
FFN_VMEM_LIMIT = V7X_VMEM_BYTES - 6 * 1024 * 1024


def _params(*sem, vmem_limit=VMEM_LIMIT):
    return pltpu.CompilerParams(dimension_semantics=sem, vmem_limit_bytes=vmem_limit)


def _mm(a, b):
    return jnp.dot(a.astype(BF16), b.astype(BF16), preferred_element_type=F32)


def _mm_nt(a, b):
    return lax.dot_general(a.astype(BF16), b.astype(BF16), (((1,), (1,)), ((), ())),
                           preferred_element_type=F32)


def _mm_tn(a, b):
    return lax.dot_general(a.astype(BF16), b.astype(BF16), (((0,), (0,)), ((), ())),
                           preferred_element_type=F32)


def _split_bf16(x):
    hi = x.astype(BF16)
    lo = (x - hi.astype(F32)).astype(BF16)
    return hi, lo


def _head_sums(x):
    lanes = 128
    r = lax.broadcasted_iota(jnp.int32, (lanes, lanes), 0) // HEAD_DIM
    c = lax.broadcasted_iota(jnp.int32, (lanes, lanes), 1) // HEAD_DIM
    bd = (r == c).astype(BF16)
    hi, lo = _split_bf16(x)
    cols = []
    for j in range(x.shape[-1] // lanes):
        sl = slice(j * lanes, (j + 1) * lanes)
        cols.append(jnp.dot(hi[:, sl], bd, preferred_element_type=F32)
                    + jnp.dot(lo[:, sl], bd, preferred_element_type=F32))
    return cols[0] if len(cols) == 1 else jnp.concatenate(cols, axis=-1)


def _sigmoid(z):
    return 1.0 / (1.0 + jnp.exp(-z))


def _norm_matmul_kernel(x_ref, g_ref, w_ref, o_ref, h_ref):
    @pl.when(pl.program_id(1) == 0)
    def _():
        x = x_ref[...]
        ms = jnp.mean(x * x, axis=-1, keepdims=True)
        h_ref[...] = (x * lax.rsqrt(ms + RMS_EPS) * g_ref[0]).astype(BF16)

    o_ref[...] = jnp.dot(h_ref[...], w_ref[0].astype(BF16),
                         preferred_element_type=F32).astype(o_ref.dtype)


def _norm_matmul(x2d, gains, w_stack, layer, tm=1024, tn=512):
    m, k = x2d.shape
    n = w_stack.shape[2]
    return pl.pallas_call(
        _norm_matmul_kernel,
        grid=(m // tm, n // tn),
        in_specs=[
            pl.BlockSpec((tm, k), lambda i, j: (i, 0)),
            pl.BlockSpec((1, 1, k), lambda i, j: (layer, 0, 0)),
            pl.BlockSpec((1, k, tn), lambda i, j: (layer, 0, j)),
        ],
        out_specs=pl.BlockSpec((tm, tn), lambda i, j: (i, j)),
        out_shape=jax.ShapeDtypeStruct((m, n), BF16),
        scratch_shapes=[pltpu.VMEM((tm, k), BF16)],
        compiler_params=_params("parallel", "arbitrary"),
        name="norm_matmul",
    )(x2d, gains.reshape(-1, 1, k), w_stack)


def _t5_bucket(dist):
    dist = jnp.maximum(dist, 0)
    max_exact = NUM_BUCKETS // 2
    scaled = (jnp.log(jnp.maximum(dist, 1).astype(F32) / max_exact)
              / math.log(BUCKET_MAX_DIST / max_exact))
    large = max_exact + (scaled * (NUM_BUCKETS - max_exact)).astype(jnp.int32)
    large = jnp.minimum(large, NUM_BUCKETS - 1)
    return jnp.where(dist < max_exact, dist, large)


def _bias_kernel(bucket_ref, table_ref, o_ref):
    h = pl.program_id(1)
    bucket = bucket_ref[0]
    acc = jnp.zeros(bucket.shape, F32)
    for b in range(NUM_BUCKETS):
        acc = jnp.where(bucket == b, table_ref[b, h], acc)
    o_ref[0, 0] = acc


def _bias_tiles(rel_bias):
    dist = BLK + jnp.arange(BLK)[:, None] - jnp.arange(2 * BLK)[None, :]
    buckets = jnp.stack([_t5_bucket(dist * r) for _, r in DIL_PAIRS]).astype(jnp.int32)
    nh = rel_bias.shape[1]
    return pl.pallas_call(
        _bias_kernel,
        grid=(len(DIL_PAIRS), nh),
        in_specs=[
            pl.BlockSpec((1, BLK, 2 * BLK), lambda s, h: (s, 0, 0)),
            pl.BlockSpec(memory_space=pltpu.SMEM),
        ],
        out_specs=pl.BlockSpec((1, 1, BLK, 2 * BLK), lambda s, h: (s, h, 0, 0)),
        out_shape=jax.ShapeDtypeStruct((len(DIL_PAIRS), nh, BLK, 2 * BLK), F32),
        compiler_params=_params("arbitrary", "arbitrary"),
        name="bias_tiles",
    )(buckets, rel_bias)


def _conv_kernel(b_ref, c_ref, u_ref, w_ref, o_ref):
    z = c_ref[0].astype(F32) * u_ref[0].astype(F32)
    row = lax.broadcasted_iota(jnp.int32, z.shape, 0)
    z1 = jnp.where(row >= 1, pltpu.roll(z, 1, axis=0), 0.0)
    z2 = jnp.where(row >= 2, pltpu.roll(z, 2, axis=0), 0.0)
    w = w_ref[...]
    y = z2 * w[0:1, :] + z1 * w[1:2, :] + z * w[2:3, :]
    o_ref[0] = (b_ref[0].astype(F32) * y).astype(o_ref.dtype)


def _conv_mixer(proj, conv_w):
    bsz, seq, _ = proj.shape
    lanes = 128
    nblk = GROUP_WIDTH // lanes
    base = OFF_CONV // lanes

    def col(seg):
        return pl.BlockSpec((1, seq, lanes), lambda b, j: (b, 0, base + seg * nblk + j))

    return pl.pallas_call(
        _conv_kernel,
        grid=(bsz, nblk),
        in_specs=[col(0), col(1), col(2), pl.BlockSpec((3, lanes), lambda b, j: (0, j))],
        out_specs=pl.BlockSpec((1, seq, lanes), lambda b, j: (b, 0, j)),
        out_shape=jax.ShapeDtypeStruct((bsz, seq, GROUP_WIDTH), BF16),
        compiler_params=_params("parallel", "parallel"),
        name="conv_mixer",
    )(proj, proj, proj, conv_w)


def _band_mask(max_dist):
    a = lax.broadcasted_iota(jnp.int32, (BLK, 2 * BLK), 0)
    b = lax.broadcasted_iota(jnp.int32, (BLK, 2 * BLK), 1)
    dist = BLK + a - b
    return (dist >= 0) & (dist <= max_dist), b


def _attend(qs, kws, vws, biases, sinks=None):
    idx = range(len(qs))
    s = [lax.dot_general(qs[i], kws[i], (((1,), (1,)), ((), ())), preferred_element_type=F32)
         + biases[i] for i in idx]
    m = [jnp.max(s[i], axis=-1, keepdims=True) for i in idx]
    if sinks is not None:
        m = [jnp.maximum(m[i], sinks[i]) for i in idx]
    p = [jnp.exp2(s[i] - m[i]) for i in idx]
    den = [jnp.sum(p[i], axis=-1, keepdims=True) for i in idx]
    if sinks is not None:
        den = [den[i] + jnp.exp2(sinks[i] - m[i]) for i in idx]
    o = [jnp.dot(p[i].astype(BF16), vws[i], preferred_element_type=F32) / den[i] for i in idx]
    return o, m, den


def _head_rms(x, gain):
    ms = _head_sums(x * x) * (1.0 / HEAD_DIM)
    return x * lax.rsqrt(ms + RMS_EPS) * gain


def _swa_kernel(q_ref, k_ref, v_ref, qg_ref, kg_ref, sink_ref, bias_ref, o_ref,
                qn_ref, kn_ref, vb_ref, bm_ref):
    seq = q_ref.shape[1]
    nb = seq // BLK
    kvw = SWA_KV_HEADS * HEAD_DIM
    scale = HEAD_DIM ** -0.5 * LOG2E

    kn_ref[0:BLK, :] = jnp.zeros((BLK, 2 * kvw), BF16)
    vb_ref[0:BLK, :] = jnp.zeros((BLK, 2 * kvw), BF16)
    lane_half = lax.broadcasted_iota(jnp.int32, (1, kvw), 1) // HEAD_DIM

    def both_halves(x):
        xr = pltpu.roll(x, HEAD_DIM, axis=1)
        return jnp.concatenate([jnp.where(lane_half == 0, x, xr),
                                jnp.where(lane_half == 0, xr, x)], axis=-1)

    def prep(i, carry):
        r0 = pl.multiple_of(i * BLK, BLK)
        q = q_ref[0, pl.ds(r0, BLK), :].astype(F32)
        qn_ref[pl.ds(r0, BLK), :] = (_head_rms(q, qg_ref[...]) * scale).astype(BF16)
        k = _head_rms(k_ref[0, pl.ds(r0, BLK), :].astype(F32), kg_ref[...])
        kn_ref[pl.ds(r0 + BLK, BLK), :] = both_halves(k).astype(BF16)
        v = v_ref[0, pl.ds(r0, BLK), :].astype(F32)
        vb_ref[pl.ds(r0 + BLK, BLK), :] = both_halves(v).astype(BF16)
        return carry

    lax.fori_loop(0, nb, prep, 0, unroll=PREP_UNROLL)

    band, kcol = _band_mask(SWA_WINDOW - 1)
    for h in range(N_HEADS):
        bm_ref[h] = jnp.where(band & (kcol >= BLK), bias_ref[h] * LOG2E, NEG)
        bm_ref[N_HEADS + h] = jnp.where(band, bias_ref[h] * LOG2E, NEG)
    sinks = [sink_ref[h] * LOG2E for h in range(N_HEADS)]

    def block(i, carry):
        r0 = pl.multiple_of(i * BLK, BLK)
        later = jnp.minimum(i, 1) * N_HEADS
        qs, kws, vws, bms = [], [], [], []
        for hk in range(SWA_KV_HEADS):
            kw = kn_ref[pl.ds(r0, 2 * BLK), hk * kvw:(hk + 1) * kvw]
            vw = vb_ref[pl.ds(r0, 2 * BLK), hk * kvw:(hk + 1) * kvw]
            for g in range(SWA_GROUP):
                h = hk * SWA_GROUP + g
                q2 = qn_ref[pl.ds(r0, BLK), (h // 2) * kvw:(h // 2 + 1) * kvw]
                qs.append(jnp.where(lane_half == h % 2, q2, jnp.zeros_like(q2)))
                kws.append(kw)
                vws.append(vw)
                bms.append(bm_ref[later + h])
        outs, _, _ = _attend(qs, kws, vws, bms, sinks)
        pairs = [jnp.where(lane_half == 0, outs[h], outs[h + 1]) for h in range(0, N_HEADS, 2)]
        o_ref[0, pl.ds(r0, BLK), :] = jnp.concatenate(pairs, axis=-1).astype(o_ref.dtype)
        return carry

    lax.fori_loop(0, nb, block, 0)


def _swa_mixer(proj, q_gain, k_gain, sink, bias):
    bsz, seq, _ = proj.shape
    kvw = SWA_KV_HEADS * HEAD_DIM
    q_gain_t = jnp.tile(q_gain, N_HEADS).reshape(1, GROUP_WIDTH)
    k_gain_t = jnp.tile(k_gain, SWA_KV_HEADS).reshape(1, kvw)
    return pl.pallas_call(
        _swa_kernel,
        grid=(bsz,),
        in_specs=[
            pl.BlockSpec((1, seq, GROUP_WIDTH), lambda b: (b, 0, OFF_SWA_Q // GROUP_WIDTH)),
            pl.BlockSpec((1, seq, kvw), lambda b: (b, 0, OFF_SWA_K // kvw)),
            pl.BlockSpec((1, seq, kvw), lambda b: (b, 0, OFF_SWA_V // kvw)),
            pl.BlockSpec((1, GROUP_WIDTH), lambda b: (0, 0)),
            pl.BlockSpec((1, kvw), lambda b: (0, 0)),
            pl.BlockSpec(memory_space=pltpu.SMEM),
            pl.BlockSpec((N_HEADS, BLK, 2 * BLK), lambda b: (0, 0, 0)),
        ],
        out_specs=pl.BlockSpec((1, seq, GROUP_WIDTH), lambda b: (b, 0, 0)),
        out_shape=jax.ShapeDtypeStruct((bsz, seq, GROUP_WIDTH), BF16),
        scratch_shapes=[
            pltpu.VMEM((seq, GROUP_WIDTH), BF16),
            pltpu.VMEM((seq + BLK, 2 * kvw), BF16),
            pltpu.VMEM((seq + BLK, 2 * kvw), BF16),
            pltpu.VMEM((2 * N_HEADS, BLK, 2 * BLK), F32),
        ],
        compiler_params=_params("parallel"),
        name="swa_mixer",
    )(proj, proj, proj, q_gain_t, k_gain_t, sink, bias)


def _dil_kernel(q_ref, k_ref, v_ref, qg_ref, kg_ref, bias_ref, o_ref,
                qn_ref, kn_ref, vn_ref, ob_ref, lb_ref, bm_ref):
    seq = q_ref.shape[1]
    lanes = q_ref.shape[2]
    heads = lanes // HEAD_DIM
    scale = HEAD_DIM ** -0.5 * LOG2E

    def prep(i, carry):
        r0 = pl.multiple_of(i * BLK, BLK)
        q = q_ref[0, pl.ds(r0, BLK), :].astype(F32)
        k = k_ref[0, pl.ds(r0, BLK), :].astype(F32)
        qn_ref[pl.ds(r0, BLK), :] = _head_rms(q, qg_ref[...]) * scale
        kn_ref[pl.ds(r0, BLK), :] = _head_rms(k, kg_ref[...])
        vn_ref[pl.ds(r0, BLK), :] = v_ref[0, pl.ds(r0, BLK), :].astype(F32)
        return carry

    lax.fori_loop(0, seq // BLK, prep, 0, unroll=PREP_UNROLL)

    for br, (window, r) in enumerate(DIL_PAIRS):
        band, kcol = _band_mask(window // r)
        for h in range(heads):
            bias = bias_ref[br, h] * LOG2E
            bm_ref[(br * heads + h) * 2] = jnp.where(band & (kcol >= BLK), bias, NEG)
            bm_ref[(br * heads + h) * 2 + 1] = jnp.where(band, bias, NEG)

    lane_head = lax.broadcasted_iota(jnp.int32, (1, lanes), 1) // HEAD_DIM

    for br, (window, r) in enumerate(DIL_PAIRS):
        nb = seq // r // BLK

        def blocks(it, carry, br=br, r=r, nb=nb):
            qs, kws, vws, bms, curs = [], [], [], [], []
            for u in range(DIL_TILES):
                t = it * DIL_TILES + u
                c = t // nb
                i = t - c * nb
                cur = c + i * (BLK * r)
                prev = jnp.maximum(cur - BLK * r, c)
                later = jnp.minimum(i, 1)

                def rows(ref, start):
                    if r == 1:
                        return ref[pl.ds(start, BLK), :]
                    return ref[pl.ds(start, BLK, stride=r), :]

                q = rows(qn_ref, cur).astype(BF16)
                if nb == 1:
                    kw = rows(kn_ref, cur).astype(BF16)
                    vw = rows(vn_ref, cur).astype(BF16)
                else:
                    kw = jnp.concatenate([rows(kn_ref, prev), rows(kn_ref, cur)],
                                         axis=0).astype(BF16)
                    vw = jnp.concatenate([rows(vn_ref, prev), rows(vn_ref, cur)],
                                         axis=0).astype(BF16)
                curs.append(cur)
                for h in range(heads):
                    qs.append(jnp.where(lane_head == h, q, jnp.zeros_like(q)))
                    kws.append(kw)
                    vws.append(vw)
                    if nb == 1:
                        bms.append(bm_ref[(br * heads + h) * 2, :, BLK:])
                    else:
                        bms.append(bm_ref[(br * heads + h) * 2 + later])
            outs, ms, dens = _attend(qs, kws, vws, bms)
            for u in range(DIL_TILES):
                o_all = outs[u * heads]
                l_all = ms[u * heads] + jnp.log2(dens[u * heads])
                for h in range(1, heads):
                    o_all = jnp.where(lane_head == h, outs[u * heads + h], o_all)
                    l_all = jnp.where(lane_head == h,
                                      ms[u * heads + h] + jnp.log2(dens[u * heads + h]), l_all)
                if r == 1:
                    ob_ref[br, pl.ds(curs[u], BLK), :] = o_all
                    lb_ref[br, pl.ds(curs[u], BLK), :] = l_all
                else:
                    ob_ref[br, pl.ds(curs[u], BLK, stride=r), :] = o_all
                    lb_ref[br, pl.ds(curs[u], BLK, stride=r), :] = l_all
            return carry

        lax.fori_loop(0, r * nb // DIL_TILES, blocks, 0)

    def combine(i, carry):
        r0 = pl.multiple_of(i * BLK, BLK)
        l0 = lb_ref[0, pl.ds(r0, BLK), :]
        l1 = lb_ref[1, pl.ds(r0, BLK), :]
        l2 = lb_ref[2, pl.ds(r0, BLK), :]
        m = jnp.maximum(jnp.maximum(l0, l1), l2)
        e0, e1, e2 = jnp.exp2(l0 - m), jnp.exp2(l1 - m), jnp.exp2(l2 - m)
        tot = e0 + e1 + e2
        o = ((e0 / tot) * ob_ref[0, pl.ds(r0, BLK), :]
             + (e1 / tot) * ob_ref[1, pl.ds(r0, BLK), :]
             + (e2 / tot) * ob_ref[2, pl.ds(r0, BLK), :])
        o_ref[0, pl.ds(r0, BLK), :] = o.astype(o_ref.dtype)
        return carry

    lax.fori_loop(0, seq // BLK, combine, 0, unroll=PREP_UNROLL)


def _dil_mixer(proj, q_gain, k_gain, bias):
    bsz, seq, _ = proj.shape
    lanes = 128
    heads = lanes // HEAD_DIM
    nblk = GROUP_WIDTH // lanes
    base = OFF_DIL // lanes
    gq = jnp.tile(q_gain, heads).reshape(1, lanes)
    gk = jnp.tile(k_gain, heads).reshape(1, lanes)

    def col(seg):
        return pl.BlockSpec((1, seq, lanes), lambda b, j: (b, 0, base + seg * nblk + j))

    return pl.pallas_call(
        _dil_kernel,
        grid=(bsz, nblk),
        in_specs=[
            col(0), col(1), col(2),
            pl.BlockSpec((1, lanes), lambda b, j: (0, 0)),
            pl.BlockSpec((1, lanes), lambda b, j: (0, 0)),
            pl.BlockSpec((len(DIL_PAIRS), heads, BLK, 2 * BLK), lambda b, j: (0, j, 0, 0)),
        ],
        out_specs=pl.BlockSpec((1, seq, lanes), lambda b, j: (b, 0, j)),
        out_shape=jax.ShapeDtypeStruct((bsz, seq, GROUP_WIDTH), BF16),
        scratch_shapes=[
            pltpu.VMEM((seq, lanes), F32),
            pltpu.VMEM((seq, lanes), F32),
            pltpu.VMEM((seq, lanes), F32),
            pltpu.VMEM((len(DIL_PAIRS), seq, lanes), F32),
            pltpu.VMEM((len(DIL_PAIRS), seq, lanes), F32),
            pltpu.VMEM((len(DIL_PAIRS) * heads * 2, BLK, 2 * BLK), F32),
        ],
        compiler_params=_params("parallel", "parallel"),
        name="dil_mixer",
    )(proj, proj, proj, gq, gk, bias)


def _rwkv_maps_kernel(*refs, has_vres):
    if has_vres:
        (p_ref, pp_ref, mu_ref, w0_ref, w2_ref, a0_ref, a2_ref, g2_ref, kk_ref, ka_ref, rk_ref,
         vf_ref, v0_ref, v1_ref, v2_ref,
         q_out, y1_out, m_out, gm_out, bonus_out, gate_out, *scratch) = refs
    else:
        (p_ref, pp_ref, mu_ref, w0_ref, w2_ref, a0_ref, a2_ref, g2_ref, kk_ref, ka_ref, rk_ref,
         q_out, y1_out, m_out, gm_out, bonus_out, gate_out, v_out, *scratch) = refs
    w = GROUP_WIDTH
    p = p_ref[0].astype(F32)
    row = lax.broadcasted_iota(jnp.int32, p.shape, 0)
    last_prev = pp_ref[0, PREV_ROWS - 1:PREV_ROWS, :].astype(F32)
    last_prev = jnp.where(pl.program_id(1) > 0, last_prev, 0.0)
    prev = jnp.where(row >= 1, pltpu.roll(p, 1, axis=0), last_prev)
    xs = p + (prev - p) * mu_ref[...]
    grows = WKV_GROUP * WKV_CHUNK
    for grp in range(p.shape[0] // grows):
        rs = slice(grp * grows, (grp + 1) * grows)
        x = xs[rs]
        r = x[:, 0:w]
        k = x[:, w:2 * w]
        v = x[:, 2 * w:3 * w]
        o = 3 * w
        wd = x[:, o:o + DECAY_LORA]
        ad = x[:, o + DECAY_LORA:o + DECAY_LORA + ICLR_LORA]
        gd = x[:, o + DECAY_LORA + ICLR_LORA:]

        z = -(w0_ref[...] + _mm(jnp.tanh(wd), w2_ref[...]))
        softplus = jnp.maximum(z, 0.0) + jnp.log(1.0 + jnp.exp(-jnp.abs(z)))
        logw = -softplus - 0.5
        lw = -jnp.exp(logw)
        a = _sigmoid(a0_ref[...] + _mm(ad, a2_ref[...]))
        gate_out[0, rs] = _mm(_sigmoid(gd), g2_ref[...])
        if has_vres:
            mix = _sigmoid(v0_ref[...] + _mm(_mm(v, v1_ref[...]), v2_ref[...]))
            v = v + (vf_ref[0, rs] - v) * mix
        else:
            v_out[0, rs] = v
        kk = k * kk_ref[...]
        ss = _head_sums(kk * kk)
        kk = kk * lax.rsqrt(jnp.maximum(ss, 1e-24))
        k = k * (1.0 + (a - 1.0) * ka_ref[...])
        bonus_out[0, rs] = _head_sums(r * k * rk_ref[...]) * v
        _chunk_maps(grp, r, lw, k, v, kk, kk * a, q_out.at[0], y1_out.at[0], m_out.at[0],
                    gm_out.at[0], *scratch)


def _rwkv_maps(proj, mu, w0, w2, a0, a2, g2, k_k, k_a, r_k, v_first, vres, ts=256):
    bsz, seq, _ = proj.shape
    w = GROUP_WIDTH
    nt = seq // ts
    has_vres = vres is not None

    def full(shape):
        return pl.BlockSpec(shape, lambda b, i: (0,) * len(shape))

    row = lambda a: a.reshape(1, -1)
    tile = pl.BlockSpec((1, ts, w), lambda b, i: (b, i, 0))
    in_specs = [
        pl.BlockSpec((pl.Element(1), pl.Element(ts), pl.Element(RWKV_IN_WIDTH)),
                     lambda b, i: (b, i * ts, OFF_RW)),
        pl.BlockSpec((pl.Element(1), pl.Element(PREV_ROWS), pl.Element(RWKV_IN_WIDTH)),
                     lambda b, i: (b, jnp.maximum(i * (ts // PREV_ROWS) - 1, 0) * PREV_ROWS, OFF_RW)),
        full((1, RWKV_IN_WIDTH)), full((1, w)), full((DECAY_LORA, w)), full((1, w)),
        full((ICLR_LORA, w)), full((GATE_LORA, w)), full((1, w)), full((1, w)), full((1, w)),
    ]
    args = [proj, proj, row(mu), row(w0), w2, row(a0), a2, g2, row(k_k), row(k_a), row(r_k)]
    if has_vres:
        v0, v1, v2 = vres
        in_specs += [tile, full((1, w)), full(v1.shape), full(v2.shape)]
        args += [v_first, row(v0), v1, v2]
    f32 = jax.ShapeDtypeStruct((bsz, seq, w), F32)
    bf16 = jax.ShapeDtypeStruct((bsz, seq, w), BF16)
    out_shape = [bf16, f32, bf16, f32, f32, f32] + ([] if has_vres else [f32])
    return pl.pallas_call(
        functools.partial(_rwkv_maps_kernel, has_vres=has_vres),
        grid=(bsz, nt),
        in_specs=in_specs,
        out_specs=[tile] * len(out_shape),
        out_shape=out_shape,
        scratch_shapes=[pltpu.VMEM((ts, w), F32)] * 5 + [pltpu.VMEM((ts, w), BF16)],
        compiler_params=_params("parallel", "parallel"),
        name="rwkv_chunk_maps",
    )(*args)


def _chunk_maps(grp, r, lw, k, v, kk, b, q_out, y1_out, m_out, g_out,
                at_ref, rt_ref, bt_ref, kt_ref, dec_ref, vb_ref):
    c = WKV_CHUNK
    n = HEAD_DIM
    rows = r.shape[0]
    rs = slice(grp * rows, (grp + 1) * rows)

    row = lax.broadcasted_iota(jnp.int32, (rows, rows), 0)
    col = lax.broadcasted_iota(jnp.int32, (rows, rows), 1)
    tri = ((row >= col) & ((row // c) == (col // c))).astype(BF16)
    lw_hi, lw_lo = _split_bf16(lw)
    cum = (jnp.dot(tri, lw_hi, preferred_element_type=F32)
           + jnp.dot(tri, lw_lo, preferred_element_type=F32))
    e_pos = jnp.exp(cum)
    e_neg = jnp.exp(-cum)
    at_ref[rs] = -kk * jnp.exp(cum - lw)
    rt_ref[rs] = r * e_pos
    bt_ref[rs] = b * e_neg
    kt_ref[rs] = k * e_neg
    dec_ref[rs] = e_pos
    vb_ref[rs] = v.astype(BF16)

    crow = lax.broadcasted_iota(jnp.int32, (c, c), 0)
    ccol = lax.broadcasted_iota(jnp.int32, (c, c), 1)
    lower = crow >= ccol
    strict = crow > ccol
    eye = (crow == ccol).astype(F32)
    wrow = lax.broadcasted_iota(jnp.int32, (c, 2 * c), 0)
    wcol = lax.broadcasted_iota(jnp.int32, (c, 2 * c), 1)
    strict_left = wrow > wcol
    right = wcol >= c
    eye_right = (wcol == wrow + c).astype(F32)

    def tile(ref, j, h, bf16=True):
        x = ref[j * c:(j + 1) * c, h * n:(h + 1) * n]
        return x.astype(BF16) if bf16 else x

    _chunk_group(range(grp * WKV_GROUP, (grp + 1) * WKV_GROUP), tile, at_ref, rt_ref, bt_ref,
                 kt_ref, dec_ref, vb_ref, q_out, y1_out, m_out, g_out,
                 lower, strict, eye, strict_left, right, eye_right)


def _chunk_group(chunks, tile, at_ref, rt_ref, bt_ref, kt_ref, dec_ref, vb_ref,
                 q_out, y1_out, m_out, g_out, lower, strict, eye, strict_left, right, eye_right):
    c = WKV_CHUNK
    n = HEAD_DIM
    pairs = [(j, h) for j in chunks for h in range(N_HEADS)]
    ah = [tile(at_ref, j, h) for j, h in pairs]
    rh = [tile(rt_ref, j, h, bf16=False) for j, h in pairs]
    bh = [tile(bt_ref, j, h) for j, h in pairs]
    kh = [tile(kt_ref, j, h) for j, h in pairs]
    vh = [vb_ref[j * c:(j + 1) * c, h * n:(h + 1) * n] for j, h in pairs]
    dh = [dec_ref[(j + 1) * c - 1:(j + 1) * c, h * n:(h + 1) * n] for j, h in pairs]
    idx = range(len(pairs))
    ar = [jnp.concatenate([ah[i], rh[i].astype(BF16)], axis=0) for i in idx]
    bk = [jnp.concatenate([bh[i], kh[i]], axis=0) for i in idx]
    aa = [_mm_nt(ar[i], bk[i]) for i in idx]
    a_kr = [jnp.concatenate([jnp.where(strict, aa[i][:c, c:], 0.0),
                             jnp.where(lower, aa[i][c:, c:], 0.0)], axis=0).astype(BF16)
            for i in idx]
    a_rb = [jnp.where(lower, aa[i][c:, :c], 0.0).astype(BF16) for i in idx]
    ps = [jnp.where(strict_left, aa[i][:c, :], 0.0) + eye_right for i in idx]
    for _ in range(int(math.log2(c))):
        ps = [_mm(ps[i][:, :c], ps[i]) + jnp.where(right, ps[i], 0.0) for i in idx]
    t = [ps[i][:, c:].astype(BF16) for i in idx]
    av = [_mm(a_kr[i], vh[i]) for i in idx]
    wu = [_mm(t[i], jnp.concatenate([ah[i], av[i][:c].astype(BF16)], axis=1)).astype(BF16)
          for i in idx]
    aw = [_mm(a_rb[i], wu[i]) for i in idx]
    q = [rh[i] + aw[i][:, :n] for i in idx]
    y1 = [aw[i][:, n:] + av[i][c:] for i in idx]
    zero = jnp.zeros((c, n), BF16)
    mg = [_mm_tn(jnp.concatenate([wu[i], jnp.concatenate([zero, vh[i]], axis=1)], axis=0), bk[i])
          for i in idx]
    m = [(eye + mg[i][:n]) * dh[i] for i in idx]
    g = [mg[i][n:] * dh[i] for i in idx]
    for j in chunks:
        sel = slice((j - chunks[0]) * N_HEADS, (j - chunks[0] + 1) * N_HEADS)
        q_out[j * c:(j + 1) * c, :] = jnp.concatenate(q[sel], axis=-1).astype(q_out.dtype)
        y1_out[j * c:(j + 1) * c, :] = jnp.concatenate(y1[sel], axis=-1)
        m_out[j * c:(j + 1) * c, :] = jnp.concatenate(m[sel], axis=-1).astype(m_out.dtype)
        g_out[j * c:(j + 1) * c, :] = jnp.concatenate(g[sel], axis=-1)


def _wkv_scan_kernel(q_ref, y1_ref, m_ref, gm_ref, bonus_ref, gate_ref, lng_ref, lnb_ref,
                     o_ref, st_ref):
    n = HEAD_DIM
    bsz = q_ref.shape[0]

    @pl.when(pl.program_id(0) == 0)
    def _():
        st_ref[...] = jnp.zeros(st_ref.shape, F32)

    for bi in range(bsz):
        ys = []
        for h in range(N_HEADS):
            sl = slice(h * n, (h + 1) * n)
            s0 = st_ref[bi * N_HEADS + h].astype(BF16)
            ys.append(_mm_nt(q_ref[bi, :, sl], s0) + y1_ref[bi, :, sl])
            st_ref[bi * N_HEADS + h] = _mm(s0, m_ref[bi, :, sl]) + gm_ref[bi, :, sl]
        y = jnp.concatenate(ys, axis=-1)
        mean = _head_sums(y) * (1.0 / n)
        yc = y - mean
        var = _head_sums(yc * yc) * (1.0 / n)
        yn = yc * lax.rsqrt(var + LN_X_EPS) * lng_ref[...] + lnb_ref[...]
        o_ref[bi] = ((yn + bonus_ref[bi]) * gate_ref[bi]).astype(o_ref.dtype)


def _wkv_scan(q, y1, m, gm, bonus, gate, ln_g, ln_b):
    bsz, seq, w = y1.shape
    c = WKV_CHUNK
    tile = pl.BlockSpec((bsz, c, w), lambda ci: (0, ci, 0))
    vec = pl.BlockSpec((1, w), lambda ci: (0, 0))
    return pl.pallas_call(
        _wkv_scan_kernel,
        grid=(seq // c,),
        in_specs=[tile] * 6 + [vec] * 2,
        out_specs=tile,
        out_shape=jax.ShapeDtypeStruct((bsz, seq, w), BF16),
        scratch_shapes=[pltpu.VMEM((bsz * N_HEADS, HEAD_DIM, HEAD_DIM), F32)],
        compiler_params=_params("arbitrary"),
        name="wkv_state_scan",
    )(q, y1, m, gm, bonus, gate, ln_g.reshape(1, w), ln_b.reshape(1, w))


def _wout_kernel(x_ref, y0_ref, y1_ref, y2_ref, y3_ref, w_ref, o_ref):
    acc = x_ref[...]
    for idx, y_ref in enumerate((y0_ref, y1_ref, y2_ref, y3_ref)):
        acc = acc + jnp.dot(y_ref[...], w_ref[idx * GROUP_WIDTH:(idx + 1) * GROUP_WIDTH, :],
                            preferred_element_type=F32)
    o_ref[...] = acc


def _wout(x2d, ys, w_bf16, tm=512):
    m, d = x2d.shape
    ytile = pl.BlockSpec((tm, GROUP_WIDTH), lambda i: (i, 0))
    xtile = pl.BlockSpec((tm, d), lambda i: (i, 0))
    return pl.pallas_call(
        _wout_kernel,
        grid=(m // tm,),
        in_specs=[xtile] + [ytile] * 4 + [pl.BlockSpec(w_bf16.shape, lambda i: (0, 0))],
        out_specs=xtile,
        out_shape=jax.ShapeDtypeStruct((m, d), F32),
        compiler_params=_params("parallel"),
        name="wout_residual",
    )(x2d, *ys, w_bf16)


def _ffn_kernel(x_ref, g_ref, wu_ref, wd_ref, o_ref, h_ref):
    @pl.when(pl.program_id(1) == 0)
    def _():
        x = x_ref[...]
        ms = jnp.mean(x * x, axis=-1, keepdims=True)
        h_ref[...] = (x * lax.rsqrt(ms + RMS_EPS) * g_ref[0]).astype(BF16)
        o_ref[...] = x

    u = jnp.dot(h_ref[...], wu_ref[0].astype(BF16), preferred_element_type=F32)
    act = jnp.square(jnp.maximum(u, 0.0)).astype(BF16)
    o_ref[...] += jnp.dot(act, wd_ref[0].astype(BF16), preferred_element_type=F32)


def _ffn(x2d, gains, wu_stack, wd_stack, layer, tm=1024, tf=512):
    m, d = x2d.shape
    f = wu_stack.shape[2]
    xtile = pl.BlockSpec((tm, d), lambda i, j: (i, 0))
    return pl.pallas_call(
        _ffn_kernel,
        grid=(m // tm, f // tf),
        in_specs=[
            xtile,
            pl.BlockSpec((1, 1, d), lambda i, j: (layer, 0, 0)),
            pl.BlockSpec((1, d, tf), lambda i, j: (layer, 0, j)),
            pl.BlockSpec((1, tf, d), lambda i, j: (layer, j, 0)),
        ],
        out_specs=xtile,
        out_shape=jax.ShapeDtypeStruct((m, d), F32),
        scratch_shapes=[pltpu.VMEM((tm, d), BF16)],
        compiler_params=_params("parallel", "arbitrary", vmem_limit=FFN_VMEM_LIMIT),
        name="ffn",
    )(x2d, gains.reshape(-1, 1, d), wu_stack, wd_stack)


def kernel(x, norm_mix, w_in, conv_w, swa_q_norm, swa_k_norm, swa_sink, dil_q_norm, dil_k_norm,
           rwkv_mu, decay_w0, decay_w2, iclr_a0, iclr_a2, gate_g2, k_k, k_a, r_k, ln_x_g, ln_x_b,
           vres_v0, vres_v1, vres_v2, w_out, norm_ffn, w_up, w_down, rel_bias):
    bsz, seq, d = x.shape
    depth = w_in.shape[0]
    bias = _bias_tiles(rel_bias)
    swa_bias = bias[0, :N_HEADS]
    dil_bias = bias[:, N_HEADS:]
    x2d = x.reshape(bsz * seq, d)
    v_first = None
    for layer in range(depth):
        proj = _norm_matmul(x2d, norm_mix, w_in, layer).reshape(bsz, seq, IN_WIDTH)
        y_conv = _conv_mixer(proj, conv_w[layer])
        y_swa = _swa_mixer(proj, swa_q_norm[layer], swa_k_norm[layer], swa_sink[layer], swa_bias)
        y_dil = _dil_mixer(proj, dil_q_norm[layer], dil_k_norm[layer], dil_bias)
        vres = None if layer == 0 else (vres_v0[layer - 1], vres_v1[layer - 1], vres_v2[layer - 1])
        maps = _rwkv_maps(
            proj, rwkv_mu[layer], decay_w0[layer], decay_w2[layer], iclr_a0[layer], iclr_a2[layer],
            gate_g2[layer], k_k[layer], k_a[layer], r_k[layer], v_first, vres)
        if layer == 0:
            v_first = maps[6]
        y_rwkv = _wkv_scan(*maps[:6], ln_x_g[layer], ln_x_b[layer])
        ys = [y.reshape(bsz * seq, GROUP_WIDTH) for y in (y_conv, y_swa, y_dil, y_rwkv)]
        x2d = _wout(x2d, ys, w_out[layer].astype(BF16))
        x2d = _ffn(x2d, norm_ffn, w_up, w_down, layer)
    return x2d.reshape(bsz, seq, d)
```

```python
import functools
import math

import jax
import jax.numpy as jnp
from jax import lax
from jax.experimental import pallas as pl
from jax.experimental.pallas import tpu as pltpu

F32 = jnp.float32
BF16 = jnp.bfloat16

D_MODEL = 2048
HEAD_DIM = 64
GROUP_WIDTH = 512
N_HEADS = GROUP_WIDTH // HEAD_DIM
SWA_KV_HEADS = 2
SWA_GROUP = N_HEADS // SWA_KV_HEADS
SWA_WINDOW = 128
DIL_PAIRS = ((128, 1), (512, 4), (2048, 16))
DECAY_LORA = 64
ICLR_LORA = 64
GATE_LORA = 128
RWKV_IN_WIDTH = 3 * GROUP_WIDTH + DECAY_LORA + ICLR_LORA + GATE_LORA
BLK = 128
NUM_BUCKETS = 32
BUCKET_MAX_DIST = 128
RMS_EPS = 1e-6
LN_X_EPS = 64e-5
NEG = -1e30
LOG2E = math.log2(math.e)
WKV_CHUNK = 64
WKV_GROUP = 4
DIL_TILES = 4
PREV_ROWS = 16
PREP_UNROLL = 4

OFF_CONV = 0
OFF_SWA_Q = OFF_CONV + 3 * GROUP_WIDTH
OFF_SWA_K = OFF_SWA_Q + GROUP_WIDTH
OFF_SWA_V = OFF_SWA_K + SWA_KV_HEADS * HEAD_DIM
OFF_DIL = OFF_SWA_V + SWA_KV_HEADS * HEAD_DIM
OFF_RW = OFF_DIL + 3 * GROUP_WIDTH
IN_WIDTH = OFF_RW + RWKV_IN_WIDTH

V7X_VMEM_BYTES = 64 * 1024 * 1024
VMEM_LIMIT = 48 * 1024 * 1024
FFN_VMEM_LIMIT = V7X_VMEM_BYTES - 6 * 1024 * 1024


def _params(*sem, vmem_limit=VMEM_LIMIT):
    return pltpu.CompilerParams(dimension_semantics=sem, vmem_limit_bytes=vmem_limit)


def _mm(a, b):
    return jnp.dot(a.astype(BF16), b.astype(BF16), preferred_element_type=F32)


def _mm_nt(a, b):
    return lax.dot_general(a.astype(BF16), b.astype(BF16), (((1,), (1,)), ((), ())),
                           preferred_element_type=F32)


def _mm_tn(a, b):
    return lax.dot_general(a.astype(BF16), b.astype(BF16), (((0,), (0,)), ((), ())),
                           preferred_element_type=F32)


def _split_bf16(x):
    hi = x.astype(BF16)
    lo = (x - hi.astype(F32)).astype(BF16)
    return hi, lo


def _head_sums(x):
    lanes = 128
    r = lax.broadcasted_iota(jnp.int32, (lanes, lanes), 0) // HEAD_DIM
    c = lax.broadcasted_iota(jnp.int32, (lanes, lanes), 1) // HEAD_DIM
    bd = (r == c).astype(BF16)
    hi, lo = _split_bf16(x)
    cols = []
    for j in range(x.shape[-1] // lanes):
        sl = slice(j * lanes, (j + 1) * lanes)
        cols.append(jnp.dot(hi[:, sl], bd, preferred_element_type=F32)
                    + jnp.dot(lo[:, sl], bd, preferred_element_type=F32))
    return cols[0] if len(cols) == 1 else jnp.concatenate(cols, axis=-1)


def _sigmoid(z):
    return 1.0 / (1.0 + jnp.exp(-z))


def _norm_matmul_kernel(x_ref, g_ref, w_ref, o_ref, h_ref):
    @pl.when(pl.program_id(1) == 0)
    def _():
        x = x_ref[...]
        ms = jnp.mean(x * x, axis=-1, keepdims=True)
        h_ref[...] = (x * lax.rsqrt(ms + RMS_EPS) * g_ref[0]).astype(BF16)

    o_ref[...] = jnp.dot(h_ref[...], w_ref[0].astype(BF16),
                         preferred_element_type=F32).astype(o_ref.dtype)


def _norm_matmul(x2d, gains, w_stack, layer, tm=1024, tn=1408):
    m, k = x2d.shape
    n = w_stack.shape[2]
    assert n % tn == 0 and m % tm == 0
    return pl.pallas_call(
        _norm_matmul_kernel,
        grid=(m // tm, n // tn),
        in_specs=[
            pl.BlockSpec((tm, k), lambda i, j: (i, 0)),
            pl.BlockSpec((1, 1, k), lambda i, j: (layer, 0, 0)),
            pl.BlockSpec((1, k, tn), lambda i, j: (layer, 0, j)),
        ],
        out_specs=pl.BlockSpec((tm, tn), lambda i, j: (i, j)),
        out_shape=jax.ShapeDtypeStruct((m, n), BF16),
        scratch_shapes=[pltpu.VMEM((tm, k), BF16)],
        compiler_params=_params("parallel", "arbitrary", vmem_limit=FFN_VMEM_LIMIT),
        name="norm_matmul",
    )(x2d, gains.reshape(-1, 1, k), w_stack)


def _t5_bucket(dist):
    dist = jnp.maximum(dist, 0)
    max_exact = NUM_BUCKETS // 2
    scaled = (jnp.log(jnp.maximum(dist, 1).astype(F32) / max_exact)
              / math.log(BUCKET_MAX_DIST / max_exact))
    large = max_exact + (scaled * (NUM_BUCKETS - max_exact)).astype(jnp.int32)
    large = jnp.minimum(large, NUM_BUCKETS - 1)
    return jnp.where(dist < max_exact, dist, large)


def _bias_kernel(bucket_ref, table_ref, o_ref):
    h = pl.program_id(1)
    bucket = bucket_ref[0]
    acc = jnp.zeros(bucket.shape, F32)
    for b in range(NUM_BUCKETS):
        acc = jnp.where(bucket == b, table_ref[b, h], acc)
    o_ref[0, 0] = acc


def _bias_tiles(rel_bias):
    dist = BLK + jnp.arange(BLK)[:, None] - jnp.arange(2 * BLK)[None, :]
    buckets = jnp.stack([_t5_bucket(dist * r) for _, r in DIL_PAIRS]).astype(jnp.int32)
    nh = rel_bias.shape[1]
    return pl.pallas_call(
        _bias_kernel,
        grid=(len(DIL_PAIRS), nh),
        in_specs=[
            pl.BlockSpec((1, BLK, 2 * BLK), lambda s, h: (s, 0, 0)),
            pl.BlockSpec(memory_space=pltpu.SMEM),
        ],
        out_specs=pl.BlockSpec((1, 1, BLK, 2 * BLK), lambda s, h: (s, h, 0, 0)),
        out_shape=jax.ShapeDtypeStruct((len(DIL_PAIRS), nh, BLK, 2 * BLK), F32),
        compiler_params=_params("arbitrary", "arbitrary"),
        name="bias_tiles",
    )(buckets, rel_bias)


def _conv_kernel(b_ref, c_ref, u_ref, w_ref, o_ref):
    z = c_ref[0].astype(F32) * u_ref[0].astype(F32)
    row = lax.broadcasted_iota(jnp.int32, z.shape, 0)
    z1 = jnp.where(row >= 1, pltpu.roll(z, 1, axis=0), 0.0)
    z2 = jnp.where(row >= 2, pltpu.roll(z, 2, axis=0), 0.0)
    w = w_ref[...]
    y = z2 * w[0:1, :] + z1 * w[1:2, :] + z * w[2:3, :]
    o_ref[0] = (b_ref[0].astype(F32) * y).astype(o_ref.dtype)


def _conv_mixer(proj, conv_w):
    bsz, seq, _ = proj.shape
    lanes = 128
    nblk = GROUP_WIDTH // lanes
    base = OFF_CONV // lanes

    def col(seg):
        return pl.BlockSpec((1, seq, lanes), lambda b, j: (b, 0, base + seg * nblk + j))

    return pl.pallas_call(
        _conv_kernel,
        grid=(bsz, nblk),
        in_specs=[col(0), col(1), col(2), pl.BlockSpec((3, lanes), lambda b, j: (0, j))],
        out_specs=pl.BlockSpec((1, seq, lanes), lambda b, j: (b, 0, j)),
        out_shape=jax.ShapeDtypeStruct((bsz, seq, GROUP_WIDTH), BF16),
        compiler_params=_params("parallel", "parallel"),
        name="conv_mixer",
    )(proj, proj, proj, conv_w)


def _band_mask(max_dist):
    a = lax.broadcasted_iota(jnp.int32, (BLK, 2 * BLK), 0)
    b = lax.broadcasted_iota(jnp.int32, (BLK, 2 * BLK), 1)
    dist = BLK + a - b
    return (dist >= 0) & (dist <= max_dist), b


def _attend(qs, kws, vws, biases, sinks=None):
    idx = range(len(qs))
    s = [lax.dot_general(qs[i], kws[i], (((1,), (1,)), ((), ())), preferred_element_type=F32)
         + biases[i] for i in idx]
    m = [jnp.max(s[i], axis=-1, keepdims=True) for i in idx]
    if sinks is not None:
        m = [jnp.maximum(m[i], sinks[i]) for i in idx]
    p = [jnp.exp2(s[i] - m[i]) for i in idx]
    den = [jnp.sum(p[i], axis=-1, keepdims=True) for i in idx]
    if sinks is not None:
        den = [den[i] + jnp.exp2(sinks[i] - m[i]) for i in idx]
    o = [jnp.dot(p[i].astype(BF16), vws[i], preferred_element_type=F32) / den[i] for i in idx]
    return o, m, den


def _head_rms(x, gain):
    ms = _head_sums(x * x) * (1.0 / HEAD_DIM)
    return x * lax.rsqrt(ms + RMS_EPS) * gain


def _swa_kernel(q_ref, k_ref, v_ref, qg_ref, kg_ref, sink_ref, bias_ref, o_ref,
                qn_ref, kn_ref, vb_ref, bm_ref):
    seq = q_ref.shape[1]
    nb = seq // BLK
    kvw = SWA_KV_HEADS * HEAD_DIM
    scale = HEAD_DIM ** -0.5 * LOG2E

    kn_ref[0:BLK, :] = jnp.zeros((BLK, 2 * kvw), BF16)
    vb_ref[0:BLK, :] = jnp.zeros((BLK, 2 * kvw), BF16)
    lane_half = lax.broadcasted_iota(jnp.int32, (1, kvw), 1) // HEAD_DIM

    def both_halves(x):
        xr = pltpu.roll(x, HEAD_DIM, axis=1)
        return jnp.concatenate([jnp.where(lane_half == 0, x, xr),
                                jnp.where(lane_half == 0, xr, x)], axis=-1)

    def prep(i, carry):
        r0 = pl.multiple_of(i * BLK, BLK)
        q = q_ref[0, pl.ds(r0, BLK), :].astype(F32)
        qn_ref[pl.ds(r0, BLK), :] = (_head_rms(q, qg_ref[...]) * scale).astype(BF16)
        k = _head_rms(k_ref[0, pl.ds(r0, BLK), :].astype(F32), kg_ref[...])
        kn_ref[pl.ds(r0 + BLK, BLK), :] = both_halves(k).astype(BF16)
        v = v_ref[0, pl.ds(r0, BLK), :].astype(F32)
        vb_ref[pl.ds(r0 + BLK, BLK), :] = both_halves(v).astype(BF16)
        return carry

    lax.fori_loop(0, nb, prep, 0, unroll=PREP_UNROLL)

    band, kcol = _band_mask(SWA_WINDOW - 1)
    for h in range(N_HEADS):
        bm_ref[h] = jnp.where(band & (kcol >= BLK), bias_ref[h] * LOG2E, NEG)
        bm_ref[N_HEADS + h] = jnp.where(band, bias_ref[h] * LOG2E, NEG)
    sinks = [sink_ref[h] * LOG2E for h in range(N_HEADS)]

    def block(i, carry):
        r0 = pl.multiple_of(i * BLK, BLK)
        later = jnp.minimum(i, 1) * N_HEADS
        qs, kws, vws, bms = [], [], [], []
        for hk in range(SWA_KV_HEADS):
            kw = kn_ref[pl.ds(r0, 2 * BLK), hk * kvw:(hk + 1) * kvw]
            vw = vb_ref[pl.ds(r0, 2 * BLK), hk * kvw:(hk + 1) * kvw]
            for g in range(SWA_GROUP):
                h = hk * SWA_GROUP + g
                q2 = qn_ref[pl.ds(r0, BLK), (h // 2) * kvw:(h // 2 + 1) * kvw]
                qs.append(jnp.where(lane_half == h % 2, q2, jnp.zeros_like(q2)))
                kws.append(kw)
                vws.append(vw)
                bms.append(bm_ref[later + h])
        outs, _, _ = _attend(qs, kws, vws, bms, sinks)
        pairs = [jnp.where(lane_half == 0, outs[h], outs[h + 1]) for h in range(0, N_HEADS, 2)]
        o_ref[0, pl.ds(r0, BLK), :] = jnp.concatenate(pairs, axis=-1).astype(o_ref.dtype)
        return carry

    lax.fori_loop(0, nb, block, 0)


def _swa_mixer(proj, q_gain, k_gain, sink, bias):
    bsz, seq, _ = proj.shape
    kvw = SWA_KV_HEADS * HEAD_DIM
    q_gain_t = jnp.tile(q_gain, N_HEADS).reshape(1, GROUP_WIDTH)
    k_gain_t = jnp.tile(k_gain, SWA_KV_HEADS).reshape(1, kvw)
    return pl.pallas_call(
        _swa_kernel,
        grid=(bsz,),
        in_specs=[
            pl.BlockSpec((1, seq, GROUP_WIDTH), lambda b: (b, 0, OFF_SWA_Q // GROUP_WIDTH)),
            pl.BlockSpec((1, seq, kvw), lambda b: (b, 0, OFF_SWA_K // kvw)),
            pl.BlockSpec((1, seq, kvw), lambda b: (b, 0, OFF_SWA_V // kvw)),
            pl.BlockSpec((1, GROUP_WIDTH), lambda b: (0, 0)),
            pl.BlockSpec((1, kvw), lambda b: (0, 0)),
            pl.BlockSpec(memory_space=pltpu.SMEM),
            pl.BlockSpec((N_HEADS, BLK, 2 * BLK), lambda b: (0, 0, 0)),
        ],
        out_specs=pl.BlockSpec((1, seq, GROUP_WIDTH), lambda b: (b, 0, 0)),
        out_shape=jax.ShapeDtypeStruct((bsz, seq, GROUP_WIDTH), BF16),
        scratch_shapes=[
            pltpu.VMEM((seq, GROUP_WIDTH), BF16),
            pltpu.VMEM((seq + BLK, 2 * kvw), BF16),
            pltpu.VMEM((seq + BLK, 2 * kvw), BF16),
            pltpu.VMEM((2 * N_HEADS, BLK, 2 * BLK), F32),
        ],
        compiler_params=_params("parallel"),
        name="swa_mixer",
    )(proj, proj, proj, q_gain_t, k_gain_t, sink, bias)


def _dil_kernel(q_ref, k_ref, v_ref, qg_ref, kg_ref, bias_ref, o_ref,
                qn_ref, kn_ref, vn_ref, ob_ref, lb_ref, bm_ref):
    seq = q_ref.shape[1]
    lanes = q_ref.shape[2]
    heads = lanes // HEAD_DIM
    scale = HEAD_DIM ** -0.5 * LOG2E

    def prep(i, carry):
        r0 = pl.multiple_of(i * BLK, BLK)
        q = q_ref[0, pl.ds(r0, BLK), :].astype(F32)
        k = k_ref[0, pl.ds(r0, BLK), :].astype(F32)
        qn_ref[pl.ds(r0, BLK), :] = _head_rms(q, qg_ref[...]) * scale
        kn_ref[pl.ds(r0, BLK), :] = _head_rms(k, kg_ref[...])
        vn_ref[pl.ds(r0, BLK), :] = v_ref[0, pl.ds(r0, BLK), :].astype(F32)
        return carry

    lax.fori_loop(0, seq // BLK, prep, 0, unroll=PREP_UNROLL)

    for br, (window, r) in enumerate(DIL_PAIRS):
        band, kcol = _band_mask(window // r)
        for h in range(heads):
            bias = bias_ref[br, h] * LOG2E
            bm_ref[(br * heads + h) * 2] = jnp.where(band & (kcol >= BLK), bias, NEG)
            bm_ref[(br * heads + h) * 2 + 1] = jnp.where(band, bias, NEG)

    lane_head = lax.broadcasted_iota(jnp.int32, (1, lanes), 1) // HEAD_DIM

    for br, (window, r) in enumerate(DIL_PAIRS):
        nb = seq // r // BLK

        def blocks(it, carry, br=br, r=r, nb=nb):
            qs, kws, vws, bms, curs = [], [], [], [], []
            for u in range(DIL_TILES):
                t = it * DIL_TILES + u
                c = t // nb
                i = t - c * nb
                cur = c + i * (BLK * r)
                prev = jnp.maximum(cur - BLK * r, c)
                later = jnp.minimum(i, 1)

                def rows(ref, start):
                    if r == 1:
                        return ref[pl.ds(start, BLK), :]
                    return ref[pl.ds(start, BLK, stride=r), :]

                q = rows(qn_ref, cur).astype(BF16)
                if nb == 1:
                    kw = rows(kn_ref, cur).astype(BF16)
                    vw = rows(vn_ref, cur).astype(BF16)
                else:
                    kw = jnp.concatenate([rows(kn_ref, prev), rows(kn_ref, cur)],
                                         axis=0).astype(BF16)
                    vw = jnp.concatenate([rows(vn_ref, prev), rows(vn_ref, cur)],
                                         axis=0).astype(BF16)
                curs.append(cur)
                for h in range(heads):
                    qs.append(jnp.where(lane_head == h, q, jnp.zeros_like(q)))
                    kws.append(kw)
                    vws.append(vw)
                    if nb == 1:
                        bms.append(bm_ref[(br * heads + h) * 2, :, BLK:])
                    else:
                        bms.append(bm_ref[(br * heads + h) * 2 + later])
            outs, ms, dens = _attend(qs, kws, vws, bms)
            for u in range(DIL_TILES):
                o_all = outs[u * heads]
                l_all = ms[u * heads] + jnp.log2(dens[u * heads])
                for h in range(1, heads):
                    o_all = jnp.where(lane_head == h, outs[u * heads + h], o_all)
                    l_all = jnp.where(lane_head == h,
                                      ms[u * heads + h] + jnp.log2(dens[u * heads + h]), l_all)
                if r == 1:
                    ob_ref[br, pl.ds(curs[u], BLK), :] = o_all
                    lb_ref[br, pl.ds(curs[u], BLK), :] = l_all
                else:
                    ob_ref[br, pl.ds(curs[u], BLK, stride=r), :] = o_all
                    lb_ref[br, pl.ds(curs[u], BLK, stride=r), :] = l_all
            return carry

        lax.fori_loop(0, r * nb // DIL_TILES, blocks, 0)

    def combine(i, carry):
        r0 = pl.multiple_of(i * BLK, BLK)
        l0 = lb_ref[0, pl.ds(r0, BLK), :]
        l1 = lb_ref[1, pl.ds(r0, BLK), :]
        l2 = lb_ref[2, pl.ds(r0, BLK), :]
        m = jnp.maximum(jnp.maximum(l0, l1), l2)
        e0, e1, e2 = jnp.exp2(l0 - m), jnp.exp2(l1 - m), jnp.exp2(l2 - m)
        tot = e0 + e1 + e2
        o = ((e0 / tot) * ob_ref[0, pl.ds(r0, BLK), :]
             + (e1 / tot) * ob_ref[1, pl.ds(r0, BLK), :]
             + (e2 / tot) * ob_ref[2, pl.ds(r0, BLK), :])
        o_ref[0, pl.ds(r0, BLK), :] = o.astype(o_ref.dtype)
        return carry

    lax.fori_loop(0, seq // BLK, combine, 0, unroll=PREP_UNROLL)


def _dil_mixer(proj, q_gain, k_gain, bias):
    bsz, seq, _ = proj.shape
    lanes = 128
    heads = lanes // HEAD_DIM
    nblk = GROUP_WIDTH // lanes
    base = OFF_DIL // lanes
    gq = jnp.tile(q_gain, heads).reshape(1, lanes)
    gk = jnp.tile(k_gain, heads).reshape(1, lanes)

    def col(seg):
        return pl.BlockSpec((1, seq, lanes), lambda b, j: (b, 0, base + seg * nblk + j))

    return pl.pallas_call(
        _dil_kernel,
        grid=(bsz, nblk),
        in_specs=[
            col(0), col(1), col(2),
            pl.BlockSpec((1, lanes), lambda b, j: (0, 0)),
            pl.BlockSpec((1, lanes), lambda b, j: (0, 0)),
            pl.BlockSpec((len(DIL_PAIRS), heads, BLK, 2 * BLK), lambda b, j: (0, j, 0, 0)),
        ],
        out_specs=pl.BlockSpec((1, seq, lanes), lambda b, j: (b, 0, j)),
        out_shape=jax.ShapeDtypeStruct((bsz, seq, GROUP_WIDTH), BF16),
        scratch_shapes=[
            pltpu.VMEM((seq, lanes), F32),
            pltpu.VMEM((seq, lanes), F32),
            pltpu.VMEM((seq, lanes), F32),
            pltpu.VMEM((len(DIL_PAIRS), seq, lanes), F32),
            pltpu.VMEM((len(DIL_PAIRS), seq, lanes), F32),
            pltpu.VMEM((len(DIL_PAIRS) * heads * 2, BLK, 2 * BLK), F32),
        ],
        compiler_params=_params("parallel", "parallel"),
        name="dil_mixer",
    )(proj, proj, proj, gq, gk, bias)


def _rwkv_maps_kernel(*refs, has_vres):
    if has_vres:
        (p_ref, pp_ref, mu_ref, w0_ref, w2_ref, a0_ref, a2_ref, g2_ref, kk_ref, ka_ref, rk_ref,
         vf_ref, v0_ref, v1_ref, v2_ref,
         q_out, y1_out, m_out, gm_out, bonus_out, gate_out, *scratch) = refs
    else:
        (p_ref, pp_ref, mu_ref, w0_ref, w2_ref, a0_ref, a2_ref, g2_ref, kk_ref, ka_ref, rk_ref,
         q_out, y1_out, m_out, gm_out, bonus_out, gate_out, v_out, *scratch) = refs
    w = GROUP_WIDTH
    p = p_ref[0].astype(F32)
    row = lax.broadcasted_iota(jnp.int32, p.shape, 0)
    last_prev = pp_ref[0, PREV_ROWS - 1:PREV_ROWS, :].astype(F32)
    last_prev = jnp.where(pl.program_id(1) > 0, last_prev, 0.0)
    prev = jnp.where(row >= 1, pltpu.roll(p, 1, axis=0), last_prev)
    xs = p + (prev - p) * mu_ref[...]
    grows = WKV_GROUP * WKV_CHUNK
    for grp in range(p.shape[0] // grows):
        rs = slice(grp * grows, (grp + 1) * grows)
        x = xs[rs]
        r = x[:, 0:w]
        k = x[:, w:2 * w]
        v = x[:, 2 * w:3 * w]
        o = 3 * w
        wd = x[:, o:o + DECAY_LORA]
        ad = x[:, o + DECAY_LORA:o + DECAY_LORA + ICLR_LORA]
        gd = x[:, o + DECAY_LORA + ICLR_LORA:]

        z = -(w0_ref[...] + _mm(jnp.tanh(wd), w2_ref[...]))
        softplus = jnp.maximum(z, 0.0) + jnp.log(1.0 + jnp.exp(-jnp.abs(z)))
        logw = -softplus - 0.5
        lw = -jnp.exp(logw)
        a = _sigmoid(a0_ref[...] + _mm(ad, a2_ref[...]))
        gate_out[0, rs] = _mm(_sigmoid(gd), g2_ref[...])
        if has_vres:
            mix = _sigmoid(v0_ref[...] + _mm(_mm(v, v1_ref[...]), v2_ref[...]))
            v = v + (vf_ref[0, rs] - v) * mix
        else:
            v_out[0, rs] = v
        kk = k * kk_ref[...]
        ss = _head_sums(kk * kk)
        kk = kk * lax.rsqrt(jnp.maximum(ss, 1e-24))
        k = k * (1.0 + (a - 1.0) * ka_ref[...])
        bonus_out[0, rs] = _head_sums(r * k * rk_ref[...]) * v
        _chunk_maps(grp, r, lw, k, v, kk, kk * a, q_out.at[0], y1_out.at[0], m_out.at[0],
                    gm_out.at[0], *scratch)


def _rwkv_maps(proj, mu, w0, w2, a0, a2, g2, k_k, k_a, r_k, v_first, vres, ts=256):
    bsz, seq, _ = proj.shape
    w = GROUP_WIDTH
    nt = seq // ts
    has_vres = vres is not None

    def full(shape):
        return pl.BlockSpec(shape, lambda b, i: (0,) * len(shape))

    row = lambda a: a.reshape(1, -1)
    tile = pl.BlockSpec((1, ts, w), lambda b, i: (b, i, 0))
    in_specs = [
        pl.BlockSpec((pl.Element(1), pl.Element(ts), pl.Element(RWKV_IN_WIDTH)),
                     lambda b, i: (b, i * ts, OFF_RW)),
        pl.BlockSpec((pl.Element(1), pl.Element(PREV_ROWS), pl.Element(RWKV_IN_WIDTH)),
                     lambda b, i: (b, jnp.maximum(i * (ts // PREV_ROWS) - 1, 0) * PREV_ROWS, OFF_RW)),
        full((1, RWKV_IN_WIDTH)), full((1, w)), full((DECAY_LORA, w)), full((1, w)),
        full((ICLR_LORA, w)), full((GATE_LORA, w)), full((1, w)), full((1, w)), full((1, w)),
    ]
    args = [proj, proj, row(mu), row(w0), w2, row(a0), a2, g2, row(k_k), row(k_a), row(r_k)]
    if has_vres:
        v0, v1, v2 = vres
        in_specs += [tile, full((1, w)), full(v1.shape), full(v2.shape)]
        args += [v_first, row(v0), v1, v2]
    f32 = jax.ShapeDtypeStruct((bsz, seq, w), F32)
    bf16 = jax.ShapeDtypeStruct((bsz, seq, w), BF16)
    out_shape = [bf16, f32, bf16, f32, f32, f32] + ([] if has_vres else [f32])
    return pl.pallas_call(
        functools.partial(_rwkv_maps_kernel, has_vres=has_vres),
        grid=(bsz, nt),
        in_specs=in_specs,
        out_specs=[tile] * len(out_shape),
        out_shape=out_shape,
        scratch_shapes=[pltpu.VMEM((ts, w), F32)] * 5 + [pltpu.VMEM((ts, w), BF16)],
        compiler_params=_params("parallel", "parallel"),
        name="rwkv_chunk_maps",
    )(*args)


def _chunk_maps(grp, r, lw, k, v, kk, b, q_out, y1_out, m_out, g_out,
                at_ref, rt_ref, bt_ref, kt_ref, dec_ref, vb_ref):
    c = WKV_CHUNK
    n = HEAD_DIM
    rows = r.shape[0]
    rs = slice(grp * rows, (grp + 1) * rows)

    row = lax.broadcasted_iota(jnp.int32, (rows, rows), 0)
    col = lax.broadcasted_iota(jnp.int32, (rows, rows), 1)
    tri = ((row >= col) & ((row // c) == (col // c))).astype(BF16)
    lw_hi, lw_lo = _split_bf16(lw)
    cum = (jnp.dot(tri, lw_hi, preferred_element_type=F32)
           + jnp.dot(tri, lw_lo, preferred_element_type=F32))
    e_pos = jnp.exp(cum)
    e_neg = jnp.exp(-cum)
    at_ref[rs] = -kk * jnp.exp(cum - lw)
    rt_ref[rs] = r * e_pos
    bt_ref[rs] = b * e_neg
    kt_ref[rs] = k * e_neg
    dec_ref[rs] = e_pos
    vb_ref[rs] = v.astype(BF16)

    crow = lax.broadcasted_iota(jnp.int32, (c, c), 0)
    ccol = lax.broadcasted_iota(jnp.int32, (c, c), 1)
    lower = crow >= ccol
    strict = crow > ccol
    eye = (crow == ccol).astype(F32)
    wrow = lax.broadcasted_iota(jnp.int32, (c, 2 * c), 0)
    wcol = lax.broadcasted_iota(jnp.int32, (c, 2 * c), 1)
    strict_left = wrow > wcol
    right = wcol >= c
    eye_right = (wcol == wrow + c).astype(F32)

    def tile(ref, j, h, bf16=True):
        x = ref[j * c:(j + 1) * c, h * n:(h + 1) * n]
        return x.astype(BF16) if bf16 else x

    _chunk_group(range(grp * WKV_GROUP, (grp + 1) * WKV_GROUP), tile, at_ref, rt_ref, bt_ref,
                 kt_ref, dec_ref, vb_ref, q_out, y1_out, m_out, g_out,
                 lower, strict, eye, strict_left, right, eye_right)


def _chunk_group(chunks, tile, at_ref, rt_ref, bt_ref, kt_ref, dec_ref, vb_ref,
                 q_out, y1_out, m_out, g_out, lower, strict, eye, strict_left, right, eye_right):
    c = WKV_CHUNK
    n = HEAD_DIM
    pairs = [(j, h) for j in chunks for h in range(N_HEADS)]
    ah = [tile(at_ref, j, h) for j, h in pairs]
    rh = [tile(rt_ref, j, h, bf16=False) for j, h in pairs]
    bh = [tile(bt_ref, j, h) for j, h in pairs]
    kh = [tile(kt_ref, j, h) for j, h in pairs]
    vh = [vb_ref[j * c:(j + 1) * c, h * n:(h + 1) * n] for j, h in pairs]
    dh = [dec_ref[(j + 1) * c - 1:(j + 1) * c, h * n:(h + 1) * n] for j, h in pairs]
    idx = range(len(pairs))
    ar = [jnp.concatenate([ah[i], rh[i].astype(BF16)], axis=0) for i in idx]
    bk = [jnp.concatenate([bh[i], kh[i]], axis=0) for i in idx]
    aa = [_mm_nt(ar[i], bk[i]) for i in idx]
    a_kr = [jnp.concatenate([jnp.where(strict, aa[i][:c, c:], 0.0),
                             jnp.where(lower, aa[i][c:, c:], 0.0)], axis=0).astype(BF16)
            for i in idx]
    a_rb = [jnp.where(lower, aa[i][c:, :c], 0.0).astype(BF16) for i in idx]
    ps = [jnp.where(strict_left, aa[i][:c, :], 0.0) + eye_right for i in idx]
    for _ in range(int(math.log2(c))):
        ps = [_mm(ps[i][:, :c], ps[i]) + jnp.where(right, ps[i], 0.0) for i in idx]
    t = [ps[i][:, c:].astype(BF16) for i in idx]
    av = [_mm(a_kr[i], vh[i]) for i in idx]
    wu = [_mm(t[i], jnp.concatenate([ah[i], av[i][:c].astype(BF16)], axis=1)).astype(BF16)
          for i in idx]
    aw = [_mm(a_rb[i], wu[i]) for i in idx]
    q = [rh[i] + aw[i][:, :n] for i in idx]
    y1 = [aw[i][:, n:] + av[i][c:] for i in idx]
    zero = jnp.zeros((c, n), BF16)
    mg = [_mm_tn(jnp.concatenate([wu[i], jnp.concatenate([zero, vh[i]], axis=1)], axis=0), bk[i])
          for i in idx]
    m = [(eye + mg[i][:n]) * dh[i] for i in idx]
    g = [mg[i][n:] * dh[i] for i in idx]
    for j in chunks:
        sel = slice((j - chunks[0]) * N_HEADS, (j - chunks[0] + 1) * N_HEADS)
        q_out[j * c:(j + 1) * c, :] = jnp.concatenate(q[sel], axis=-1).astype(q_out.dtype)
        y1_out[j * c:(j + 1) * c, :] = jnp.concatenate(y1[sel], axis=-1)
        m_out[j * c:(j + 1) * c, :] = jnp.concatenate(m[sel], axis=-1).astype(m_out.dtype)
        g_out[j * c:(j + 1) * c, :] = jnp.concatenate(g[sel], axis=-1)


def _wkv_scan_kernel(q_ref, y1_ref, m_ref, gm_ref, bonus_ref, gate_ref, lng_ref, lnb_ref,
                     o_ref, st_ref):
    n = HEAD_DIM
    bsz = q_ref.shape[0]

    @pl.when(pl.program_id(0) == 0)
    def _():
        st_ref[...] = jnp.zeros(st_ref.shape, F32)

    chains = [(bi, h) for bi in range(bsz) for h in range(N_HEADS)]
    s0 = [st_ref[bi * N_HEADS + h].astype(BF16) for bi, h in chains]
    ys = [_mm_nt(q_ref[bi, :, h * n:(h + 1) * n], s0[i]) for i, (bi, h) in enumerate(chains)]
    s1 = [_mm(s0[i], m_ref[bi, :, h * n:(h + 1) * n]) for i, (bi, h) in enumerate(chains)]
    for i, (bi, h) in enumerate(chains):
        st_ref[bi * N_HEADS + h] = s1[i] + gm_ref[bi, :, h * n:(h + 1) * n]
    y = jnp.concatenate(
        [jnp.concatenate(ys[bi * N_HEADS:(bi + 1) * N_HEADS], axis=-1) + y1_ref[bi]
         for bi in range(bsz)], axis=0)
    mean = _head_sums(y) * (1.0 / n)
    yc = y - mean
    var = _head_sums(yc * yc) * (1.0 / n)
    yn = yc * lax.rsqrt(var + LN_X_EPS) * lng_ref[...] + lnb_ref[...]
    c = q_ref.shape[1]
    for bi in range(bsz):
        rows = slice(bi * c, (bi + 1) * c)
        o_ref[bi] = ((yn[rows] + bonus_ref[bi]) * gate_ref[bi]).astype(o_ref.dtype)


def _wkv_scan(q, y1, m, gm, bonus, gate, ln_g, ln_b):
    bsz, seq, w = y1.shape
    c = WKV_CHUNK
    tile = pl.BlockSpec((bsz, c, w), lambda ci: (0, ci, 0))
    vec = pl.BlockSpec((1, w), lambda ci: (0, 0))
    return pl.pallas_call(
        _wkv_scan_kernel,
        grid=(seq // c,),
        in_specs=[tile] * 6 + [vec] * 2,
        out_specs=tile,
        out_shape=jax.ShapeDtypeStruct((bsz, seq, w), BF16),
        scratch_shapes=[pltpu.VMEM((bsz * N_HEADS, HEAD_DIM, HEAD_DIM), F32)],
        compiler_params=_params("arbitrary"),
        name="wkv_state_scan",
    )(q, y1, m, gm, bonus, gate, ln_g.reshape(1, w), ln_b.reshape(1, w))


def _wout_kernel(x_ref, y0_ref, y1_ref, y2_ref, y3_ref, w_ref, o_ref):
    acc = x_ref[...]
    for idx, y_ref in enumerate((y0_ref, y1_ref, y2_ref, y3_ref)):
        acc = acc + jnp.dot(y_ref[...], w_ref[idx * GROUP_WIDTH:(idx + 1) * GROUP_WIDTH, :],
                            preferred_element_type=F32)
    o_ref[...] = acc


def _wout(x2d, ys, w_bf16, tm=512):
    m, d = x2d.shape
    ytile = pl.BlockSpec((tm, GROUP_WIDTH), lambda i: (i, 0))
    xtile = pl.BlockSpec((tm, d), lambda i: (i, 0))
    return pl.pallas_call(
        _wout_kernel,
        grid=(m // tm,),
        in_specs=[xtile] + [ytile] * 4 + [pl.BlockSpec(w_bf16.shape, lambda i: (0, 0))],
        out_specs=xtile,
        out_shape=jax.ShapeDtypeStruct((m, d), F32),
        compiler_params=_params("parallel"),
        name="wout_residual",
    )(x2d, *ys, w_bf16)


def _ffn_kernel(x_ref, g_ref, wu_ref, wd_ref, o_ref, h_ref):
    @pl.when(pl.program_id(1) == 0)
    def _():
        x = x_ref[...]
        ms = jnp.mean(x * x, axis=-1, keepdims=True)
        h_ref[...] = (x * lax.rsqrt(ms + RMS_EPS) * g_ref[0]).astype(BF16)
        o_ref[...] = x

    u = jnp.dot(h_ref[...], wu_ref[0].astype(BF16), preferred_element_type=F32)
    act = jnp.square(jnp.maximum(u, 0.0)).astype(BF16)
    o_ref[...] += jnp.dot(act, wd_ref[0].astype(BF16), preferred_element_type=F32)


def _ffn(x2d, gains, wu_stack, wd_stack, layer, tm=1024, tf=512):
    m, d = x2d.shape
    f = wu_stack.shape[2]
    xtile = pl.BlockSpec((tm, d), lambda i, j: (i, 0))
    return pl.pallas_call(
        _ffn_kernel,
        grid=(m // tm, f // tf),
        in_specs=[
            xtile,
            pl.BlockSpec((1, 1, d), lambda i, j: (layer, 0, 0)),
            pl.BlockSpec((1, d, tf), lambda i, j: (layer, 0, j)),
            pl.BlockSpec((1, tf, d), lambda i, j: (layer, j, 0)),
        ],
        out_specs=xtile,
        out_shape=jax.ShapeDtypeStruct((m, d), F32),
        scratch_shapes=[pltpu.VMEM((tm, d), BF16)],
        compiler_params=_params("parallel", "arbitrary", vmem_limit=FFN_VMEM_LIMIT),
        name="ffn",
    )(x2d, gains.reshape(-1, 1, d), wu_stack, wd_stack)


def kernel(x, norm_mix, w_in, conv_w, swa_q_norm, swa_k_norm, swa_sink, dil_q_norm, dil_k_norm,
           rwkv_mu, decay_w0, decay_w2, iclr_a0, iclr_a2, gate_g2, k_k, k_a, r_k, ln_x_g, ln_x_b,
           vres_v0, vres_v1, vres_v2, w_out, norm_ffn, w_up, w_down, rel_bias):
    bsz, seq, d = x.shape
    depth = w_in.shape[0]
    bias = _bias_tiles(rel_bias)
    swa_bias = bias[0, :N_HEADS]
    dil_bias = bias[:, N_HEADS:]
    x2d = x.reshape(bsz * seq, d)
    v_first = None
    for layer in range(depth):
        proj = _norm_matmul(x2d, norm_mix, w_in, layer).reshape(bsz, seq, IN_WIDTH)
        y_conv = _conv_mixer(proj, conv_w[layer])
        y_swa = _swa_mixer(proj, swa_q_norm[layer], swa_k_norm[layer], swa_sink[layer], swa_bias)
        y_dil = _dil_mixer(proj, dil_q_norm[layer], dil_k_norm[layer], dil_bias)
        vres = None if layer == 0 else (vres_v0[layer - 1], vres_v1[layer - 1], vres_v2[layer - 1])
        maps = _rwkv_maps(
            proj, rwkv_mu[layer], decay_w0[layer], decay_w2[layer], iclr_a0[layer], iclr_a2[layer],
            gate_g2[layer], k_k[layer], k_a[layer], r_k[layer], v_first, vres)
        if layer == 0:
            v_first = maps[6]
        y_rwkv = _wkv_scan(*maps[:6], ln_x_g[layer], ln_x_b[layer])
        ys = [y.reshape(bsz * seq, GROUP_WIDTH) for y in (y_conv, y_swa, y_dil, y_rwkv)]
        x2d = _wout(x2d, ys, w_out[layer].astype(BF16))
        x2d = _ffn(x2d, norm_ffn, w_up, w_down, layer)
    return x2d.reshape(bsz, seq, d)
```

```python
import functools
import math

import jax
import jax.numpy as jnp
from jax import lax
from jax.experimental import pallas as pl
from jax.experimental.pallas import tpu as pltpu

F32 = jnp.float32
BF16 = jnp.bfloat16

D_MODEL = 2048
HEAD_DIM = 64
GROUP_WIDTH = 512
N_HEADS = GROUP_WIDTH // HEAD_DIM
SWA_KV_HEADS = 2
SWA_GROUP = N_HEADS // SWA_KV_HEADS
SWA_WINDOW = 128
DIL_PAIRS = ((128, 1), (512, 4), (2048, 16))
DECAY_LORA = 64
ICLR_LORA = 64
GATE_LORA = 128
RWKV_IN_WIDTH = 3 * GROUP_WIDTH + DECAY_LORA + ICLR_LORA + GATE_LORA
BLK = 128
NUM_BUCKETS = 32
BUCKET_MAX_DIST = 128
RMS_EPS = 1e-6
LN_X_EPS = 64e-5
NEG = -1e30
LOG2E = math.log2(math.e)
WKV_CHUNK = 64
WKV_GROUP = 4
DIL_TILES = 4
PREV_ROWS = 16
PREP_UNROLL = 4

OFF_CONV = 0
OFF_SWA_Q = OFF_CONV + 3 * GROUP_WIDTH
OFF_SWA_K = OFF_SWA_Q + GROUP_WIDTH
OFF_SWA_V = OFF_SWA_K + SWA_KV_HEADS * HEAD_DIM
OFF_DIL = OFF_SWA_V + SWA_KV_HEADS * HEAD_DIM
OFF_RW = OFF_DIL + 3 * GROUP_WIDTH
IN_WIDTH = OFF_RW + RWKV_IN_WIDTH

V7X_VMEM_BYTES = 64 * 1024 * 1024
VMEM_LIMIT = 48 * 1024 * 1024
FFN_VMEM_LIMIT = V7X_VMEM_BYTES - 6 * 1024 * 1024


def _params(*sem, vmem_limit=VMEM_LIMIT):
    return pltpu.CompilerParams(dimension_semantics=sem, vmem_limit_bytes=vmem_limit)


def _mm(a, b):
    return jnp.dot(a.astype(BF16), b.astype(BF16), preferred_element_type=F32)


def _mm_nt(a, b):
    return lax.dot_general(a.astype(BF16), b.astype(BF16), (((1,), (1,)), ((), ())),
                           preferred_element_type=F32)


def _mm_tn(a, b):
    return lax.dot_general(a.astype(BF16), b.astype(BF16), (((0,), (0,)), ((), ())),
                           preferred_element_type=F32)


def _split_bf16(x):
    hi = x.astype(BF16)
    lo = (x - hi.astype(F32)).astype(BF16)
    return hi, lo


def _head_sums(x):
    lanes = 128
    r = lax.broadcasted_iota(jnp.int32, (lanes, lanes), 0) // HEAD_DIM
    c = lax.broadcasted_iota(jnp.int32, (lanes, lanes), 1) // HEAD_DIM
    bd = (r == c).astype(BF16)
    hi, lo = _split_bf16(x)
    cols = []
    for j in range(x.shape[-1] // lanes):
        sl = slice(j * lanes, (j + 1) * lanes)
        cols.append(jnp.dot(hi[:, sl], bd, preferred_element_type=F32)
                    + jnp.dot(lo[:, sl], bd, preferred_element_type=F32))
    return cols[0] if len(cols) == 1 else jnp.concatenate(cols, axis=-1)


def _sigmoid(z):
    return 1.0 / (1.0 + jnp.exp(-z))


def _norm_matmul_kernel(x_ref, g_ref, w_ref, o_ref, h_ref):
    @pl.when(pl.program_id(1) == 0)
    def _():
        x = x_ref[...]
        ms = jnp.mean(x * x, axis=-1, keepdims=True)
        h_ref[...] = (x * lax.rsqrt(ms + RMS_EPS) * g_ref[0]).astype(BF16)

    o_ref[...] = jnp.dot(h_ref[...], w_ref[0].astype(BF16),
                         preferred_element_type=F32).astype(o_ref.dtype)


def _norm_matmul(x2d, gains, w_stack, layer, tm=2048, tn=512):
    m, k = x2d.shape
    n = w_stack.shape[2]
    assert n % tn == 0 and m % tm == 0
    return pl.pallas_call(
        _norm_matmul_kernel,
        grid=(m // tm, n // tn),
        in_specs=[
            pl.BlockSpec((tm, k), lambda i, j: (i, 0)),
            pl.BlockSpec((1, 1, k), lambda i, j: (layer, 0, 0)),
            pl.BlockSpec((1, k, tn), lambda i, j: (layer, 0, j)),
        ],
        out_specs=pl.BlockSpec((tm, tn), lambda i, j: (i, j)),
        out_shape=jax.ShapeDtypeStruct((m, n), BF16),
        scratch_shapes=[pltpu.VMEM((tm, k), BF16)],
        compiler_params=_params("parallel", "arbitrary", vmem_limit=FFN_VMEM_LIMIT),
        name="norm_matmul",
    )(x2d, gains.reshape(-1, 1, k), w_stack)


def _t5_bucket(dist):
    dist = jnp.maximum(dist, 0)
    max_exact = NUM_BUCKETS // 2
    scaled = (jnp.log(jnp.maximum(dist, 1).astype(F32) / max_exact)
              / math.log(BUCKET_MAX_DIST / max_exact))
    large = max_exact + (scaled * (NUM_BUCKETS - max_exact)).astype(jnp.int32)
    large = jnp.minimum(large, NUM_BUCKETS - 1)
    return jnp.where(dist < max_exact, dist, large)


def _bias_kernel(bucket_ref, table_ref, o_ref):
    h = pl.program_id(1)
    bucket = bucket_ref[0]
    acc = jnp.zeros(bucket.shape, F32)
    for b in range(NUM_BUCKETS):
        acc = jnp.where(bucket == b, table_ref[b, h], acc)
    o_ref[0, 0] = acc


def _bias_tiles(rel_bias):
    dist = BLK + jnp.arange(BLK)[:, None] - jnp.arange(2 * BLK)[None, :]
    buckets = jnp.stack([_t5_bucket(dist * r) for _, r in DIL_PAIRS]).astype(jnp.int32)
    nh = rel_bias.shape[1]
    return pl.pallas_call(
        _bias_kernel,
        grid=(len(DIL_PAIRS), nh),
        in_specs=[
            pl.BlockSpec((1, BLK, 2 * BLK), lambda s, h: (s, 0, 0)),
            pl.BlockSpec(memory_space=pltpu.SMEM),
        ],
        out_specs=pl.BlockSpec((1, 1, BLK, 2 * BLK), lambda s, h: (s, h, 0, 0)),
        out_shape=jax.ShapeDtypeStruct((len(DIL_PAIRS), nh, BLK, 2 * BLK), F32),
        compiler_params=_params("arbitrary", "arbitrary"),
        name="bias_tiles",
    )(buckets, rel_bias)


def _short_conv(b_ref, c_ref, u_ref, w_ref, o_ref):
    lanes = 128
    for j in range(b_ref.shape[2] // lanes):
        cols = slice(j * lanes, (j + 1) * lanes)
        z = c_ref[0, :, cols].astype(F32) * u_ref[0, :, cols].astype(F32)
        row = lax.broadcasted_iota(jnp.int32, z.shape, 0)
        z1 = jnp.where(row >= 1, pltpu.roll(z, 1, axis=0), 0.0)
        z2 = jnp.where(row >= 2, pltpu.roll(z, 2, axis=0), 0.0)
        w = w_ref[:, cols]
        y = z2 * w[0:1, :] + z1 * w[1:2, :] + z * w[2:3, :]
        o_ref[0, :, cols] = (b_ref[0, :, cols].astype(F32) * y).astype(o_ref.dtype)


def _band_mask(max_dist):
    a = lax.broadcasted_iota(jnp.int32, (BLK, 2 * BLK), 0)
    b = lax.broadcasted_iota(jnp.int32, (BLK, 2 * BLK), 1)
    dist = BLK + a - b
    return (dist >= 0) & (dist <= max_dist), b


def _attend(qs, kws, vws, biases, sinks=None):
    idx = range(len(qs))
    s = [lax.dot_general(qs[i], kws[i], (((1,), (1,)), ((), ())), preferred_element_type=F32)
         + biases[i] for i in idx]
    m = [jnp.max(s[i], axis=-1, keepdims=True) for i in idx]
    if sinks is not None:
        m = [jnp.maximum(m[i], sinks[i]) for i in idx]
    p = [jnp.exp2(s[i] - m[i]) for i in idx]
    den = [jnp.sum(p[i], axis=-1, keepdims=True) for i in idx]
    if sinks is not None:
        den = [den[i] + jnp.exp2(sinks[i] - m[i]) for i in idx]
    o = [jnp.dot(p[i].astype(BF16), vws[i], preferred_element_type=F32) / den[i] for i in idx]
    return o, m, den


def _head_rms(x, gain):
    ms = _head_sums(x * x) * (1.0 / HEAD_DIM)
    return x * lax.rsqrt(ms + RMS_EPS) * gain


def _swa_kernel(q_ref, k_ref, v_ref, qg_ref, kg_ref, sink_ref, bias_ref,
                cb_ref, cc_ref, cu_ref, cw_ref, o_ref, oc_ref,
                qn_ref, kn_ref, vb_ref, bm_ref):
    seq = q_ref.shape[1]
    nb = seq // BLK
    kvw = SWA_KV_HEADS * HEAD_DIM
    scale = HEAD_DIM ** -0.5 * LOG2E

    _short_conv(cb_ref, cc_ref, cu_ref, cw_ref, oc_ref)

    kn_ref[0:BLK, :] = jnp.zeros((BLK, 2 * kvw), BF16)
    vb_ref[0:BLK, :] = jnp.zeros((BLK, 2 * kvw), BF16)
    lane_half = lax.broadcasted_iota(jnp.int32, (1, kvw), 1) // HEAD_DIM

    def both_halves(x):
        xr = pltpu.roll(x, HEAD_DIM, axis=1)
        return jnp.concatenate([jnp.where(lane_half == 0, x, xr),
                                jnp.where(lane_half == 0, xr, x)], axis=-1)

    def prep(i, carry):
        r0 = pl.multiple_of(i * BLK, BLK)
        q = q_ref[0, pl.ds(r0, BLK), :].astype(F32)
        qn_ref[pl.ds(r0, BLK), :] = (_head_rms(q, qg_ref[...]) * scale).astype(BF16)
        k = _head_rms(k_ref[0, pl.ds(r0, BLK), :].astype(F32), kg_ref[...])
        kn_ref[pl.ds(r0 + BLK, BLK), :] = both_halves(k).astype(BF16)
        v = v_ref[0, pl.ds(r0, BLK), :].astype(F32)
        vb_ref[pl.ds(r0 + BLK, BLK), :] = both_halves(v).astype(BF16)
        return carry

    lax.fori_loop(0, nb, prep, 0, unroll=PREP_UNROLL)

    band, kcol = _band_mask(SWA_WINDOW - 1)
    for h in range(N_HEADS):
        bm_ref[h] = jnp.where(band & (kcol >= BLK), bias_ref[h] * LOG2E, NEG)
        bm_ref[N_HEADS + h] = jnp.where(band, bias_ref[h] * LOG2E, NEG)
    sinks = [sink_ref[h] * LOG2E for h in range(N_HEADS)]

    def block(i, carry):
        r0 = pl.multiple_of(i * BLK, BLK)
        later = jnp.minimum(i, 1) * N_HEADS
        qs, kws, vws, bms = [], [], [], []
        for hk in range(SWA_KV_HEADS):
            kw = kn_ref[pl.ds(r0, 2 * BLK), hk * kvw:(hk + 1) * kvw]
            vw = vb_ref[pl.ds(r0, 2 * BLK), hk * kvw:(hk + 1) * kvw]
            for g in range(SWA_GROUP):
                h = hk * SWA_GROUP + g
                q2 = qn_ref[pl.ds(r0, BLK), (h // 2) * kvw:(h // 2 + 1) * kvw]
                qs.append(jnp.where(lane_half == h % 2, q2, jnp.zeros_like(q2)))
                kws.append(kw)
                vws.append(vw)
                bms.append(bm_ref[later + h])
        outs, _, _ = _attend(qs, kws, vws, bms, sinks)
        pairs = [jnp.where(lane_half == 0, outs[h], outs[h + 1]) for h in range(0, N_HEADS, 2)]
        o_ref[0, pl.ds(r0, BLK), :] = jnp.concatenate(pairs, axis=-1).astype(o_ref.dtype)
        return carry

    lax.fori_loop(0, nb, block, 0)


def _swa_conv_mixers(proj, q_gain, k_gain, sink, bias, conv_w):
    bsz, seq, _ = proj.shape
    kvw = SWA_KV_HEADS * HEAD_DIM
    q_gain_t = jnp.tile(q_gain, N_HEADS).reshape(1, GROUP_WIDTH)
    k_gain_t = jnp.tile(k_gain, SWA_KV_HEADS).reshape(1, kvw)
    wide = lambda off: pl.BlockSpec((1, seq, GROUP_WIDTH), lambda b: (b, 0, off // GROUP_WIDTH))
    out = jax.ShapeDtypeStruct((bsz, seq, GROUP_WIDTH), BF16)
    return pl.pallas_call(
        _swa_kernel,
        grid=(bsz,),
        in_specs=[
            wide(OFF_SWA_Q),
            pl.BlockSpec((1, seq, kvw), lambda b: (b, 0, OFF_SWA_K // kvw)),
            pl.BlockSpec((1, seq, kvw), lambda b: (b, 0, OFF_SWA_V // kvw)),
            pl.BlockSpec((1, GROUP_WIDTH), lambda b: (0, 0)),
            pl.BlockSpec((1, kvw), lambda b: (0, 0)),
            pl.BlockSpec(memory_space=pltpu.SMEM),
            pl.BlockSpec((N_HEADS, BLK, 2 * BLK), lambda b: (0, 0, 0)),
            wide(OFF_CONV), wide(OFF_CONV + GROUP_WIDTH), wide(OFF_CONV + 2 * GROUP_WIDTH),
            pl.BlockSpec((3, GROUP_WIDTH), lambda b: (0, 0)),
        ],
        out_specs=[pl.BlockSpec((1, seq, GROUP_WIDTH), lambda b: (b, 0, 0))] * 2,
        out_shape=[out, out],
        scratch_shapes=[
            pltpu.VMEM((seq, GROUP_WIDTH), BF16),
            pltpu.VMEM((seq + BLK, 2 * kvw), BF16),
            pltpu.VMEM((seq + BLK, 2 * kvw), BF16),
            pltpu.VMEM((2 * N_HEADS, BLK, 2 * BLK), F32),
        ],
        compiler_params=_params("parallel"),
        name="swa_conv_mixers",
    )(proj, proj, proj, q_gain_t, k_gain_t, sink, bias, proj, proj, proj, conv_w)


def _dil_kernel(q_ref, k_ref, v_ref, qg_ref, kg_ref, bias_ref, o_ref,
                qn_ref, kn_ref, vn_ref, ob_ref, lb_ref, bm_ref):
    seq = q_ref.shape[1]
    lanes = q_ref.shape[2]
    heads = lanes // HEAD_DIM
    scale = HEAD_DIM ** -0.5 * LOG2E

    def prep(i, carry):
        r0 = pl.multiple_of(i * BLK, BLK)
        q = q_ref[0, pl.ds(r0, BLK), :].astype(F32)
        k = k_ref[0, pl.ds(r0, BLK), :].astype(F32)
        qn_ref[pl.ds(r0, BLK), :] = _head_rms(q, qg_ref[...]) * scale
        kn_ref[pl.ds(r0, BLK), :] = _head_rms(k, kg_ref[...])
        vn_ref[pl.ds(r0, BLK), :] = v_ref[0, pl.ds(r0, BLK), :].astype(F32)
        return carry

    lax.fori_loop(0, seq // BLK, prep, 0, unroll=PREP_UNROLL)

    for br, (window, r) in enumerate(DIL_PAIRS):
        band, kcol = _band_mask(window // r)
        for h in range(heads):
            bias = bias_ref[br, h] * LOG2E
            bm_ref[(br * heads + h) * 2] = jnp.where(band & (kcol >= BLK), bias, NEG)
            bm_ref[(br * heads + h) * 2 + 1] = jnp.where(band, bias, NEG)

    lane_head = lax.broadcasted_iota(jnp.int32, (1, lanes), 1) // HEAD_DIM

    for br, (window, r) in enumerate(DIL_PAIRS):
        nb = seq // r // BLK

        def blocks(it, carry, br=br, r=r, nb=nb):
            qs, kws, vws, bms, curs = [], [], [], [], []
            for u in range(DIL_TILES):
                t = it * DIL_TILES + u
                c = t // nb
                i = t - c * nb
                cur = c + i * (BLK * r)
                prev = jnp.maximum(cur - BLK * r, c)
                later = jnp.minimum(i, 1)

                def rows(ref, start):
                    if r == 1:
                        return ref[pl.ds(start, BLK), :]
                    return ref[pl.ds(start, BLK, stride=r), :]

                q = rows(qn_ref, cur).astype(BF16)
                if nb == 1:
                    kw = rows(kn_ref, cur).astype(BF16)
                    vw = rows(vn_ref, cur).astype(BF16)
                else:
                    kw = jnp.concatenate([rows(kn_ref, prev), rows(kn_ref, cur)],
                                         axis=0).astype(BF16)
                    vw = jnp.concatenate([rows(vn_ref, prev), rows(vn_ref, cur)],
                                         axis=0).astype(BF16)
                curs.append(cur)
                for h in range(heads):
                    qs.append(jnp.where(lane_head == h, q, jnp.zeros_like(q)))
                    kws.append(kw)
                    vws.append(vw)
                    if nb == 1:
                        bms.append(bm_ref[(br * heads + h) * 2, :, BLK:])
                    else:
                        bms.append(bm_ref[(br * heads + h) * 2 + later])
            outs, ms, dens = _attend(qs, kws, vws, bms)
            for u in range(DIL_TILES):
                o_all = outs[u * heads]
                l_all = ms[u * heads] + jnp.log2(dens[u * heads])
                for h in range(1, heads):
                    o_all = jnp.where(lane_head == h, outs[u * heads + h], o_all)
                    l_all = jnp.where(lane_head == h,
                                      ms[u * heads + h] + jnp.log2(dens[u * heads + h]), l_all)
                if r == 1:
                    ob_ref[br, pl.ds(curs[u], BLK), :] = o_all
                    lb_ref[br, pl.ds(curs[u], BLK), :] = l_all
                else:
                    ob_ref[br, pl.ds(curs[u], BLK, stride=r), :] = o_all
                    lb_ref[br, pl.ds(curs[u], BLK, stride=r), :] = l_all
            return carry

        lax.fori_loop(0, r * nb // DIL_TILES, blocks, 0)

    def combine(i, carry):
        r0 = pl.multiple_of(i * BLK, BLK)
        l0 = lb_ref[0, pl.ds(r0, BLK), :]
        l1 = lb_ref[1, pl.ds(r0, BLK), :]
        l2 = lb_ref[2, pl.ds(r0, BLK), :]
        m = jnp.maximum(jnp.maximum(l0, l1), l2)
        e0, e1, e2 = jnp.exp2(l0 - m), jnp.exp2(l1 - m), jnp.exp2(l2 - m)
        tot = e0 + e1 + e2
        o = ((e0 / tot) * ob_ref[0, pl.ds(r0, BLK), :]
             + (e1 / tot) * ob_ref[1, pl.ds(r0, BLK), :]
             + (e2 / tot) * ob_ref[2, pl.ds(r0, BLK), :])
        o_ref[0, pl.ds(r0, BLK), :] = o.astype(o_ref.dtype)
        return carry

    lax.fori_loop(0, seq // BLK, combine, 0, unroll=PREP_UNROLL)


def _dil_mixer(proj, q_gain, k_gain, bias):
    bsz, seq, _ = proj.shape
    lanes = 128
    heads = lanes // HEAD_DIM
    nblk = GROUP_WIDTH // lanes
    base = OFF_DIL // lanes
    gq = jnp.tile(q_gain, heads).reshape(1, lanes)
    gk = jnp.tile(k_gain, heads).reshape(1, lanes)

    def col(seg):
        return pl.BlockSpec((1, seq, lanes), lambda b, j: (b, 0, base + seg * nblk + j))

    return pl.pallas_call(
        _dil_kernel,
        grid=(bsz, nblk),
        in_specs=[
            col(0), col(1), col(2),
            pl.BlockSpec((1, lanes), lambda b, j: (0, 0)),
            pl.BlockSpec((1, lanes), lambda b, j: (0, 0)),
            pl.BlockSpec((len(DIL_PAIRS), heads, BLK, 2 * BLK), lambda b, j: (0, j, 0, 0)),
        ],
        out_specs=pl.BlockSpec((1, seq, lanes), lambda b, j: (b, 0, j)),
        out_shape=jax.ShapeDtypeStruct((bsz, seq, GROUP_WIDTH), BF16),
        scratch_shapes=[
            pltpu.VMEM((seq, lanes), F32),
            pltpu.VMEM((seq, lanes), F32),
            pltpu.VMEM((seq, lanes), F32),
            pltpu.VMEM((len(DIL_PAIRS), seq, lanes), F32),
            pltpu.VMEM((len(DIL_PAIRS), seq, lanes), F32),
            pltpu.VMEM((len(DIL_PAIRS) * heads * 2, BLK, 2 * BLK), F32),
        ],
        compiler_params=_params("parallel", "parallel"),
        name="dil_mixer",
    )(proj, proj, proj, gq, gk, bias)


def _rwkv_maps_kernel(*refs, has_vres):
    if has_vres:
        (p_ref, pp_ref, mu_ref, w0_ref, w2_ref, a0_ref, a2_ref, g2_ref, kk_ref, ka_ref, rk_ref,
         vf_ref, v0_ref, v1_ref, v2_ref,
         q_out, y1_out, m_out, gm_out, bonus_out, gate_out, *scratch) = refs
    else:
        (p_ref, pp_ref, mu_ref, w0_ref, w2_ref, a0_ref, a2_ref, g2_ref, kk_ref, ka_ref, rk_ref,
         q_out, y1_out, m_out, gm_out, bonus_out, gate_out, v_out, *scratch) = refs
    w = GROUP_WIDTH
    p = p_ref[0].astype(F32)
    row = lax.broadcasted_iota(jnp.int32, p.shape, 0)
    last_prev = pp_ref[0, PREV_ROWS - 1:PREV_ROWS, :].astype(F32)
    last_prev = jnp.where(pl.program_id(1) > 0, last_prev, 0.0)
    prev = jnp.where(row >= 1, pltpu.roll(p, 1, axis=0), last_prev)
    xs = p + (prev - p) * mu_ref[...]
    grows = WKV_GROUP * WKV_CHUNK
    for grp in range(p.shape[0] // grows):
        rs = slice(grp * grows, (grp + 1) * grows)
        x = xs[rs]
        r = x[:, 0:w]
        k = x[:, w:2 * w]
        v = x[:, 2 * w:3 * w]
        o = 3 * w
        wd = x[:, o:o + DECAY_LORA]
        ad = x[:, o + DECAY_LORA:o + DECAY_LORA + ICLR_LORA]
        gd = x[:, o + DECAY_LORA + ICLR_LORA:]

        z = -(w0_ref[...] + _mm(jnp.tanh(wd), w2_ref[...]))
        softplus = jnp.maximum(z, 0.0) + jnp.log(1.0 + jnp.exp(-jnp.abs(z)))
        logw = -softplus - 0.5
        lw = -jnp.exp(logw)
        a = _sigmoid(a0_ref[...] + _mm(ad, a2_ref[...]))
        gate_out[0, rs] = _mm(_sigmoid(gd), g2_ref[...])
        if has_vres:
            mix = _sigmoid(v0_ref[...] + _mm(_mm(v, v1_ref[...]), v2_ref[...]))
            v = v + (vf_ref[0, rs] - v) * mix
        else:
            v_out[0, rs] = v
        kk = k * kk_ref[...]
        ss = _head_sums(kk * kk)
        kk = kk * lax.rsqrt(jnp.maximum(ss, 1e-24))
        k = k * (1.0 + (a - 1.0) * ka_ref[...])
        bonus_out[0, rs] = _head_sums(r * k * rk_ref[...]) * v
        _chunk_maps(grp, r, lw, k, v, kk, kk * a, q_out.at[0], y1_out.at[0], m_out.at[0],
                    gm_out.at[0], *scratch)


def _rwkv_maps(proj, mu, w0, w2, a0, a2, g2, k_k, k_a, r_k, v_first, vres, ts=256):
    bsz, seq, _ = proj.shape
    w = GROUP_WIDTH
    nt = seq // ts
    has_vres = vres is not None

    def full(shape):
        return pl.BlockSpec(shape, lambda b, i: (0,) * len(shape))

    row = lambda a: a.reshape(1, -1)
    tile = pl.BlockSpec((1, ts, w), lambda b, i: (b, i, 0))
    in_specs = [
        pl.BlockSpec((pl.Element(1), pl.Element(ts), pl.Element(RWKV_IN_WIDTH)),
                     lambda b, i: (b, i * ts, OFF_RW)),
        pl.BlockSpec((pl.Element(1), pl.Element(PREV_ROWS), pl.Element(RWKV_IN_WIDTH)),
                     lambda b, i: (b, jnp.maximum(i * (ts // PREV_ROWS) - 1, 0) * PREV_ROWS, OFF_RW)),
        full((1, RWKV_IN_WIDTH)), full((1, w)), full((DECAY_LORA, w)), full((1, w)),
        full((ICLR_LORA, w)), full((GATE_LORA, w)), full((1, w)), full((1, w)), full((1, w)),
    ]
    args = [proj, proj, row(mu), row(w0), w2, row(a0), a2, g2, row(k_k), row(k_a), row(r_k)]
    if has_vres:
        v0, v1, v2 = vres
        in_specs += [tile, full((1, w)), full(v1.shape), full(v2.shape)]
        args += [v_first, row(v0), v1, v2]
    f32 = jax.ShapeDtypeStruct((bsz, seq, w), F32)
    bf16 = jax.ShapeDtypeStruct((bsz, seq, w), BF16)
    out_shape = [bf16, f32, bf16, f32, f32, f32] + ([] if has_vres else [f32])
    return pl.pallas_call(
        functools.partial(_rwkv_maps_kernel, has_vres=has_vres),
        grid=(bsz, nt),
        in_specs=in_specs,
        out_specs=[tile] * len(out_shape),
        out_shape=out_shape,
        scratch_shapes=[pltpu.VMEM((ts, w), dt) for dt in (BF16, F32, BF16, BF16, F32, BF16, BF16)],
        compiler_params=_params("parallel", "parallel"),
        name="rwkv_chunk_maps",
    )(*args)


def _chunk_maps(grp, r, lw, k, v, kk, b, q_out, y1_out, m_out, g_out,
                at_ref, rt_ref, bt_ref, kt_ref, dec_ref, vb_ref, rb_ref):
    c = WKV_CHUNK
    n = HEAD_DIM
    rows = r.shape[0]
    rs = slice(grp * rows, (grp + 1) * rows)

    row = lax.broadcasted_iota(jnp.int32, (rows, rows), 0)
    col = lax.broadcasted_iota(jnp.int32, (rows, rows), 1)
    tri = ((row >= col) & ((row // c) == (col // c))).astype(BF16)
    lw_hi, lw_lo = _split_bf16(lw)
    cum = (jnp.dot(tri, lw_hi, preferred_element_type=F32)
           + jnp.dot(tri, lw_lo, preferred_element_type=F32))
    e_pos = jnp.exp(cum)
    e_neg = jnp.exp(-cum)
    r_t = r * e_pos
    at_ref[rs] = (-kk * jnp.exp(cum - lw)).astype(BF16)
    rt_ref[rs] = r_t
    rb_ref[rs] = r_t.astype(BF16)
    bt_ref[rs] = (b * e_neg).astype(BF16)
    kt_ref[rs] = (k * e_neg).astype(BF16)
    dec_ref[rs] = e_pos
    vb_ref[rs] = v.astype(BF16)

    crow = lax.broadcasted_iota(jnp.int32, (c, c), 0)
    ccol = lax.broadcasted_iota(jnp.int32, (c, c), 1)
    lower = crow >= ccol
    strict = crow > ccol
    eye = (crow == ccol).astype(F32)
    wrow = lax.broadcasted_iota(jnp.int32, (c, 2 * c), 0)
    wcol = lax.broadcasted_iota(jnp.int32, (c, 2 * c), 1)
    strict_left = wrow > wcol
    right = wcol >= c
    eye_right = (wcol == wrow + c).astype(F32)

    def tile(ref, j, h):
        return ref[j * c:(j + 1) * c, h * n:(h + 1) * n]

    _chunk_group(range(grp * WKV_GROUP, (grp + 1) * WKV_GROUP), tile, at_ref, rt_ref, bt_ref,
                 kt_ref, dec_ref, vb_ref, rb_ref, q_out, y1_out, m_out, g_out,
                 lower, strict, eye, strict_left, right, eye_right)


def _chunk_group(chunks, tile, at_ref, rt_ref, bt_ref, kt_ref, dec_ref, vb_ref, rb_ref,
                 q_out, y1_out, m_out, g_out, lower, strict, eye, strict_left, right, eye_right):
    c = WKV_CHUNK
    n = HEAD_DIM
    pairs = [(j, h) for j in chunks for h in range(N_HEADS)]
    ah = [tile(at_ref, j, h) for j, h in pairs]
    rh = [tile(rt_ref, j, h) for j, h in pairs]
    bh = [tile(bt_ref, j, h) for j, h in pairs]
    kh = [tile(kt_ref, j, h) for j, h in pairs]
    vh = [tile(vb_ref, j, h) for j, h in pairs]
    dh = [dec_ref[(j + 1) * c - 1:(j + 1) * c, h * n:(h + 1) * n] for j, h in pairs]
    idx = range(len(pairs))
    ar = [jnp.concatenate([ah[i], tile(rb_ref, *pairs[i])], axis=0) for i in idx]
    bk = [jnp.concatenate([bh[i], kh[i]], axis=0) for i in idx]
    aa = [_mm_nt(ar[i], bk[i]) for i in idx]
    a_kr = [jnp.concatenate([jnp.where(strict, aa[i][:c, c:], 0.0),
                             jnp.where(lower, aa[i][c:, c:], 0.0)], axis=0).astype(BF16)
            for i in idx]
    a_rb = [jnp.where(lower, aa[i][c:, :c], 0.0).astype(BF16) for i in idx]
    ps = [jnp.where(strict_left, aa[i][:c, :], 0.0) + eye_right for i in idx]
    for _ in range(int(math.log2(c))):
        ps = [_mm(ps[i][:, :c], ps[i]) + jnp.where(right, ps[i], 0.0) for i in idx]
    t = [ps[i][:, c:].astype(BF16) for i in idx]
    av = [_mm(a_kr[i], vh[i]) for i in idx]
    wu = [_mm(t[i], jnp.concatenate([ah[i], av[i][:c].astype(BF16)], axis=1)).astype(BF16)
          for i in idx]
    aw = [_mm(a_rb[i], wu[i]) for i in idx]
    q = [rh[i] + aw[i][:, :n] for i in idx]
    y1 = [aw[i][:, n:] + av[i][c:] for i in idx]
    zero = jnp.zeros((c, n), BF16)
    mg = [_mm_tn(jnp.concatenate([wu[i], jnp.concatenate([zero, vh[i]], axis=1)], axis=0), bk[i])
          for i in idx]
    m = [(eye + mg[i][:n]) * dh[i] for i in idx]
    g = [mg[i][n:] * dh[i] for i in idx]
    for j in chunks:
        sel = slice((j - chunks[0]) * N_HEADS, (j - chunks[0] + 1) * N_HEADS)
        q_out[j * c:(j + 1) * c, :] = jnp.concatenate(q[sel], axis=-1).astype(q_out.dtype)
        y1_out[j * c:(j + 1) * c, :] = jnp.concatenate(y1[sel], axis=-1)
        m_out[j * c:(j + 1) * c, :] = jnp.concatenate(m[sel], axis=-1).astype(m_out.dtype)
        g_out[j * c:(j + 1) * c, :] = jnp.concatenate(g[sel], axis=-1)


def _wkv_scan_kernel(q_ref, y1_ref, m_ref, gm_ref, bonus_ref, gate_ref, lng_ref, lnb_ref,
                     o_ref, st_ref):
    n = HEAD_DIM
    bsz = q_ref.shape[0]

    @pl.when(pl.program_id(0) == 0)
    def _():
        st_ref[...] = jnp.zeros(st_ref.shape, F32)

    chains = [(bi, h) for bi in range(bsz) for h in range(N_HEADS)]
    s0 = [st_ref[bi * N_HEADS + h].astype(BF16) for bi, h in chains]
    ys = [_mm_nt(q_ref[bi, :, h * n:(h + 1) * n], s0[i]) for i, (bi, h) in enumerate(chains)]
    s1 = [_mm(s0[i], m_ref[bi, :, h * n:(h + 1) * n]) for i, (bi, h) in enumerate(chains)]
    for i, (bi, h) in enumerate(chains):
        st_ref[bi * N_HEADS + h] = s1[i] + gm_ref[bi, :, h * n:(h + 1) * n]
    y = jnp.concatenate(
        [jnp.concatenate(ys[bi * N_HEADS:(bi + 1) * N_HEADS], axis=-1) + y1_ref[bi]
         for bi in range(bsz)], axis=0)
    mean = _head_sums(y) * (1.0 / n)
    yc = y - mean
    var = _head_sums(yc * yc) * (1.0 / n)
    yn = yc * lax.rsqrt(var + LN_X_EPS) * lng_ref[...] + lnb_ref[...]
    c = q_ref.shape[1]
    for bi in range(bsz):
        rows = slice(bi * c, (bi + 1) * c)
        o_ref[bi] = ((yn[rows] + bonus_ref[bi]) * gate_ref[bi]).astype(o_ref.dtype)


def _wkv_scan(q, y1, m, gm, bonus, gate, ln_g, ln_b):
    bsz, seq, w = y1.shape
    c = WKV_CHUNK
    tile = pl.BlockSpec((bsz, c, w), lambda ci: (0, ci, 0))
    vec = pl.BlockSpec((1, w), lambda ci: (0, 0))
    return pl.pallas_call(
        _wkv_scan_kernel,
        grid=(seq // c,),
        in_specs=[tile] * 6 + [vec] * 2,
        out_specs=tile,
        out_shape=jax.ShapeDtypeStruct((bsz, seq, w), BF16),
        scratch_shapes=[pltpu.VMEM((bsz * N_HEADS, HEAD_DIM, HEAD_DIM), F32)],
        compiler_params=_params("arbitrary"),
        name="wkv_state_scan",
    )(q, y1, m, gm, bonus, gate, ln_g.reshape(1, w), ln_b.reshape(1, w))


def _wout_kernel(x_ref, y0_ref, y1_ref, y2_ref, y3_ref, w_ref, o_ref):
    acc = x_ref[...]
    for idx, y_ref in enumerate((y0_ref, y1_ref, y2_ref, y3_ref)):
        acc = acc + jnp.dot(y_ref[...], w_ref[idx * GROUP_WIDTH:(idx + 1) * GROUP_WIDTH, :],
                            preferred_element_type=F32)
    o_ref[...] = acc


def _wout(x2d, ys, w_bf16, tm=512):
    m, d = x2d.shape
    ytile = pl.BlockSpec((tm, GROUP_WIDTH), lambda i: (i, 0))
    xtile = pl.BlockSpec((tm, d), lambda i: (i, 0))
    return pl.pallas_call(
        _wout_kernel,
        grid=(m // tm,),
        in_specs=[xtile] + [ytile] * 4 + [pl.BlockSpec(w_bf16.shape, lambda i: (0, 0))],
        out_specs=xtile,
        out_shape=jax.ShapeDtypeStruct((m, d), F32),
        compiler_params=_params("parallel"),
        name="wout_residual",
    )(x2d, *ys, w_bf16)


def _ffn_kernel(x_ref, g_ref, wu_ref, wd_ref, o_ref, h_ref):
    @pl.when(pl.program_id(1) == 0)
    def _():
        x = x_ref[...]
        ms = jnp.mean(x * x, axis=-1, keepdims=True)
        h_ref[...] = (x * lax.rsqrt(ms + RMS_EPS) * g_ref[0]).astype(BF16)
        o_ref[...] = x

    u = jnp.dot(h_ref[...], wu_ref[0].astype(BF16), preferred_element_type=F32)
    act = jnp.square(jnp.maximum(u, 0.0)).astype(BF16)
    o_ref[...] += jnp.dot(act, wd_ref[0].astype(BF16), preferred_element_type=F32)


def _ffn(x2d, gains, wu_stack, wd_stack, layer, tm=1024, tf=512):
    m, d = x2d.shape
    f = wu_stack.shape[2]
    xtile = pl.BlockSpec((tm, d), lambda i, j: (i, 0))
    return pl.pallas_call(
        _ffn_kernel,
        grid=(m // tm, f // tf),
        in_specs=[
            xtile,
            pl.BlockSpec((1, 1, d), lambda i, j: (layer, 0, 0)),
            pl.BlockSpec((1, d, tf), lambda i, j: (layer, 0, j)),
            pl.BlockSpec((1, tf, d), lambda i, j: (layer, j, 0)),
        ],
        out_specs=xtile,
        out_shape=jax.ShapeDtypeStruct((m, d), F32),
        scratch_shapes=[pltpu.VMEM((tm, d), BF16)],
        compiler_params=_params("parallel", "arbitrary", vmem_limit=FFN_VMEM_LIMIT),
        name="ffn",
    )(x2d, gains.reshape(-1, 1, d), wu_stack, wd_stack)


def kernel(x, norm_mix, w_in, conv_w, swa_q_norm, swa_k_norm, swa_sink, dil_q_norm, dil_k_norm,
           rwkv_mu, decay_w0, decay_w2, iclr_a0, iclr_a2, gate_g2, k_k, k_a, r_k, ln_x_g, ln_x_b,
           vres_v0, vres_v1, vres_v2, w_out, norm_ffn, w_up, w_down, rel_bias):
    bsz, seq, d = x.shape
    depth = w_in.shape[0]
    bias = _bias_tiles(rel_bias)
    swa_bias = bias[0, :N_HEADS]
    dil_bias = bias[:, N_HEADS:]
    x2d = x.reshape(bsz * seq, d)
    v_first = None
    for layer in range(depth):
        proj = _norm_matmul(x2d, norm_mix, w_in, layer).reshape(bsz, seq, IN_WIDTH)
        y_swa, y_conv = _swa_conv_mixers(proj, swa_q_norm[layer], swa_k_norm[layer],
                                         swa_sink[layer], swa_bias, conv_w[layer])
        y_dil = _dil_mixer(proj, dil_q_norm[layer], dil_k_norm[layer], dil_bias)
        vres = None if layer == 0 else (vres_v0[layer - 1], vres_v1[layer - 1], vres_v2[layer - 1])
        maps = _rwkv_maps(
            proj, rwkv_mu[layer], decay_w0[layer], decay_w2[layer], iclr_a0[layer], iclr_a2[layer],
            gate_g2[layer], k_k[layer], k_a[layer], r_k[layer], v_first, vres)
        if layer == 0:
            v_first = maps[6]
        y_rwkv = _wkv_scan(*maps[:6], ln_x_g[layer], ln_x_b[layer])
        ys = [y.reshape(bsz * seq, GROUP_WIDTH) for y in (y_conv, y_swa, y_dil, y_rwkv)]
        x2d = _wout(x2d, ys, w_out[layer].astype(BF16))
        x2d = _ffn(x2d, norm_ffn, w_up, w_down, layer)
    return x2d.reshape(bsz, seq, d)
```

```python
import functools
import math

import jax
import jax.numpy as jnp
from jax import lax
from jax.experimental import pallas as pl
from jax.experimental.pallas import tpu as pltpu

F32 = jnp.float32
BF16 = jnp.bfloat16

D_MODEL = 2048
HEAD_DIM = 64
GROUP_WIDTH = 512
N_HEADS = GROUP_WIDTH // HEAD_DIM
SWA_KV_HEADS = 2
SWA_GROUP = N_HEADS // SWA_KV_HEADS
SWA_WINDOW = 128
DIL_PAIRS = ((128, 1), (512, 4), (2048, 16))
DECAY_LORA = 64
ICLR_LORA = 64
GATE_LORA = 128
RWKV_IN_WIDTH = 3 * GROUP_WIDTH + DECAY_LORA + ICLR_LORA + GATE_LORA
BLK = 128
NUM_BUCKETS = 32
BUCKET_MAX_DIST = 128
RMS_EPS = 1e-6
LN_X_EPS = 64e-5
NEG = -1e30
LOG2E = math.log2(math.e)
WKV_CHUNK = 64
WKV_GROUP = 4
DIL_TILES = 4
PREV_ROWS = 16
PREP_UNROLL = 4

OFF_CONV = 0
OFF_SWA_Q = OFF_CONV + 3 * GROUP_WIDTH
OFF_SWA_K = OFF_SWA_Q + GROUP_WIDTH
OFF_SWA_V = OFF_SWA_K + SWA_KV_HEADS * HEAD_DIM
OFF_DIL = OFF_SWA_V + SWA_KV_HEADS * HEAD_DIM
OFF_RW = OFF_DIL + 3 * GROUP_WIDTH
IN_WIDTH = OFF_RW + RWKV_IN_WIDTH

V7X_VMEM_BYTES = 64 * 1024 * 1024
VMEM_LIMIT = 48 * 1024 * 1024
FFN_VMEM_LIMIT = V7X_VMEM_BYTES - 6 * 1024 * 1024


def _params(*sem, vmem_limit=VMEM_LIMIT):
    return pltpu.CompilerParams(dimension_semantics=sem, vmem_limit_bytes=vmem_limit)


def _mm(a, b):
    return jnp.dot(a.astype(BF16), b.astype(BF16), preferred_element_type=F32)


def _mm_nt(a, b):
    return lax.dot_general(a.astype(BF16), b.astype(BF16), (((1,), (1,)), ((), ())),
                           preferred_element_type=F32)


def _mm_tn(a, b):
    return lax.dot_general(a.astype(BF16), b.astype(BF16), (((0,), (0,)), ((), ())),
                           preferred_element_type=F32)


def _split_bf16(x):
    hi = x.astype(BF16)
    lo = (x - hi.astype(F32)).astype(BF16)
    return hi, lo


def _head_sums(x, split=True):
    lanes = 128
    r = lax.broadcasted_iota(jnp.int32, (lanes, lanes), 0) // HEAD_DIM
    c = lax.broadcasted_iota(jnp.int32, (lanes, lanes), 1) // HEAD_DIM
    bd = (r == c).astype(BF16)
    parts = _split_bf16(x) if split else (x.astype(BF16),)
    cols = []
    for j in range(x.shape[-1] // lanes):
        sl = slice(j * lanes, (j + 1) * lanes)
        cols.append(sum(jnp.dot(part[:, sl], bd, preferred_element_type=F32) for part in parts))
    return cols[0] if len(cols) == 1 else jnp.concatenate(cols, axis=-1)


def _sigmoid(z):
    return 1.0 / (1.0 + jnp.exp(-z))


def _norm_matmul_kernel(x_ref, g_ref, w_ref, o_ref, h_ref):
    @pl.when(pl.program_id(1) == 0)
    def _():
        x = x_ref[...]
        ms = jnp.mean(x * x, axis=-1, keepdims=True)
        h_ref[...] = (x * lax.rsqrt(ms + RMS_EPS) * g_ref[0]).astype(BF16)

    o_ref[...] = jnp.dot(h_ref[...], w_ref[0].astype(BF16),
                         preferred_element_type=F32).astype(o_ref.dtype)


def _norm_matmul(x2d, gains, w_stack, layer, tm=2048, tn=512):
    m, k = x2d.shape
    n = w_stack.shape[2]
    assert n % tn == 0 and m % tm == 0
    return pl.pallas_call(
        _norm_matmul_kernel,
        grid=(m // tm, n // tn),
        in_specs=[
            pl.BlockSpec((tm, k), lambda i, j: (i, 0)),
            pl.BlockSpec((1, 1, k), lambda i, j: (layer, 0, 0)),
            pl.BlockSpec((1, k, tn), lambda i, j: (layer, 0, j)),
        ],
        out_specs=pl.BlockSpec((tm, tn), lambda i, j: (i, j)),
        out_shape=jax.ShapeDtypeStruct((m, n), BF16),
        scratch_shapes=[pltpu.VMEM((tm, k), BF16)],
        compiler_params=_params("parallel", "arbitrary", vmem_limit=FFN_VMEM_LIMIT),
        name="norm_matmul",
    )(x2d, gains.reshape(-1, 1, k), w_stack)


def _t5_bucket(dist):
    dist = jnp.maximum(dist, 0)
    max_exact = NUM_BUCKETS // 2
    scaled = (jnp.log(jnp.maximum(dist, 1).astype(F32) / max_exact)
              / math.log(BUCKET_MAX_DIST / max_exact))
    large = max_exact + (scaled * (NUM_BUCKETS - max_exact)).astype(jnp.int32)
    large = jnp.minimum(large, NUM_BUCKETS - 1)
    return jnp.where(dist < max_exact, dist, large)


def _bias_kernel(bucket_ref, table_ref, o_ref):
    bucket = bucket_ref[0]

    def head(h, carry):
        acc = jnp.zeros(bucket.shape, F32)
        for b in range(NUM_BUCKETS):
            acc = jnp.where(bucket == b, table_ref[b, h], acc)
        o_ref[0, h] = acc
        return carry

    lax.fori_loop(0, o_ref.shape[1], head, 0)


def _bias_tiles(rel_bias):
    dist = BLK + jnp.arange(BLK)[:, None] - jnp.arange(2 * BLK)[None, :]
    buckets = jnp.stack([_t5_bucket(dist * r) for _, r in DIL_PAIRS]).astype(jnp.int32)
    nh = rel_bias.shape[1]
    return pl.pallas_call(
        _bias_kernel,
        grid=(len(DIL_PAIRS),),
        in_specs=[
            pl.BlockSpec((1, BLK, 2 * BLK), lambda s: (s, 0, 0)),
            pl.BlockSpec(memory_space=pltpu.SMEM),
        ],
        out_specs=pl.BlockSpec((1, nh, BLK, 2 * BLK), lambda s: (s, 0, 0, 0)),
        out_shape=jax.ShapeDtypeStruct((len(DIL_PAIRS), nh, BLK, 2 * BLK), F32),
        compiler_params=_params("arbitrary"),
        name="bias_tiles",
    )(buckets, rel_bias)


def _short_conv(b_ref, c_ref, u_ref, w_ref, o_ref):
    lanes = 128
    for j in range(b_ref.shape[2] // lanes):
        cols = slice(j * lanes, (j + 1) * lanes)
        z = c_ref[0, :, cols].astype(F32) * u_ref[0, :, cols].astype(F32)
        row = lax.broadcasted_iota(jnp.int32, z.shape, 0)
        z1 = jnp.where(row >= 1, pltpu.roll(z, 1, axis=0), 0.0)
        z2 = jnp.where(row >= 2, pltpu.roll(z, 2, axis=0), 0.0)
        w = w_ref[:, cols]
        y = z2 * w[0:1, :] + z1 * w[1:2, :] + z * w[2:3, :]
        o_ref[0, :, cols] = (b_ref[0, :, cols].astype(F32) * y).astype(o_ref.dtype)


def _band_mask(max_dist):
    a = lax.broadcasted_iota(jnp.int32, (BLK, 2 * BLK), 0)
    b = lax.broadcasted_iota(jnp.int32, (BLK, 2 * BLK), 1)
    dist = BLK + a - b
    return (dist >= 0) & (dist <= max_dist), b


def _attend(qs, kws, vws, biases, sinks=None):
    idx = range(len(qs))
    s = [lax.dot_general(qs[i], kws[i], (((1,), (1,)), ((), ())), preferred_element_type=F32)
         + biases[i] for i in idx]
    m = [jnp.max(s[i], axis=-1, keepdims=True) for i in idx]
    if sinks is not None:
        m = [jnp.maximum(m[i], sinks[i]) for i in idx]
    p = [jnp.exp2(s[i] - m[i]) for i in idx]
    den = [jnp.sum(p[i], axis=-1, keepdims=True) for i in idx]
    if sinks is not None:
        den = [den[i] + jnp.exp2(sinks[i] - m[i]) for i in idx]
    o = [jnp.dot(p[i].astype(BF16), vws[i], preferred_element_type=F32) / den[i] for i in idx]
    return o, m, den


def _head_rms(x, gain):
    ms = _head_sums(x * x, split=False) * (1.0 / HEAD_DIM)
    return x * lax.rsqrt(ms + RMS_EPS) * gain


def _swa_kernel(q_ref, k_ref, v_ref, qg_ref, kg_ref, sink_ref, bias_ref,
                cb_ref, cc_ref, cu_ref, cw_ref, o_ref, oc_ref,
                qn_ref, kn_ref, vb_ref, bm_ref):
    seq = q_ref.shape[1]
    nb = seq // BLK
    kvw = SWA_KV_HEADS * HEAD_DIM
    scale = HEAD_DIM ** -0.5 * LOG2E

    _short_conv(cb_ref, cc_ref, cu_ref, cw_ref, oc_ref)

    kn_ref[0:BLK, :] = jnp.zeros((BLK, 2 * kvw), BF16)
    vb_ref[0:BLK, :] = jnp.zeros((BLK, 2 * kvw), BF16)
    lane_half = lax.broadcasted_iota(jnp.int32, (1, kvw), 1) // HEAD_DIM

    def both_halves(x):
        xr = pltpu.roll(x, HEAD_DIM, axis=1)
        return jnp.concatenate([jnp.where(lane_half == 0, x, xr),
                                jnp.where(lane_half == 0, xr, x)], axis=-1)

    def prep(i, carry):
        r0 = pl.multiple_of(i * BLK, BLK)
        q = q_ref[0, pl.ds(r0, BLK), :].astype(F32)
        qn_ref[pl.ds(r0, BLK), :] = (_head_rms(q, qg_ref[...]) * scale).astype(BF16)
        k = _head_rms(k_ref[0, pl.ds(r0, BLK), :].astype(F32), kg_ref[...])
        kn_ref[pl.ds(r0 + BLK, BLK), :] = both_halves(k).astype(BF16)
        v = v_ref[0, pl.ds(r0, BLK), :].astype(F32)
        vb_ref[pl.ds(r0 + BLK, BLK), :] = both_halves(v).astype(BF16)
        return carry

    lax.fori_loop(0, nb, prep, 0, unroll=PREP_UNROLL)

    band, kcol = _band_mask(SWA_WINDOW - 1)
    for h in range(N_HEADS):
        bm_ref[h] = jnp.where(band & (kcol >= BLK), bias_ref[h] * LOG2E, NEG)
        bm_ref[N_HEADS + h] = jnp.where(band, bias_ref[h] * LOG2E, NEG)
    sinks = [sink_ref[h] * LOG2E for h in range(N_HEADS)]

    def block(i, carry):
        r0 = pl.multiple_of(i * BLK, BLK)
        later = jnp.minimum(i, 1) * N_HEADS
        qs, kws, vws, bms = [], [], [], []
        for hk in range(SWA_KV_HEADS):
            kw = kn_ref[pl.ds(r0, 2 * BLK), hk * kvw:(hk + 1) * kvw]
            vw = vb_ref[pl.ds(r0, 2 * BLK), hk * kvw:(hk + 1) * kvw]
            for g in range(SWA_GROUP):
                h = hk * SWA_GROUP + g
                q2 = qn_ref[pl.ds(r0, BLK), (h // 2) * kvw:(h // 2 + 1) * kvw]
                qs.append(jnp.where(lane_half == h % 2, q2, jnp.zeros_like(q2)))
                kws.append(kw)
                vws.append(vw)
                bms.append(bm_ref[later + h])
        outs, _, _ = _attend(qs, kws, vws, bms, sinks)
        pairs = [jnp.where(lane_half == 0, outs[h], outs[h + 1]) for h in range(0, N_HEADS, 2)]
        o_ref[0, pl.ds(r0, BLK), :] = jnp.concatenate(pairs, axis=-1).astype(o_ref.dtype)
        return carry

    lax.fori_loop(0, nb, block, 0)


def _swa_conv_mixers(proj, q_gain, k_gain, sink, bias, conv_w):
    bsz, seq, _ = proj.shape
    kvw = SWA_KV_HEADS * HEAD_DIM
    q_gain_t = jnp.tile(q_gain, N_HEADS).reshape(1, GROUP_WIDTH)
    k_gain_t = jnp.tile(k_gain, SWA_KV_HEADS).reshape(1, kvw)
    wide = lambda off: pl.BlockSpec((1, seq, GROUP_WIDTH), lambda b: (b, 0, off // GROUP_WIDTH))
    out = jax.ShapeDtypeStruct((bsz, seq, GROUP_WIDTH), BF16)
    return pl.pallas_call(
        _swa_kernel,
        grid=(bsz,),
        in_specs=[
            wide(OFF_SWA_Q),
            pl.BlockSpec((1, seq, kvw), lambda b: (b, 0, OFF_SWA_K // kvw)),
            pl.BlockSpec((1, seq, kvw), lambda b: (b, 0, OFF_SWA_V // kvw)),
            pl.BlockSpec((1, GROUP_WIDTH), lambda b: (0, 0)),
            pl.BlockSpec((1, kvw), lambda b: (0, 0)),
            pl.BlockSpec(memory_space=pltpu.SMEM),
            pl.BlockSpec((N_HEADS, BLK, 2 * BLK), lambda b: (0, 0, 0)),
            wide(OFF_CONV), wide(OFF_CONV + GROUP_WIDTH), wide(OFF_CONV + 2 * GROUP_WIDTH),
            pl.BlockSpec((3, GROUP_WIDTH), lambda b: (0, 0)),
        ],
        out_specs=[pl.BlockSpec((1, seq, GROUP_WIDTH), lambda b: (b, 0, 0))] * 2,
        out_shape=[out, out],
        scratch_shapes=[
            pltpu.VMEM((seq, GROUP_WIDTH), BF16),
            pltpu.VMEM((seq + BLK, 2 * kvw), BF16),
            pltpu.VMEM((seq + BLK, 2 * kvw), BF16),
            pltpu.VMEM((2 * N_HEADS, BLK, 2 * BLK), F32),
        ],
        compiler_params=_params("parallel"),
        name="swa_conv_mixers",
    )(proj, proj, proj, q_gain_t, k_gain_t, sink, bias, proj, proj, proj, conv_w)


def _dil_kernel(q_ref, k_ref, v_ref, qg_ref, kg_ref, bias_ref, o_ref,
                qn_ref, kn_ref, vn_ref, ob_ref, lb_ref, bm_ref):
    seq = q_ref.shape[1]
    lanes = q_ref.shape[2]
    heads = lanes // HEAD_DIM
    scale = HEAD_DIM ** -0.5 * LOG2E

    def prep(i, carry):
        r0 = pl.multiple_of(i * BLK, BLK)
        q = q_ref[0, pl.ds(r0, BLK), :].astype(F32)
        k = k_ref[0, pl.ds(r0, BLK), :].astype(F32)
        qn_ref[pl.ds(r0, BLK), :] = _head_rms(q, qg_ref[...]) * scale
        kn_ref[pl.ds(r0, BLK), :] = _head_rms(k, kg_ref[...])
        vn_ref[pl.ds(r0, BLK), :] = v_ref[0, pl.ds(r0, BLK), :].astype(F32)
        return carry

    lax.fori_loop(0, seq // BLK, prep, 0, unroll=PREP_UNROLL)

    for br, (window, r) in enumerate(DIL_PAIRS):
        band, kcol = _band_mask(window // r)
        for h in range(heads):
            bias = bias_ref[br, h] * LOG2E
            bm_ref[(br * heads + h) * 2] = jnp.where(band & (kcol >= BLK), bias, NEG)
            bm_ref[(br * heads + h) * 2 + 1] = jnp.where(band, bias, NEG)

    lane_head = lax.broadcasted_iota(jnp.int32, (1, lanes), 1) // HEAD_DIM

    for br, (window, r) in enumerate(DIL_PAIRS):
        nb = seq // r // BLK

        def blocks(it, carry, br=br, r=r, nb=nb):
            qs, kws, vws, bms, curs = [], [], [], [], []
            for u in range(DIL_TILES):
                t = it * DIL_TILES + u
                c = t // nb
                i = t - c * nb
                cur = c + i * (BLK * r)
                prev = jnp.maximum(cur - BLK * r, c)
                later = jnp.minimum(i, 1)

                def rows(ref, start):
                    if r == 1:
                        return ref[pl.ds(start, BLK), :]
                    return ref[pl.ds(start, BLK, stride=r), :]

                q = rows(qn_ref, cur).astype(BF16)
                if nb == 1:
                    kw = rows(kn_ref, cur).astype(BF16)
                    vw = rows(vn_ref, cur).astype(BF16)
                else:
                    kw = jnp.concatenate([rows(kn_ref, prev), rows(kn_ref, cur)],
                                         axis=0).astype(BF16)
                    vw = jnp.concatenate([rows(vn_ref, prev), rows(vn_ref, cur)],
                                         axis=0).astype(BF16)
                curs.append(cur)
                for h in range(heads):
                    qs.append(jnp.where(lane_head == h, q, jnp.zeros_like(q)))
                    kws.append(kw)
                    vws.append(vw)
                    if nb == 1:
                        bms.append(bm_ref[(br * heads + h) * 2, :, BLK:])
                    else:
                        bms.append(bm_ref[(br * heads + h) * 2 + later])
            outs, ms, dens = _attend(qs, kws, vws, bms)
            for u in range(DIL_TILES):
                o_all = outs[u * heads]
                l_all = ms[u * heads] + jnp.log2(dens[u * heads])
                for h in range(1, heads):
                    o_all = jnp.where(lane_head == h, outs[u * heads + h], o_all)
                    l_all = jnp.where(lane_head == h,
                                      ms[u * heads + h] + jnp.log2(dens[u * heads + h]), l_all)
                if r == 1:
                    ob_ref[br, pl.ds(curs[u], BLK), :] = o_all
                    lb_ref[br, pl.ds(curs[u], BLK), :] = l_all
                else:
                    ob_ref[br, pl.ds(curs[u], BLK, stride=r), :] = o_all
                    lb_ref[br, pl.ds(curs[u], BLK, stride=r), :] = l_all
            return carry

        lax.fori_loop(0, r * nb // DIL_TILES, blocks, 0)

    def combine(i, carry):
        r0 = pl.multiple_of(i * BLK, BLK)
        l0 = lb_ref[0, pl.ds(r0, BLK), :]
        l1 = lb_ref[1, pl.ds(r0, BLK), :]
        l2 = lb_ref[2, pl.ds(r0, BLK), :]
        m = jnp.maximum(jnp.maximum(l0, l1), l2)
        e0, e1, e2 = jnp.exp2(l0 - m), jnp.exp2(l1 - m), jnp.exp2(l2 - m)
        tot = e0 + e1 + e2
        o = ((e0 / tot) * ob_ref[0, pl.ds(r0, BLK), :]
             + (e1 / tot) * ob_ref[1, pl.ds(r0, BLK), :]
             + (e2 / tot) * ob_ref[2, pl.ds(r0, BLK), :])
        o_ref[0, pl.ds(r0, BLK), :] = o.astype(o_ref.dtype)
        return carry

    lax.fori_loop(0, seq // BLK, combine, 0, unroll=PREP_UNROLL)


def _dil_mixer(proj, q_gain, k_gain, bias):
    bsz, seq, _ = proj.shape
    lanes = 128
    heads = lanes // HEAD_DIM
    nblk = GROUP_WIDTH // lanes
    base = OFF_DIL // lanes
    gq = jnp.tile(q_gain, heads).reshape(1, lanes)
    gk = jnp.tile(k_gain, heads).reshape(1, lanes)

    def col(seg):
        return pl.BlockSpec((1, seq, lanes), lambda b, j: (b, 0, base + seg * nblk + j))

    return pl.pallas_call(
        _dil_kernel,
        grid=(bsz, nblk),
        in_specs=[
            col(0), col(1), col(2),
            pl.BlockSpec((1, lanes), lambda b, j: (0, 0)),
            pl.BlockSpec((1, lanes), lambda b, j: (0, 0)),
            pl.BlockSpec((len(DIL_PAIRS), heads, BLK, 2 * BLK), lambda b, j: (0, j, 0, 0)),
        ],
        out_specs=pl.BlockSpec((1, seq, lanes), lambda b, j: (b, 0, j)),
        out_shape=jax.ShapeDtypeStruct((bsz, seq, GROUP_WIDTH), BF16),
        scratch_shapes=[
            pltpu.VMEM((seq, lanes), F32),
            pltpu.VMEM((seq, lanes), F32),
            pltpu.VMEM((seq, lanes), F32),
            pltpu.VMEM((len(DIL_PAIRS), seq, lanes), F32),
            pltpu.VMEM((len(DIL_PAIRS), seq, lanes), F32),
            pltpu.VMEM((len(DIL_PAIRS) * heads * 2, BLK, 2 * BLK), F32),
        ],
        compiler_params=_params("parallel", "parallel"),
        name="dil_mixer",
    )(proj, proj, proj, gq, gk, bias)


def _rwkv_maps_kernel(*refs, has_vres):
    if has_vres:
        (p_ref, pp_ref, mu_ref, w0_ref, w2_ref, a0_ref, a2_ref, g2_ref, kk_ref, ka_ref, rk_ref,
         vf_ref, v0_ref, v1_ref, v2_ref,
         q_out, y1_out, m_out, gm_out, bonus_out, gate_out, *scratch) = refs
    else:
        (p_ref, pp_ref, mu_ref, w0_ref, w2_ref, a0_ref, a2_ref, g2_ref, kk_ref, ka_ref, rk_ref,
         q_out, y1_out, m_out, gm_out, bonus_out, gate_out, v_out, *scratch) = refs
    w = GROUP_WIDTH
    p = p_ref[0].astype(F32)
    row = lax.broadcasted_iota(jnp.int32, p.shape, 0)
    last_prev = pp_ref[0, PREV_ROWS - 1:PREV_ROWS, :].astype(F32)
    last_prev = jnp.where(pl.program_id(1) > 0, last_prev, 0.0)
    prev = jnp.where(row >= 1, pltpu.roll(p, 1, axis=0), last_prev)
    xs = p + (prev - p) * mu_ref[...]
    grows = WKV_GROUP * WKV_CHUNK
    for grp in range(p.shape[0] // grows):
        rs = slice(grp * grows, (grp + 1) * grows)
        x = xs[rs]
        r = x[:, 0:w]
        k = x[:, w:2 * w]
        v = x[:, 2 * w:3 * w]
        o = 3 * w
        wd = x[:, o:o + DECAY_LORA]
        ad = x[:, o + DECAY_LORA:o + DECAY_LORA + ICLR_LORA]
        gd = x[:, o + DECAY_LORA + ICLR_LORA:]

        z = -(w0_ref[...] + _mm(jnp.tanh(wd), w2_ref[...]))
        softplus = jnp.maximum(z, 0.0) + jnp.log(1.0 + jnp.exp(-jnp.abs(z)))
        logw = -softplus - 0.5
        lw = -jnp.exp(logw)
        a = _sigmoid(a0_ref[...] + _mm(ad, a2_ref[...]))
        gate_out[0, rs] = _mm(_sigmoid(gd), g2_ref[...])
        if has_vres:
            mix = _sigmoid(v0_ref[...] + _mm(_mm(v, v1_ref[...]), v2_ref[...]))
            v = v + (vf_ref[0, rs] - v) * mix
        else:
            v_out[0, rs] = v
        kk = k * kk_ref[...]
        ss = _head_sums(kk * kk, split=False)
        kk = kk * lax.rsqrt(jnp.maximum(ss, 1e-24))
        k = k * (1.0 + (a - 1.0) * ka_ref[...])
        bonus_out[0, rs] = _head_sums(r * k * rk_ref[...]) * v
        _chunk_maps(grp, r, lw, k, v, kk, kk * a, q_out.at[0], y1_out.at[0], m_out.at[0],
                    gm_out.at[0], *scratch)


def _rwkv_maps(proj, mu, w0, w2, a0, a2, g2, k_k, k_a, r_k, v_first, vres, ts=512):
    bsz, seq, _ = proj.shape
    w = GROUP_WIDTH
    nt = seq // ts
    has_vres = vres is not None

    def full(shape):
        return pl.BlockSpec(shape, lambda b, i: (0,) * len(shape))

    row = lambda a: a.reshape(1, -1)
    tile = pl.BlockSpec((1, ts, w), lambda b, i: (b, i, 0))
    in_specs = [
        pl.BlockSpec((pl.Element(1), pl.Element(ts), pl.Element(RWKV_IN_WIDTH)),
                     lambda b, i: (b, i * ts, OFF_RW)),
        pl.BlockSpec((pl.Element(1), pl.Element(PREV_ROWS), pl.Element(RWKV_IN_WIDTH)),
                     lambda b, i: (b, jnp.maximum(i * (ts // PREV_ROWS) - 1, 0) * PREV_ROWS, OFF_RW)),
        full((1, RWKV_IN_WIDTH)), full((1, w)), full((DECAY_LORA, w)), full((1, w)),
        full((ICLR_LORA, w)), full((GATE_LORA, w)), full((1, w)), full((1, w)), full((1, w)),
    ]
    args = [proj, proj, row(mu), row(w0), w2, row(a0), a2, g2, row(k_k), row(k_a), row(r_k)]
    if has_vres:
        v0, v1, v2 = vres
        in_specs += [tile, full((1, w)), full(v1.shape), full(v2.shape)]
        args += [v_first, row(v0), v1, v2]
    f32 = jax.ShapeDtypeStruct((bsz, seq, w), F32)
    bf16 = jax.ShapeDtypeStruct((bsz, seq, w), BF16)
    out_shape = [bf16, f32, bf16, f32, f32, f32] + ([] if has_vres else [f32])
    return pl.pallas_call(
        functools.partial(_rwkv_maps_kernel, has_vres=has_vres),
        grid=(bsz, nt),
        in_specs=in_specs,
        out_specs=[tile] * len(out_shape),
        out_shape=out_shape,
        scratch_shapes=[pltpu.VMEM((ts, w), dt) for dt in (BF16, F32, BF16, BF16, F32, BF16, BF16)],
        compiler_params=_params("parallel", "parallel"),
        name="rwkv_chunk_maps",
    )(*args)


def _chunk_maps(grp, r, lw, k, v, kk, b, q_out, y1_out, m_out, g_out,
                at_ref, rt_ref, bt_ref, kt_ref, dec_ref, vb_ref, rb_ref):
    c = WKV_CHUNK
    n = HEAD_DIM
    rows = r.shape[0]
    rs = slice(grp * rows, (grp + 1) * rows)

    row = lax.broadcasted_iota(jnp.int32, (rows, rows), 0)
    col = lax.broadcasted_iota(jnp.int32, (rows, rows), 1)
    tri = ((row >= col) & ((row // c) == (col // c))).astype(BF16)
    lw_hi, lw_lo = _split_bf16(lw)
    cum = (jnp.dot(tri, lw_hi, preferred_element_type=F32)
           + jnp.dot(tri, lw_lo, preferred_element_type=F32))
    e_pos = jnp.exp(cum)
    e_neg = jnp.exp(-cum)
    r_t = r * e_pos
    at_ref[rs] = (-kk * jnp.exp(cum - lw)).astype(BF16)
    rt_ref[rs] = r_t
    rb_ref[rs] = r_t.astype(BF16)
    bt_ref[rs] = (b * e_neg).astype(BF16)
    kt_ref[rs] = (k * e_neg).astype(BF16)
    dec_ref[rs] = e_pos
    vb_ref[rs] = v.astype(BF16)

    crow = lax.broadcasted_iota(jnp.int32, (c, c), 0)
    ccol = lax.broadcasted_iota(jnp.int32, (c, c), 1)
    lower = crow >= ccol
    strict = crow > ccol
    eye = (crow == ccol).astype(F32)
    wrow = lax.broadcasted_iota(jnp.int32, (c, 2 * c), 0)
    wcol = lax.broadcasted_iota(jnp.int32, (c, 2 * c), 1)
    strict_left = wrow > wcol
    right = wcol >= c
    eye_right = (wcol == wrow + c).astype(F32)

    def tile(ref, j, h):
        return ref[j * c:(j + 1) * c, h * n:(h + 1) * n]

    _chunk_group(range(grp * WKV_GROUP, (grp + 1) * WKV_GROUP), tile, at_ref, rt_ref, bt_ref,
                 kt_ref, dec_ref, vb_ref, rb_ref, q_out, y1_out, m_out, g_out,
                 lower, strict, eye, strict_left, right, eye_right)


def _chunk_group(chunks, tile, at_ref, rt_ref, bt_ref, kt_ref, dec_ref, vb_ref, rb_ref,
                 q_out, y1_out, m_out, g_out, lower, strict, eye, strict_left, right, eye_right):
    c = WKV_CHUNK
    n = HEAD_DIM
    pairs = [(j, h) for j in chunks for h in range(N_HEADS)]
    ah = [tile(at_ref, j, h) for j, h in pairs]
    rh = [tile(rt_ref, j, h) for j, h in pairs]
    bh = [tile(bt_ref, j, h) for j, h in pairs]
    kh = [tile(kt_ref, j, h) for j, h in pairs]
    vh = [tile(vb_ref, j, h) for j, h in pairs]
    dh = [dec_ref[(j + 1) * c - 1:(j + 1) * c, h * n:(h + 1) * n] for j, h in pairs]
    idx = range(len(pairs))
    ar = [jnp.concatenate([ah[i], tile(rb_ref, *pairs[i])], axis=0) for i in idx]
    bk = [jnp.concatenate([bh[i], kh[i]], axis=0) for i in idx]
    aa = [_mm_nt(ar[i], bk[i]) for i in idx]
    a_kr = [jnp.concatenate([jnp.where(strict, aa[i][:c, c:], 0.0),
                             jnp.where(lower, aa[i][c:, c:], 0.0)], axis=0).astype(BF16)
            for i in idx]
    a_rb = [jnp.where(lower, aa[i][c:, :c], 0.0).astype(BF16) for i in idx]
    ps = [jnp.where(strict_left, aa[i][:c, :], 0.0) + eye_right for i in idx]
    for _ in range(int(math.log2(c))):
        ps = [_mm(ps[i][:, :c], ps[i]) + jnp.where(right, ps[i], 0.0) for i in idx]
    t = [ps[i][:, c:].astype(BF16) for i in idx]
    av = [_mm(a_kr[i], vh[i]) for i in idx]
    wu = [_mm(t[i], jnp.concatenate([ah[i], av[i][:c].astype(BF16)], axis=1)).astype(BF16)
          for i in idx]
    aw = [_mm(a_rb[i], wu[i]) for i in idx]
    q = [rh[i] + aw[i][:, :n] for i in idx]
    y1 = [aw[i][:, n:] + av[i][c:] for i in idx]
    zero = jnp.zeros((c, n), BF16)
    mg = [_mm_tn(jnp.concatenate([wu[i], jnp.concatenate([zero, vh[i]], axis=1)], axis=0), bk[i])
          for i in idx]
    m = [(eye + mg[i][:n]) * dh[i] for i in idx]
    g = [mg[i][n:] * dh[i] for i in idx]
    for j in chunks:
        sel = slice((j - chunks[0]) * N_HEADS, (j - chunks[0] + 1) * N_HEADS)
        q_out[j * c:(j + 1) * c, :] = jnp.concatenate(q[sel], axis=-1).astype(q_out.dtype)
        y1_out[j * c:(j + 1) * c, :] = jnp.concatenate(y1[sel], axis=-1)
        m_out[j * c:(j + 1) * c, :] = jnp.concatenate(m[sel], axis=-1).astype(m_out.dtype)
        g_out[j * c:(j + 1) * c, :] = jnp.concatenate(g[sel], axis=-1)


def _wkv_scan_kernel(q_ref, y1_ref, m_ref, gm_ref, bonus_ref, gate_ref, lng_ref, lnb_ref,
                     o_ref, st_ref):
    n = HEAD_DIM
    bsz = q_ref.shape[0]

    @pl.when(pl.program_id(0) == 0)
    def _():
        st_ref[...] = jnp.zeros(st_ref.shape, F32)

    c = WKV_CHUNK
    rows = q_ref.shape[1]
    chains = [(bi, h) for bi in range(bsz) for h in range(N_HEADS)]
    state = [st_ref[bi * N_HEADS + h] for bi, h in chains]
    entering = []
    for j in range(rows // c):
        rs = slice(j * c, (j + 1) * c)
        s_in = [s.astype(BF16) for s in state]
        entering.append(s_in)
        state = [_mm(s_in[i], m_ref[bi, rs, h * n:(h + 1) * n]) + gm_ref[bi, rs, h * n:(h + 1) * n]
                 for i, (bi, h) in enumerate(chains)]
    for i, (bi, h) in enumerate(chains):
        st_ref[bi * N_HEADS + h] = state[i]
    ys = [[_mm_nt(q_ref[bi, j * c:(j + 1) * c, h * n:(h + 1) * n], entering[j][i])
           for i, (bi, h) in enumerate(chains)] for j in range(rows // c)]
    y = jnp.concatenate(
        [jnp.concatenate([jnp.concatenate(ys[j][bi * N_HEADS:(bi + 1) * N_HEADS], axis=-1)
                          for j in range(rows // c)], axis=0) + y1_ref[bi]
         for bi in range(bsz)], axis=0)
    mean = _head_sums(y) * (1.0 / n)
    yc = y - mean
    var = _head_sums(yc * yc) * (1.0 / n)
    yn = yc * lax.rsqrt(var + LN_X_EPS) * lng_ref[...] + lnb_ref[...]
    for bi in range(bsz):
        o_ref[bi] = ((yn[bi * rows:(bi + 1) * rows] + bonus_ref[bi]) * gate_ref[bi]).astype(o_ref.dtype)


def _wkv_scan(q, y1, m, gm, bonus, gate, ln_g, ln_b, rows=2 * WKV_CHUNK):
    bsz, seq, w = y1.shape
    tile = pl.BlockSpec((bsz, rows, w), lambda ci: (0, ci, 0))
    vec = pl.BlockSpec((1, w), lambda ci: (0, 0))
    return pl.pallas_call(
        _wkv_scan_kernel,
        grid=(seq // rows,),
        in_specs=[tile] * 6 + [vec] * 2,
        out_specs=tile,
        out_shape=jax.ShapeDtypeStruct((bsz, seq, w), BF16),
        scratch_shapes=[pltpu.VMEM((bsz * N_HEADS, HEAD_DIM, HEAD_DIM), F32)],
        compiler_params=_params("arbitrary"),
        name="wkv_state_scan",
    )(q, y1, m, gm, bonus, gate, ln_g.reshape(1, w), ln_b.reshape(1, w))


def _wout_kernel(x_ref, y0_ref, y1_ref, y2_ref, y3_ref, w_ref, o_ref):
    acc = x_ref[...]
    for idx, y_ref in enumerate((y0_ref, y1_ref, y2_ref, y3_ref)):
        acc = acc + jnp.dot(y_ref[...], w_ref[idx * GROUP_WIDTH:(idx + 1) * GROUP_WIDTH, :],
                            preferred_element_type=F32)
    o_ref[...] = acc


def _wout(x2d, ys, w_bf16, tm=512):
    m, d = x2d.shape
    ytile = pl.BlockSpec((tm, GROUP_WIDTH), lambda i: (i, 0))
    xtile = pl.BlockSpec((tm, d), lambda i: (i, 0))
    return pl.pallas_call(
        _wout_kernel,
        grid=(m // tm,),
        in_specs=[xtile] + [ytile] * 4 + [pl.BlockSpec(w_bf16.shape, lambda i: (0, 0))],
        out_specs=xtile,
        out_shape=jax.ShapeDtypeStruct((m, d), F32),
        compiler_params=_params("parallel"),
        name="wout_residual",
    )(x2d, *ys, w_bf16)


def _ffn_kernel(x_ref, g_ref, wu_ref, wd_ref, o_ref, h_ref):
    @pl.when(pl.program_id(1) == 0)
    def _():
        x = x_ref[...]
        ms = jnp.mean(x * x, axis=-1, keepdims=True)
        h_ref[...] = (x * lax.rsqrt(ms + RMS_EPS) * g_ref[0]).astype(BF16)
        o_ref[...] = x

    u = jnp.dot(h_ref[...], wu_ref[0].astype(BF16), preferred_element_type=F32)
    act = jnp.square(jnp.maximum(u, 0.0)).astype(BF16)
    o_ref[...] += jnp.dot(act, wd_ref[0].astype(BF16), preferred_element_type=F32)


def _ffn(x2d, gains, wu_stack, wd_stack, layer, tm=1024, tf=512):
    m, d = x2d.shape
    f = wu_stack.shape[2]
    xtile = pl.BlockSpec((tm, d), lambda i, j: (i, 0))
    return pl.pallas_call(
        _ffn_kernel,
        grid=(m // tm, f // tf),
        in_specs=[
            xtile,
            pl.BlockSpec((1, 1, d), lambda i, j: (layer, 0, 0)),
            pl.BlockSpec((1, d, tf), lambda i, j: (layer, 0, j)),
            pl.BlockSpec((1, tf, d), lambda i, j: (layer, j, 0)),
        ],
        out_specs=xtile,
        out_shape=jax.ShapeDtypeStruct((m, d), F32),
        scratch_shapes=[pltpu.VMEM((tm, d), BF16)],
        compiler_params=_params("parallel", "arbitrary", vmem_limit=FFN_VMEM_LIMIT),
        name="ffn",
    )(x2d, gains.reshape(-1, 1, d), wu_stack, wd_stack)


def kernel(x, norm_mix, w_in, conv_w, swa_q_norm, swa_k_norm, swa_sink, dil_q_norm, dil_k_norm,
           rwkv_mu, decay_w0, decay_w2, iclr_a0, iclr_a2, gate_g2, k_k, k_a, r_k, ln_x_g, ln_x_b,
           vres_v0, vres_v1, vres_v2, w_out, norm_ffn, w_up, w_down, rel_bias):
    bsz, seq, d = x.shape
    depth = w_in.shape[0]
    bias = _bias_tiles(rel_bias)
    swa_bias = bias[0, :N_HEADS]
    dil_bias = bias[:, N_HEADS:]
    x2d = x.reshape(bsz * seq, d)
    v_first = None
    for layer in range(depth):
        proj = _norm_matmul(x2d, norm_mix, w_in, layer).reshape(bsz, seq, IN_WIDTH)
        y_swa, y_conv = _swa_conv_mixers(proj, swa_q_norm[layer], swa_k_norm[layer],
                                         swa_sink[layer], swa_bias, conv_w[layer])
        y_dil = _dil_mixer(proj, dil_q_norm[layer], dil_k_norm[layer], dil_bias)
        vres = None if layer == 0 else (vres_v0[layer - 1], vres_v1[layer - 1], vres_v2[layer - 1])
        maps = _rwkv_maps(
            proj, rwkv_mu[layer], decay_w0[layer], decay_w2[layer], iclr_a0[layer], iclr_a2[layer],
            gate_g2[layer], k_k[layer], k_a[layer], r_k[layer], v_first, vres)
        if layer == 0:
            v_first = maps[6]
        y_rwkv = _wkv_scan(*maps[:6], ln_x_g[layer], ln_x_b[layer])
        ys = [y.reshape(bsz * seq, GROUP_WIDTH) for y in (y_conv, y_swa, y_dil, y_rwkv)]
        x2d = _wout(x2d, ys, w_out[layer].astype(BF16))
        x2d = _ffn(x2d, norm_ffn, w_up, w_down, layer)
    return x2d.reshape(bsz, seq, d)
```

```python
import functools
import math

import jax
import jax.numpy as jnp
from jax import lax
from jax.experimental import pallas as pl
from jax.experimental.pallas import tpu as pltpu

F32 = jnp.float32
BF16 = jnp.bfloat16

D_MODEL = 2048
HEAD_DIM = 64
GROUP_WIDTH = 512
N_HEADS = GROUP_WIDTH // HEAD_DIM
SWA_KV_HEADS = 2
SWA_GROUP = N_HEADS // SWA_KV_HEADS
SWA_WINDOW = 128
DIL_PAIRS = ((128, 1), (512, 4), (2048, 16))
DECAY_LORA = 64
ICLR_LORA = 64
GATE_LORA = 128
RWKV_IN_WIDTH = 3 * GROUP_WIDTH + DECAY_LORA + ICLR_LORA + GATE_LORA
BLK = 128
NUM_BUCKETS = 32
BUCKET_MAX_DIST = 128
RMS_EPS = 1e-6
LN_X_EPS = 64e-5
NEG = -1e30
LOG2E = math.log2(math.e)
WKV_CHUNK = 64
WKV_GROUP = 4
DIL_TILES = 4
PREV_ROWS = 16
PREP_UNROLL = 4

OFF_CONV = 0
OFF_SWA_Q = OFF_CONV + 3 * GROUP_WIDTH
OFF_SWA_K = OFF_SWA_Q + GROUP_WIDTH
OFF_SWA_V = OFF_SWA_K + SWA_KV_HEADS * HEAD_DIM
OFF_DIL = OFF_SWA_V + SWA_KV_HEADS * HEAD_DIM
OFF_RW = OFF_DIL + 3 * GROUP_WIDTH
IN_WIDTH = OFF_RW + RWKV_IN_WIDTH

V7X_VMEM_BYTES = 64 * 1024 * 1024
VMEM_LIMIT = 48 * 1024 * 1024
FFN_VMEM_LIMIT = V7X_VMEM_BYTES - 6 * 1024 * 1024


def _params(*sem, vmem_limit=VMEM_LIMIT):
    return pltpu.CompilerParams(dimension_semantics=sem, vmem_limit_bytes=vmem_limit)


def _mm(a, b):
    return jnp.dot(a.astype(BF16), b.astype(BF16), preferred_element_type=F32)


def _mm_nt(a, b):
    return lax.dot_general(a.astype(BF16), b.astype(BF16), (((1,), (1,)), ((), ())),
                           preferred_element_type=F32)


def _mm_tn(a, b):
    return lax.dot_general(a.astype(BF16), b.astype(BF16), (((0,), (0,)), ((), ())),
                           preferred_element_type=F32)


def _split_bf16(x):
    hi = x.astype(BF16)
    lo = (x - hi.astype(F32)).astype(BF16)
    return hi, lo


def _head_sums(x, split=True):
    lanes = 128
    r = lax.broadcasted_iota(jnp.int32, (lanes, lanes), 0) // HEAD_DIM
    c = lax.broadcasted_iota(jnp.int32, (lanes, lanes), 1) // HEAD_DIM
    bd = (r == c).astype(BF16)
    parts = _split_bf16(x) if split else (x.astype(BF16),)
    cols = []
    for j in range(x.shape[-1] // lanes):
        sl = slice(j * lanes, (j + 1) * lanes)
        cols.append(sum(jnp.dot(part[:, sl], bd, preferred_element_type=F32) for part in parts))
    return cols[0] if len(cols) == 1 else jnp.concatenate(cols, axis=-1)


def _sigmoid(z):
    return 1.0 / (1.0 + jnp.exp(-z))


def _norm_matmul_kernel(x_ref, g_ref, w_ref, o_ref, h_ref):
    @pl.when(pl.program_id(1) == 0)
    def _():
        x = x_ref[...]
        ms = jnp.mean(x * x, axis=-1, keepdims=True)
        h_ref[...] = (x * lax.rsqrt(ms + RMS_EPS) * g_ref[0]).astype(BF16)

    o_ref[...] = jnp.dot(h_ref[...], w_ref[0].astype(BF16),
                         preferred_element_type=F32).astype(o_ref.dtype)


def _norm_matmul(x2d, gains, w_stack, layer, tm=2048, tn=512):
    m, k = x2d.shape
    n = w_stack.shape[2]
    assert n % tn == 0 and m % tm == 0
    return pl.pallas_call(
        _norm_matmul_kernel,
        grid=(m // tm, n // tn),
        in_specs=[
            pl.BlockSpec((tm, k), lambda i, j: (i, 0)),
            pl.BlockSpec((1, 1, k), lambda i, j: (layer, 0, 0)),
            pl.BlockSpec((1, k, tn), lambda i, j: (layer, 0, j)),
        ],
        out_specs=pl.BlockSpec((tm, tn), lambda i, j: (i, j)),
        out_shape=jax.ShapeDtypeStruct((m, n), BF16),
        scratch_shapes=[pltpu.VMEM((tm, k), BF16)],
        compiler_params=_params("parallel", "arbitrary", vmem_limit=FFN_VMEM_LIMIT),
        name="norm_matmul",
    )(x2d, gains.reshape(-1, 1, k), w_stack)


def _t5_bucket(dist):
    dist = jnp.maximum(dist, 0)
    max_exact = NUM_BUCKETS // 2
    scaled = (jnp.log(jnp.maximum(dist, 1).astype(F32) / max_exact)
              / math.log(BUCKET_MAX_DIST / max_exact))
    large = max_exact + (scaled * (NUM_BUCKETS - max_exact)).astype(jnp.int32)
    large = jnp.minimum(large, NUM_BUCKETS - 1)
    return jnp.where(dist < max_exact, dist, large)


def _bias_kernel(bucket_ref, table_ref, o_ref):
    bucket = bucket_ref[0]

    def head(h, carry):
        g = jnp.zeros(bucket.shape, F32)
        for b in range(NUM_BUCKETS):
            g = jnp.where(bucket == b, table_ref[b, h], g)
        rows = jnp.broadcast_to(g[0:1], (BLK, 2 * BLK))
        o_ref[0, h] = pltpu.roll(rows, 0, axis=1, stride=1, stride_axis=0)
        return carry

    lax.fori_loop(0, o_ref.shape[1], head, 0)


def _bias_tiles(rel_bias):
    lag_dist = BLK - jnp.arange(2 * BLK)
    buckets = jnp.stack([_t5_bucket(lag_dist * r) for _, r in DIL_PAIRS]).astype(jnp.int32)
    buckets = jnp.broadcast_to(buckets[:, None, :], (len(DIL_PAIRS), 8, 2 * BLK))
    nh = rel_bias.shape[1]
    return pl.pallas_call(
        _bias_kernel,
        grid=(len(DIL_PAIRS),),
        in_specs=[
            pl.BlockSpec((1, 8, 2 * BLK), lambda s: (s, 0, 0)),
            pl.BlockSpec(memory_space=pltpu.SMEM),
        ],
        out_specs=pl.BlockSpec((1, nh, BLK, 2 * BLK), lambda s: (s, 0, 0, 0)),
        out_shape=jax.ShapeDtypeStruct((len(DIL_PAIRS), nh, BLK, 2 * BLK), F32),
        compiler_params=_params("arbitrary"),
        name="bias_tiles",
    )(buckets, rel_bias)


def _short_conv(b_ref, c_ref, u_ref, w_ref, o_ref):
    lanes = 128
    for j in range(b_ref.shape[2] // lanes):
        cols = slice(j * lanes, (j + 1) * lanes)
        z = c_ref[0, :, cols].astype(F32) * u_ref[0, :, cols].astype(F32)
        row = lax.broadcasted_iota(jnp.int32, z.shape, 0)
        z1 = jnp.where(row >= 1, pltpu.roll(z, 1, axis=0), 0.0)
        z2 = jnp.where(row >= 2, pltpu.roll(z, 2, axis=0), 0.0)
        w = w_ref[:, cols]
        y = z2 * w[0:1, :] + z1 * w[1:2, :] + z * w[2:3, :]
        o_ref[0, :, cols] = (b_ref[0, :, cols].astype(F32) * y).astype(o_ref.dtype)


def _band_mask(max_dist):
    a = lax.broadcasted_iota(jnp.int32, (BLK, 2 * BLK), 0)
    b = lax.broadcasted_iota(jnp.int32, (BLK, 2 * BLK), 1)
    dist = BLK + a - b
    return (dist >= 0) & (dist <= max_dist), b


def _attend(qs, kws, vws, biases, sinks=None):
    idx = range(len(qs))
    s = [lax.dot_general(qs[i], kws[i], (((1,), (1,)), ((), ())), preferred_element_type=F32)
         + biases[i] for i in idx]
    m = [jnp.max(s[i], axis=-1, keepdims=True) for i in idx]
    if sinks is not None:
        m = [jnp.maximum(m[i], sinks[i]) for i in idx]
    p = [jnp.exp2(s[i] - m[i]) for i in idx]
    den = [jnp.sum(p[i], axis=-1, keepdims=True) for i in idx]
    if sinks is not None:
        den = [den[i] + jnp.exp2(sinks[i] - m[i]) for i in idx]
    o = [jnp.dot(p[i].astype(BF16), vws[i], preferred_element_type=F32) / den[i] for i in idx]
    return o, m, den


def _head_rms(x, gain):
    ms = _head_sums(x * x, split=False) * (1.0 / HEAD_DIM)
    return x * lax.rsqrt(ms + RMS_EPS) * gain


def _swa_kernel(q_ref, k_ref, v_ref, qg_ref, kg_ref, sink_ref, bias_ref,
                cb_ref, cc_ref, cu_ref, cw_ref, o_ref, oc_ref,
                qn_ref, kn_ref, vb_ref, bm_ref):
    seq = q_ref.shape[1]
    nb = seq // BLK
    kvw = SWA_KV_HEADS * HEAD_DIM
    scale = HEAD_DIM ** -0.5 * LOG2E

    _short_conv(cb_ref, cc_ref, cu_ref, cw_ref, oc_ref)

    kn_ref[0:BLK, :] = jnp.zeros((BLK, 2 * kvw), BF16)
    vb_ref[0:BLK, :] = jnp.zeros((BLK, 2 * kvw), BF16)
    lane_half = lax.broadcasted_iota(jnp.int32, (1, kvw), 1) // HEAD_DIM

    def both_halves(x):
        xr = pltpu.roll(x, HEAD_DIM, axis=1)
        return jnp.concatenate([jnp.where(lane_half == 0, x, xr),
                                jnp.where(lane_half == 0, xr, x)], axis=-1)

    def prep(i, carry):
        r0 = pl.multiple_of(i * BLK, BLK)
        q = q_ref[0, pl.ds(r0, BLK), :].astype(F32)
        qn_ref[pl.ds(r0, BLK), :] = (_head_rms(q, qg_ref[...]) * scale).astype(BF16)
        k = _head_rms(k_ref[0, pl.ds(r0, BLK), :].astype(F32), kg_ref[...])
        kn_ref[pl.ds(r0 + BLK, BLK), :] = both_halves(k).astype(BF16)
        v = v_ref[0, pl.ds(r0, BLK), :].astype(F32)
        vb_ref[pl.ds(r0 + BLK, BLK), :] = both_halves(v).astype(BF16)
        return carry

    lax.fori_loop(0, nb, prep, 0, unroll=PREP_UNROLL)

    band, kcol = _band_mask(SWA_WINDOW - 1)
    for h in range(N_HEADS):
        bm_ref[h] = jnp.where(band & (kcol >= BLK), bias_ref[h] * LOG2E, NEG)
        bm_ref[N_HEADS + h] = jnp.where(band, bias_ref[h] * LOG2E, NEG)
    sinks = [sink_ref[h] * LOG2E for h in range(N_HEADS)]

    def block(i, carry):
        r0 = pl.multiple_of(i * BLK, BLK)
        later = jnp.minimum(i, 1) * N_HEADS
        qs, kws, vws, bms = [], [], [], []
        for hk in range(SWA_KV_HEADS):
            kw = kn_ref[pl.ds(r0, 2 * BLK), hk * kvw:(hk + 1) * kvw]
            vw = vb_ref[pl.ds(r0, 2 * BLK), hk * kvw:(hk + 1) * kvw]
            for g in range(SWA_GROUP):
                h = hk * SWA_GROUP + g
                q2 = qn_ref[pl.ds(r0, BLK), (h // 2) * kvw:(h // 2 + 1) * kvw]
                qs.append(jnp.where(lane_half == h % 2, q2, jnp.zeros_like(q2)))
                kws.append(kw)
                vws.append(vw)
                bms.append(bm_ref[later + h])
        outs, _, _ = _attend(qs, kws, vws, bms, sinks)
        pairs = [jnp.where(lane_half == 0, outs[h], outs[h + 1]) for h in range(0, N_HEADS, 2)]
        o_ref[0, pl.ds(r0, BLK), :] = jnp.concatenate(pairs, axis=-1).astype(o_ref.dtype)
        return carry

    lax.fori_loop(0, nb, block, 0)


def _swa_conv_mixers(proj, q_gain, k_gain, sink, bias, conv_w):
    bsz, seq, _ = proj.shape
    kvw = SWA_KV_HEADS * HEAD_DIM
    q_gain_t = jnp.tile(q_gain, N_HEADS).reshape(1, GROUP_WIDTH)
    k_gain_t = jnp.tile(k_gain, SWA_KV_HEADS).reshape(1, kvw)
    wide = lambda off: pl.BlockSpec((1, seq, GROUP_WIDTH), lambda b: (b, 0, off // GROUP_WIDTH))
    out = jax.ShapeDtypeStruct((bsz, seq, GROUP_WIDTH), BF16)
    return pl.pallas_call(
        _swa_kernel,
        grid=(bsz,),
        in_specs=[
            wide(OFF_SWA_Q),
            pl.BlockSpec((1, seq, kvw), lambda b: (b, 0, OFF_SWA_K // kvw)),
            pl.BlockSpec((1, seq, kvw), lambda b: (b, 0, OFF_SWA_V // kvw)),
            pl.BlockSpec((1, GROUP_WIDTH), lambda b: (0, 0)),
            pl.BlockSpec((1, kvw), lambda b: (0, 0)),
            pl.BlockSpec(memory_space=pltpu.SMEM),
            pl.BlockSpec((N_HEADS, BLK, 2 * BLK), lambda b: (0, 0, 0)),
            wide(OFF_CONV), wide(OFF_CONV + GROUP_WIDTH), wide(OFF_CONV + 2 * GROUP_WIDTH),
            pl.BlockSpec((3, GROUP_WIDTH), lambda b: (0, 0)),
        ],
        out_specs=[pl.BlockSpec((1, seq, GROUP_WIDTH), lambda b: (b, 0, 0))] * 2,
        out_shape=[out, out],
        scratch_shapes=[
            pltpu.VMEM((seq, GROUP_WIDTH), BF16),
            pltpu.VMEM((seq + BLK, 2 * kvw), BF16),
            pltpu.VMEM((seq + BLK, 2 * kvw), BF16),
            pltpu.VMEM((2 * N_HEADS, BLK, 2 * BLK), F32),
        ],
        compiler_params=_params("parallel"),
        name="swa_conv_mixers",
    )(proj, proj, proj, q_gain_t, k_gain_t, sink, bias, proj, proj, proj, conv_w)


def _dil_kernel(q_ref, k_ref, v_ref, qg_ref, kg_ref, bias_ref, o_ref,
                qn_ref, kn_ref, vn_ref, ob_ref, lb_ref, bm_ref):
    seq = q_ref.shape[1]
    lanes = q_ref.shape[2]
    heads = lanes // HEAD_DIM
    scale = HEAD_DIM ** -0.5 * LOG2E

    def prep(i, carry):
        r0 = pl.multiple_of(i * BLK, BLK)
        q = q_ref[0, pl.ds(r0, BLK), :].astype(F32)
        k = k_ref[0, pl.ds(r0, BLK), :].astype(F32)
        qn_ref[pl.ds(r0, BLK), :] = _head_rms(q, qg_ref[...]) * scale
        kn_ref[pl.ds(r0, BLK), :] = _head_rms(k, kg_ref[...])
        vn_ref[pl.ds(r0, BLK), :] = v_ref[0, pl.ds(r0, BLK), :].astype(F32)
        return carry

    lax.fori_loop(0, seq // BLK, prep, 0, unroll=PREP_UNROLL)

    for br, (window, r) in enumerate(DIL_PAIRS):
        band, kcol = _band_mask(window // r)
        for h in range(heads):
            bias = bias_ref[br, h] * LOG2E
            bm_ref[(br * heads + h) * 2] = jnp.where(band & (kcol >= BLK), bias, NEG)
            bm_ref[(br * heads + h) * 2 + 1] = jnp.where(band, bias, NEG)

    lane_head = lax.broadcasted_iota(jnp.int32, (1, lanes), 1) // HEAD_DIM

    for br, (window, r) in enumerate(DIL_PAIRS):
        nb = seq // r // BLK

        def blocks(it, carry, br=br, r=r, nb=nb):
            qs, kws, vws, bms, curs = [], [], [], [], []
            for u in range(DIL_TILES):
                t = it * DIL_TILES + u
                c = t // nb
                i = t - c * nb
                cur = c + i * (BLK * r)
                prev = jnp.maximum(cur - BLK * r, c)
                later = jnp.minimum(i, 1)

                def rows(ref, start):
                    if r == 1:
                        return ref[pl.ds(start, BLK), :]
                    return ref[pl.ds(start, BLK, stride=r), :]

                q = rows(qn_ref, cur).astype(BF16)
                if nb == 1:
                    kw = rows(kn_ref, cur).astype(BF16)
                    vw = rows(vn_ref, cur).astype(BF16)
                else:
                    kw = jnp.concatenate([rows(kn_ref, prev), rows(kn_ref, cur)],
                                         axis=0).astype(BF16)
                    vw = jnp.concatenate([rows(vn_ref, prev), rows(vn_ref, cur)],
                                         axis=0).astype(BF16)
                curs.append(cur)
                for h in range(heads):
                    qs.append(jnp.where(lane_head == h, q, jnp.zeros_like(q)))
                    kws.append(kw)
                    vws.append(vw)
                    if nb == 1:
                        bms.append(bm_ref[(br * heads + h) * 2, :, BLK:])
                    else:
                        bms.append(bm_ref[(br * heads + h) * 2 + later])
            outs, ms, dens = _attend(qs, kws, vws, bms)
            for u in range(DIL_TILES):
                o_all = outs[u * heads]
                l_all = ms[u * heads] + jnp.log2(dens[u * heads])
                for h in range(1, heads):
                    o_all = jnp.where(lane_head == h, outs[u * heads + h], o_all)
                    l_all = jnp.where(lane_head == h,
                                      ms[u * heads + h] + jnp.log2(dens[u * heads + h]), l_all)
                if r == 1:
                    ob_ref[br, pl.ds(curs[u], BLK), :] = o_all
                    lb_ref[br, pl.ds(curs[u], BLK), :] = l_all
                else:
                    ob_ref[br, pl.ds(curs[u], BLK, stride=r), :] = o_all
                    lb_ref[br, pl.ds(curs[u], BLK, stride=r), :] = l_all
            return carry

        lax.fori_loop(0, r * nb // DIL_TILES, blocks, 0)

    def combine(i, carry):
        r0 = pl.multiple_of(i * BLK, BLK)
        l0 = lb_ref[0, pl.ds(r0, BLK), :]
        l1 = lb_ref[1, pl.ds(r0, BLK), :]
        l2 = lb_ref[2, pl.ds(r0, BLK), :]
        m = jnp.maximum(jnp.maximum(l0, l1), l2)
        e0, e1, e2 = jnp.exp2(l0 - m), jnp.exp2(l1 - m), jnp.exp2(l2 - m)
        tot = e0 + e1 + e2
        o = ((e0 / tot) * ob_ref[0, pl.ds(r0, BLK), :]
             + (e1 / tot) * ob_ref[1, pl.ds(r0, BLK), :]
             + (e2 / tot) * ob_ref[2, pl.ds(r0, BLK), :])
        o_ref[0, pl.ds(r0, BLK), :] = o.astype(o_ref.dtype)
        return carry

    lax.fori_loop(0, seq // BLK, combine, 0, unroll=PREP_UNROLL)


def _dil_mixer(proj, q_gain, k_gain, bias):
    bsz, seq, _ = proj.shape
    lanes = 128
    heads = lanes // HEAD_DIM
    nblk = GROUP_WIDTH // lanes
    base = OFF_DIL // lanes
    gq = jnp.tile(q_gain, heads).reshape(1, lanes)
    gk = jnp.tile(k_gain, heads).reshape(1, lanes)

    def col(seg):
        return pl.BlockSpec((1, seq, lanes), lambda b, j: (b, 0, base + seg * nblk + j))

    return pl.pallas_call(
        _dil_kernel,
        grid=(bsz, nblk),
        in_specs=[
            col(0), col(1), col(2),
            pl.BlockSpec((1, lanes), lambda b, j: (0, 0)),
            pl.BlockSpec((1, lanes), lambda b, j: (0, 0)),
            pl.BlockSpec((len(DIL_PAIRS), heads, BLK, 2 * BLK), lambda b, j: (0, j, 0, 0)),
        ],
        out_specs=pl.BlockSpec((1, seq, lanes), lambda b, j: (b, 0, j)),
        out_shape=jax.ShapeDtypeStruct((bsz, seq, GROUP_WIDTH), BF16),
        scratch_shapes=[
            pltpu.VMEM((seq, lanes), F32),
            pltpu.VMEM((seq, lanes), F32),
            pltpu.VMEM((seq, lanes), F32),
            pltpu.VMEM((len(DIL_PAIRS), seq, lanes), F32),
            pltpu.VMEM((len(DIL_PAIRS), seq, lanes), F32),
            pltpu.VMEM((len(DIL_PAIRS) * heads * 2, BLK, 2 * BLK), F32),
        ],
        compiler_params=_params("parallel", "parallel"),
        name="dil_mixer",
    )(proj, proj, proj, gq, gk, bias)


def _rwkv_maps_kernel(*refs, has_vres):
    if has_vres:
        (p_ref, pp_ref, mu_ref, w0_ref, w2_ref, a0_ref, a2_ref, g2_ref, kk_ref, ka_ref, rk_ref,
         vf_ref, v0_ref, v1_ref, v2_ref,
         q_out, y1_out, m_out, gm_out, bonus_out, gate_out, *scratch) = refs
    else:
        (p_ref, pp_ref, mu_ref, w0_ref, w2_ref, a0_ref, a2_ref, g2_ref, kk_ref, ka_ref, rk_ref,
         q_out, y1_out, m_out, gm_out, bonus_out, gate_out, v_out, *scratch) = refs
    w = GROUP_WIDTH
    p = p_ref[0].astype(F32)
    row = lax.broadcasted_iota(jnp.int32, p.shape, 0)
    last_prev = pp_ref[0, PREV_ROWS - 1:PREV_ROWS, :].astype(F32)
    last_prev = jnp.where(pl.program_id(1) > 0, last_prev, 0.0)
    prev = jnp.where(row >= 1, pltpu.roll(p, 1, axis=0), last_prev)
    xs = p + (prev - p) * mu_ref[...]
    grows = WKV_GROUP * WKV_CHUNK
    for grp in range(p.shape[0] // grows):
        rs = slice(grp * grows, (grp + 1) * grows)
        x = xs[rs]
        r = x[:, 0:w]
        k = x[:, w:2 * w]
        v = x[:, 2 * w:3 * w]
        o = 3 * w
        wd = x[:, o:o + DECAY_LORA]
        ad = x[:, o + DECAY_LORA:o + DECAY_LORA + ICLR_LORA]
        gd = x[:, o + DECAY_LORA + ICLR_LORA:]

        z = -(w0_ref[...] + _mm(jnp.tanh(wd), w2_ref[...]))
        softplus = jnp.maximum(z, 0.0) + jnp.log(1.0 + jnp.exp(-jnp.abs(z)))
        logw = -softplus - 0.5
        lw = -jnp.exp(logw)
        a = _sigmoid(a0_ref[...] + _mm(ad, a2_ref[...]))
        gate_out[0, rs] = _mm(_sigmoid(gd), g2_ref[...])
        if has_vres:
            mix = _sigmoid(v0_ref[...] + _mm(_mm(v, v1_ref[...]), v2_ref[...]))
            v = v + (vf_ref[0, rs] - v) * mix
        else:
            v_out[0, rs] = v
        kk = k * kk_ref[...]
        ss = _head_sums(kk * kk, split=False)
        kk = kk * lax.rsqrt(jnp.maximum(ss, 1e-24))
        k = k * (1.0 + (a - 1.0) * ka_ref[...])
        bonus_out[0, rs] = _head_sums(r * k * rk_ref[...]) * v
        _chunk_maps(grp, r, lw, k, v, kk, kk * a, q_out.at[0], y1_out.at[0], m_out.at[0],
                    gm_out.at[0], *scratch)


def _rwkv_maps(proj, mu, w0, w2, a0, a2, g2, k_k, k_a, r_k, v_first, vres, ts=512):
    bsz, seq, _ = proj.shape
    w = GROUP_WIDTH
    nt = seq // ts
    has_vres = vres is not None

    def full(shape):
        return pl.BlockSpec(shape, lambda b, i: (0,) * len(shape))

    row = lambda a: a.reshape(1, -1)
    tile = pl.BlockSpec((1, ts, w), lambda b, i: (b, i, 0))
    in_specs = [
        pl.BlockSpec((pl.Element(1), pl.Element(ts), pl.Element(RWKV_IN_WIDTH)),
                     lambda b, i: (b, i * ts, OFF_RW)),
        pl.BlockSpec((pl.Element(1), pl.Element(PREV_ROWS), pl.Element(RWKV_IN_WIDTH)),
                     lambda b, i: (b, jnp.maximum(i * (ts // PREV_ROWS) - 1, 0) * PREV_ROWS, OFF_RW)),
        full((1, RWKV_IN_WIDTH)), full((1, w)), full((DECAY_LORA, w)), full((1, w)),
        full((ICLR_LORA, w)), full((GATE_LORA, w)), full((1, w)), full((1, w)), full((1, w)),
    ]
    args = [proj, proj, row(mu), row(w0), w2, row(a0), a2, g2, row(k_k), row(k_a), row(r_k)]
    if has_vres:
        v0, v1, v2 = vres
        in_specs += [tile, full((1, w)), full(v1.shape), full(v2.shape)]
        args += [v_first, row(v0), v1, v2]
    f32 = jax.ShapeDtypeStruct((bsz, seq, w), F32)
    bf16 = jax.ShapeDtypeStruct((bsz, seq, w), BF16)
    out_shape = [bf16, f32, bf16, f32, f32, f32] + ([] if has_vres else [f32])
    return pl.pallas_call(
        functools.partial(_rwkv_maps_kernel, has_vres=has_vres),
        grid=(bsz, nt),
        in_specs=in_specs,
        out_specs=[tile] * len(out_shape),
        out_shape=out_shape,
        scratch_shapes=[pltpu.VMEM((ts, w), dt) for dt in (BF16, F32, BF16, BF16, F32, BF16, BF16)],
        compiler_params=_params("parallel", "parallel"),
        name="rwkv_chunk_maps",
    )(*args)


def _chunk_maps(grp, r, lw, k, v, kk, b, q_out, y1_out, m_out, g_out,
                at_ref, rt_ref, bt_ref, kt_ref, dec_ref, vb_ref, rb_ref):
    c = WKV_CHUNK
    n = HEAD_DIM
    rows = r.shape[0]
    rs = slice(grp * rows, (grp + 1) * rows)

    row = lax.broadcasted_iota(jnp.int32, (rows, rows), 0)
    col = lax.broadcasted_iota(jnp.int32, (rows, rows), 1)
    tri = ((row >= col) & ((row // c) == (col // c))).astype(BF16)
    lw_hi, lw_lo = _split_bf16(lw)
    cum = (jnp.dot(tri, lw_hi, preferred_element_type=F32)
           + jnp.dot(tri, lw_lo, preferred_element_type=F32))
    e_pos = jnp.exp(cum)
    e_neg = jnp.exp(-cum)
    r_t = r * e_pos
    at_ref[rs] = (-kk * jnp.exp(cum - lw)).astype(BF16)
    rt_ref[rs] = r_t
    rb_ref[rs] = r_t.astype(BF16)
    bt_ref[rs] = (b * e_neg).astype(BF16)
    kt_ref[rs] = (k * e_neg).astype(BF16)
    dec_ref[rs] = e_pos
    vb_ref[rs] = v.astype(BF16)

    crow = lax.broadcasted_iota(jnp.int32, (c, c), 0)
    ccol = lax.broadcasted_iota(jnp.int32, (c, c), 1)
    lower = crow >= ccol
    strict = crow > ccol
    eye = (crow == ccol).astype(F32)
    wrow = lax.broadcasted_iota(jnp.int32, (c, 2 * c), 0)
    wcol = lax.broadcasted_iota(jnp.int32, (c, 2 * c), 1)
    strict_left = wrow > wcol
    right = wcol >= c
    eye_right = (wcol == wrow + c).astype(F32)

    def tile(ref, j, h):
        return ref[j * c:(j + 1) * c, h * n:(h + 1) * n]

    _chunk_group(range(grp * WKV_GROUP, (grp + 1) * WKV_GROUP), tile, at_ref, rt_ref, bt_ref,
                 kt_ref, dec_ref, vb_ref, rb_ref, q_out, y1_out, m_out, g_out,
                 lower, strict, eye, strict_left, right, eye_right)


def _chunk_group(chunks, tile, at_ref, rt_ref, bt_ref, kt_ref, dec_ref, vb_ref, rb_ref,
                 q_out, y1_out, m_out, g_out, lower, strict, eye, strict_left, right, eye_right):
    c = WKV_CHUNK
    n = HEAD_DIM
    pairs = [(j, h) for j in chunks for h in range(N_HEADS)]
    ah = [tile(at_ref, j, h) for j, h in pairs]
    rh = [tile(rt_ref, j, h) for j, h in pairs]
    bh = [tile(bt_ref, j, h) for j, h in pairs]
    kh = [tile(kt_ref, j, h) for j, h in pairs]
    vh = [tile(vb_ref, j, h) for j, h in pairs]
    dh = [dec_ref[(j + 1) * c - 1:(j + 1) * c, h * n:(h + 1) * n] for j, h in pairs]
    idx = range(len(pairs))
    ar = [jnp.concatenate([ah[i], tile(rb_ref, *pairs[i])], axis=0) for i in idx]
    bk = [jnp.concatenate([bh[i], kh[i]], axis=0) for i in idx]
    aa = [_mm_nt(ar[i], bk[i]) for i in idx]
    a_kr = [jnp.concatenate([jnp.where(strict, aa[i][:c, c:], 0.0),
                             jnp.where(lower, aa[i][c:, c:], 0.0)], axis=0).astype(BF16)
            for i in idx]
    a_rb = [jnp.where(lower, aa[i][c:, :c], 0.0).astype(BF16) for i in idx]
    ps = [jnp.where(strict_left, aa[i][:c, :], 0.0) + eye_right for i in idx]
    for _ in range(int(math.log2(c))):
        ps = [_mm(ps[i][:, :c], ps[i]) + jnp.where(right, ps[i], 0.0) for i in idx]
    t = [ps[i][:, c:].astype(BF16) for i in idx]
    av = [_mm(a_kr[i], vh[i]) for i in idx]
    wu = [_mm(t[i], jnp.concatenate([ah[i], av[i][:c].astype(BF16)], axis=1)).astype(BF16)
          for i in idx]
    aw = [_mm(a_rb[i], wu[i]) for i in idx]
    q = [rh[i] + aw[i][:, :n] for i in idx]
    y1 = [aw[i][:, n:] + av[i][c:] for i in idx]
    zero = jnp.zeros((c, n), BF16)
    mg = [_mm_tn(jnp.concatenate([wu[i], jnp.concatenate([zero, vh[i]], axis=1)], axis=0), bk[i])
          for i in idx]
    m = [(eye + mg[i][:n]) * dh[i] for i in idx]
    g = [mg[i][n:] * dh[i] for i in idx]
    for j in chunks:
        sel = slice((j - chunks[0]) * N_HEADS, (j - chunks[0] + 1) * N_HEADS)
        q_out[j * c:(j + 1) * c, :] = jnp.concatenate(q[sel], axis=-1).astype(q_out.dtype)
        y1_out[j * c:(j + 1) * c, :] = jnp.concatenate(y1[sel], axis=-1)
        m_out[j * c:(j + 1) * c, :] = jnp.concatenate(m[sel], axis=-1).astype(m_out.dtype)
        g_out[j * c:(j + 1) * c, :] = jnp.concatenate(g[sel], axis=-1)


def _wkv_scan_kernel(q_ref, y1_ref, m_ref, gm_ref, bonus_ref, gate_ref, lng_ref, lnb_ref,
                     o_ref, st_ref):
    n = HEAD_DIM
    bsz = q_ref.shape[0]

    @pl.when(pl.program_id(0) == 0)
    def _():
        st_ref[...] = jnp.zeros(st_ref.shape, F32)

    c = WKV_CHUNK
    rows = q_ref.shape[1]
    chains = [(bi, h) for bi in range(bsz) for h in range(N_HEADS)]
    state = [st_ref[bi * N_HEADS + h] for bi, h in chains]
    entering = []
    for j in range(rows // c):
        rs = slice(j * c, (j + 1) * c)
        s_in = [s.astype(BF16) for s in state]
        entering.append(s_in)
        state = [_mm(s_in[i], m_ref[bi, rs, h * n:(h + 1) * n]) + gm_ref[bi, rs, h * n:(h + 1) * n]
                 for i, (bi, h) in enumerate(chains)]
    for i, (bi, h) in enumerate(chains):
        st_ref[bi * N_HEADS + h] = state[i]
    ys = [[_mm_nt(q_ref[bi, j * c:(j + 1) * c, h * n:(h + 1) * n], entering[j][i])
           for i, (bi, h) in enumerate(chains)] for j in range(rows // c)]
    y = jnp.concatenate(
        [jnp.concatenate([jnp.concatenate(ys[j][bi * N_HEADS:(bi + 1) * N_HEADS], axis=-1)
                          for j in range(rows // c)], axis=0) + y1_ref[bi]
         for bi in range(bsz)], axis=0)
    mean = _head_sums(y) * (1.0 / n)
    yc = y - mean
    var = _head_sums(yc * yc) * (1.0 / n)
    yn = yc * lax.rsqrt(var + LN_X_EPS) * lng_ref[...] + lnb_ref[...]
    for bi in range(bsz):
        o_ref[bi] = ((yn[bi * rows:(bi + 1) * rows] + bonus_ref[bi]) * gate_ref[bi]).astype(o_ref.dtype)


def _wkv_scan(q, y1, m, gm, bonus, gate, ln_g, ln_b, rows=2 * WKV_CHUNK):
    bsz, seq, w = y1.shape
    tile = pl.BlockSpec((bsz, rows, w), lambda ci: (0, ci, 0))
    vec = pl.BlockSpec((1, w), lambda ci: (0, 0))
    return pl.pallas_call(
        _wkv_scan_kernel,
        grid=(seq // rows,),
        in_specs=[tile] * 6 + [vec] * 2,
        out_specs=tile,
        out_shape=jax.ShapeDtypeStruct((bsz, seq, w), BF16),
        scratch_shapes=[pltpu.VMEM((bsz * N_HEADS, HEAD_DIM, HEAD_DIM), F32)],
        compiler_params=_params("arbitrary"),
        name="wkv_state_scan",
    )(q, y1, m, gm, bonus, gate, ln_g.reshape(1, w), ln_b.reshape(1, w))


def _wout_kernel(x_ref, y0_ref, y1_ref, y2_ref, y3_ref, w_ref, o_ref):
    acc = x_ref[...]
    for idx, y_ref in enumerate((y0_ref, y1_ref, y2_ref, y3_ref)):
        w = w_ref[0, idx * GROUP_WIDTH:(idx + 1) * GROUP_WIDTH, :].astype(BF16)
        acc = acc + jnp.dot(y_ref[...], w, preferred_element_type=F32)
    o_ref[...] = acc


def _wout(x2d, ys, w_stack, layer, tm=512):
    m, d = x2d.shape
    ytile = pl.BlockSpec((tm, GROUP_WIDTH), lambda i: (i, 0))
    xtile = pl.BlockSpec((tm, d), lambda i: (i, 0))
    wspec = pl.BlockSpec((1,) + w_stack.shape[1:], lambda i: (layer, 0, 0))
    return pl.pallas_call(
        _wout_kernel,
        grid=(m // tm,),
        in_specs=[xtile] + [ytile] * 4 + [wspec],
        out_specs=xtile,
        out_shape=jax.ShapeDtypeStruct((m, d), F32),
        compiler_params=_params("parallel", vmem_limit=FFN_VMEM_LIMIT),
        name="wout_residual",
    )(x2d, *ys, w_stack)


def _ffn_kernel(x_ref, g_ref, wu_ref, wd_ref, o_ref, h_ref):
    @pl.when(pl.program_id(1) == 0)
    def _():
        x = x_ref[...]
        ms = jnp.mean(x * x, axis=-1, keepdims=True)
        h_ref[...] = (x * lax.rsqrt(ms + RMS_EPS) * g_ref[0]).astype(BF16)
        o_ref[...] = x

    u = jnp.dot(h_ref[...], wu_ref[0].astype(BF16), preferred_element_type=F32)
    act = jnp.square(jnp.maximum(u, 0.0)).astype(BF16)
    o_ref[...] += jnp.dot(act, wd_ref[0].astype(BF16), preferred_element_type=F32)


def _ffn(x2d, gains, wu_stack, wd_stack, layer, tm=1024, tf=512):
    m, d = x2d.shape
    f = wu_stack.shape[2]
    xtile = pl.BlockSpec((tm, d), lambda i, j: (i, 0))
    return pl.pallas_call(
        _ffn_kernel,
        grid=(m // tm, f // tf),
        in_specs=[
            xtile,
            pl.BlockSpec((1, 1, d), lambda i, j: (layer, 0, 0)),
            pl.BlockSpec((1, d, tf), lambda i, j: (layer, 0, j)),
            pl.BlockSpec((1, tf, d), lambda i, j: (layer, j, 0)),
        ],
        out_specs=xtile,
        out_shape=jax.ShapeDtypeStruct((m, d), F32),
        scratch_shapes=[pltpu.VMEM((tm, d), BF16)],
        compiler_params=_params("parallel", "arbitrary", vmem_limit=FFN_VMEM_LIMIT),
        name="ffn",
    )(x2d, gains.reshape(-1, 1, d), wu_stack, wd_stack)


def kernel(x, norm_mix, w_in, conv_w, swa_q_norm, swa_k_norm, swa_sink, dil_q_norm, dil_k_norm,
           rwkv_mu, decay_w0, decay_w2, iclr_a0, iclr_a2, gate_g2, k_k, k_a, r_k, ln_x_g, ln_x_b,
           vres_v0, vres_v1, vres_v2, w_out, norm_ffn, w_up, w_down, rel_bias):
    bsz, seq, d = x.shape
    depth = w_in.shape[0]
    bias = _bias_tiles(rel_bias)
    swa_bias = bias[0, :N_HEADS]
    dil_bias = bias[:, N_HEADS:]
    x2d = x.reshape(bsz * seq, d)
    v_first = None
    for layer in range(depth):
        proj = _norm_matmul(x2d, norm_mix, w_in, layer).reshape(bsz, seq, IN_WIDTH)
        y_swa, y_conv = _swa_conv_mixers(proj, swa_q_norm[layer], swa_k_norm[layer],
                                         swa_sink[layer], swa_bias, conv_w[layer])
        y_dil = _dil_mixer(proj, dil_q_norm[layer], dil_k_norm[layer], dil_bias)
        vres = None if layer == 0 else (vres_v0[layer - 1], vres_v1[layer - 1], vres_v2[layer - 1])
        maps = _rwkv_maps(
            proj, rwkv_mu[layer], decay_w0[layer], decay_w2[layer], iclr_a0[layer], iclr_a2[layer],
            gate_g2[layer], k_k[layer], k_a[layer], r_k[layer], v_first, vres)
        if layer == 0:
            v_first = maps[6]
        y_rwkv = _wkv_scan(*maps[:6], ln_x_g[layer], ln_x_b[layer])
        ys = [y.reshape(bsz * seq, GROUP_WIDTH) for y in (y_conv, y_swa, y_dil, y_rwkv)]
        x2d = _wout(x2d, ys, w_out, layer)
        x2d = _ffn(x2d, norm_ffn, w_up, w_down, layer)
    return x2d.reshape(bsz, seq, d)
```

```python
import functools
import math

import jax
import jax.numpy as jnp
from jax import lax
from jax.experimental import pallas as pl
from jax.experimental.pallas import tpu as pltpu

F32 = jnp.float32
BF16 = jnp.bfloat16

D_MODEL = 2048
HEAD_DIM = 64
GROUP_WIDTH = 512
N_HEADS = GROUP_WIDTH // HEAD_DIM
SWA_KV_HEADS = 2
SWA_GROUP = N_HEADS // SWA_KV_HEADS
SWA_WINDOW = 128
DIL_PAIRS = ((128, 1), (512, 4), (2048, 16))
DECAY_LORA = 64
ICLR_LORA = 64
GATE_LORA = 128
RWKV_IN_WIDTH = 3 * GROUP_WIDTH + DECAY_LORA + ICLR_LORA + GATE_LORA
BLK = 128
NUM_BUCKETS = 32
BUCKET_MAX_DIST = 128
RMS_EPS = 1e-6
LN_X_EPS = 64e-5
NEG = -1e30
LOG2E = math.log2(math.e)
WKV_CHUNK = 64
WKV_GROUP = 4
DIL_TILES = 4
PREV_ROWS = 16
PREP_UNROLL = 4

OFF_CONV = 0
OFF_SWA_Q = OFF_CONV + 3 * GROUP_WIDTH
OFF_SWA_K = OFF_SWA_Q + GROUP_WIDTH
OFF_SWA_V = OFF_SWA_K + SWA_KV_HEADS * HEAD_DIM
OFF_DIL = OFF_SWA_V + SWA_KV_HEADS * HEAD_DIM
OFF_RW = OFF_DIL + 3 * GROUP_WIDTH
IN_WIDTH = OFF_RW + RWKV_IN_WIDTH

V7X_VMEM_BYTES = 64 * 1024 * 1024
VMEM_LIMIT = 48 * 1024 * 1024
FFN_VMEM_LIMIT = V7X_VMEM_BYTES - 6 * 1024 * 1024


def _params(*sem, vmem_limit=VMEM_LIMIT):
    return pltpu.CompilerParams(dimension_semantics=sem, vmem_limit_bytes=vmem_limit)


def _mm(a, b):
    return jnp.dot(a.astype(BF16), b.astype(BF16), preferred_element_type=F32)


def _mm_nt(a, b):
    return lax.dot_general(a.astype(BF16), b.astype(BF16), (((1,), (1,)), ((), ())),
                           preferred_element_type=F32)


def _mm_tn(a, b):
    return lax.dot_general(a.astype(BF16), b.astype(BF16), (((0,), (0,)), ((), ())),
                           preferred_element_type=F32)


def _split_bf16(x):
    hi = x.astype(BF16)
    lo = (x - hi.astype(F32)).astype(BF16)
    return hi, lo


def _head_sums(x, split=True):
    lanes = 128
    r = lax.broadcasted_iota(jnp.int32, (lanes, lanes), 0) // HEAD_DIM
    c = lax.broadcasted_iota(jnp.int32, (lanes, lanes), 1) // HEAD_DIM
    bd = (r == c).astype(BF16)
    parts = _split_bf16(x) if split else (x.astype(BF16),)
    cols = []
    for j in range(x.shape[-1] // lanes):
        sl = slice(j * lanes, (j + 1) * lanes)
        cols.append(sum(jnp.dot(part[:, sl], bd, preferred_element_type=F32) for part in parts))
    return cols[0] if len(cols) == 1 else jnp.concatenate(cols, axis=-1)


def _sigmoid(z):
    return 1.0 / (1.0 + jnp.exp(-z))


def _norm_matmul_kernel(x_ref, g_ref, w_ref, o_ref, h_ref):
    @pl.when(pl.program_id(1) == 0)
    def _():
        x = x_ref[...]
        ms = jnp.mean(x * x, axis=-1, keepdims=True)
        h_ref[...] = (x * lax.rsqrt(ms + RMS_EPS) * g_ref[0]).astype(BF16)

    o_ref[...] = jnp.dot(h_ref[...], w_ref[0].astype(BF16),
                         preferred_element_type=F32).astype(o_ref.dtype)


def _norm_matmul(x2d, gains, w_stack, layer, tm=2048, tn=512):
    m, k = x2d.shape
    n = w_stack.shape[2]
    assert n % tn == 0 and m % tm == 0
    return pl.pallas_call(
        _norm_matmul_kernel,
        grid=(m // tm, n // tn),
        in_specs=[
            pl.BlockSpec((tm, k), lambda i, j: (i, 0)),
            pl.BlockSpec((1, 1, k), lambda i, j: (layer, 0, 0)),
            pl.BlockSpec((1, k, tn), lambda i, j: (layer, 0, j)),
        ],
        out_specs=pl.BlockSpec((tm, tn), lambda i, j: (i, j)),
        out_shape=jax.ShapeDtypeStruct((m, n), BF16),
        scratch_shapes=[pltpu.VMEM((tm, k), BF16)],
        compiler_params=_params("parallel", "arbitrary", vmem_limit=FFN_VMEM_LIMIT),
        name="norm_matmul",
    )(x2d, gains.reshape(-1, 1, k), w_stack)


def _t5_bucket(dist):
    dist = jnp.maximum(dist, 0)
    max_exact = NUM_BUCKETS // 2
    scaled = (jnp.log(jnp.maximum(dist, 1).astype(F32) / max_exact)
              / math.log(BUCKET_MAX_DIST / max_exact))
    large = max_exact + (scaled * (NUM_BUCKETS - max_exact)).astype(jnp.int32)
    large = jnp.minimum(large, NUM_BUCKETS - 1)
    return jnp.where(dist < max_exact, dist, large)


def _bias_kernel(bucket_ref, table_ref, o_ref):
    bucket = bucket_ref[0]

    def head(h, carry):
        g = jnp.zeros(bucket.shape, F32)
        for b in range(NUM_BUCKETS):
            g = jnp.where(bucket == b, table_ref[b, h], g)
        rows = jnp.broadcast_to(g[0:1], (BLK, 2 * BLK))
        o_ref[0, h] = pltpu.roll(rows, 0, axis=1, stride=1, stride_axis=0)
        return carry

    lax.fori_loop(0, o_ref.shape[1], head, 0)


def _bias_tiles(rel_bias):
    lag_dist = BLK - jnp.arange(2 * BLK)
    buckets = jnp.stack([_t5_bucket(lag_dist * r) for _, r in DIL_PAIRS]).astype(jnp.int32)
    buckets = jnp.broadcast_to(buckets[:, None, :], (len(DIL_PAIRS), 8, 2 * BLK))
    nh = rel_bias.shape[1]
    return pl.pallas_call(
        _bias_kernel,
        grid=(len(DIL_PAIRS),),
        in_specs=[
            pl.BlockSpec((1, 8, 2 * BLK), lambda s: (s, 0, 0)),
            pl.BlockSpec(memory_space=pltpu.SMEM),
        ],
        out_specs=pl.BlockSpec((1, nh, BLK, 2 * BLK), lambda s: (s, 0, 0, 0)),
        out_shape=jax.ShapeDtypeStruct((len(DIL_PAIRS), nh, BLK, 2 * BLK), F32),
        compiler_params=_params("arbitrary"),
        name="bias_tiles",
    )(buckets, rel_bias)


def _short_conv(b_ref, c_ref, u_ref, w_ref, o_ref):
    lanes = 128
    for j in range(b_ref.shape[2] // lanes):
        cols = slice(j * lanes, (j + 1) * lanes)
        z = c_ref[0, :, cols].astype(F32) * u_ref[0, :, cols].astype(F32)
        row = lax.broadcasted_iota(jnp.int32, z.shape, 0)
        z1 = jnp.where(row >= 1, pltpu.roll(z, 1, axis=0), 0.0)
        z2 = jnp.where(row >= 2, pltpu.roll(z, 2, axis=0), 0.0)
        w = w_ref[:, cols]
        y = z2 * w[0:1, :] + z1 * w[1:2, :] + z * w[2:3, :]
        o_ref[0, :, cols] = (b_ref[0, :, cols].astype(F32) * y).astype(o_ref.dtype)


def _band_mask(max_dist):
    a = lax.broadcasted_iota(jnp.int32, (BLK, 2 * BLK), 0)
    b = lax.broadcasted_iota(jnp.int32, (BLK, 2 * BLK), 1)
    dist = BLK + a - b
    return (dist >= 0) & (dist <= max_dist), b


def _attend(qs, kws, vws, biases, sinks=None, normalize=True):
    idx = range(len(qs))
    s = [lax.dot_general(qs[i], kws[i], (((1,), (1,)), ((), ())), preferred_element_type=F32)
         + biases[i] for i in idx]
    m = [jnp.max(s[i], axis=-1, keepdims=True) for i in idx]
    if sinks is not None:
        m = [jnp.maximum(m[i], sinks[i]) for i in idx]
    p = [jnp.exp2(s[i] - m[i]) for i in idx]
    den = [jnp.sum(p[i], axis=-1, keepdims=True) for i in idx]
    if sinks is not None:
        den = [den[i] + jnp.exp2(sinks[i] - m[i]) for i in idx]
    o = [jnp.dot(p[i].astype(BF16), vws[i], preferred_element_type=F32) for i in idx]
    if normalize:
        o = [o[i] / den[i] for i in idx]
    return o, m, den


def _head_rms(x, gain):
    ms = _head_sums(x * x, split=False) * (1.0 / HEAD_DIM)
    return x * lax.rsqrt(ms + RMS_EPS) * gain


def _swa_kernel(q_ref, k_ref, v_ref, qg_ref, kg_ref, sink_ref, bias_ref,
                cb_ref, cc_ref, cu_ref, cw_ref, o_ref, oc_ref,
                qn_ref, kn_ref, vb_ref, bm_ref):
    seq = q_ref.shape[1]
    nb = seq // BLK
    kvw = SWA_KV_HEADS * HEAD_DIM
    scale = HEAD_DIM ** -0.5 * LOG2E

    _short_conv(cb_ref, cc_ref, cu_ref, cw_ref, oc_ref)

    kn_ref[0:BLK, :] = jnp.zeros((BLK, 2 * kvw), BF16)
    vb_ref[0:BLK, :] = jnp.zeros((BLK, 2 * kvw), BF16)
    lane_half = lax.broadcasted_iota(jnp.int32, (1, kvw), 1) // HEAD_DIM

    def both_halves(x):
        xr = pltpu.roll(x, HEAD_DIM, axis=1)
        return jnp.concatenate([jnp.where(lane_half == 0, x, xr),
                                jnp.where(lane_half == 0, xr, x)], axis=-1)

    def prep(i, carry):
        r0 = pl.multiple_of(i * BLK, BLK)
        q = q_ref[0, pl.ds(r0, BLK), :].astype(F32)
        qn_ref[pl.ds(r0, BLK), :] = (_head_rms(q, qg_ref[...]) * scale).astype(BF16)
        k = _head_rms(k_ref[0, pl.ds(r0, BLK), :].astype(F32), kg_ref[...])
        kn_ref[pl.ds(r0 + BLK, BLK), :] = both_halves(k).astype(BF16)
        v = v_ref[0, pl.ds(r0, BLK), :].astype(F32)
        vb_ref[pl.ds(r0 + BLK, BLK), :] = both_halves(v).astype(BF16)
        return carry

    lax.fori_loop(0, nb, prep, 0, unroll=PREP_UNROLL)

    band, kcol = _band_mask(SWA_WINDOW - 1)
    for h in range(N_HEADS):
        bm_ref[h] = jnp.where(band & (kcol >= BLK), bias_ref[h] * LOG2E, NEG)
        bm_ref[N_HEADS + h] = jnp.where(band, bias_ref[h] * LOG2E, NEG)
    sinks = [sink_ref[h] * LOG2E for h in range(N_HEADS)]

    def block(i, carry):
        r0 = pl.multiple_of(i * BLK, BLK)
        later = jnp.minimum(i, 1) * N_HEADS
        qs, kws, vws, bms = [], [], [], []
        for hk in range(SWA_KV_HEADS):
            kw = kn_ref[pl.ds(r0, 2 * BLK), hk * kvw:(hk + 1) * kvw]
            vw = vb_ref[pl.ds(r0, 2 * BLK), hk * kvw:(hk + 1) * kvw]
            for g in range(SWA_GROUP):
                h = hk * SWA_GROUP + g
                q2 = qn_ref[pl.ds(r0, BLK), (h // 2) * kvw:(h // 2 + 1) * kvw]
                qs.append(jnp.where(lane_half == h % 2, q2, jnp.zeros_like(q2)))
                kws.append(kw)
                vws.append(vw)
                bms.append(bm_ref[later + h])
        outs, _, _ = _attend(qs, kws, vws, bms, sinks)
        pairs = [jnp.where(lane_half == 0, outs[h], outs[h + 1]) for h in range(0, N_HEADS, 2)]
        o_ref[0, pl.ds(r0, BLK), :] = jnp.concatenate(pairs, axis=-1).astype(o_ref.dtype)
        return carry

    lax.fori_loop(0, nb, block, 0)


def _swa_conv_mixers(proj, q_gain, k_gain, sink, bias, conv_w):
    bsz, seq, _ = proj.shape
    kvw = SWA_KV_HEADS * HEAD_DIM
    q_gain_t = jnp.tile(q_gain, N_HEADS).reshape(1, GROUP_WIDTH)
    k_gain_t = jnp.tile(k_gain, SWA_KV_HEADS).reshape(1, kvw)
    wide = lambda off: pl.BlockSpec((1, seq, GROUP_WIDTH), lambda b: (b, 0, off // GROUP_WIDTH))
    out = jax.ShapeDtypeStruct((bsz, seq, GROUP_WIDTH), BF16)
    return pl.pallas_call(
        _swa_kernel,
        grid=(bsz,),
        in_specs=[
            wide(OFF_SWA_Q),
            pl.BlockSpec((1, seq, kvw), lambda b: (b, 0, OFF_SWA_K // kvw)),
            pl.BlockSpec((1, seq, kvw), lambda b: (b, 0, OFF_SWA_V // kvw)),
            pl.BlockSpec((1, GROUP_WIDTH), lambda b: (0, 0)),
            pl.BlockSpec((1, kvw), lambda b: (0, 0)),
            pl.BlockSpec(memory_space=pltpu.SMEM),
            pl.BlockSpec((N_HEADS, BLK, 2 * BLK), lambda b: (0, 0, 0)),
            wide(OFF_CONV), wide(OFF_CONV + GROUP_WIDTH), wide(OFF_CONV + 2 * GROUP_WIDTH),
            pl.BlockSpec((3, GROUP_WIDTH), lambda b: (0, 0)),
        ],
        out_specs=[pl.BlockSpec((1, seq, GROUP_WIDTH), lambda b: (b, 0, 0))] * 2,
        out_shape=[out, out],
        scratch_shapes=[
            pltpu.VMEM((seq, GROUP_WIDTH), BF16),
            pltpu.VMEM((seq + BLK, 2 * kvw), BF16),
            pltpu.VMEM((seq + BLK, 2 * kvw), BF16),
            pltpu.VMEM((2 * N_HEADS, BLK, 2 * BLK), F32),
        ],
        compiler_params=_params("parallel"),
        name="swa_conv_mixers",
    )(proj, proj, proj, q_gain_t, k_gain_t, sink, bias, proj, proj, proj, conv_w)


def _dil_kernel(q_ref, k_ref, v_ref, qg_ref, kg_ref, bias_ref, o_ref,
                qn_ref, kn_ref, vn_ref, ob_ref, mb_ref, db_ref, bm_ref):
    seq = q_ref.shape[1]
    lanes = q_ref.shape[2]
    heads = lanes // HEAD_DIM
    scale = HEAD_DIM ** -0.5 * LOG2E

    def prep(i, carry):
        r0 = pl.multiple_of(i * BLK, BLK)
        q = q_ref[0, pl.ds(r0, BLK), :].astype(F32)
        k = k_ref[0, pl.ds(r0, BLK), :].astype(F32)
        qn_ref[pl.ds(r0, BLK), :] = _head_rms(q, qg_ref[...]) * scale
        kn_ref[pl.ds(r0, BLK), :] = _head_rms(k, kg_ref[...])
        vn_ref[pl.ds(r0, BLK), :] = v_ref[0, pl.ds(r0, BLK), :].astype(F32)
        return carry

    lax.fori_loop(0, seq // BLK, prep, 0, unroll=PREP_UNROLL)

    for br, (window, r) in enumerate(DIL_PAIRS):
        band, kcol = _band_mask(window // r)
        for h in range(heads):
            bias = bias_ref[br, h] * LOG2E
            bm_ref[(br * heads + h) * 2] = jnp.where(band & (kcol >= BLK), bias, NEG)
            bm_ref[(br * heads + h) * 2 + 1] = jnp.where(band, bias, NEG)

    lane_head = lax.broadcasted_iota(jnp.int32, (1, lanes), 1) // HEAD_DIM

    for br, (window, r) in enumerate(DIL_PAIRS):
        nb = seq // r // BLK

        def blocks(it, carry, br=br, r=r, nb=nb):
            qs, kws, vws, bms, curs = [], [], [], [], []
            for u in range(DIL_TILES):
                t = it * DIL_TILES + u
                c = t // nb
                i = t - c * nb
                cur = c + i * (BLK * r)
                prev = jnp.maximum(cur - BLK * r, c)
                later = jnp.minimum(i, 1)

                def rows(ref, start):
                    if r == 1:
                        return ref[pl.ds(start, BLK), :]
                    return ref[pl.ds(start, BLK, stride=r), :]

                q = rows(qn_ref, cur).astype(BF16)
                if nb == 1:
                    kw = rows(kn_ref, cur).astype(BF16)
                    vw = rows(vn_ref, cur).astype(BF16)
                else:
                    kw = jnp.concatenate([rows(kn_ref, prev), rows(kn_ref, cur)],
                                         axis=0).astype(BF16)
                    vw = jnp.concatenate([rows(vn_ref, prev), rows(vn_ref, cur)],
                                         axis=0).astype(BF16)
                curs.append(cur)
                for h in range(heads):
                    qs.append(jnp.where(lane_head == h, q, jnp.zeros_like(q)))
                    kws.append(kw)
                    vws.append(vw)
                    if nb == 1:
                        bms.append(bm_ref[(br * heads + h) * 2, :, BLK:])
                    else:
                        bms.append(bm_ref[(br * heads + h) * 2 + later])
            outs, ms, dens = _attend(qs, kws, vws, bms, normalize=False)
            for u in range(DIL_TILES):
                o_all, m_all, d_all = outs[u * heads], ms[u * heads], dens[u * heads]
                for h in range(1, heads):
                    o_all = jnp.where(lane_head == h, outs[u * heads + h], o_all)
                    m_all = jnp.where(lane_head == h, ms[u * heads + h], m_all)
                    d_all = jnp.where(lane_head == h, dens[u * heads + h], d_all)
                if r == 1:
                    rows = pl.ds(curs[u], BLK)
                else:
                    rows = pl.ds(curs[u], BLK, stride=r)
                ob_ref[br, rows, :] = o_all
                mb_ref[br, rows, :] = m_all
                db_ref[br, rows, :] = d_all
            return carry

        lax.fori_loop(0, r * nb // DIL_TILES, blocks, 0)

    def combine(i, carry):
        r0 = pl.multiple_of(i * BLK, BLK)
        rows = pl.ds(r0, BLK)
        m0, m1, m2 = mb_ref[0, rows, :], mb_ref[1, rows, :], mb_ref[2, rows, :]
        m = jnp.maximum(jnp.maximum(m0, m1), m2)
        e0, e1, e2 = jnp.exp2(m0 - m), jnp.exp2(m1 - m), jnp.exp2(m2 - m)
        num = e0 * ob_ref[0, rows, :] + e1 * ob_ref[1, rows, :] + e2 * ob_ref[2, rows, :]
        den = e0 * db_ref[0, rows, :] + e1 * db_ref[1, rows, :] + e2 * db_ref[2, rows, :]
        o_ref[0, rows, :] = (num / den).astype(o_ref.dtype)
        return carry

    lax.fori_loop(0, seq // BLK, combine, 0, unroll=PREP_UNROLL)


def _dil_mixer(proj, q_gain, k_gain, bias):
    bsz, seq, _ = proj.shape
    lanes = 128
    heads = lanes // HEAD_DIM
    nblk = GROUP_WIDTH // lanes
    base = OFF_DIL // lanes
    gq = jnp.tile(q_gain, heads).reshape(1, lanes)
    gk = jnp.tile(k_gain, heads).reshape(1, lanes)

    def col(seg):
        return pl.BlockSpec((1, seq, lanes), lambda b, j: (b, 0, base + seg * nblk + j))

    return pl.pallas_call(
        _dil_kernel,
        grid=(bsz, nblk),
        in_specs=[
            col(0), col(1), col(2),
            pl.BlockSpec((1, lanes), lambda b, j: (0, 0)),
            pl.BlockSpec((1, lanes), lambda b, j: (0, 0)),
            pl.BlockSpec((len(DIL_PAIRS), heads, BLK, 2 * BLK), lambda b, j: (0, j, 0, 0)),
        ],
        out_specs=pl.BlockSpec((1, seq, lanes), lambda b, j: (b, 0, j)),
        out_shape=jax.ShapeDtypeStruct((bsz, seq, GROUP_WIDTH), BF16),
        scratch_shapes=[
            pltpu.VMEM((seq, lanes), F32),
            pltpu.VMEM((seq, lanes), F32),
            pltpu.VMEM((seq, lanes), F32),
            pltpu.VMEM((len(DIL_PAIRS), seq, lanes), F32),
            pltpu.VMEM((len(DIL_PAIRS), seq, lanes), F32),
            pltpu.VMEM((len(DIL_PAIRS), seq, lanes), F32),
            pltpu.VMEM((len(DIL_PAIRS) * heads * 2, BLK, 2 * BLK), F32),
        ],
        compiler_params=_params("parallel", "parallel"),
        name="dil_mixer",
    )(proj, proj, proj, gq, gk, bias)


def _rwkv_maps_kernel(*refs, has_vres):
    if has_vres:
        (p_ref, pp_ref, mu_ref, w0_ref, w2_ref, a0_ref, a2_ref, g2_ref, kk_ref, ka_ref, rk_ref,
         vf_ref, v0_ref, v1_ref, v2_ref,
         q_out, y1_out, m_out, gm_out, bonus_out, gate_out, *scratch) = refs
    else:
        (p_ref, pp_ref, mu_ref, w0_ref, w2_ref, a0_ref, a2_ref, g2_ref, kk_ref, ka_ref, rk_ref,
         q_out, y1_out, m_out, gm_out, bonus_out, gate_out, v_out, *scratch) = refs
    w = GROUP_WIDTH
    p = p_ref[0].astype(F32)
    row = lax.broadcasted_iota(jnp.int32, p.shape, 0)
    last_prev = pp_ref[0, PREV_ROWS - 1:PREV_ROWS, :].astype(F32)
    last_prev = jnp.where(pl.program_id(1) > 0, last_prev, 0.0)
    prev = jnp.where(row >= 1, pltpu.roll(p, 1, axis=0), last_prev)
    xs = p + (prev - p) * mu_ref[...]
    grows = WKV_GROUP * WKV_CHUNK
    for grp in range(p.shape[0] // grows):
        rs = slice(grp * grows, (grp + 1) * grows)
        x = xs[rs]
        r = x[:, 0:w]
        k = x[:, w:2 * w]
        v = x[:, 2 * w:3 * w]
        o = 3 * w
        wd = x[:, o:o + DECAY_LORA]
        ad = x[:, o + DECAY_LORA:o + DECAY_LORA + ICLR_LORA]
        gd = x[:, o + DECAY_LORA + ICLR_LORA:]

        z = -(w0_ref[...] + _mm(jnp.tanh(wd), w2_ref[...]))
        softplus = jnp.maximum(z, 0.0) + jnp.log(1.0 + jnp.exp(-jnp.abs(z)))
        logw = -softplus - 0.5
        lw = -jnp.exp(logw)
        a = _sigmoid(a0_ref[...] + _mm(ad, a2_ref[...]))
        gate_out[0, rs] = _mm(_sigmoid(gd), g2_ref[...])
        if has_vres:
            mix = _sigmoid(v0_ref[...] + _mm(_mm(v, v1_ref[...]), v2_ref[...]))
            v = v + (vf_ref[0, rs] - v) * mix
        else:
            v_out[0, rs] = v
        kk = k * kk_ref[...]
        ss = _head_sums(kk * kk, split=False)
        kk = kk * lax.rsqrt(jnp.maximum(ss, 1e-24))
        k = k * (1.0 + (a - 1.0) * ka_ref[...])
        bonus_out[0, rs] = _head_sums(r * k * rk_ref[...]) * v
        _chunk_maps(grp, r, lw, k, v, kk, kk * a, q_out.at[0], y1_out.at[0], m_out.at[0],
                    gm_out.at[0], *scratch)


def _rwkv_maps(proj, mu, w0, w2, a0, a2, g2, k_k, k_a, r_k, v_first, vres, ts=512):
    bsz, seq, _ = proj.shape
    w = GROUP_WIDTH
    nt = seq // ts
    has_vres = vres is not None

    def full(shape):
        return pl.BlockSpec(shape, lambda b, i: (0,) * len(shape))

    row = lambda a: a.reshape(1, -1)
    tile = pl.BlockSpec((1, ts, w), lambda b, i: (b, i, 0))
    in_specs = [
        pl.BlockSpec((pl.Element(1), pl.Element(ts), pl.Element(RWKV_IN_WIDTH)),
                     lambda b, i: (b, i * ts, OFF_RW)),
        pl.BlockSpec((pl.Element(1), pl.Element(PREV_ROWS), pl.Element(RWKV_IN_WIDTH)),
                     lambda b, i: (b, jnp.maximum(i * (ts // PREV_ROWS) - 1, 0) * PREV_ROWS, OFF_RW)),
        full((1, RWKV_IN_WIDTH)), full((1, w)), full((DECAY_LORA, w)), full((1, w)),
        full((ICLR_LORA, w)), full((GATE_LORA, w)), full((1, w)), full((1, w)), full((1, w)),
    ]
    args = [proj, proj, row(mu), row(w0), w2, row(a0), a2, g2, row(k_k), row(k_a), row(r_k)]
    if has_vres:
        v0, v1, v2 = vres
        in_specs += [tile, full((1, w)), full(v1.shape), full(v2.shape)]
        args += [v_first, row(v0), v1, v2]
    f32 = jax.ShapeDtypeStruct((bsz, seq, w), F32)
    bf16 = jax.ShapeDtypeStruct((bsz, seq, w), BF16)
    out_shape = [bf16, f32, bf16, f32, f32, f32] + ([] if has_vres else [f32])
    return pl.pallas_call(
        functools.partial(_rwkv_maps_kernel, has_vres=has_vres),
        grid=(bsz, nt),
        in_specs=in_specs,
        out_specs=[tile] * len(out_shape),
        out_shape=out_shape,
        scratch_shapes=[pltpu.VMEM((ts, w), dt) for dt in (BF16, F32, BF16, BF16, F32, BF16, BF16)],
        compiler_params=_params("parallel", "parallel"),
        name="rwkv_chunk_maps",
    )(*args)


def _chunk_maps(grp, r, lw, k, v, kk, b, q_out, y1_out, m_out, g_out,
                at_ref, rt_ref, bt_ref, kt_ref, dec_ref, vb_ref, rb_ref):
    c = WKV_CHUNK
    n = HEAD_DIM
    rows = r.shape[0]
    rs = slice(grp * rows, (grp + 1) * rows)

    row = lax.broadcasted_iota(jnp.int32, (rows, rows), 0)
    col = lax.broadcasted_iota(jnp.int32, (rows, rows), 1)
    tri = ((row >= col) & ((row // c) == (col // c))).astype(BF16)
    lw_hi, lw_lo = _split_bf16(lw)
    cum = (jnp.dot(tri, lw_hi, preferred_element_type=F32)
           + jnp.dot(tri, lw_lo, preferred_element_type=F32))
    e_pos = jnp.exp(cum)
    e_neg = jnp.exp(-cum)
    r_t = r * e_pos
    at_ref[rs] = (-kk * jnp.exp(cum - lw)).astype(BF16)
    rt_ref[rs] = r_t
    rb_ref[rs] = r_t.astype(BF16)
    bt_ref[rs] = (b * e_neg).astype(BF16)
    kt_ref[rs] = (k * e_neg).astype(BF16)
    dec_ref[rs] = e_pos
    vb_ref[rs] = v.astype(BF16)

    crow = lax.broadcasted_iota(jnp.int32, (c, c), 0)
    ccol = lax.broadcasted_iota(jnp.int32, (c, c), 1)
    lower = crow >= ccol
    strict = crow > ccol
    eye = (crow == ccol).astype(F32)
    wrow = lax.broadcasted_iota(jnp.int32, (c, 2 * c), 0)
    wcol = lax.broadcasted_iota(jnp.int32, (c, 2 * c), 1)
    strict_left = wrow > wcol
    right = wcol >= c
    eye_right = (wcol == wrow + c).astype(F32)

    def tile(ref, j, h):
        return ref[j * c:(j + 1) * c, h * n:(h + 1) * n]

    _chunk_group(range(grp * WKV_GROUP, (grp + 1) * WKV_GROUP), tile, at_ref, rt_ref, bt_ref,
                 kt_ref, dec_ref, vb_ref, rb_ref, q_out, y1_out, m_out, g_out,
                 lower, strict, eye, strict_left, right, eye_right)


def _chunk_group(chunks, tile, at_ref, rt_ref, bt_ref, kt_ref, dec_ref, vb_ref, rb_ref,
                 q_out, y1_out, m_out, g_out, lower, strict, eye, strict_left, right, eye_right):
    c = WKV_CHUNK
    n = HEAD_DIM
    pairs = [(j, h) for j in chunks for h in range(N_HEADS)]
    ah = [tile(at_ref, j, h) for j, h in pairs]
    rh = [tile(rt_ref, j, h) for j, h in pairs]
    bh = [tile(bt_ref, j, h) for j, h in pairs]
    kh = [tile(kt_ref, j, h) for j, h in pairs]
    vh = [tile(vb_ref, j, h) for j, h in pairs]
    dh = [dec_ref[(j + 1) * c - 1:(j + 1) * c, h * n:(h + 1) * n] for j, h in pairs]
    idx = range(len(pairs))
    ar = [jnp.concatenate([ah[i], tile(rb_ref, *pairs[i])], axis=0) for i in idx]
    bk = [jnp.concatenate([bh[i], kh[i]], axis=0) for i in idx]
    aa = [_mm_nt(ar[i], bk[i]) for i in idx]
    a_kr = [jnp.concatenate([jnp.where(strict, aa[i][:c, c:], 0.0),
                             jnp.where(lower, aa[i][c:, c:], 0.0)], axis=0).astype(BF16)
            for i in idx]
    a_rb = [jnp.where(lower, aa[i][c:, :c], 0.0).astype(BF16) for i in idx]
    ps = [jnp.where(strict_left, aa[i][:c, :], 0.0) + eye_right for i in idx]
    for _ in range(int(math.log2(c))):
        ps = [_mm(ps[i][:, :c], ps[i]) + jnp.where(right, ps[i], 0.0) for i in idx]
    t = [ps[i][:, c:].astype(BF16) for i in idx]
    av = [_mm(a_kr[i], vh[i]) for i in idx]
    wu = [_mm(t[i], jnp.concatenate([ah[i], av[i][:c].astype(BF16)], axis=1)).astype(BF16)
          for i in idx]
    aw = [_mm(a_rb[i], wu[i]) for i in idx]
    q = [rh[i] + aw[i][:, :n] for i in idx]
    y1 = [aw[i][:, n:] + av[i][c:] for i in idx]
    zero = jnp.zeros((c, n), BF16)
    mg = [_mm_tn(jnp.concatenate([wu[i], jnp.concatenate([zero, vh[i]], axis=1)], axis=0), bk[i])
          for i in idx]
    m = [(eye + mg[i][:n]) * dh[i] for i in idx]
    g = [mg[i][n:] * dh[i] for i in idx]
    for j in chunks:
        sel = slice((j - chunks[0]) * N_HEADS, (j - chunks[0] + 1) * N_HEADS)
        q_out[j * c:(j + 1) * c, :] = jnp.concatenate(q[sel], axis=-1).astype(q_out.dtype)
        y1_out[j * c:(j + 1) * c, :] = jnp.concatenate(y1[sel], axis=-1)
        m_out[j * c:(j + 1) * c, :] = jnp.concatenate(m[sel], axis=-1).astype(m_out.dtype)
        g_out[j * c:(j + 1) * c, :] = jnp.concatenate(g[sel], axis=-1)


def _wkv_scan_kernel(q_ref, y1_ref, m_ref, gm_ref, bonus_ref, gate_ref, lng_ref, lnb_ref,
                     o_ref, st_ref):
    n = HEAD_DIM
    bsz = q_ref.shape[0]

    @pl.when(pl.program_id(0) == 0)
    def _():
        st_ref[...] = jnp.zeros(st_ref.shape, F32)

    c = WKV_CHUNK
    rows = q_ref.shape[1]
    chains = [(bi, h) for bi in range(bsz) for h in range(N_HEADS)]
    state = [st_ref[bi * N_HEADS + h] for bi, h in chains]
    entering = []
    for j in range(rows // c):
        rs = slice(j * c, (j + 1) * c)
        s_in = [s.astype(BF16) for s in state]
        entering.append(s_in)
        state = [_mm(s_in[i], m_ref[bi, rs, h * n:(h + 1) * n]) + gm_ref[bi, rs, h * n:(h + 1) * n]
                 for i, (bi, h) in enumerate(chains)]
    for i, (bi, h) in enumerate(chains):
        st_ref[bi * N_HEADS + h] = state[i]
    ys = [[_mm_nt(q_ref[bi, j * c:(j + 1) * c, h * n:(h + 1) * n], entering[j][i])
           for i, (bi, h) in enumerate(chains)] for j in range(rows // c)]
    y = jnp.concatenate(
        [jnp.concatenate([jnp.concatenate(ys[j][bi * N_HEADS:(bi + 1) * N_HEADS], axis=-1)
                          for j in range(rows // c)], axis=0) + y1_ref[bi]
         for bi in range(bsz)], axis=0)
    mean = _head_sums(y) * (1.0 / n)
    yc = y - mean
    var = _head_sums(yc * yc) * (1.0 / n)
    yn = yc * lax.rsqrt(var + LN_X_EPS) * lng_ref[...] + lnb_ref[...]
    for bi in range(bsz):
        o_ref[bi] = ((yn[bi * rows:(bi + 1) * rows] + bonus_ref[bi]) * gate_ref[bi]).astype(o_ref.dtype)


def _wkv_scan(q, y1, m, gm, bonus, gate, ln_g, ln_b, rows=2 * WKV_CHUNK):
    bsz, seq, w = y1.shape
    tile = pl.BlockSpec((bsz, rows, w), lambda ci: (0, ci, 0))
    vec = pl.BlockSpec((1, w), lambda ci: (0, 0))
    return pl.pallas_call(
        _wkv_scan_kernel,
        grid=(seq // rows,),
        in_specs=[tile] * 6 + [vec] * 2,
        out_specs=tile,
        out_shape=jax.ShapeDtypeStruct((bsz, seq, w), BF16),
        scratch_shapes=[pltpu.VMEM((bsz * N_HEADS, HEAD_DIM, HEAD_DIM), F32)],
        compiler_params=_params("arbitrary"),
        name="wkv_state_scan",
    )(q, y1, m, gm, bonus, gate, ln_g.reshape(1, w), ln_b.reshape(1, w))


def _wout_kernel(x_ref, y0_ref, y1_ref, y2_ref, y3_ref, w_ref, o_ref):
    acc = x_ref[...]
    for idx, y_ref in enumerate((y0_ref, y1_ref, y2_ref, y3_ref)):
        w = w_ref[0, idx * GROUP_WIDTH:(idx + 1) * GROUP_WIDTH, :].astype(BF16)
        acc = acc + jnp.dot(y_ref[...], w, preferred_element_type=F32)
    o_ref[...] = acc


def _wout(x2d, ys, w_stack, layer, tm=512):
    m, d = x2d.shape
    ytile = pl.BlockSpec((tm, GROUP_WIDTH), lambda i: (i, 0))
    xtile = pl.BlockSpec((tm, d), lambda i: (i, 0))
    wspec = pl.BlockSpec((1,) + w_stack.shape[1:], lambda i: (layer, 0, 0))
    return pl.pallas_call(
        _wout_kernel,
        grid=(m // tm,),
        in_specs=[xtile] + [ytile] * 4 + [wspec],
        out_specs=xtile,
        out_shape=jax.ShapeDtypeStruct((m, d), F32),
        compiler_params=_params("parallel", vmem_limit=FFN_VMEM_LIMIT),
        name="wout_residual",
    )(x2d, *ys, w_stack)


def _ffn_kernel(x_ref, g_ref, wu_ref, wd_ref, o_ref, h_ref):
    @pl.when(pl.program_id(1) == 0)
    def _():
        x = x_ref[...]
        ms = jnp.mean(x * x, axis=-1, keepdims=True)
        h_ref[...] = (x * lax.rsqrt(ms + RMS_EPS) * g_ref[0]).astype(BF16)
        o_ref[...] = x

    u = jnp.dot(h_ref[...], wu_ref[0].astype(BF16), preferred_element_type=F32)
    act = jnp.square(jnp.maximum(u, 0.0)).astype(BF16)
    o_ref[...] += jnp.dot(act, wd_ref[0].astype(BF16), preferred_element_type=F32)


def _ffn(x2d, gains, wu_stack, wd_stack, layer, tm=1024, tf=512):
    m, d = x2d.shape
    f = wu_stack.shape[2]
    xtile = pl.BlockSpec((tm, d), lambda i, j: (i, 0))
    return pl.pallas_call(
        _ffn_kernel,
        grid=(m // tm, f // tf),
        in_specs=[
            xtile,
            pl.BlockSpec((1, 1, d), lambda i, j: (layer, 0, 0)),
            pl.BlockSpec((1, d, tf), lambda i, j: (layer, 0, j)),
            pl.BlockSpec((1, tf, d), lambda i, j: (layer, j, 0)),
        ],
        out_specs=xtile,
        out_shape=jax.ShapeDtypeStruct((m, d), F32),
        scratch_shapes=[pltpu.VMEM((tm, d), BF16)],
        compiler_params=_params("parallel", "arbitrary", vmem_limit=FFN_VMEM_LIMIT),
        name="ffn",
    )(x2d, gains.reshape(-1, 1, d), wu_stack, wd_stack)


def kernel(x, norm_mix, w_in, conv_w, swa_q_norm, swa_k_norm, swa_sink, dil_q_norm, dil_k_norm,
           rwkv_mu, decay_w0, decay_w2, iclr_a0, iclr_a2, gate_g2, k_k, k_a, r_k, ln_x_g, ln_x_b,
           vres_v0, vres_v1, vres_v2, w_out, norm_ffn, w_up, w_down, rel_bias):
    bsz, seq, d = x.shape
    depth = w_in.shape[0]
    bias = _bias_tiles(rel_bias)
    swa_bias = bias[0, :N_HEADS]
    dil_bias = bias[:, N_HEADS:]
    x2d = x.reshape(bsz * seq, d)
    v_first = None
    for layer in range(depth):
        proj = _norm_matmul(x2d, norm_mix, w_in, layer).reshape(bsz, seq, IN_WIDTH)
        y_swa, y_conv = _swa_conv_mixers(proj, swa_q_norm[layer], swa_k_norm[layer],
                                         swa_sink[layer], swa_bias, conv_w[layer])
        y_dil = _dil_mixer(proj, dil_q_norm[layer], dil_k_norm[layer], dil_bias)
        vres = None if layer == 0 else (vres_v0[layer - 1], vres_v1[layer - 1], vres_v2[layer - 1])
        maps = _rwkv_maps(
            proj, rwkv_mu[layer], decay_w0[layer], decay_w2[layer], iclr_a0[layer], iclr_a2[layer],
            gate_g2[layer], k_k[layer], k_a[layer], r_k[layer], v_first, vres)
        if layer == 0:
            v_first = maps[6]
        y_rwkv = _wkv_scan(*maps[:6], ln_x_g[layer], ln_x_b[layer])
        ys = [y.reshape(bsz * seq, GROUP_WIDTH) for y in (y_conv, y_swa, y_dil, y_rwkv)]
        x2d = _wout(x2d, ys, w_out, layer)
        x2d = _ffn(x2d, norm_ffn, w_up, w_down, layer)
    return x2d.reshape(bsz, seq, d)
```

```python
import functools
import math

import jax
import jax.numpy as jnp
from jax import lax
from jax.experimental import pallas as pl
from jax.experimental.pallas import tpu as pltpu

F32 = jnp.float32
BF16 = jnp.bfloat16

D_MODEL = 2048
HEAD_DIM = 64
GROUP_WIDTH = 512
N_HEADS = GROUP_WIDTH // HEAD_DIM
SWA_KV_HEADS = 2
SWA_GROUP = N_HEADS // SWA_KV_HEADS
SWA_WINDOW = 128
DIL_PAIRS = ((128, 1), (512, 4), (2048, 16))
DECAY_LORA = 64
ICLR_LORA = 64
GATE_LORA = 128
RWKV_IN_WIDTH = 3 * GROUP_WIDTH + DECAY_LORA + ICLR_LORA + GATE_LORA
BLK = 128
NUM_BUCKETS = 32
BUCKET_MAX_DIST = 128
RMS_EPS = 1e-6
LN_X_EPS = 64e-5
NEG = -1e30
LOG2E = math.log2(math.e)
WKV_CHUNK = 64
WKV_GROUP = 4
DIL_TILES = 4
PREV_ROWS = 16
PREP_UNROLL = 4

OFF_CONV = 0
OFF_SWA_Q = OFF_CONV + 3 * GROUP_WIDTH
OFF_SWA_K = OFF_SWA_Q + GROUP_WIDTH
OFF_SWA_V = OFF_SWA_K + SWA_KV_HEADS * HEAD_DIM
OFF_DIL = OFF_SWA_V + SWA_KV_HEADS * HEAD_DIM
OFF_RW = OFF_DIL + 3 * GROUP_WIDTH
IN_WIDTH = OFF_RW + RWKV_IN_WIDTH

V7X_VMEM_BYTES = 64 * 1024 * 1024
VMEM_LIMIT = 48 * 1024 * 1024
FFN_VMEM_LIMIT = V7X_VMEM_BYTES - 6 * 1024 * 1024


def _params(*sem, vmem_limit=VMEM_LIMIT):
    return pltpu.CompilerParams(dimension_semantics=sem, vmem_limit_bytes=vmem_limit)


def _mm(a, b):
    return jnp.dot(a.astype(BF16), b.astype(BF16), preferred_element_type=F32)


def _mm_nt(a, b):
    return lax.dot_general(a.astype(BF16), b.astype(BF16), (((1,), (1,)), ((), ())),
                           preferred_element_type=F32)


def _mm_tn(a, b):
    return lax.dot_general(a.astype(BF16), b.astype(BF16), (((0,), (0,)), ((), ())),
                           preferred_element_type=F32)


def _split_bf16(x):
    hi = x.astype(BF16)
    lo = (x - hi.astype(F32)).astype(BF16)
    return hi, lo


def _head_sums(x, split=True):
    lanes = 128
    r = lax.broadcasted_iota(jnp.int32, (lanes, lanes), 0) // HEAD_DIM
    c = lax.broadcasted_iota(jnp.int32, (lanes, lanes), 1) // HEAD_DIM
    bd = (r == c).astype(BF16)
    parts = _split_bf16(x) if split else (x.astype(BF16),)
    cols = []
    for j in range(x.shape[-1] // lanes):
        sl = slice(j * lanes, (j + 1) * lanes)
        cols.append(sum(jnp.dot(part[:, sl], bd, preferred_element_type=F32) for part in parts))
    return cols[0] if len(cols) == 1 else jnp.concatenate(cols, axis=-1)


def _sigmoid(z):
    return 1.0 / (1.0 + jnp.exp(-z))


def _norm_matmul_kernel(x_ref, g_ref, w_ref, o_ref, h_ref):
    @pl.when(pl.program_id(1) == 0)
    def _():
        x = x_ref[...]
        ms = jnp.mean(x * x, axis=-1, keepdims=True)
        h_ref[...] = (x * lax.rsqrt(ms + RMS_EPS) * g_ref[0]).astype(BF16)

    o_ref[...] = jnp.dot(h_ref[...], w_ref[0].astype(BF16),
                         preferred_element_type=F32).astype(o_ref.dtype)


def _norm_matmul(x2d, gains, w_stack, layer, tm=2048, tn=512):
    m, k = x2d.shape
    n = w_stack.shape[2]
    assert n % tn == 0 and m % tm == 0
    return pl.pallas_call(
        _norm_matmul_kernel,
        grid=(m // tm, n // tn),
        in_specs=[
            pl.BlockSpec((tm, k), lambda i, j: (i, 0)),
            pl.BlockSpec((1, 1, k), lambda i, j: (layer, 0, 0)),
            pl.BlockSpec((1, k, tn), lambda i, j: (layer, 0, j)),
        ],
        out_specs=pl.BlockSpec((tm, tn), lambda i, j: (i, j)),
        out_shape=jax.ShapeDtypeStruct((m, n), BF16),
        scratch_shapes=[pltpu.VMEM((tm, k), BF16)],
        compiler_params=_params("parallel", "arbitrary", vmem_limit=FFN_VMEM_LIMIT),
        name="norm_matmul",
    )(x2d, gains.reshape(-1, 1, k), w_stack)


def _t5_bucket(dist):
    dist = jnp.maximum(dist, 0)
    max_exact = NUM_BUCKETS // 2
    scaled = (jnp.log(jnp.maximum(dist, 1).astype(F32) / max_exact)
              / math.log(BUCKET_MAX_DIST / max_exact))
    large = max_exact + (scaled * (NUM_BUCKETS - max_exact)).astype(jnp.int32)
    large = jnp.minimum(large, NUM_BUCKETS - 1)
    return jnp.where(dist < max_exact, dist, large)


def _bias_kernel(bucket_ref, table_ref, o_ref):
    bucket = bucket_ref[0]

    def head(h, carry):
        g = jnp.zeros(bucket.shape, F32)
        for b in range(NUM_BUCKETS):
            g = jnp.where(bucket == b, table_ref[b, h], g)
        rows = jnp.broadcast_to(g[0:1], (BLK, 2 * BLK))
        o_ref[0, h] = pltpu.roll(rows, 0, axis=1, stride=1, stride_axis=0)
        return carry

    lax.fori_loop(0, o_ref.shape[1], head, 0)


def _bias_tiles(rel_bias):
    lag_dist = BLK - jnp.arange(2 * BLK)
    buckets = jnp.stack([_t5_bucket(lag_dist * r) for _, r in DIL_PAIRS]).astype(jnp.int32)
    buckets = jnp.broadcast_to(buckets[:, None, :], (len(DIL_PAIRS), 8, 2 * BLK))
    nh = rel_bias.shape[1]
    return pl.pallas_call(
        _bias_kernel,
        grid=(len(DIL_PAIRS),),
        in_specs=[
            pl.BlockSpec((1, 8, 2 * BLK), lambda s: (s, 0, 0)),
            pl.BlockSpec(memory_space=pltpu.SMEM),
        ],
        out_specs=pl.BlockSpec((1, nh, BLK, 2 * BLK), lambda s: (s, 0, 0, 0)),
        out_shape=jax.ShapeDtypeStruct((len(DIL_PAIRS), nh, BLK, 2 * BLK), F32),
        compiler_params=_params("arbitrary"),
        name="bias_tiles",
    )(buckets, rel_bias)


def _short_conv(b_ref, c_ref, u_ref, w_ref, o_ref):
    lanes = 128
    for j in range(b_ref.shape[2] // lanes):
        cols = slice(j * lanes, (j + 1) * lanes)
        z = c_ref[0, :, cols].astype(F32) * u_ref[0, :, cols].astype(F32)
        row = lax.broadcasted_iota(jnp.int32, z.shape, 0)
        z1 = jnp.where(row >= 1, pltpu.roll(z, 1, axis=0), 0.0)
        z2 = jnp.where(row >= 2, pltpu.roll(z, 2, axis=0), 0.0)
        w = w_ref[:, cols]
        y = z2 * w[0:1, :] + z1 * w[1:2, :] + z * w[2:3, :]
        o_ref[0, :, cols] = (b_ref[0, :, cols].astype(F32) * y).astype(o_ref.dtype)


def _band_mask(max_dist):
    a = lax.broadcasted_iota(jnp.int32, (BLK, 2 * BLK), 0)
    b = lax.broadcasted_iota(jnp.int32, (BLK, 2 * BLK), 1)
    dist = BLK + a - b
    return (dist >= 0) & (dist <= max_dist), b


def _attend(qs, kws, vws, biases, sinks=None, normalize=True):
    idx = range(len(qs))
    s = [lax.dot_general(qs[i], kws[i], (((1,), (1,)), ((), ())), preferred_element_type=F32)
         + biases[i] for i in idx]
    m = [jnp.max(s[i], axis=-1, keepdims=True) for i in idx]
    if sinks is not None:
        m = [jnp.maximum(m[i], sinks[i]) for i in idx]
    p = [jnp.exp2(s[i] - m[i]) for i in idx]
    den = [jnp.sum(p[i], axis=-1, keepdims=True) for i in idx]
    if sinks is not None:
        den = [den[i] + jnp.exp2(sinks[i] - m[i]) for i in idx]
    o = [jnp.dot(p[i].astype(BF16), vws[i], preferred_element_type=F32) for i in idx]
    if normalize:
        o = [o[i] / den[i] for i in idx]
    return o, m, den


def _head_rms(x, gain):
    ms = _head_sums(x * x, split=False) * (1.0 / HEAD_DIM)
    return x * lax.rsqrt(ms + RMS_EPS) * gain


def _swa_kernel(q_ref, k_ref, v_ref, qg_ref, kg_ref, sink_ref, bias_ref,
                cb_ref, cc_ref, cu_ref, cw_ref, o_ref, oc_ref,
                qn_ref, kn_ref, vb_ref, bm_ref):
    seq = q_ref.shape[1]
    nb = seq // BLK
    kvw = SWA_KV_HEADS * HEAD_DIM
    scale = HEAD_DIM ** -0.5 * LOG2E

    _short_conv(cb_ref, cc_ref, cu_ref, cw_ref, oc_ref)

    kn_ref[0:BLK, :] = jnp.zeros((BLK, 2 * kvw), BF16)
    vb_ref[0:BLK, :] = jnp.zeros((BLK, 2 * kvw), BF16)
    lane_half = lax.broadcasted_iota(jnp.int32, (1, kvw), 1) // HEAD_DIM

    def both_halves(x):
        xr = pltpu.roll(x, HEAD_DIM, axis=1)
        return jnp.concatenate([jnp.where(lane_half == 0, x, xr),
                                jnp.where(lane_half == 0, xr, x)], axis=-1)

    def prep(i, carry):
        r0 = pl.multiple_of(i * BLK, BLK)
        q = q_ref[0, pl.ds(r0, BLK), :].astype(F32)
        qn_ref[pl.ds(r0, BLK), :] = (_head_rms(q, qg_ref[...]) * scale).astype(BF16)
        k = _head_rms(k_ref[0, pl.ds(r0, BLK), :].astype(F32), kg_ref[...])
        kn_ref[pl.ds(r0 + BLK, BLK), :] = both_halves(k).astype(BF16)
        v = v_ref[0, pl.ds(r0, BLK), :].astype(F32)
        vb_ref[pl.ds(r0 + BLK, BLK), :] = both_halves(v).astype(BF16)
        return carry

    lax.fori_loop(0, nb, prep, 0, unroll=PREP_UNROLL)

    band, kcol = _band_mask(SWA_WINDOW - 1)
    for h in range(N_HEADS):
        bm_ref[h] = jnp.where(band & (kcol >= BLK), bias_ref[h] * LOG2E, NEG)
        bm_ref[N_HEADS + h] = jnp.where(band, bias_ref[h] * LOG2E, NEG)
    sinks = [sink_ref[h] * LOG2E for h in range(N_HEADS)]

    def block(i, carry):
        r0 = pl.multiple_of(i * BLK, BLK)
        later = jnp.minimum(i, 1) * N_HEADS
        qs, kws, vws, bms = [], [], [], []
        for hk in range(SWA_KV_HEADS):
            kw = kn_ref[pl.ds(r0, 2 * BLK), hk * kvw:(hk + 1) * kvw]
            vw = vb_ref[pl.ds(r0, 2 * BLK), hk * kvw:(hk + 1) * kvw]
            for g in range(SWA_GROUP):
                h = hk * SWA_GROUP + g
                q2 = qn_ref[pl.ds(r0, BLK), (h // 2) * kvw:(h // 2 + 1) * kvw]
                qs.append(jnp.where(lane_half == h % 2, q2, jnp.zeros_like(q2)))
                kws.append(kw)
                vws.append(vw)
                bms.append(bm_ref[later + h])
        outs, _, _ = _attend(qs, kws, vws, bms, sinks)
        pairs = [jnp.where(lane_half == 0, outs[h], outs[h + 1]) for h in range(0, N_HEADS, 2)]
        o_ref[0, pl.ds(r0, BLK), :] = jnp.concatenate(pairs, axis=-1).astype(o_ref.dtype)
        return carry

    lax.fori_loop(0, nb, block, 0)


def _swa_conv_mixers(proj, q_gain, k_gain, sink, bias, conv_w):
    bsz, seq, _ = proj.shape
    kvw = SWA_KV_HEADS * HEAD_DIM
    q_gain_t = jnp.tile(q_gain, N_HEADS).reshape(1, GROUP_WIDTH)
    k_gain_t = jnp.tile(k_gain, SWA_KV_HEADS).reshape(1, kvw)
    wide = lambda off: pl.BlockSpec((1, seq, GROUP_WIDTH), lambda b: (b, 0, off // GROUP_WIDTH))
    out = jax.ShapeDtypeStruct((bsz, seq, GROUP_WIDTH), BF16)
    return pl.pallas_call(
        _swa_kernel,
        grid=(bsz,),
        in_specs=[
            wide(OFF_SWA_Q),
            pl.BlockSpec((1, seq, kvw), lambda b: (b, 0, OFF_SWA_K // kvw)),
            pl.BlockSpec((1, seq, kvw), lambda b: (b, 0, OFF_SWA_V // kvw)),
            pl.BlockSpec((1, GROUP_WIDTH), lambda b: (0, 0)),
            pl.BlockSpec((1, kvw), lambda b: (0, 0)),
            pl.BlockSpec(memory_space=pltpu.SMEM),
            pl.BlockSpec((N_HEADS, BLK, 2 * BLK), lambda b: (0, 0, 0)),
            wide(OFF_CONV), wide(OFF_CONV + GROUP_WIDTH), wide(OFF_CONV + 2 * GROUP_WIDTH),
            pl.BlockSpec((3, GROUP_WIDTH), lambda b: (0, 0)),
        ],
        out_specs=[pl.BlockSpec((1, seq, GROUP_WIDTH), lambda b: (b, 0, 0))] * 2,
        out_shape=[out, out],
        scratch_shapes=[
            pltpu.VMEM((seq, GROUP_WIDTH), BF16),
            pltpu.VMEM((seq + BLK, 2 * kvw), BF16),
            pltpu.VMEM((seq + BLK, 2 * kvw), BF16),
            pltpu.VMEM((2 * N_HEADS, BLK, 2 * BLK), F32),
        ],
        compiler_params=_params("parallel"),
        name="swa_conv_mixers",
    )(proj, proj, proj, q_gain_t, k_gain_t, sink, bias, proj, proj, proj, conv_w)


def _dil_kernel(q_ref, k_ref, v_ref, qg_ref, kg_ref, bias_ref, o_ref,
                qn_ref, kn_ref, vn_ref, ob_ref, mb_ref, db_ref, bm_ref):
    seq = q_ref.shape[1]
    lanes = q_ref.shape[2]
    heads = lanes // HEAD_DIM
    scale = HEAD_DIM ** -0.5 * LOG2E

    def prep(i, carry):
        r0 = pl.multiple_of(i * BLK, BLK)
        q = q_ref[0, pl.ds(r0, BLK), :].astype(F32)
        k = k_ref[0, pl.ds(r0, BLK), :].astype(F32)
        qn_ref[pl.ds(r0, BLK), :] = _head_rms(q, qg_ref[...]) * scale
        kn_ref[pl.ds(r0, BLK), :] = _head_rms(k, kg_ref[...])
        vn_ref[pl.ds(r0, BLK), :] = v_ref[0, pl.ds(r0, BLK), :].astype(F32)
        return carry

    lax.fori_loop(0, seq // BLK, prep, 0, unroll=PREP_UNROLL)

    for br, (window, r) in enumerate(DIL_PAIRS):
        band, kcol = _band_mask(window // r)
        for h in range(heads):
            bias = bias_ref[br, h] * LOG2E
            bm_ref[(br * heads + h) * 2] = jnp.where(band & (kcol >= BLK), bias, NEG)
            bm_ref[(br * heads + h) * 2 + 1] = jnp.where(band, bias, NEG)

    lane_head = lax.broadcasted_iota(jnp.int32, (1, lanes), 1) // HEAD_DIM

    for br, (window, r) in enumerate(DIL_PAIRS):
        nb = seq // r // BLK

        def blocks(it, carry, br=br, r=r, nb=nb):
            qs, kws, vws, bms, curs = [], [], [], [], []
            for u in range(DIL_TILES):
                t = it * DIL_TILES + u
                c = t // nb
                i = t - c * nb
                cur = c + i * (BLK * r)
                prev = jnp.maximum(cur - BLK * r, c)
                later = jnp.minimum(i, 1)

                def rows(ref, start):
                    if r == 1:
                        return ref[pl.ds(start, BLK), :]
                    return ref[pl.ds(start, BLK, stride=r), :]

                q = rows(qn_ref, cur).astype(BF16)
                if nb == 1:
                    kw = rows(kn_ref, cur).astype(BF16)
                    vw = rows(vn_ref, cur).astype(BF16)
                else:
                    kw = jnp.concatenate([rows(kn_ref, prev), rows(kn_ref, cur)],
                                         axis=0).astype(BF16)
                    vw = jnp.concatenate([rows(vn_ref, prev), rows(vn_ref, cur)],
                                         axis=0).astype(BF16)
                curs.append(cur)
                for h in range(heads):
                    qs.append(jnp.where(lane_head == h, q, jnp.zeros_like(q)))
                    kws.append(kw)
                    vws.append(vw)
                    if nb == 1:
                        bms.append(bm_ref[(br * heads + h) * 2, :, BLK:])
                    else:
                        bms.append(bm_ref[(br * heads + h) * 2 + later])
            outs, ms, dens = _attend(qs, kws, vws, bms, normalize=False)
            for u in range(DIL_TILES):
                o_all, m_all, d_all = outs[u * heads], ms[u * heads], dens[u * heads]
                for h in range(1, heads):
                    o_all = jnp.where(lane_head == h, outs[u * heads + h], o_all)
                    m_all = jnp.where(lane_head == h, ms[u * heads + h], m_all)
                    d_all = jnp.where(lane_head == h, dens[u * heads + h], d_all)
                if r == 1:
                    rows = pl.ds(curs[u], BLK)
                else:
                    rows = pl.ds(curs[u], BLK, stride=r)
                ob_ref[br, rows, :] = o_all
                mb_ref[br, rows, :] = m_all
                db_ref[br, rows, :] = d_all
            return carry

        lax.fori_loop(0, r * nb // DIL_TILES, blocks, 0)

    def combine(i, carry):
        r0 = pl.multiple_of(i * BLK, BLK)
        rows = pl.ds(r0, BLK)
        m0, m1, m2 = mb_ref[0, rows, :], mb_ref[1, rows, :], mb_ref[2, rows, :]
        m = jnp.maximum(jnp.maximum(m0, m1), m2)
        e0, e1, e2 = jnp.exp2(m0 - m), jnp.exp2(m1 - m), jnp.exp2(m2 - m)
        num = e0 * ob_ref[0, rows, :] + e1 * ob_ref[1, rows, :] + e2 * ob_ref[2, rows, :]
        den = e0 * db_ref[0, rows, :] + e1 * db_ref[1, rows, :] + e2 * db_ref[2, rows, :]
        o_ref[0, rows, :] = (num / den).astype(o_ref.dtype)
        return carry

    lax.fori_loop(0, seq // BLK, combine, 0, unroll=PREP_UNROLL)


def _dil_mixer(proj, q_gain, k_gain, bias):
    bsz, seq, _ = proj.shape
    lanes = 128
    heads = lanes // HEAD_DIM
    nblk = GROUP_WIDTH // lanes
    base = OFF_DIL // lanes
    gq = jnp.tile(q_gain, heads).reshape(1, lanes)
    gk = jnp.tile(k_gain, heads).reshape(1, lanes)

    def col(seg):
        return pl.BlockSpec((1, seq, lanes), lambda b, j: (b, 0, base + seg * nblk + j))

    return pl.pallas_call(
        _dil_kernel,
        grid=(bsz, nblk),
        in_specs=[
            col(0), col(1), col(2),
            pl.BlockSpec((1, lanes), lambda b, j: (0, 0)),
            pl.BlockSpec((1, lanes), lambda b, j: (0, 0)),
            pl.BlockSpec((len(DIL_PAIRS), heads, BLK, 2 * BLK), lambda b, j: (0, j, 0, 0)),
        ],
        out_specs=pl.BlockSpec((1, seq, lanes), lambda b, j: (b, 0, j)),
        out_shape=jax.ShapeDtypeStruct((bsz, seq, GROUP_WIDTH), BF16),
        scratch_shapes=[
            pltpu.VMEM((seq, lanes), F32),
            pltpu.VMEM((seq, lanes), F32),
            pltpu.VMEM((seq, lanes), F32),
            pltpu.VMEM((len(DIL_PAIRS), seq, lanes), F32),
            pltpu.VMEM((len(DIL_PAIRS), seq, lanes), F32),
            pltpu.VMEM((len(DIL_PAIRS), seq, lanes), F32),
            pltpu.VMEM((len(DIL_PAIRS) * heads * 2, BLK, 2 * BLK), F32),
        ],
        compiler_params=_params("parallel", "parallel"),
        name="dil_mixer",
    )(proj, proj, proj, gq, gk, bias)


def _rwkv_maps_kernel(*refs, has_vres):
    if has_vres:
        (p_ref, pp_ref, mu_ref, w0_ref, w2_ref, a0_ref, a2_ref, g2_ref, kk_ref, ka_ref, rk_ref,
         vf_ref, v0_ref, v1_ref, v2_ref,
         q_out, y1_out, m_out, gm_out, bonus_out, gate_out, *scratch) = refs
    else:
        (p_ref, pp_ref, mu_ref, w0_ref, w2_ref, a0_ref, a2_ref, g2_ref, kk_ref, ka_ref, rk_ref,
         q_out, y1_out, m_out, gm_out, bonus_out, gate_out, v_out, *scratch) = refs
    w = GROUP_WIDTH
    p = p_ref[0].astype(F32)
    row = lax.broadcasted_iota(jnp.int32, p.shape, 0)
    last_prev = pp_ref[0, PREV_ROWS - 1:PREV_ROWS, :].astype(F32)
    last_prev = jnp.where(pl.program_id(1) > 0, last_prev, 0.0)
    prev = jnp.where(row >= 1, pltpu.roll(p, 1, axis=0), last_prev)
    xs = p + (prev - p) * mu_ref[...]
    grows = WKV_GROUP * WKV_CHUNK
    for grp in range(p.shape[0] // grows):
        rs = slice(grp * grows, (grp + 1) * grows)
        x = xs[rs]
        r = x[:, 0:w]
        k = x[:, w:2 * w]
        v = x[:, 2 * w:3 * w]
        o = 3 * w
        wd = x[:, o:o + DECAY_LORA]
        ad = x[:, o + DECAY_LORA:o + DECAY_LORA + ICLR_LORA]
        gd = x[:, o + DECAY_LORA + ICLR_LORA:]

        z = -(w0_ref[...] + _mm(jnp.tanh(wd), w2_ref[...]))
        softplus = jnp.maximum(z, 0.0) + jnp.log(1.0 + jnp.exp(-jnp.abs(z)))
        logw = -softplus - 0.5
        lw = -jnp.exp(logw)
        a = _sigmoid(a0_ref[...] + _mm(ad, a2_ref[...]))
        gate_out[0, rs] = _mm(_sigmoid(gd), g2_ref[...])
        if has_vres:
            mix = _sigmoid(v0_ref[...] + _mm(_mm(v, v1_ref[...]), v2_ref[...]))
            v = v + (vf_ref[0, rs] - v) * mix
        else:
            v_out[0, rs] = v
        kk = k * kk_ref[...]
        ss = _head_sums(kk * kk, split=False)
        kk = kk * lax.rsqrt(jnp.maximum(ss, 1e-24))
        k = k * (1.0 + (a - 1.0) * ka_ref[...])
        bonus_out[0, rs] = _head_sums(r * k * rk_ref[...]) * v
        _chunk_maps(grp, r, lw, k, v, kk, kk * a, q_out.at[0], y1_out.at[0], m_out.at[0],
                    gm_out.at[0], *scratch)


def _rwkv_maps(proj, mu, w0, w2, a0, a2, g2, k_k, k_a, r_k, v_first, vres, ts=512):
    bsz, seq, _ = proj.shape
    w = GROUP_WIDTH
    nt = seq // ts
    has_vres = vres is not None

    def full(shape):
        return pl.BlockSpec(shape, lambda b, i: (0,) * len(shape))

    row = lambda a: a.reshape(1, -1)
    tile = pl.BlockSpec((1, ts, w), lambda b, i: (b, i, 0))
    in_specs = [
        pl.BlockSpec((pl.Element(1), pl.Element(ts), pl.Element(RWKV_IN_WIDTH)),
                     lambda b, i: (b, i * ts, OFF_RW)),
        pl.BlockSpec((pl.Element(1), pl.Element(PREV_ROWS), pl.Element(RWKV_IN_WIDTH)),
                     lambda b, i: (b, jnp.maximum(i * (ts // PREV_ROWS) - 1, 0) * PREV_ROWS, OFF_RW)),
        full((1, RWKV_IN_WIDTH)), full((1, w)), full((DECAY_LORA, w)), full((1, w)),
        full((ICLR_LORA, w)), full((GATE_LORA, w)), full((1, w)), full((1, w)), full((1, w)),
    ]
    args = [proj, proj, row(mu), row(w0), w2, row(a0), a2, g2, row(k_k), row(k_a), row(r_k)]
    if has_vres:
        v0, v1, v2 = vres
        in_specs += [tile, full((1, w)), full(v1.shape), full(v2.shape)]
        args += [v_first, row(v0), v1, v2]
    f32 = jax.ShapeDtypeStruct((bsz, seq, w), F32)
    bf16 = jax.ShapeDtypeStruct((bsz, seq, w), BF16)
    out_shape = [bf16, f32, bf16, f32, f32, f32] + ([] if has_vres else [f32])
    return pl.pallas_call(
        functools.partial(_rwkv_maps_kernel, has_vres=has_vres),
        grid=(bsz, nt),
        in_specs=in_specs,
        out_specs=[tile] * len(out_shape),
        out_shape=out_shape,
        scratch_shapes=[pltpu.VMEM((ts, w), dt) for dt in (BF16, F32, BF16, BF16, F32, BF16, BF16)],
        compiler_params=_params("parallel", "parallel"),
        name="rwkv_chunk_maps",
    )(*args)


def _chunk_maps(grp, r, lw, k, v, kk, b, q_out, y1_out, m_out, g_out,
                at_ref, rt_ref, bt_ref, kt_ref, dec_ref, vb_ref, rb_ref):
    c = WKV_CHUNK
    n = HEAD_DIM
    rows = r.shape[0]
    rs = slice(grp * rows, (grp + 1) * rows)

    row = lax.broadcasted_iota(jnp.int32, (rows, rows), 0)
    col = lax.broadcasted_iota(jnp.int32, (rows, rows), 1)
    tri = ((row >= col) & ((row // c) == (col // c))).astype(BF16)
    lw_hi, lw_lo = _split_bf16(lw)
    cum = (jnp.dot(tri, lw_hi, preferred_element_type=F32)
           + jnp.dot(tri, lw_lo, preferred_element_type=F32))
    e_pos = jnp.exp(cum)
    e_neg = jnp.exp(-cum)
    r_t = r * e_pos
    at_ref[rs] = (-kk * jnp.exp(cum - lw)).astype(BF16)
    rt_ref[rs] = r_t
    rb_ref[rs] = r_t.astype(BF16)
    bt_ref[rs] = (b * e_neg).astype(BF16)
    kt_ref[rs] = (k * e_neg).astype(BF16)
    dec_ref[rs] = e_pos
    vb_ref[rs] = v.astype(BF16)

    crow = lax.broadcasted_iota(jnp.int32, (c, c), 0)
    ccol = lax.broadcasted_iota(jnp.int32, (c, c), 1)
    lower = crow >= ccol
    strict = crow > ccol
    eye = (crow == ccol).astype(F32)
    wrow = lax.broadcasted_iota(jnp.int32, (c, 2 * c), 0)
    wcol = lax.broadcasted_iota(jnp.int32, (c, 2 * c), 1)
    strict_left = wrow > wcol
    right = wcol >= c
    eye_right = (wcol == wrow + c).astype(F32)

    def tile(ref, j, h):
        return ref[j * c:(j + 1) * c, h * n:(h + 1) * n]

    _chunk_group(range(grp * WKV_GROUP, (grp + 1) * WKV_GROUP), tile, at_ref, rt_ref, bt_ref,
                 kt_ref, dec_ref, vb_ref, rb_ref, q_out, y1_out, m_out, g_out,
                 lower, strict, eye, strict_left, right, eye_right)


def _chunk_group(chunks, tile, at_ref, rt_ref, bt_ref, kt_ref, dec_ref, vb_ref, rb_ref,
                 q_out, y1_out, m_out, g_out, lower, strict, eye, strict_left, right, eye_right):
    c = WKV_CHUNK
    n = HEAD_DIM
    pairs = [(j, h) for j in chunks for h in range(N_HEADS)]
    ah = [tile(at_ref, j, h) for j, h in pairs]
    rh = [tile(rt_ref, j, h) for j, h in pairs]
    bh = [tile(bt_ref, j, h) for j, h in pairs]
    kh = [tile(kt_ref, j, h) for j, h in pairs]
    vh = [tile(vb_ref, j, h) for j, h in pairs]
    dh = [dec_ref[(j + 1) * c - 1:(j + 1) * c, h * n:(h + 1) * n] for j, h in pairs]
    idx = range(len(pairs))
    ar = [jnp.concatenate([ah[i], tile(rb_ref, *pairs[i])], axis=0) for i in idx]
    bk = [jnp.concatenate([bh[i], kh[i]], axis=0) for i in idx]
    aa = [_mm_nt(ar[i], bk[i]) for i in idx]
    a_kr = [jnp.concatenate([jnp.where(strict, aa[i][:c, c:], 0.0),
                             jnp.where(lower, aa[i][c:, c:], 0.0)], axis=0).astype(BF16)
            for i in idx]
    a_rb = [jnp.where(lower, aa[i][c:, :c], 0.0).astype(BF16) for i in idx]
    ps = [jnp.where(strict_left, aa[i][:c, :], 0.0) + eye_right for i in idx]
    for _ in range(int(math.log2(c))):
        ps = [_mm(ps[i][:, :c], ps[i]) + jnp.where(right, ps[i], 0.0) for i in idx]
    t = [ps[i][:, c:].astype(BF16) for i in idx]
    av = [_mm(a_kr[i], vh[i]) for i in idx]
    wu = [_mm(t[i], jnp.concatenate([ah[i], av[i][:c].astype(BF16)], axis=1)).astype(BF16)
          for i in idx]
    aw = [_mm(a_rb[i], wu[i]) for i in idx]
    q = [rh[i] + aw[i][:, :n] for i in idx]
    y1 = [aw[i][:, n:] + av[i][c:] for i in idx]
    zero = jnp.zeros((c, n), BF16)
    mg = [_mm_tn(jnp.concatenate([wu[i], jnp.concatenate([zero, vh[i]], axis=1)], axis=0), bk[i])
          for i in idx]
    m = [(eye + mg[i][:n]) * dh[i] for i in idx]
    g = [mg[i][n:] * dh[i] for i in idx]
    for j in chunks:
        sel = slice((j - chunks[0]) * N_HEADS, (j - chunks[0] + 1) * N_HEADS)
        q_out[j * c:(j + 1) * c, :] = jnp.concatenate(q[sel], axis=-1).astype(q_out.dtype)
        y1_out[j * c:(j + 1) * c, :] = jnp.concatenate(y1[sel], axis=-1)
        m_out[j * c:(j + 1) * c, :] = jnp.concatenate(m[sel], axis=-1).astype(m_out.dtype)
        g_out[j * c:(j + 1) * c, :] = jnp.concatenate(g[sel], axis=-1)


def _wkv_scan_kernel(q_ref, y1_ref, m_ref, gm_ref, bonus_ref, gate_ref, lng_ref, lnb_ref,
                     o_ref, st_ref):
    n = HEAD_DIM
    bsz = q_ref.shape[0]

    @pl.when(pl.program_id(0) == 0)
    def _():
        st_ref[...] = jnp.zeros(st_ref.shape, F32)

    c = WKV_CHUNK
    rows = q_ref.shape[1]
    chains = [(bi, h) for bi in range(bsz) for h in range(N_HEADS)]
    state = [st_ref[bi * N_HEADS + h] for bi, h in chains]
    entering = []
    for j in range(rows // c):
        rs = slice(j * c, (j + 1) * c)
        s_in = [s.astype(BF16) for s in state]
        entering.append(s_in)
        state = [_mm(s_in[i], m_ref[bi, rs, h * n:(h + 1) * n]) + gm_ref[bi, rs, h * n:(h + 1) * n]
                 for i, (bi, h) in enumerate(chains)]
    for i, (bi, h) in enumerate(chains):
        st_ref[bi * N_HEADS + h] = state[i]
    ys = [[_mm_nt(q_ref[bi, j * c:(j + 1) * c, h * n:(h + 1) * n], entering[j][i])
           for i, (bi, h) in enumerate(chains)] for j in range(rows // c)]
    y = jnp.concatenate(
        [jnp.concatenate([jnp.concatenate(ys[j][bi * N_HEADS:(bi + 1) * N_HEADS], axis=-1)
                          for j in range(rows // c)], axis=0) + y1_ref[bi]
         for bi in range(bsz)], axis=0)
    mean = _head_sums(y) * (1.0 / n)
    yc = y - mean
    var = _head_sums(yc * yc) * (1.0 / n)
    yn = yc * lax.rsqrt(var + LN_X_EPS) * lng_ref[...] + lnb_ref[...]
    for bi in range(bsz):
        o_ref[bi] = ((yn[bi * rows:(bi + 1) * rows] + bonus_ref[bi]) * gate_ref[bi]).astype(o_ref.dtype)


def _wkv_scan(q, y1, m, gm, bonus, gate, ln_g, ln_b, rows=2 * WKV_CHUNK):
    bsz, seq, w = y1.shape
    tile = pl.BlockSpec((bsz, rows, w), lambda ci: (0, ci, 0))
    vec = pl.BlockSpec((1, w), lambda ci: (0, 0))
    return pl.pallas_call(
        _wkv_scan_kernel,
        grid=(seq // rows,),
        in_specs=[tile] * 6 + [vec] * 2,
        out_specs=tile,
        out_shape=jax.ShapeDtypeStruct((bsz, seq, w), BF16),
        scratch_shapes=[pltpu.VMEM((bsz * N_HEADS, HEAD_DIM, HEAD_DIM), F32)],
        compiler_params=_params("arbitrary"),
        name="wkv_state_scan",
    )(q, y1, m, gm, bonus, gate, ln_g.reshape(1, w), ln_b.reshape(1, w))


def _wout_kernel(x_ref, y0_ref, y1_ref, y2_ref, y3_ref, w_ref, o_ref):
    acc = x_ref[...]
    for idx, y_ref in enumerate((y0_ref, y1_ref, y2_ref, y3_ref)):
        w = w_ref[0, idx * GROUP_WIDTH:(idx + 1) * GROUP_WIDTH, :].astype(BF16)
        acc = acc + jnp.dot(y_ref[...], w, preferred_element_type=F32)
    o_ref[...] = acc


def _wout(x2d, ys, w_stack, layer, tm=512):
    m, d = x2d.shape
    ytile = pl.BlockSpec((tm, GROUP_WIDTH), lambda i: (i, 0))
    xtile = pl.BlockSpec((tm, d), lambda i: (i, 0))
    wspec = pl.BlockSpec((1,) + w_stack.shape[1:], lambda i: (layer, 0, 0))
    return pl.pallas_call(
        _wout_kernel,
        grid=(m // tm,),
        in_specs=[xtile] + [ytile] * 4 + [wspec],
        out_specs=xtile,
        out_shape=jax.ShapeDtypeStruct((m, d), F32),
        compiler_params=_params("parallel", vmem_limit=FFN_VMEM_LIMIT),
        name="wout_residual",
    )(x2d, *ys, w_stack)


def _ffn_kernel(x_ref, g_ref, wu_ref, wd_ref, o_ref, h_ref):
    @pl.when(pl.program_id(1) == 0)
    def _():
        x = x_ref[...]
        ms = jnp.mean(x * x, axis=-1, keepdims=True)
        h_ref[...] = (x * lax.rsqrt(ms + RMS_EPS) * g_ref[0]).astype(BF16)
        o_ref[...] = x

    u = jnp.dot(h_ref[...], wu_ref[0].astype(BF16), preferred_element_type=F32)
    act = jnp.square(jnp.maximum(u, 0.0)).astype(BF16)
    o_ref[...] += jnp.dot(act, wd_ref[0].astype(BF16), preferred_element_type=F32)


def _ffn(x2d, gains, wu_stack, wd_stack, layer, tm=1024, tf=512):
    m, d = x2d.shape
    f = wu_stack.shape[2]
    xtile = pl.BlockSpec((tm, d), lambda i, j: (i, 0))
    return pl.pallas_call(
        _ffn_kernel,
        grid=(m // tm, f // tf),
        in_specs=[
            xtile,
            pl.BlockSpec((1, 1, d), lambda i, j: (layer, 0, 0)),
            pl.BlockSpec((1, d, tf), lambda i, j: (layer, 0, j)),
            pl.BlockSpec((1, tf, d), lambda i, j: (layer, j, 0)),
        ],
        out_specs=xtile,
        out_shape=jax.ShapeDtypeStruct((m, d), F32),
        scratch_shapes=[pltpu.VMEM((tm, d), BF16)],
        compiler_params=_params("parallel", "arbitrary", vmem_limit=FFN_VMEM_LIMIT),
        name="ffn",
    )(x2d, gains.reshape(-1, 1, d), wu_stack, wd_stack)


def kernel(x, norm_mix, w_in, conv_w, swa_q_norm, swa_k_norm, swa_sink, dil_q_norm, dil_k_norm,
           rwkv_mu, decay_w0, decay_w2, iclr_a0, iclr_a2, gate_g2, k_k, k_a, r_k, ln_x_g, ln_x_b,
           vres_v0, vres_v1, vres_v2, w_out, norm_ffn, w_up, w_down, rel_bias):
    bsz, seq, d = x.shape
    depth = w_in.shape[0]
    bias = _bias_tiles(rel_bias)
    swa_bias = bias[0, :N_HEADS]
    dil_bias = bias[:, N_HEADS:]
    x2d = x.reshape(bsz * seq, d)
    v_first = None
    for layer in range(depth):
        proj = _norm_matmul(x2d, norm_mix, w_in, layer).reshape(bsz, seq, IN_WIDTH)
        y_swa, y_conv = _swa_conv_mixers(proj, swa_q_norm[layer], swa_k_norm[layer],
                                         swa_sink[layer], swa_bias, conv_w[layer])
        y_dil = _dil_mixer(proj, dil_q_norm[layer], dil_k_norm[layer], dil_bias)
        vres = None if layer == 0 else (vres_v0[layer - 1], vres_v1[layer - 1], vres_v2[layer - 1])
        maps = _rwkv_maps(
            proj, rwkv_mu[layer], decay_w0[layer], decay_w2[layer], iclr_a0[layer], iclr_a2[layer],
            gate_g2[layer], k_k[layer], k_a[layer], r_k[layer], v_first, vres)
        if layer == 0:
            v_first = maps[6]
        y_rwkv = _wkv_scan(*maps[:6], ln_x_g[layer], ln_x_b[layer])
        ys = [y.reshape(bsz * seq, GROUP_WIDTH) for y in (y_conv, y_swa, y_dil, y_rwkv)]
        x2d = _wout(x2d, ys, w_out, layer)
        x2d = _ffn(x2d, norm_ffn, w_up.astype(BF16), w_down.astype(BF16), layer)
    return x2d.reshape(bsz, seq, d)
```

```python
import functools
import math

import jax
import jax.numpy as jnp
from jax import lax
from jax.experimental import pallas as pl
from jax.experimental.pallas import tpu as pltpu

F32 = jnp.float32
BF16 = jnp.bfloat16

D_MODEL = 2048
HEAD_DIM = 64
GROUP_WIDTH = 512
N_HEADS = GROUP_WIDTH // HEAD_DIM
SWA_KV_HEADS = 2
SWA_GROUP = N_HEADS // SWA_KV_HEADS
SWA_WINDOW = 128
DIL_PAIRS = ((128, 1), (512, 4), (2048, 16))
DECAY_LORA = 64
ICLR_LORA = 64
GATE_LORA = 128
RWKV_IN_WIDTH = 3 * GROUP_WIDTH + DECAY_LORA + ICLR_LORA + GATE_LORA
BLK = 128
NUM_BUCKETS = 32
BUCKET_MAX_DIST = 128
RMS_EPS = 1e-6
LN_X_EPS = 64e-5
NEG = -1e30
LOG2E = math.log2(math.e)
WKV_CHUNK = 64
WKV_GROUP = 4
DIL_TILES = 4
PREV_ROWS = 16
NORM_CHUNKS = 4
PREP_UNROLL = 4

OFF_CONV = 0
OFF_SWA_Q = OFF_CONV + 3 * GROUP_WIDTH
OFF_SWA_K = OFF_SWA_Q + GROUP_WIDTH
OFF_SWA_V = OFF_SWA_K + SWA_KV_HEADS * HEAD_DIM
OFF_DIL = OFF_SWA_V + SWA_KV_HEADS * HEAD_DIM
OFF_RW = OFF_DIL + 3 * GROUP_WIDTH
IN_WIDTH = OFF_RW + RWKV_IN_WIDTH

V7X_VMEM_BYTES = 64 * 1024 * 1024
VMEM_LIMIT = 48 * 1024 * 1024
FFN_VMEM_LIMIT = V7X_VMEM_BYTES - 6 * 1024 * 1024


def _params(*sem, vmem_limit=VMEM_LIMIT):
    return pltpu.CompilerParams(dimension_semantics=sem, vmem_limit_bytes=vmem_limit)


def _mm(a, b):
    return jnp.dot(a.astype(BF16), b.astype(BF16), preferred_element_type=F32)


def _mm_nt(a, b):
    return lax.dot_general(a.astype(BF16), b.astype(BF16), (((1,), (1,)), ((), ())),
                           preferred_element_type=F32)


def _mm_tn(a, b):
    return lax.dot_general(a.astype(BF16), b.astype(BF16), (((0,), (0,)), ((), ())),
                           preferred_element_type=F32)


def _split_bf16(x):
    hi = x.astype(BF16)
    lo = (x - hi.astype(F32)).astype(BF16)
    return hi, lo


def _head_sums(x, split=True):
    lanes = 128
    r = lax.broadcasted_iota(jnp.int32, (lanes, lanes), 0) // HEAD_DIM
    c = lax.broadcasted_iota(jnp.int32, (lanes, lanes), 1) // HEAD_DIM
    bd = (r == c).astype(BF16)
    parts = _split_bf16(x) if split else (x.astype(BF16),)
    cols = []
    for j in range(x.shape[-1] // lanes):
        sl = slice(j * lanes, (j + 1) * lanes)
        cols.append(sum(jnp.dot(part[:, sl], bd, preferred_element_type=F32) for part in parts))
    return cols[0] if len(cols) == 1 else jnp.concatenate(cols, axis=-1)


def _sigmoid(z):
    return 1.0 / (1.0 + jnp.exp(-z))


def _rms_scale(x_ref):
    x = x_ref[...]
    return lax.rsqrt(jnp.mean(x * x, axis=-1, keepdims=True) + RMS_EPS)


def _norm_chunks(x_ref, g_ref, h_ref, scale):
    k = x_ref.shape[1]
    kc = k // NORM_CHUNKS
    for c in range(NORM_CHUNKS):
        cols = slice(c * kc, (c + 1) * kc)
        h = (x_ref[:, cols] * scale * g_ref[0, :, cols]).astype(BF16)
        h_ref[:, cols] = h
        yield cols, h


def _norm_matmul_kernel(x_ref, g_ref, w_ref, o_ref, h_ref):
    @pl.when(pl.program_id(1) == 0)
    def _():
        for _ in _norm_chunks(x_ref, g_ref, h_ref, _rms_scale(x_ref)):
            pass

    o_ref[...] = jnp.dot(h_ref[...], w_ref[0].astype(BF16),
                         preferred_element_type=F32).astype(o_ref.dtype)


def _norm_matmul(x2d, gains, w_stack, layer, tm=2048, tn=512):
    m, k = x2d.shape
    n = w_stack.shape[2]
    assert n % tn == 0 and m % tm == 0
    return pl.pallas_call(
        _norm_matmul_kernel,
        grid=(m // tm, n // tn),
        in_specs=[
            pl.BlockSpec((tm, k), lambda i, j: (i, 0)),
            pl.BlockSpec((1, 1, k), lambda i, j: (layer, 0, 0)),
            pl.BlockSpec((1, k, tn), lambda i, j: (layer, 0, j)),
        ],
        out_specs=pl.BlockSpec((tm, tn), lambda i, j: (i, j)),
        out_shape=jax.ShapeDtypeStruct((m, n), BF16),
        scratch_shapes=[pltpu.VMEM((tm, k), BF16)],
        compiler_params=_params("parallel", "arbitrary", vmem_limit=FFN_VMEM_LIMIT),
        name="norm_matmul",
    )(x2d, gains.reshape(-1, 1, k), w_stack)


def _t5_bucket(dist):
    dist = jnp.maximum(dist, 0)
    max_exact = NUM_BUCKETS // 2
    scaled = (jnp.log(jnp.maximum(dist, 1).astype(F32) / max_exact)
              / math.log(BUCKET_MAX_DIST / max_exact))
    large = max_exact + (scaled * (NUM_BUCKETS - max_exact)).astype(jnp.int32)
    large = jnp.minimum(large, NUM_BUCKETS - 1)
    return jnp.where(dist < max_exact, dist, large)


def _bias_kernel(bucket_ref, table_ref, o_ref):
    bucket = bucket_ref[0]

    def head(h, carry):
        g = jnp.zeros(bucket.shape, F32)
        for b in range(NUM_BUCKETS):
            g = jnp.where(bucket == b, table_ref[b, h], g)
        rows = jnp.broadcast_to(g[0:1], (BLK, 2 * BLK))
        o_ref[0, h] = pltpu.roll(rows, 0, axis=1, stride=1, stride_axis=0)
        return carry

    lax.fori_loop(0, o_ref.shape[1], head, 0)


def _bias_tiles(rel_bias):
    lag_dist = BLK - jnp.arange(2 * BLK)
    buckets = jnp.stack([_t5_bucket(lag_dist * r) for _, r in DIL_PAIRS]).astype(jnp.int32)
    buckets = jnp.broadcast_to(buckets[:, None, :], (len(DIL_PAIRS), 8, 2 * BLK))
    nh = rel_bias.shape[1]
    return pl.pallas_call(
        _bias_kernel,
        grid=(len(DIL_PAIRS),),
        in_specs=[
            pl.BlockSpec((1, 8, 2 * BLK), lambda s: (s, 0, 0)),
            pl.BlockSpec(memory_space=pltpu.SMEM),
        ],
        out_specs=pl.BlockSpec((1, nh, BLK, 2 * BLK), lambda s: (s, 0, 0, 0)),
        out_shape=jax.ShapeDtypeStruct((len(DIL_PAIRS), nh, BLK, 2 * BLK), F32),
        compiler_params=_params("arbitrary"),
        name="bias_tiles",
    )(buckets, rel_bias)


def _short_conv(b_ref, c_ref, u_ref, w_ref, o_ref):
    lanes = 128
    for j in range(b_ref.shape[2] // lanes):
        cols = slice(j * lanes, (j + 1) * lanes)
        z = c_ref[0, :, cols].astype(F32) * u_ref[0, :, cols].astype(F32)
        row = lax.broadcasted_iota(jnp.int32, z.shape, 0)
        z1 = jnp.where(row >= 1, pltpu.roll(z, 1, axis=0), 0.0)
        z2 = jnp.where(row >= 2, pltpu.roll(z, 2, axis=0), 0.0)
        w = w_ref[:, cols]
        y = z2 * w[0:1, :] + z1 * w[1:2, :] + z * w[2:3, :]
        o_ref[0, :, cols] = (b_ref[0, :, cols].astype(F32) * y).astype(o_ref.dtype)


def _band_mask(max_dist):
    a = lax.broadcasted_iota(jnp.int32, (BLK, 2 * BLK), 0)
    b = lax.broadcasted_iota(jnp.int32, (BLK, 2 * BLK), 1)
    dist = BLK + a - b
    return (dist >= 0) & (dist <= max_dist), b


def _attend(qs, kws, vws, biases, sinks=None, normalize=True):
    idx = range(len(qs))
    s = [lax.dot_general(qs[i], kws[i], (((1,), (1,)), ((), ())), preferred_element_type=F32)
         + biases[i] for i in idx]
    m = [jnp.max(s[i], axis=-1, keepdims=True) for i in idx]
    if sinks is not None:
        m = [jnp.maximum(m[i], sinks[i]) for i in idx]
    p = [jnp.exp2(s[i] - m[i]) for i in idx]
    den = [jnp.sum(p[i], axis=-1, keepdims=True) for i in idx]
    if sinks is not None:
        den = [den[i] + jnp.exp2(sinks[i] - m[i]) for i in idx]
    o = [jnp.dot(p[i].astype(BF16), vws[i], preferred_element_type=F32) for i in idx]
    if normalize:
        o = [o[i] / den[i] for i in idx]
    return o, m, den


def _head_rms(x, gain):
    ms = _head_sums(x * x, split=False) * (1.0 / HEAD_DIM)
    return x * lax.rsqrt(ms + RMS_EPS) * gain


def _swa_kernel(q_ref, k_ref, v_ref, qg_ref, kg_ref, sink_ref, bias_ref,
                cb_ref, cc_ref, cu_ref, cw_ref, o_ref, oc_ref,
                qn_ref, kn_ref, vb_ref, bm_ref):
    seq = q_ref.shape[1]
    nb = seq // BLK
    kvw = SWA_KV_HEADS * HEAD_DIM
    scale = HEAD_DIM ** -0.5 * LOG2E

    _short_conv(cb_ref, cc_ref, cu_ref, cw_ref, oc_ref)

    kn_ref[0:BLK, :] = jnp.zeros((BLK, 2 * kvw), BF16)
    vb_ref[0:BLK, :] = jnp.zeros((BLK, 2 * kvw), BF16)
    lane_half = lax.broadcasted_iota(jnp.int32, (1, kvw), 1) // HEAD_DIM

    def both_halves(x):
        xr = pltpu.roll(x, HEAD_DIM, axis=1)
        return jnp.concatenate([jnp.where(lane_half == 0, x, xr),
                                jnp.where(lane_half == 0, xr, x)], axis=-1)

    def prep(i, carry):
        r0 = pl.multiple_of(i * BLK, BLK)
        q = q_ref[0, pl.ds(r0, BLK), :].astype(F32)
        qn_ref[pl.ds(r0, BLK), :] = (_head_rms(q, qg_ref[...]) * scale).astype(BF16)
        k = _head_rms(k_ref[0, pl.ds(r0, BLK), :].astype(F32), kg_ref[...])
        kn_ref[pl.ds(r0 + BLK, BLK), :] = both_halves(k).astype(BF16)
        v = v_ref[0, pl.ds(r0, BLK), :].astype(F32)
        vb_ref[pl.ds(r0 + BLK, BLK), :] = both_halves(v).astype(BF16)
        return carry

    lax.fori_loop(0, nb, prep, 0, unroll=PREP_UNROLL)

    band, kcol = _band_mask(SWA_WINDOW - 1)
    for h in range(N_HEADS):
        bm_ref[h] = jnp.where(band & (kcol >= BLK), bias_ref[h] * LOG2E, NEG)
        bm_ref[N_HEADS + h] = jnp.where(band, bias_ref[h] * LOG2E, NEG)
    sinks = [sink_ref[h] * LOG2E for h in range(N_HEADS)]

    def block(i, carry):
        r0 = pl.multiple_of(i * BLK, BLK)
        later = jnp.minimum(i, 1) * N_HEADS
        qs, kws, vws, bms = [], [], [], []
        for hk in range(SWA_KV_HEADS):
            kw = kn_ref[pl.ds(r0, 2 * BLK), hk * kvw:(hk + 1) * kvw]
            vw = vb_ref[pl.ds(r0, 2 * BLK), hk * kvw:(hk + 1) * kvw]
            for g in range(SWA_GROUP):
                h = hk * SWA_GROUP + g
                q2 = qn_ref[pl.ds(r0, BLK), (h // 2) * kvw:(h // 2 + 1) * kvw]
                qs.append(jnp.where(lane_half == h % 2, q2, jnp.zeros_like(q2)))
                kws.append(kw)
                vws.append(vw)
                bms.append(bm_ref[later + h])
        outs, _, _ = _attend(qs, kws, vws, bms, sinks)
        pairs = [jnp.where(lane_half == 0, outs[h], outs[h + 1]) for h in range(0, N_HEADS, 2)]
        o_ref[0, pl.ds(r0, BLK), :] = jnp.concatenate(pairs, axis=-1).astype(o_ref.dtype)
        return carry

    lax.fori_loop(0, nb, block, 0)


def _swa_conv_mixers(proj, q_gain, k_gain, sink, bias, conv_w):
    bsz, seq, _ = proj.shape
    kvw = SWA_KV_HEADS * HEAD_DIM
    q_gain_t = jnp.tile(q_gain, N_HEADS).reshape(1, GROUP_WIDTH)
    k_gain_t = jnp.tile(k_gain, SWA_KV_HEADS).reshape(1, kvw)
    wide = lambda off: pl.BlockSpec((1, seq, GROUP_WIDTH), lambda b: (b, 0, off // GROUP_WIDTH))
    out = jax.ShapeDtypeStruct((bsz, seq, GROUP_WIDTH), BF16)
    return pl.pallas_call(
        _swa_kernel,
        grid=(bsz,),
        in_specs=[
            wide(OFF_SWA_Q),
            pl.BlockSpec((1, seq, kvw), lambda b: (b, 0, OFF_SWA_K // kvw)),
            pl.BlockSpec((1, seq, kvw), lambda b: (b, 0, OFF_SWA_V // kvw)),
            pl.BlockSpec((1, GROUP_WIDTH), lambda b: (0, 0)),
            pl.BlockSpec((1, kvw), lambda b: (0, 0)),
            pl.BlockSpec(memory_space=pltpu.SMEM),
            pl.BlockSpec((N_HEADS, BLK, 2 * BLK), lambda b: (0, 0, 0)),
            wide(OFF_CONV), wide(OFF_CONV + GROUP_WIDTH), wide(OFF_CONV + 2 * GROUP_WIDTH),
            pl.BlockSpec((3, GROUP_WIDTH), lambda b: (0, 0)),
        ],
        out_specs=[pl.BlockSpec((1, seq, GROUP_WIDTH), lambda b: (b, 0, 0))] * 2,
        out_shape=[out, out],
        scratch_shapes=[
            pltpu.VMEM((seq, GROUP_WIDTH), BF16),
            pltpu.VMEM((seq + BLK, 2 * kvw), BF16),
            pltpu.VMEM((seq + BLK, 2 * kvw), BF16),
            pltpu.VMEM((2 * N_HEADS, BLK, 2 * BLK), F32),
        ],
        compiler_params=_params("parallel"),
        name="swa_conv_mixers",
    )(proj, proj, proj, q_gain_t, k_gain_t, sink, bias, proj, proj, proj, conv_w)


def _dil_kernel(q_ref, k_ref, v_ref, qg_ref, kg_ref, bias_ref, o_ref,
                qn_ref, kn_ref, vn_ref, ob_ref, mb_ref, db_ref, bm_ref):
    seq = q_ref.shape[1]
    lanes = q_ref.shape[2]
    heads = lanes // HEAD_DIM
    scale = HEAD_DIM ** -0.5 * LOG2E

    def prep(i, carry):
        r0 = pl.multiple_of(i * BLK, BLK)
        q = q_ref[0, pl.ds(r0, BLK), :].astype(F32)
        k = k_ref[0, pl.ds(r0, BLK), :].astype(F32)
        qn_ref[pl.ds(r0, BLK), :] = _head_rms(q, qg_ref[...]) * scale
        kn_ref[pl.ds(r0, BLK), :] = _head_rms(k, kg_ref[...])
        vn_ref[pl.ds(r0, BLK), :] = v_ref[0, pl.ds(r0, BLK), :].astype(F32)
        return carry

    lax.fori_loop(0, seq // BLK, prep, 0, unroll=PREP_UNROLL)

    for br, (window, r) in enumerate(DIL_PAIRS):
        band, kcol = _band_mask(window // r)
        for h in range(heads):
            bias = bias_ref[br, h] * LOG2E
            bm_ref[(br * heads + h) * 2] = jnp.where(band & (kcol >= BLK), bias, NEG)
            bm_ref[(br * heads + h) * 2 + 1] = jnp.where(band, bias, NEG)

    lane_head = lax.broadcasted_iota(jnp.int32, (1, lanes), 1) // HEAD_DIM

    for br, (window, r) in enumerate(DIL_PAIRS):
        nb = seq // r // BLK

        def blocks(it, carry, br=br, r=r, nb=nb):
            qs, kws, vws, bms, curs = [], [], [], [], []
            for u in range(DIL_TILES):
                t = it * DIL_TILES + u
                c = t // nb
                i = t - c * nb
                cur = c + i * (BLK * r)
                prev = jnp.maximum(cur - BLK * r, c)
                later = jnp.minimum(i, 1)

                def rows(ref, start):
                    if r == 1:
                        return ref[pl.ds(start, BLK), :]
                    return ref[pl.ds(start, BLK, stride=r), :]

                q = rows(qn_ref, cur).astype(BF16)
                if nb == 1:
                    kw = rows(kn_ref, cur).astype(BF16)
                    vw = rows(vn_ref, cur).astype(BF16)
                else:
                    kw = jnp.concatenate([rows(kn_ref, prev), rows(kn_ref, cur)],
                                         axis=0).astype(BF16)
                    vw = jnp.concatenate([rows(vn_ref, prev), rows(vn_ref, cur)],
                                         axis=0).astype(BF16)
                curs.append(cur)
                for h in range(heads):
                    qs.append(jnp.where(lane_head == h, q, jnp.zeros_like(q)))
                    kws.append(kw)
                    vws.append(vw)
                    if nb == 1:
                        bms.append(bm_ref[(br * heads + h) * 2, :, BLK:])
                    else:
                        bms.append(bm_ref[(br * heads + h) * 2 + later])
            outs, ms, dens = _attend(qs, kws, vws, bms, normalize=False)
            for u in range(DIL_TILES):
                o_all, m_all, d_all = outs[u * heads], ms[u * heads], dens[u * heads]
                for h in range(1, heads):
                    o_all = jnp.where(lane_head == h, outs[u * heads + h], o_all)
                    m_all = jnp.where(lane_head == h, ms[u * heads + h], m_all)
                    d_all = jnp.where(lane_head == h, dens[u * heads + h], d_all)
                if r == 1:
                    rows = pl.ds(curs[u], BLK)
                else:
                    rows = pl.ds(curs[u], BLK, stride=r)
                ob_ref[br, rows, :] = o_all
                mb_ref[br, rows, :] = m_all
                db_ref[br, rows, :] = d_all
            return carry

        lax.fori_loop(0, r * nb // DIL_TILES, blocks, 0)

    def combine(i, carry):
        r0 = pl.multiple_of(i * BLK, BLK)
        rows = pl.ds(r0, BLK)
        m0, m1, m2 = mb_ref[0, rows, :], mb_ref[1, rows, :], mb_ref[2, rows, :]
        m = jnp.maximum(jnp.maximum(m0, m1), m2)
        e0, e1, e2 = jnp.exp2(m0 - m), jnp.exp2(m1 - m), jnp.exp2(m2 - m)
        num = e0 * ob_ref[0, rows, :] + e1 * ob_ref[1, rows, :] + e2 * ob_ref[2, rows, :]
        den = e0 * db_ref[0, rows, :] + e1 * db_ref[1, rows, :] + e2 * db_ref[2, rows, :]
        o_ref[0, rows, :] = (num / den).astype(o_ref.dtype)
        return carry

    lax.fori_loop(0, seq // BLK, combine, 0, unroll=PREP_UNROLL)


def _dil_mixer(proj, q_gain, k_gain, bias):
    bsz, seq, _ = proj.shape
    lanes = 128
    heads = lanes // HEAD_DIM
    nblk = GROUP_WIDTH // lanes
    base = OFF_DIL // lanes
    gq = jnp.tile(q_gain, heads).reshape(1, lanes)
    gk = jnp.tile(k_gain, heads).reshape(1, lanes)

    def col(seg):
        return pl.BlockSpec((1, seq, lanes), lambda b, j: (b, 0, base + seg * nblk + j))

    return pl.pallas_call(
        _dil_kernel,
        grid=(bsz, nblk),
        in_specs=[
            col(0), col(1), col(2),
            pl.BlockSpec((1, lanes), lambda b, j: (0, 0)),
            pl.BlockSpec((1, lanes), lambda b, j: (0, 0)),
            pl.BlockSpec((len(DIL_PAIRS), heads, BLK, 2 * BLK), lambda b, j: (0, j, 0, 0)),
        ],
        out_specs=pl.BlockSpec((1, seq, lanes), lambda b, j: (b, 0, j)),
        out_shape=jax.ShapeDtypeStruct((bsz, seq, GROUP_WIDTH), BF16),
        scratch_shapes=[
            pltpu.VMEM((seq, lanes), F32),
            pltpu.VMEM((seq, lanes), F32),
            pltpu.VMEM((seq, lanes), F32),
            pltpu.VMEM((len(DIL_PAIRS), seq, lanes), F32),
            pltpu.VMEM((len(DIL_PAIRS), seq, lanes), F32),
            pltpu.VMEM((len(DIL_PAIRS), seq, lanes), F32),
            pltpu.VMEM((len(DIL_PAIRS) * heads * 2, BLK, 2 * BLK), F32),
        ],
        compiler_params=_params("parallel", "parallel"),
        name="dil_mixer",
    )(proj, proj, proj, gq, gk, bias)


def _rwkv_maps_kernel(*refs, has_vres):
    if has_vres:
        (p_ref, pp_ref, mu_ref, w0_ref, w2_ref, a0_ref, a2_ref, g2_ref, kk_ref, ka_ref, rk_ref,
         vf_ref, v0_ref, v1_ref, v2_ref,
         q_out, y1_out, m_out, gm_out, bonus_out, gate_out, *scratch) = refs
    else:
        (p_ref, pp_ref, mu_ref, w0_ref, w2_ref, a0_ref, a2_ref, g2_ref, kk_ref, ka_ref, rk_ref,
         q_out, y1_out, m_out, gm_out, bonus_out, gate_out, v_out, *scratch) = refs
    w = GROUP_WIDTH
    p = p_ref[0].astype(F32)
    row = lax.broadcasted_iota(jnp.int32, p.shape, 0)
    last_prev = pp_ref[0, PREV_ROWS - 1:PREV_ROWS, :].astype(F32)
    last_prev = jnp.where(pl.program_id(1) > 0, last_prev, 0.0)
    prev = jnp.where(row >= 1, pltpu.roll(p, 1, axis=0), last_prev)
    xs = p + (prev - p) * mu_ref[...]
    grows = WKV_GROUP * WKV_CHUNK
    for grp in range(p.shape[0] // grows):
        rs = slice(grp * grows, (grp + 1) * grows)
        x = xs[rs]
        r = x[:, 0:w]
        k = x[:, w:2 * w]
        v = x[:, 2 * w:3 * w]
        o = 3 * w
        wd = x[:, o:o + DECAY_LORA]
        ad = x[:, o + DECAY_LORA:o + DECAY_LORA + ICLR_LORA]
        gd = x[:, o + DECAY_LORA + ICLR_LORA:]

        z = -(w0_ref[...] + _mm(jnp.tanh(wd), w2_ref[...]))
        softplus = jnp.maximum(z, 0.0) + jnp.log(1.0 + jnp.exp(-jnp.abs(z)))
        logw = -softplus - 0.5
        lw = -jnp.exp(logw)
        a = _sigmoid(a0_ref[...] + _mm(ad, a2_ref[...]))
        gate_out[0, rs] = _mm(_sigmoid(gd), g2_ref[...])
        if has_vres:
            mix = _sigmoid(v0_ref[...] + _mm(_mm(v, v1_ref[...]), v2_ref[...]))
            v = v + (vf_ref[0, rs] - v) * mix
        else:
            v_out[0, rs] = v
        kk = k * kk_ref[...]
        ss = _head_sums(kk * kk, split=False)
        kk = kk * lax.rsqrt(jnp.maximum(ss, 1e-24))
        k = k * (1.0 + (a - 1.0) * ka_ref[...])
        bonus_out[0, rs] = _head_sums(r * k * rk_ref[...]) * v
        _chunk_maps(grp, r, lw, k, v, kk, kk * a, q_out.at[0], y1_out.at[0], m_out.at[0],
                    gm_out.at[0], *scratch)


def _rwkv_maps(proj, mu, w0, w2, a0, a2, g2, k_k, k_a, r_k, v_first, vres, ts=512):
    bsz, seq, _ = proj.shape
    w = GROUP_WIDTH
    nt = seq // ts
    has_vres = vres is not None

    def full(shape):
        return pl.BlockSpec(shape, lambda b, i: (0,) * len(shape))

    row = lambda a: a.reshape(1, -1)
    tile = pl.BlockSpec((1, ts, w), lambda b, i: (b, i, 0))
    in_specs = [
        pl.BlockSpec((pl.Element(1), pl.Element(ts), pl.Element(RWKV_IN_WIDTH)),
                     lambda b, i: (b, i * ts, OFF_RW)),
        pl.BlockSpec((pl.Element(1), pl.Element(PREV_ROWS), pl.Element(RWKV_IN_WIDTH)),
                     lambda b, i: (b, jnp.maximum(i * (ts // PREV_ROWS) - 1, 0) * PREV_ROWS, OFF_RW)),
        full((1, RWKV_IN_WIDTH)), full((1, w)), full((DECAY_LORA, w)), full((1, w)),
        full((ICLR_LORA, w)), full((GATE_LORA, w)), full((1, w)), full((1, w)), full((1, w)),
    ]
    args = [proj, proj, row(mu), row(w0), w2, row(a0), a2, g2, row(k_k), row(k_a), row(r_k)]
    if has_vres:
        v0, v1, v2 = vres
        in_specs += [tile, full((1, w)), full(v1.shape), full(v2.shape)]
        args += [v_first, row(v0), v1, v2]
    f32 = jax.ShapeDtypeStruct((bsz, seq, w), F32)
    bf16 = jax.ShapeDtypeStruct((bsz, seq, w), BF16)
    out_shape = [bf16, f32, bf16, f32, f32, f32] + ([] if has_vres else [f32])
    return pl.pallas_call(
        functools.partial(_rwkv_maps_kernel, has_vres=has_vres),
        grid=(bsz, nt),
        in_specs=in_specs,
        out_specs=[tile] * len(out_shape),
        out_shape=out_shape,
        scratch_shapes=[pltpu.VMEM((ts, w), dt) for dt in (BF16, F32, BF16, BF16, F32, BF16, BF16)],
        compiler_params=_params("parallel", "parallel"),
        name="rwkv_chunk_maps",
    )(*args)


def _chunk_maps(grp, r, lw, k, v, kk, b, q_out, y1_out, m_out, g_out,
                at_ref, rt_ref, bt_ref, kt_ref, dec_ref, vb_ref, rb_ref):
    c = WKV_CHUNK
    n = HEAD_DIM
    rows = r.shape[0]
    rs = slice(grp * rows, (grp + 1) * rows)

    row = lax.broadcasted_iota(jnp.int32, (rows, rows), 0)
    col = lax.broadcasted_iota(jnp.int32, (rows, rows), 1)
    tri = ((row >= col) & ((row // c) == (col // c))).astype(BF16)
    lw_hi, lw_lo = _split_bf16(lw)
    cum = (jnp.dot(tri, lw_hi, preferred_element_type=F32)
           + jnp.dot(tri, lw_lo, preferred_element_type=F32))
    e_pos = jnp.exp(cum)
    e_neg = jnp.exp(-cum)
    r_t = r * e_pos
    at_ref[rs] = (-kk * jnp.exp(cum - lw)).astype(BF16)
    rt_ref[rs] = r_t
    rb_ref[rs] = r_t.astype(BF16)
    bt_ref[rs] = (b * e_neg).astype(BF16)
    kt_ref[rs] = (k * e_neg).astype(BF16)
    dec_ref[rs] = e_pos
    vb_ref[rs] = v.astype(BF16)

    crow = lax.broadcasted_iota(jnp.int32, (c, c), 0)
    ccol = lax.broadcasted_iota(jnp.int32, (c, c), 1)
    lower = crow >= ccol
    strict = crow > ccol
    eye = (crow == ccol).astype(F32)
    wrow = lax.broadcasted_iota(jnp.int32, (c, 2 * c), 0)
    wcol = lax.broadcasted_iota(jnp.int32, (c, 2 * c), 1)
    strict_left = wrow > wcol
    right = wcol >= c
    eye_right = (wcol == wrow + c).astype(F32)

    def tile(ref, j, h):
        return ref[j * c:(j + 1) * c, h * n:(h + 1) * n]

    _chunk_group(range(grp * WKV_GROUP, (grp + 1) * WKV_GROUP), tile, at_ref, rt_ref, bt_ref,
                 kt_ref, dec_ref, vb_ref, rb_ref, q_out, y1_out, m_out, g_out,
                 lower, strict, eye, strict_left, right, eye_right)


def _chunk_group(chunks, tile, at_ref, rt_ref, bt_ref, kt_ref, dec_ref, vb_ref, rb_ref,
                 q_out, y1_out, m_out, g_out, lower, strict, eye, strict_left, right, eye_right):
    c = WKV_CHUNK
    n = HEAD_DIM
    pairs = [(j, h) for j in chunks for h in range(N_HEADS)]
    ah = [tile(at_ref, j, h) for j, h in pairs]
    rh = [tile(rt_ref, j, h) for j, h in pairs]
    bh = [tile(bt_ref, j, h) for j, h in pairs]
    kh = [tile(kt_ref, j, h) for j, h in pairs]
    vh = [tile(vb_ref, j, h) for j, h in pairs]
    dh = [dec_ref[(j + 1) * c - 1:(j + 1) * c, h * n:(h + 1) * n] for j, h in pairs]
    idx = range(len(pairs))
    ar = [jnp.concatenate([ah[i], tile(rb_ref, *pairs[i])], axis=0) for i in idx]
    bk = [jnp.concatenate([bh[i], kh[i]], axis=0) for i in idx]
    aa = [_mm_nt(ar[i], bk[i]) for i in idx]
    a_kr = [jnp.concatenate([jnp.where(strict, aa[i][:c, c:], 0.0),
                             jnp.where(lower, aa[i][c:, c:], 0.0)], axis=0).astype(BF16)
            for i in idx]
    a_rb = [jnp.where(lower, aa[i][c:, :c], 0.0).astype(BF16) for i in idx]
    ps = [jnp.where(strict_left, aa[i][:c, :], 0.0) + eye_right for i in idx]
    for _ in range(int(math.log2(c))):
        ps = [_mm(ps[i][:, :c], ps[i]) + jnp.where(right, ps[i], 0.0) for i in idx]
    t = [ps[i][:, c:].astype(BF16) for i in idx]
    av = [_mm(a_kr[i], vh[i]) for i in idx]
    wu = [_mm(t[i], jnp.concatenate([ah[i], av[i][:c].astype(BF16)], axis=1)).astype(BF16)
          for i in idx]
    aw = [_mm(a_rb[i], wu[i]) for i in idx]
    q = [rh[i] + aw[i][:, :n] for i in idx]
    y1 = [aw[i][:, n:] + av[i][c:] for i in idx]
    zero = jnp.zeros((c, n), BF16)
    mg = [_mm_tn(jnp.concatenate([wu[i], jnp.concatenate([zero, vh[i]], axis=1)], axis=0), bk[i])
          for i in idx]
    m = [(eye + mg[i][:n]) * dh[i] for i in idx]
    g = [mg[i][n:] * dh[i] for i in idx]
    for j in chunks:
        sel = slice((j - chunks[0]) * N_HEADS, (j - chunks[0] + 1) * N_HEADS)
        q_out[j * c:(j + 1) * c, :] = jnp.concatenate(q[sel], axis=-1).astype(q_out.dtype)
        y1_out[j * c:(j + 1) * c, :] = jnp.concatenate(y1[sel], axis=-1)
        m_out[j * c:(j + 1) * c, :] = jnp.concatenate(m[sel], axis=-1).astype(m_out.dtype)
        g_out[j * c:(j + 1) * c, :] = jnp.concatenate(g[sel], axis=-1)


def _wkv_scan_kernel(q_ref, y1_ref, m_ref, gm_ref, bonus_ref, gate_ref, lng_ref, lnb_ref,
                     o_ref, st_ref):
    n = HEAD_DIM
    bsz = q_ref.shape[0]

    @pl.when(pl.program_id(0) == 0)
    def _():
        st_ref[...] = jnp.zeros(st_ref.shape, F32)

    c = WKV_CHUNK
    rows = q_ref.shape[1]
    chains = [(bi, h) for bi in range(bsz) for h in range(N_HEADS)]
    state = [st_ref[bi * N_HEADS + h] for bi, h in chains]
    entering = []
    for j in range(rows // c):
        rs = slice(j * c, (j + 1) * c)
        s_in = [s.astype(BF16) for s in state]
        entering.append(s_in)
        state = [_mm(s_in[i], m_ref[bi, rs, h * n:(h + 1) * n]) + gm_ref[bi, rs, h * n:(h + 1) * n]
                 for i, (bi, h) in enumerate(chains)]
    for i, (bi, h) in enumerate(chains):
        st_ref[bi * N_HEADS + h] = state[i]
    ys = [[_mm_nt(q_ref[bi, j * c:(j + 1) * c, h * n:(h + 1) * n], entering[j][i])
           for i, (bi, h) in enumerate(chains)] for j in range(rows // c)]
    y = jnp.concatenate(
        [jnp.concatenate([jnp.concatenate(ys[j][bi * N_HEADS:(bi + 1) * N_HEADS], axis=-1)
                          for j in range(rows // c)], axis=0) + y1_ref[bi]
         for bi in range(bsz)], axis=0)
    mean = _head_sums(y) * (1.0 / n)
    yc = y - mean
    var = _head_sums(yc * yc) * (1.0 / n)
    yn = yc * lax.rsqrt(var + LN_X_EPS) * lng_ref[...] + lnb_ref[...]
    for bi in range(bsz):
        o_ref[bi] = ((yn[bi * rows:(bi + 1) * rows] + bonus_ref[bi]) * gate_ref[bi]).astype(o_ref.dtype)


def _wkv_scan(q, y1, m, gm, bonus, gate, ln_g, ln_b, rows=2 * WKV_CHUNK):
    bsz, seq, w = y1.shape
    tile = pl.BlockSpec((bsz, rows, w), lambda ci: (0, ci, 0))
    vec = pl.BlockSpec((1, w), lambda ci: (0, 0))
    return pl.pallas_call(
        _wkv_scan_kernel,
        grid=(seq // rows,),
        in_specs=[tile] * 6 + [vec] * 2,
        out_specs=tile,
        out_shape=jax.ShapeDtypeStruct((bsz, seq, w), BF16),
        scratch_shapes=[pltpu.VMEM((bsz * N_HEADS, HEAD_DIM, HEAD_DIM), F32)],
        compiler_params=_params("arbitrary"),
        name="wkv_state_scan",
    )(q, y1, m, gm, bonus, gate, ln_g.reshape(1, w), ln_b.reshape(1, w))


def _wout_kernel(x_ref, y0_ref, y1_ref, y2_ref, y3_ref, w_ref, o_ref):
    acc = x_ref[...]
    for idx, y_ref in enumerate((y0_ref, y1_ref, y2_ref, y3_ref)):
        w = w_ref[0, idx * GROUP_WIDTH:(idx + 1) * GROUP_WIDTH, :].astype(BF16)
        acc = acc + jnp.dot(y_ref[...], w, preferred_element_type=F32)
    o_ref[...] = acc


def _wout(x2d, ys, w_stack, layer, tm=512):
    m, d = x2d.shape
    ytile = pl.BlockSpec((tm, GROUP_WIDTH), lambda i: (i, 0))
    xtile = pl.BlockSpec((tm, d), lambda i: (i, 0))
    wspec = pl.BlockSpec((1,) + w_stack.shape[1:], lambda i: (layer, 0, 0))
    return pl.pallas_call(
        _wout_kernel,
        grid=(m // tm,),
        in_specs=[xtile] + [ytile] * 4 + [wspec],
        out_specs=xtile,
        out_shape=jax.ShapeDtypeStruct((m, d), F32),
        compiler_params=_params("parallel", vmem_limit=FFN_VMEM_LIMIT),
        name="wout_residual",
    )(x2d, *ys, w_stack)


def _ffn_kernel(x_ref, g_ref, wu_ref, wd_ref, o_ref, h_ref):
    def down(u):
        act = jnp.square(jnp.maximum(u, 0.0)).astype(BF16)
        return jnp.dot(act, wd_ref[0].astype(BF16), preferred_element_type=F32)

    @pl.when(pl.program_id(1) == 0)
    def _():
        u = None
        for cols, h in _norm_chunks(x_ref, g_ref, h_ref, _rms_scale(x_ref)):
            part = jnp.dot(h, wu_ref[0, cols, :].astype(BF16), preferred_element_type=F32)
            u = part if u is None else u + part
        o_ref[...] = x_ref[...] + down(u)

    @pl.when(pl.program_id(1) > 0)
    def _():
        u = jnp.dot(h_ref[...], wu_ref[0].astype(BF16), preferred_element_type=F32)
        o_ref[...] += down(u)


def _ffn(x2d, gains, wu_stack, wd_stack, layer, tm=1024, tf=512):
    m, d = x2d.shape
    f = wu_stack.shape[2]
    xtile = pl.BlockSpec((tm, d), lambda i, j: (i, 0))
    return pl.pallas_call(
        _ffn_kernel,
        grid=(m // tm, f // tf),
        in_specs=[
            xtile,
            pl.BlockSpec((1, 1, d), lambda i, j: (layer, 0, 0)),
            pl.BlockSpec((1, d, tf), lambda i, j: (layer, 0, j)),
            pl.BlockSpec((1, tf, d), lambda i, j: (layer, j, 0)),
        ],
        out_specs=xtile,
        out_shape=jax.ShapeDtypeStruct((m, d), F32),
        scratch_shapes=[pltpu.VMEM((tm, d), BF16)],
        compiler_params=_params("parallel", "arbitrary", vmem_limit=FFN_VMEM_LIMIT),
        name="ffn",
    )(x2d, gains.reshape(-1, 1, d), wu_stack, wd_stack)


def kernel(x, norm_mix, w_in, conv_w, swa_q_norm, swa_k_norm, swa_sink, dil_q_norm, dil_k_norm,
           rwkv_mu, decay_w0, decay_w2, iclr_a0, iclr_a2, gate_g2, k_k, k_a, r_k, ln_x_g, ln_x_b,
           vres_v0, vres_v1, vres_v2, w_out, norm_ffn, w_up, w_down, rel_bias):
    bsz, seq, d = x.shape
    depth = w_in.shape[0]
    bias = _bias_tiles(rel_bias)
    swa_bias = bias[0, :N_HEADS]
    dil_bias = bias[:, N_HEADS:]
    x2d = x.reshape(bsz * seq, d)
    v_first = None
    for layer in range(depth):
        proj = _norm_matmul(x2d, norm_mix, w_in, layer).reshape(bsz, seq, IN_WIDTH)
        y_swa, y_conv = _swa_conv_mixers(proj, swa_q_norm[layer], swa_k_norm[layer],
                                         swa_sink[layer], swa_bias, conv_w[layer])
        y_dil = _dil_mixer(proj, dil_q_norm[layer], dil_k_norm[layer], dil_bias)
        vres = None if layer == 0 else (vres_v0[layer - 1], vres_v1[layer - 1], vres_v2[layer - 1])
        maps = _rwkv_maps(
            proj, rwkv_mu[layer], decay_w0[layer], decay_w2[layer], iclr_a0[layer], iclr_a2[layer],
            gate_g2[layer], k_k[layer], k_a[layer], r_k[layer], v_first, vres)
        if layer == 0:
            v_first = maps[6]
        y_rwkv = _wkv_scan(*maps[:6], ln_x_g[layer], ln_x_b[layer])
        ys = [y.reshape(bsz * seq, GROUP_WIDTH) for y in (y_conv, y_swa, y_dil, y_rwkv)]
        x2d = _wout(x2d, ys, w_out, layer)
        x2d = _ffn(x2d, norm_ffn, w_up, w_down, layer)
    return x2d.reshape(bsz, seq, d)
```

```python
import functools
import math

import jax
import jax.numpy as jnp
from jax import lax
from jax.experimental import pallas as pl
from jax.experimental.pallas import tpu as pltpu

F32 = jnp.float32
BF16 = jnp.bfloat16

HEAD_DIM = 64
GROUP_WIDTH = 512
N_HEADS = GROUP_WIDTH // HEAD_DIM
SWA_KV_HEADS = 2
SWA_GROUP = N_HEADS // SWA_KV_HEADS
SWA_WINDOW = 128
DIL_PAIRS = ((128, 1), (512, 4), (2048, 16))
DECAY_LORA = 64
ICLR_LORA = 64
GATE_LORA = 128
RWKV_IN_WIDTH = 3 * GROUP_WIDTH + DECAY_LORA + ICLR_LORA + GATE_LORA
BLK = 128
NUM_BUCKETS = 32
BUCKET_MAX_DIST = 128
RMS_EPS = 1e-6
LN_X_EPS = 64e-5
NEG = -1e30
LOG2E = math.log2(math.e)
WKV_CHUNK = 64
WKV_GROUP = 4
DIL_TILES = 4
PREV_ROWS = 16
NORM_CHUNKS = 4
PREP_UNROLL = 4

OFF_CONV = 0
OFF_SWA_Q = OFF_CONV + 3 * GROUP_WIDTH
OFF_SWA_K = OFF_SWA_Q + GROUP_WIDTH
OFF_SWA_V = OFF_SWA_K + SWA_KV_HEADS * HEAD_DIM
OFF_DIL = OFF_SWA_V + SWA_KV_HEADS * HEAD_DIM
OFF_RW = OFF_DIL + 3 * GROUP_WIDTH
IN_WIDTH = OFF_RW + RWKV_IN_WIDTH

V7X_VMEM_BYTES = 64 * 1024 * 1024
VMEM_LIMIT = 48 * 1024 * 1024
FFN_VMEM_LIMIT = V7X_VMEM_BYTES - 6 * 1024 * 1024


def _params(*sem, vmem_limit=VMEM_LIMIT):
    return pltpu.CompilerParams(dimension_semantics=sem, vmem_limit_bytes=vmem_limit)


def _mm(a, b):
    return jnp.dot(a.astype(BF16), b.astype(BF16), preferred_element_type=F32)


def _mm_nt(a, b):
    return lax.dot_general(a.astype(BF16), b.astype(BF16), (((1,), (1,)), ((), ())),
                           preferred_element_type=F32)


def _mm_tn(a, b):
    return lax.dot_general(a.astype(BF16), b.astype(BF16), (((0,), (0,)), ((), ())),
                           preferred_element_type=F32)


def _split_bf16(x):
    hi = x.astype(BF16)
    lo = (x - hi.astype(F32)).astype(BF16)
    return hi, lo


def _head_sums(x, split=True):
    lanes = 128
    r = lax.broadcasted_iota(jnp.int32, (lanes, lanes), 0) // HEAD_DIM
    c = lax.broadcasted_iota(jnp.int32, (lanes, lanes), 1) // HEAD_DIM
    bd = (r == c).astype(BF16)
    parts = _split_bf16(x) if split else (x.astype(BF16),)
    cols = []
    for j in range(x.shape[-1] // lanes):
        sl = slice(j * lanes, (j + 1) * lanes)
        cols.append(sum(jnp.dot(part[:, sl], bd, preferred_element_type=F32) for part in parts))
    return cols[0] if len(cols) == 1 else jnp.concatenate(cols, axis=-1)


def _sigmoid(z):
    return 1.0 / (1.0 + jnp.exp(-z))


def _rms_scale(x_ref):
    x = x_ref[...]
    return lax.rsqrt(jnp.mean(x * x, axis=-1, keepdims=True) + RMS_EPS)


def _norm_chunks(x_ref, g_ref, h_ref, scale):
    k = x_ref.shape[1]
    kc = k // NORM_CHUNKS
    for c in range(NORM_CHUNKS):
        cols = slice(c * kc, (c + 1) * kc)
        h = (x_ref[:, cols] * scale * g_ref[0, :, cols]).astype(BF16)
        h_ref[:, cols] = h
        yield cols, h


def _norm_matmul_kernel(x_ref, g_ref, w_ref, o_ref, h_ref):
    @pl.when(pl.program_id(1) == 0)
    def _():
        for _ in _norm_chunks(x_ref, g_ref, h_ref, _rms_scale(x_ref)):
            pass

    o_ref[...] = jnp.dot(h_ref[...], w_ref[0].astype(BF16),
                         preferred_element_type=F32).astype(o_ref.dtype)


def _norm_matmul(x2d, gains, w_stack, layer, tm=2048, tn=512):
    m, k = x2d.shape
    n = w_stack.shape[2]
    assert n % tn == 0 and m % tm == 0
    return pl.pallas_call(
        _norm_matmul_kernel,
        grid=(m // tm, n // tn),
        in_specs=[
            pl.BlockSpec((tm, k), lambda i, j: (i, 0)),
            pl.BlockSpec((1, 1, k), lambda i, j: (layer, 0, 0)),
            pl.BlockSpec((1, k, tn), lambda i, j: (layer, 0, j)),
        ],
        out_specs=pl.BlockSpec((tm, tn), lambda i, j: (i, j)),
        out_shape=jax.ShapeDtypeStruct((m, n), BF16),
        scratch_shapes=[pltpu.VMEM((tm, k), BF16)],
        compiler_params=_params("parallel", "arbitrary", vmem_limit=FFN_VMEM_LIMIT),
        name="norm_matmul",
    )(x2d, gains.reshape(-1, 1, k), w_stack)


def _t5_bucket(dist):
    dist = jnp.maximum(dist, 0)
    max_exact = NUM_BUCKETS // 2
    scaled = (jnp.log(jnp.maximum(dist, 1).astype(F32) / max_exact)
              / math.log(BUCKET_MAX_DIST / max_exact))
    large = max_exact + (scaled * (NUM_BUCKETS - max_exact)).astype(jnp.int32)
    large = jnp.minimum(large, NUM_BUCKETS - 1)
    return jnp.where(dist < max_exact, dist, large)


def _bias_kernel(bucket_ref, table_ref, o_ref):
    bucket = bucket_ref[0]

    def head(h, carry):
        g = jnp.zeros(bucket.shape, F32)
        for b in range(NUM_BUCKETS):
            g = jnp.where(bucket == b, table_ref[b, h], g)
        rows = jnp.broadcast_to(g[0:1], (BLK, 2 * BLK))
        o_ref[0, h] = pltpu.roll(rows, 0, axis=1, stride=1, stride_axis=0)
        return carry

    lax.fori_loop(0, o_ref.shape[1], head, 0)


def _bias_tiles(rel_bias):
    lag_dist = BLK - jnp.arange(2 * BLK)
    buckets = jnp.stack([_t5_bucket(lag_dist * r) for _, r in DIL_PAIRS]).astype(jnp.int32)
    buckets = jnp.broadcast_to(buckets[:, None, :], (len(DIL_PAIRS), 8, 2 * BLK))
    nh = rel_bias.shape[1]
    return pl.pallas_call(
        _bias_kernel,
        grid=(len(DIL_PAIRS),),
        in_specs=[
            pl.BlockSpec((1, 8, 2 * BLK), lambda s: (s, 0, 0)),
            pl.BlockSpec(memory_space=pltpu.SMEM),
        ],
        out_specs=pl.BlockSpec((1, nh, BLK, 2 * BLK), lambda s: (s, 0, 0, 0)),
        out_shape=jax.ShapeDtypeStruct((len(DIL_PAIRS), nh, BLK, 2 * BLK), F32),
        compiler_params=_params("arbitrary"),
        name="bias_tiles",
    )(buckets, rel_bias)


def _short_conv(b_ref, c_ref, u_ref, w_ref, o_ref):
    lanes = 128
    for j in range(b_ref.shape[2] // lanes):
        cols = slice(j * lanes, (j + 1) * lanes)
        z = c_ref[0, :, cols].astype(F32) * u_ref[0, :, cols].astype(F32)
        row = lax.broadcasted_iota(jnp.int32, z.shape, 0)
        z1 = jnp.where(row >= 1, pltpu.roll(z, 1, axis=0), 0.0)
        z2 = jnp.where(row >= 2, pltpu.roll(z, 2, axis=0), 0.0)
        w = w_ref[:, cols]
        y = z2 * w[0:1, :] + z1 * w[1:2, :] + z * w[2:3, :]
        o_ref[0, :, cols] = (b_ref[0, :, cols].astype(F32) * y).astype(o_ref.dtype)


def _band_mask(max_dist):
    a = lax.broadcasted_iota(jnp.int32, (BLK, 2 * BLK), 0)
    b = lax.broadcasted_iota(jnp.int32, (BLK, 2 * BLK), 1)
    dist = BLK + a - b
    return (dist >= 0) & (dist <= max_dist), b


def _attend(qs, kws, vws, biases, normalize=True):
    idx = range(len(qs))
    s = [lax.dot_general(qs[i], kws[i], (((1,), (1,)), ((), ())), preferred_element_type=F32)
         + biases[i] for i in idx]
    m = [jnp.max(s[i], axis=-1, keepdims=True) for i in idx]
    p = [jnp.exp2(s[i] - m[i]) for i in idx]
    den = [jnp.sum(p[i], axis=-1, keepdims=True) for i in idx]
    o = [jnp.dot(p[i].astype(BF16), vws[i], preferred_element_type=F32) for i in idx]
    if normalize:
        o = [o[i] / den[i] for i in idx]
    return o, m, den


def _head_rms(x, gain):
    ms = _head_sums(x * x, split=False) * (1.0 / HEAD_DIM)
    return x * lax.rsqrt(ms + RMS_EPS) * gain


def _swa_kernel(q_ref, k_ref, v_ref, qg_ref, kg_ref, sink_ref, bias_ref,
                cb_ref, cc_ref, cu_ref, cw_ref, o_ref, oc_ref,
                qn_ref, kn_ref, vb_ref, bm_ref):
    seq = q_ref.shape[1]
    nb = seq // BLK
    kvw = SWA_KV_HEADS * HEAD_DIM
    scale = HEAD_DIM ** -0.5 * LOG2E

    _short_conv(cb_ref, cc_ref, cu_ref, cw_ref, oc_ref)

    kn_ref[0:BLK, :] = jnp.zeros((BLK, 2 * kvw), BF16)
    vb_ref[0:BLK, :] = jnp.zeros((BLK, 2 * kvw), BF16)
    lane_half = lax.broadcasted_iota(jnp.int32, (1, kvw), 1) // HEAD_DIM

    def both_halves(x):
        xr = pltpu.roll(x, HEAD_DIM, axis=1)
        return jnp.concatenate([jnp.where(lane_half == 0, x, xr),
                                jnp.where(lane_half == 0, xr, x)], axis=-1)

    def prep(i, carry):
        r0 = pl.multiple_of(i * BLK, BLK)
        q = q_ref[0, pl.ds(r0, BLK), :].astype(F32)
        qn_ref[pl.ds(r0, BLK), :] = (_head_rms(q, qg_ref[...]) * scale).astype(BF16)
        k = _head_rms(k_ref[0, pl.ds(r0, BLK), :].astype(F32), kg_ref[...])
        kn_ref[pl.ds(r0 + BLK, BLK), :] = both_halves(k).astype(BF16)
        v = v_ref[0, pl.ds(r0, BLK), :].astype(F32)
        vb_ref[pl.ds(r0 + BLK, BLK), :] = both_halves(v).astype(BF16)
        return carry

    lax.fori_loop(0, nb, prep, 0, unroll=PREP_UNROLL)

    assert SWA_WINDOW <= BLK
    band, kcol = _band_mask(SWA_WINDOW - 1)
    for h in range(N_HEADS):
        sink = sink_ref[h] * LOG2E
        bias = bias_ref[h] * LOG2E
        bm_ref[h] = jnp.where(kcol == 0, sink, jnp.where(band & (kcol >= BLK), bias, NEG))
        bm_ref[N_HEADS + h] = jnp.where(kcol == 0, sink, jnp.where(band, bias, NEG))
    first_key = lax.broadcasted_iota(jnp.int32, (2 * BLK, kvw), 0) == 0

    def block(i, carry):
        r0 = pl.multiple_of(i * BLK, BLK)
        later = jnp.minimum(i, 1) * N_HEADS
        qs, kws, vws, bms = [], [], [], []
        for hk in range(SWA_KV_HEADS):
            kw = kn_ref[pl.ds(r0, 2 * BLK), hk * kvw:(hk + 1) * kvw]
            vw = vb_ref[pl.ds(r0, 2 * BLK), hk * kvw:(hk + 1) * kvw]
            kw = jnp.where(first_key, jnp.zeros_like(kw), kw)
            vw = jnp.where(first_key, jnp.zeros_like(vw), vw)
            for g in range(SWA_GROUP):
                h = hk * SWA_GROUP + g
                q2 = qn_ref[pl.ds(r0, BLK), (h // 2) * kvw:(h // 2 + 1) * kvw]
                qs.append(jnp.where(lane_half == h % 2, q2, jnp.zeros_like(q2)))
                kws.append(kw)
                vws.append(vw)
                bms.append(bm_ref[later + h])
        outs, _, _ = _attend(qs, kws, vws, bms)
        pairs = [jnp.where(lane_half == 0, outs[h], outs[h + 1]) for h in range(0, N_HEADS, 2)]
        o_ref[0, pl.ds(r0, BLK), :] = jnp.concatenate(pairs, axis=-1).astype(o_ref.dtype)
        return carry

    lax.fori_loop(0, nb, block, 0)


def _swa_conv_mixers(proj, q_gain, k_gain, sink, bias, conv_w):
    bsz, seq, _ = proj.shape
    kvw = SWA_KV_HEADS * HEAD_DIM
    q_gain_t = jnp.tile(q_gain, N_HEADS).reshape(1, GROUP_WIDTH)
    k_gain_t = jnp.tile(k_gain, SWA_KV_HEADS).reshape(1, kvw)
    wide = lambda off: pl.BlockSpec((1, seq, GROUP_WIDTH), lambda b: (b, 0, off // GROUP_WIDTH))
    out = jax.ShapeDtypeStruct((bsz, seq, GROUP_WIDTH), BF16)
    return pl.pallas_call(
        _swa_kernel,
        grid=(bsz,),
        in_specs=[
            wide(OFF_SWA_Q),
            pl.BlockSpec((1, seq, kvw), lambda b: (b, 0, OFF_SWA_K // kvw)),
            pl.BlockSpec((1, seq, kvw), lambda b: (b, 0, OFF_SWA_V // kvw)),
            pl.BlockSpec((1, GROUP_WIDTH), lambda b: (0, 0)),
            pl.BlockSpec((1, kvw), lambda b: (0, 0)),
            pl.BlockSpec(memory_space=pltpu.SMEM),
            pl.BlockSpec((N_HEADS, BLK, 2 * BLK), lambda b: (0, 0, 0)),
            wide(OFF_CONV), wide(OFF_CONV + GROUP_WIDTH), wide(OFF_CONV + 2 * GROUP_WIDTH),
            pl.BlockSpec((3, GROUP_WIDTH), lambda b: (0, 0)),
        ],
        out_specs=[pl.BlockSpec((1, seq, GROUP_WIDTH), lambda b: (b, 0, 0))] * 2,
        out_shape=[out, out],
        scratch_shapes=[
            pltpu.VMEM((seq, GROUP_WIDTH), BF16),
            pltpu.VMEM((seq + BLK, 2 * kvw), BF16),
            pltpu.VMEM((seq + BLK, 2 * kvw), BF16),
            pltpu.VMEM((2 * N_HEADS, BLK, 2 * BLK), F32),
        ],
        compiler_params=_params("parallel"),
        name="swa_conv_mixers",
    )(proj, proj, proj, q_gain_t, k_gain_t, sink, bias, proj, proj, proj, conv_w)


def _dil_kernel(q_ref, k_ref, v_ref, qg_ref, kg_ref, bias_ref, o_ref,
                qn_ref, kn_ref, vn_ref, ob_ref, mb_ref, db_ref, bm_ref):
    seq = q_ref.shape[1]
    lanes = q_ref.shape[2]
    heads = lanes // HEAD_DIM
    scale = HEAD_DIM ** -0.5 * LOG2E

    def prep(i, carry):
        r0 = pl.multiple_of(i * BLK, BLK)
        q = q_ref[0, pl.ds(r0, BLK), :].astype(F32)
        k = k_ref[0, pl.ds(r0, BLK), :].astype(F32)
        qn_ref[pl.ds(r0, BLK), :] = _head_rms(q, qg_ref[...]) * scale
        kn_ref[pl.ds(r0, BLK), :] = _head_rms(k, kg_ref[...])
        vn_ref[pl.ds(r0, BLK), :] = v_ref[0, pl.ds(r0, BLK), :].astype(F32)
        return carry

    lax.fori_loop(0, seq // BLK, prep, 0, unroll=PREP_UNROLL)

    for br, (window, r) in enumerate(DIL_PAIRS):
        band, kcol = _band_mask(window // r)
        for h in range(heads):
            bias = bias_ref[br, h] * LOG2E
            bm_ref[(br * heads + h) * 2] = jnp.where(band & (kcol >= BLK), bias, NEG)
            bm_ref[(br * heads + h) * 2 + 1] = jnp.where(band, bias, NEG)

    lane_head = lax.broadcasted_iota(jnp.int32, (1, lanes), 1) // HEAD_DIM

    for br, (window, r) in enumerate(DIL_PAIRS):
        nb = seq // r // BLK

        def blocks(it, carry, br=br, r=r, nb=nb):
            qs, kws, vws, bms, curs = [], [], [], [], []
            for u in range(DIL_TILES):
                t = it * DIL_TILES + u
                c = t // nb
                i = t - c * nb
                cur = c + i * (BLK * r)
                prev = jnp.maximum(cur - BLK * r, c)
                later = jnp.minimum(i, 1)

                def rows(ref, start):
                    if r == 1:
                        return ref[pl.ds(start, BLK), :]
                    return ref[pl.ds(start, BLK, stride=r), :]

                q = rows(qn_ref, cur).astype(BF16)
                if nb == 1:
                    kw = rows(kn_ref, cur).astype(BF16)
                    vw = rows(vn_ref, cur).astype(BF16)
                else:
                    kw = jnp.concatenate([rows(kn_ref, prev), rows(kn_ref, cur)],
                                         axis=0).astype(BF16)
                    vw = jnp.concatenate([rows(vn_ref, prev), rows(vn_ref, cur)],
                                         axis=0).astype(BF16)
                curs.append(cur)
                for h in range(heads):
                    qs.append(jnp.where(lane_head == h, q, jnp.zeros_like(q)))
                    kws.append(kw)
                    vws.append(vw)
                    if nb == 1:
                        bms.append(bm_ref[(br * heads + h) * 2, :, BLK:])
                    else:
                        bms.append(bm_ref[(br * heads + h) * 2 + later])
            outs, ms, dens = _attend(qs, kws, vws, bms, normalize=False)
            for u in range(DIL_TILES):
                o_all, m_all, d_all = outs[u * heads], ms[u * heads], dens[u * heads]
                for h in range(1, heads):
                    o_all = jnp.where(lane_head == h, outs[u * heads + h], o_all)
                    m_all = jnp.where(lane_head == h, ms[u * heads + h], m_all)
                    d_all = jnp.where(lane_head == h, dens[u * heads + h], d_all)
                if r == 1:
                    rows = pl.ds(curs[u], BLK)
                else:
                    rows = pl.ds(curs[u], BLK, stride=r)
                ob_ref[br, rows, :] = o_all
                mb_ref[br, rows, :] = m_all
                db_ref[br, rows, :] = d_all
            return carry

        lax.fori_loop(0, r * nb // DIL_TILES, blocks, 0)

    def combine(i, carry):
        r0 = pl.multiple_of(i * BLK, BLK)
        rows = pl.ds(r0, BLK)
        m0, m1, m2 = mb_ref[0, rows, :], mb_ref[1, rows, :], mb_ref[2, rows, :]
        m = jnp.maximum(jnp.maximum(m0, m1), m2)
        e0, e1, e2 = jnp.exp2(m0 - m), jnp.exp2(m1 - m), jnp.exp2(m2 - m)
        num = e0 * ob_ref[0, rows, :] + e1 * ob_ref[1, rows, :] + e2 * ob_ref[2, rows, :]
        den = e0 * db_ref[0, rows, :] + e1 * db_ref[1, rows, :] + e2 * db_ref[2, rows, :]
        o_ref[0, rows, :] = (num / den).astype(o_ref.dtype)
        return carry

    lax.fori_loop(0, seq // BLK, combine, 0, unroll=PREP_UNROLL)


def _dil_mixer(proj, q_gain, k_gain, bias):
    bsz, seq, _ = proj.shape
    lanes = 128
    heads = lanes // HEAD_DIM
    nblk = GROUP_WIDTH // lanes
    base = OFF_DIL // lanes
    gq = jnp.tile(q_gain, heads).reshape(1, lanes)
    gk = jnp.tile(k_gain, heads).reshape(1, lanes)

    def col(seg):
        return pl.BlockSpec((1, seq, lanes), lambda b, j: (b, 0, base + seg * nblk + j))

    return pl.pallas_call(
        _dil_kernel,
        grid=(bsz, nblk),
        in_specs=[
            col(0), col(1), col(2),
            pl.BlockSpec((1, lanes), lambda b, j: (0, 0)),
            pl.BlockSpec((1, lanes), lambda b, j: (0, 0)),
            pl.BlockSpec((len(DIL_PAIRS), heads, BLK, 2 * BLK), lambda b, j: (0, j, 0, 0)),
        ],
        out_specs=pl.BlockSpec((1, seq, lanes), lambda b, j: (b, 0, j)),
        out_shape=jax.ShapeDtypeStruct((bsz, seq, GROUP_WIDTH), BF16),
        scratch_shapes=[
            pltpu.VMEM((seq, lanes), F32),
            pltpu.VMEM((seq, lanes), F32),
            pltpu.VMEM((seq, lanes), F32),
            pltpu.VMEM((len(DIL_PAIRS), seq, lanes), F32),
            pltpu.VMEM((len(DIL_PAIRS), seq, lanes), F32),
            pltpu.VMEM((len(DIL_PAIRS), seq, lanes), F32),
            pltpu.VMEM((len(DIL_PAIRS) * heads * 2, BLK, 2 * BLK), F32),
        ],
        compiler_params=_params("parallel", "parallel"),
        name="dil_mixer",
    )(proj, proj, proj, gq, gk, bias)


def _rwkv_maps_kernel(*refs, has_vres):
    if has_vres:
        (p_ref, pp_ref, mu_ref, w0_ref, w2_ref, a0_ref, a2_ref, g2_ref, kk_ref, ka_ref, rk_ref,
         vf_ref, v0_ref, v1_ref, v2_ref,
         q_out, y1_out, m_out, gm_out, bonus_out, gate_out, *scratch) = refs
    else:
        (p_ref, pp_ref, mu_ref, w0_ref, w2_ref, a0_ref, a2_ref, g2_ref, kk_ref, ka_ref, rk_ref,
         q_out, y1_out, m_out, gm_out, bonus_out, gate_out, v_out, *scratch) = refs
    w = GROUP_WIDTH
    p = p_ref[0].astype(F32)
    row = lax.broadcasted_iota(jnp.int32, p.shape, 0)
    last_prev = pp_ref[0, PREV_ROWS - 1:PREV_ROWS, :].astype(F32)
    last_prev = jnp.where(pl.program_id(1) > 0, last_prev, 0.0)
    prev = jnp.where(row >= 1, pltpu.roll(p, 1, axis=0), last_prev)
    xs = p + (prev - p) * mu_ref[...]
    grows = WKV_GROUP * WKV_CHUNK
    for grp in range(p.shape[0] // grows):
        rs = slice(grp * grows, (grp + 1) * grows)
        x = xs[rs]
        r = x[:, 0:w]
        k = x[:, w:2 * w]
        v = x[:, 2 * w:3 * w]
        o = 3 * w
        wd = x[:, o:o + DECAY_LORA]
        ad = x[:, o + DECAY_LORA:o + DECAY_LORA + ICLR_LORA]
        gd = x[:, o + DECAY_LORA + ICLR_LORA:]

        z = -(w0_ref[...] + _mm(jnp.tanh(wd), w2_ref[...]))
        softplus = jnp.maximum(z, 0.0) + jnp.log(1.0 + jnp.exp(-jnp.abs(z)))
        logw = -softplus - 0.5
        lw = -jnp.exp(logw)
        a = _sigmoid(a0_ref[...] + _mm(ad, a2_ref[...]))
        gate_out[0, rs] = _mm(_sigmoid(gd), g2_ref[...])
        if has_vres:
            mix = _sigmoid(v0_ref[...] + _mm(_mm(v, v1_ref[...]), v2_ref[...]))
            v = v + (vf_ref[0, rs] - v) * mix
        else:
            v_out[0, rs] = v
        kk = k * kk_ref[...]
        ss = _head_sums(kk * kk, split=False)
        kk = kk * lax.rsqrt(jnp.maximum(ss, 1e-24))
        k = k * (1.0 + (a - 1.0) * ka_ref[...])
        bonus_out[0, rs] = _head_sums(r * k * rk_ref[...]) * v
        _chunk_maps(grp, r, lw, k, v, kk, kk * a, q_out.at[0], y1_out.at[0], m_out.at[0],
                    gm_out.at[0], *scratch)


def _rwkv_maps(proj, mu, w0, w2, a0, a2, g2, k_k, k_a, r_k, v_first, vres, ts=512):
    bsz, seq, _ = proj.shape
    w = GROUP_WIDTH
    nt = seq // ts
    has_vres = vres is not None

    def full(shape):
        return pl.BlockSpec(shape, lambda b, i: (0,) * len(shape))

    row = lambda a: a.reshape(1, -1)
    tile = pl.BlockSpec((1, ts, w), lambda b, i: (b, i, 0))
    in_specs = [
        pl.BlockSpec((pl.Element(1), pl.Element(ts), pl.Element(RWKV_IN_WIDTH)),
                     lambda b, i: (b, i * ts, OFF_RW)),
        pl.BlockSpec((pl.Element(1), pl.Element(PREV_ROWS), pl.Element(RWKV_IN_WIDTH)),
                     lambda b, i: (b, jnp.maximum(i * (ts // PREV_ROWS) - 1, 0) * PREV_ROWS, OFF_RW)),
        full((1, RWKV_IN_WIDTH)), full((1, w)), full((DECAY_LORA, w)), full((1, w)),
        full((ICLR_LORA, w)), full((GATE_LORA, w)), full((1, w)), full((1, w)), full((1, w)),
    ]
    args = [proj, proj, row(mu), row(w0), w2, row(a0), a2, g2, row(k_k), row(k_a), row(r_k)]
    if has_vres:
        v0, v1, v2 = vres
        in_specs += [tile, full((1, w)), full(v1.shape), full(v2.shape)]
        args += [v_first, row(v0), v1, v2]
    f32 = jax.ShapeDtypeStruct((bsz, seq, w), F32)
    bf16 = jax.ShapeDtypeStruct((bsz, seq, w), BF16)
    out_shape = [bf16, f32, bf16, f32, f32, f32] + ([] if has_vres else [f32])
    return pl.pallas_call(
        functools.partial(_rwkv_maps_kernel, has_vres=has_vres),
        grid=(bsz, nt),
        in_specs=in_specs,
        out_specs=[tile] * len(out_shape),
        out_shape=out_shape,
        scratch_shapes=[pltpu.VMEM((ts, w), dt) for dt in (BF16, F32, BF16, BF16, F32, BF16, BF16)],
        compiler_params=_params("parallel", "parallel"),
        name="rwkv_chunk_maps",
    )(*args)


def _chunk_maps(grp, r, lw, k, v, kk, b, q_out, y1_out, m_out, g_out,
                at_ref, rt_ref, bt_ref, kt_ref, dec_ref, vb_ref, rb_ref):
    c = WKV_CHUNK
    n = HEAD_DIM
    rows = r.shape[0]
    rs = slice(grp * rows, (grp + 1) * rows)

    row = lax.broadcasted_iota(jnp.int32, (rows, rows), 0)
    col = lax.broadcasted_iota(jnp.int32, (rows, rows), 1)
    tri = ((row >= col) & ((row // c) == (col // c))).astype(BF16)
    lw_hi, lw_lo = _split_bf16(lw)
    cum = (jnp.dot(tri, lw_hi, preferred_element_type=F32)
           + jnp.dot(tri, lw_lo, preferred_element_type=F32))
    e_pos = jnp.exp(cum)
    e_neg = jnp.exp(-cum)
    r_t = r * e_pos
    at_ref[rs] = (-kk * jnp.exp(cum - lw)).astype(BF16)
    rt_ref[rs] = r_t
    rb_ref[rs] = r_t.astype(BF16)
    bt_ref[rs] = (b * e_neg).astype(BF16)
    kt_ref[rs] = (k * e_neg).astype(BF16)
    dec_ref[rs] = e_pos
    vb_ref[rs] = v.astype(BF16)

    crow = lax.broadcasted_iota(jnp.int32, (c, c), 0)
    ccol = lax.broadcasted_iota(jnp.int32, (c, c), 1)
    lower = crow >= ccol
    strict = crow > ccol
    eye = (crow == ccol).astype(F32)
    wrow = lax.broadcasted_iota(jnp.int32, (c, 2 * c), 0)
    wcol = lax.broadcasted_iota(jnp.int32, (c, 2 * c), 1)
    strict_left = wrow > wcol
    right = wcol >= c
    eye_right = (wcol == wrow + c).astype(F32)

    def tile(ref, j, h):
        return ref[j * c:(j + 1) * c, h * n:(h + 1) * n]

    _chunk_group(range(grp * WKV_GROUP, (grp + 1) * WKV_GROUP), tile, at_ref, rt_ref, bt_ref,
                 kt_ref, dec_ref, vb_ref, rb_ref, q_out, y1_out, m_out, g_out,
                 lower, strict, eye, strict_left, right, eye_right)


def _chunk_group(chunks, tile, at_ref, rt_ref, bt_ref, kt_ref, dec_ref, vb_ref, rb_ref,
                 q_out, y1_out, m_out, g_out, lower, strict, eye, strict_left, right, eye_right):
    c = WKV_CHUNK
    n = HEAD_DIM
    pairs = [(j, h) for j in chunks for h in range(N_HEADS)]
    ah = [tile(at_ref, j, h) for j, h in pairs]
    rh = [tile(rt_ref, j, h) for j, h in pairs]
    bh = [tile(bt_ref, j, h) for j, h in pairs]
    kh = [tile(kt_ref, j, h) for j, h in pairs]
    vh = [tile(vb_ref, j, h) for j, h in pairs]
    dh = [dec_ref[(j + 1) * c - 1:(j + 1) * c, h * n:(h + 1) * n] for j, h in pairs]
    idx = range(len(pairs))
    ar = [jnp.concatenate([ah[i], tile(rb_ref, *pairs[i])], axis=0) for i in idx]
    bk = [jnp.concatenate([bh[i], kh[i]], axis=0) for i in idx]
    aa = [_mm_nt(ar[i], bk[i]) for i in idx]
    a_kr = [jnp.concatenate([jnp.where(strict, aa[i][:c, c:], 0.0),
                             jnp.where(lower, aa[i][c:, c:], 0.0)], axis=0).astype(BF16)
            for i in idx]
    a_rb = [jnp.where(lower, aa[i][c:, :c], 0.0).astype(BF16) for i in idx]
    ps = [jnp.where(strict_left, aa[i][:c, :], 0.0) + eye_right for i in idx]
    for _ in range(int(math.log2(c))):
        ps = [_mm(ps[i][:, :c], ps[i]) + jnp.where(right, ps[i], 0.0) for i in idx]
    t = [ps[i][:, c:].astype(BF16) for i in idx]
    av = [_mm(a_kr[i], vh[i]) for i in idx]
    wu = [_mm(t[i], jnp.concatenate([ah[i], av[i][:c].astype(BF16)], axis=1)).astype(BF16)
          for i in idx]
    aw = [_mm(a_rb[i], wu[i]) for i in idx]
    q = [rh[i] + aw[i][:, :n] for i in idx]
    y1 = [aw[i][:, n:] + av[i][c:] for i in idx]
    zero = jnp.zeros((c, n), BF16)
    mg = [_mm_tn(jnp.concatenate([wu[i], jnp.concatenate([zero, vh[i]], axis=1)], axis=0), bk[i])
          for i in idx]
    m = [(eye + mg[i][:n]) * dh[i] for i in idx]
    g = [mg[i][n:] * dh[i] for i in idx]
    for j in chunks:
        sel = slice((j - chunks[0]) * N_HEADS, (j - chunks[0] + 1) * N_HEADS)
        q_out[j * c:(j + 1) * c, :] = jnp.concatenate(q[sel], axis=-1).astype(q_out.dtype)
        y1_out[j * c:(j + 1) * c, :] = jnp.concatenate(y1[sel], axis=-1)
        m_out[j * c:(j + 1) * c, :] = jnp.concatenate(m[sel], axis=-1).astype(m_out.dtype)
        g_out[j * c:(j + 1) * c, :] = jnp.concatenate(g[sel], axis=-1)


def _wkv_scan_kernel(q_ref, y1_ref, m_ref, gm_ref, bonus_ref, gate_ref, lng_ref, lnb_ref,
                     o_ref, st_ref):
    n = HEAD_DIM
    bsz = q_ref.shape[0]

    @pl.when(pl.program_id(0) == 0)
    def _():
        st_ref[...] = jnp.zeros(st_ref.shape, F32)

    c = WKV_CHUNK
    rows = q_ref.shape[1]
    chains = [(bi, h) for bi in range(bsz) for h in range(N_HEADS)]
    state = [st_ref[bi * N_HEADS + h] for bi, h in chains]
    entering = []
    for j in range(rows // c):
        rs = slice(j * c, (j + 1) * c)
        s_in = [s.astype(BF16) for s in state]
        entering.append(s_in)
        state = [_mm(s_in[i], m_ref[bi, rs, h * n:(h + 1) * n]) + gm_ref[bi, rs, h * n:(h + 1) * n]
                 for i, (bi, h) in enumerate(chains)]
    for i, (bi, h) in enumerate(chains):
        st_ref[bi * N_HEADS + h] = state[i]
    ys = [[_mm_nt(q_ref[bi, j * c:(j + 1) * c, h * n:(h + 1) * n], entering[j][i])
           for i, (bi, h) in enumerate(chains)] for j in range(rows // c)]
    y = jnp.concatenate(
        [jnp.concatenate([jnp.concatenate(ys[j][bi * N_HEADS:(bi + 1) * N_HEADS], axis=-1)
                          for j in range(rows // c)], axis=0) + y1_ref[bi]
         for bi in range(bsz)], axis=0)
    mean = _head_sums(y) * (1.0 / n)
    yc = y - mean
    var = _head_sums(yc * yc) * (1.0 / n)
    yn = yc * lax.rsqrt(var + LN_X_EPS) * lng_ref[...] + lnb_ref[...]
    for bi in range(bsz):
        o_ref[bi] = ((yn[bi * rows:(bi + 1) * rows] + bonus_ref[bi]) * gate_ref[bi]).astype(o_ref.dtype)


def _wkv_scan(q, y1, m, gm, bonus, gate, ln_g, ln_b, rows=4 * WKV_CHUNK):
    bsz, seq, w = y1.shape
    tile = pl.BlockSpec((bsz, rows, w), lambda ci: (0, ci, 0))
    vec = pl.BlockSpec((1, w), lambda ci: (0, 0))
    return pl.pallas_call(
        _wkv_scan_kernel,
        grid=(seq // rows,),
        in_specs=[tile] * 6 + [vec] * 2,
        out_specs=tile,
        out_shape=jax.ShapeDtypeStruct((bsz, seq, w), BF16),
        scratch_shapes=[pltpu.VMEM((bsz * N_HEADS, HEAD_DIM, HEAD_DIM), F32)],
        compiler_params=_params("arbitrary"),
        name="wkv_state_scan",
    )(q, y1, m, gm, bonus, gate, ln_g.reshape(1, w), ln_b.reshape(1, w))


def _wout_kernel(x_ref, y0_ref, y1_ref, y2_ref, y3_ref, w_ref, o_ref):
    acc = x_ref[...]
    for idx, y_ref in enumerate((y0_ref, y1_ref, y2_ref, y3_ref)):
        w = w_ref[0, idx * GROUP_WIDTH:(idx + 1) * GROUP_WIDTH, :].astype(BF16)
        acc = acc + jnp.dot(y_ref[...], w, preferred_element_type=F32)
    o_ref[...] = acc


def _wout(x2d, ys, w_stack, layer, tm=512):
    m, d = x2d.shape
    ytile = pl.BlockSpec((tm, GROUP_WIDTH), lambda i: (i, 0))
    xtile = pl.BlockSpec((tm, d), lambda i: (i, 0))
    wspec = pl.BlockSpec((1,) + w_stack.shape[1:], lambda i: (layer, 0, 0))
    return pl.pallas_call(
        _wout_kernel,
        grid=(m // tm,),
        in_specs=[xtile] + [ytile] * 4 + [wspec],
        out_specs=xtile,
        out_shape=jax.ShapeDtypeStruct((m, d), F32),
        compiler_params=_params("parallel", vmem_limit=FFN_VMEM_LIMIT),
        name="wout_residual",
    )(x2d, *ys, w_stack)


def _ffn_kernel(x_ref, g_ref, wu_ref, wd_ref, o_ref, h_ref):
    def down(u):
        act = jnp.square(jnp.maximum(u, 0.0)).astype(BF16)
        return jnp.dot(act, wd_ref[0].astype(BF16), preferred_element_type=F32)

    @pl.when(pl.program_id(1) == 0)
    def _():
        u = None
        for cols, h in _norm_chunks(x_ref, g_ref, h_ref, _rms_scale(x_ref)):
            part = jnp.dot(h, wu_ref[0, cols, :].astype(BF16), preferred_element_type=F32)
            u = part if u is None else u + part
        o_ref[...] = x_ref[...] + down(u)

    @pl.when(pl.program_id(1) > 0)
    def _():
        u = jnp.dot(h_ref[...], wu_ref[0].astype(BF16), preferred_element_type=F32)
        o_ref[...] += down(u)


def _ffn(x2d, gains, wu_stack, wd_stack, layer, tm=1024, tf=512):
    m, d = x2d.shape
    f = wu_stack.shape[2]
    xtile = pl.BlockSpec((tm, d), lambda i, j: (i, 0))
    return pl.pallas_call(
        _ffn_kernel,
        grid=(m // tm, f // tf),
        in_specs=[
            xtile,
            pl.BlockSpec((1, 1, d), lambda i, j: (layer, 0, 0)),
            pl.BlockSpec((1, d, tf), lambda i, j: (layer, 0, j)),
            pl.BlockSpec((1, tf, d), lambda i, j: (layer, j, 0)),
        ],
        out_specs=xtile,
        out_shape=jax.ShapeDtypeStruct((m, d), F32),
        scratch_shapes=[pltpu.VMEM((tm, d), BF16)],
        compiler_params=_params("parallel", "arbitrary", vmem_limit=FFN_VMEM_LIMIT),
        name="ffn",
    )(x2d, gains.reshape(-1, 1, d), wu_stack, wd_stack)


def kernel(x, norm_mix, w_in, conv_w, swa_q_norm, swa_k_norm, swa_sink, dil_q_norm, dil_k_norm,
           rwkv_mu, decay_w0, decay_w2, iclr_a0, iclr_a2, gate_g2, k_k, k_a, r_k, ln_x_g, ln_x_b,
           vres_v0, vres_v1, vres_v2, w_out, norm_ffn, w_up, w_down, rel_bias):
    bsz, seq, d = x.shape
    depth = w_in.shape[0]
    bias = _bias_tiles(rel_bias)
    swa_bias = bias[0, :N_HEADS]
    dil_bias = bias[:, N_HEADS:]
    x2d = x.reshape(bsz * seq, d)
    v_first = None
    for layer in range(depth):
        proj = _norm_matmul(x2d, norm_mix, w_in, layer).reshape(bsz, seq, IN_WIDTH)
        y_swa, y_conv = _swa_conv_mixers(proj, swa_q_norm[layer], swa_k_norm[layer],
                                         swa_sink[layer], swa_bias, conv_w[layer])
        y_dil = _dil_mixer(proj, dil_q_norm[layer], dil_k_norm[layer], dil_bias)
        vres = None if layer == 0 else (vres_v0[layer - 1], vres_v1[layer - 1], vres_v2[layer - 1])
        maps = _rwkv_maps(
            proj, rwkv_mu[layer], decay_w0[layer], decay_w2[layer], iclr_a0[layer], iclr_a2[layer],
            gate_g2[layer], k_k[layer], k_a[layer], r_k[layer], v_first, vres)
        if layer == 0:
            v_first = maps[6]
        y_rwkv = _wkv_scan(*maps[:6], ln_x_g[layer], ln_x_b[layer])
        ys = [y.reshape(bsz * seq, GROUP_WIDTH) for y in (y_conv, y_swa, y_dil, y_rwkv)]
        x2d = _wout(x2d, ys, w_out, layer)
        x2d = _ffn(x2d, norm_ffn, w_up, w_down, layer)
    return x2d.reshape(bsz, seq, d)
```

```python
import functools
import math

import jax
import jax.numpy as jnp
from jax import lax
from jax.experimental import pallas as pl
from jax.experimental.pallas import tpu as pltpu

F32 = jnp.float32
BF16 = jnp.bfloat16

HEAD_DIM = 64
GROUP_WIDTH = 512
N_HEADS = GROUP_WIDTH // HEAD_DIM
SWA_KV_HEADS = 2
SWA_GROUP = N_HEADS // SWA_KV_HEADS
SWA_WINDOW = 128
DIL_PAIRS = ((128, 1), (512, 4), (2048, 16))
DECAY_LORA = 64
ICLR_LORA = 64
GATE_LORA = 128
RWKV_IN_WIDTH = 3 * GROUP_WIDTH + DECAY_LORA + ICLR_LORA + GATE_LORA
BLK = 128
NUM_BUCKETS = 32
BUCKET_MAX_DIST = 128
RMS_EPS = 1e-6
LN_X_EPS = 64e-5
NEG = -1e30
LOG2E = math.log2(math.e)
WKV_CHUNK = 64
WKV_GROUP = 4
DIL_TILES = 4
PREV_ROWS = 16
NORM_CHUNKS = 4
PREP_UNROLL = 4

OFF_CONV = 0
OFF_SWA_Q = OFF_CONV + 3 * GROUP_WIDTH
OFF_SWA_K = OFF_SWA_Q + GROUP_WIDTH
OFF_SWA_V = OFF_SWA_K + SWA_KV_HEADS * HEAD_DIM
OFF_DIL = OFF_SWA_V + SWA_KV_HEADS * HEAD_DIM
OFF_RW = OFF_DIL + 3 * GROUP_WIDTH
IN_WIDTH = OFF_RW + RWKV_IN_WIDTH

V7X_VMEM_BYTES = 64 * 1024 * 1024
VMEM_LIMIT = 48 * 1024 * 1024
FFN_VMEM_LIMIT = V7X_VMEM_BYTES - 6 * 1024 * 1024


def _params(*sem, vmem_limit=VMEM_LIMIT):
    return pltpu.CompilerParams(dimension_semantics=sem, vmem_limit_bytes=vmem_limit)


def _mm(a, b):
    return jnp.dot(a.astype(BF16), b.astype(BF16), preferred_element_type=F32)


def _mm_nt(a, b):
    return lax.dot_general(a.astype(BF16), b.astype(BF16), (((1,), (1,)), ((), ())),
                           preferred_element_type=F32)


def _mm_tn(a, b):
    return lax.dot_general(a.astype(BF16), b.astype(BF16), (((0,), (0,)), ((), ())),
                           preferred_element_type=F32)


def _split_bf16(x):
    hi = x.astype(BF16)
    lo = (x - hi.astype(F32)).astype(BF16)
    return hi, lo


def _head_sums(x, split=True):
    lanes = 128
    r = lax.broadcasted_iota(jnp.int32, (lanes, lanes), 0) // HEAD_DIM
    c = lax.broadcasted_iota(jnp.int32, (lanes, lanes), 1) // HEAD_DIM
    bd = (r == c).astype(BF16)
    parts = _split_bf16(x) if split else (x.astype(BF16),)
    cols = []
    for j in range(x.shape[-1] // lanes):
        sl = slice(j * lanes, (j + 1) * lanes)
        cols.append(sum(jnp.dot(part[:, sl], bd, preferred_element_type=F32) for part in parts))
    return cols[0] if len(cols) == 1 else jnp.concatenate(cols, axis=-1)


def _sigmoid(z):
    return 1.0 / (1.0 + jnp.exp(-z))


def _rms_scale(x_ref):
    x = x_ref[...]
    return lax.rsqrt(jnp.mean(x * x, axis=-1, keepdims=True) + RMS_EPS)


def _norm_chunks(x_ref, g_ref, h_ref, scale):
    k = x_ref.shape[1]
    kc = k // NORM_CHUNKS
    for c in range(NORM_CHUNKS):
        cols = slice(c * kc, (c + 1) * kc)
        h = (x_ref[:, cols] * scale * g_ref[0, :, cols]).astype(BF16)
        h_ref[:, cols] = h
        yield cols, h


def _norm_matmul_kernel(x_ref, g_ref, w_ref, o_ref, h_ref):
    @pl.when(pl.program_id(1) == 0)
    def _():
        for _ in _norm_chunks(x_ref, g_ref, h_ref, _rms_scale(x_ref)):
            pass

    o_ref[...] = jnp.dot(h_ref[...], w_ref[0].astype(BF16),
                         preferred_element_type=F32).astype(o_ref.dtype)


def _norm_matmul(x2d, gains, w_stack, layer, tm=2048, tn=512):
    m, k = x2d.shape
    n = w_stack.shape[2]
    assert n % tn == 0 and m % tm == 0
    return pl.pallas_call(
        _norm_matmul_kernel,
        grid=(m // tm, n // tn),
        in_specs=[
            pl.BlockSpec((tm, k), lambda i, j: (i, 0)),
            pl.BlockSpec((1, 1, k), lambda i, j: (layer, 0, 0)),
            pl.BlockSpec((1, k, tn), lambda i, j: (layer, 0, j)),
        ],
        out_specs=pl.BlockSpec((tm, tn), lambda i, j: (i, j)),
        out_shape=jax.ShapeDtypeStruct((m, n), BF16),
        scratch_shapes=[pltpu.VMEM((tm, k), BF16)],
        compiler_params=_params("parallel", "arbitrary", vmem_limit=FFN_VMEM_LIMIT),
        name="norm_matmul",
    )(x2d, gains.reshape(-1, 1, k), w_stack)


def _t5_bucket(dist):
    dist = jnp.maximum(dist, 0)
    max_exact = NUM_BUCKETS // 2
    scaled = (jnp.log(jnp.maximum(dist, 1).astype(F32) / max_exact)
              / math.log(BUCKET_MAX_DIST / max_exact))
    large = max_exact + (scaled * (NUM_BUCKETS - max_exact)).astype(jnp.int32)
    large = jnp.minimum(large, NUM_BUCKETS - 1)
    return jnp.where(dist < max_exact, dist, large)


def _bias_kernel(bucket_ref, table_ref, o_ref):
    bucket = bucket_ref[0]

    def head(h, carry):
        g = jnp.zeros(bucket.shape, F32)
        for b in range(NUM_BUCKETS):
            g = jnp.where(bucket == b, table_ref[b, h], g)
        rows = jnp.broadcast_to(g[0:1], (BLK, 2 * BLK))
        o_ref[0, h] = pltpu.roll(rows, 0, axis=1, stride=1, stride_axis=0)
        return carry

    lax.fori_loop(0, o_ref.shape[1], head, 0)


def _bias_tiles(rel_bias):
    lag_dist = BLK - jnp.arange(2 * BLK)
    buckets = jnp.stack([_t5_bucket(lag_dist * r) for _, r in DIL_PAIRS]).astype(jnp.int32)
    buckets = jnp.broadcast_to(buckets[:, None, :], (len(DIL_PAIRS), 8, 2 * BLK))
    nh = rel_bias.shape[1]
    return pl.pallas_call(
        _bias_kernel,
        grid=(len(DIL_PAIRS),),
        in_specs=[
            pl.BlockSpec((1, 8, 2 * BLK), lambda s: (s, 0, 0)),
            pl.BlockSpec(memory_space=pltpu.SMEM),
        ],
        out_specs=pl.BlockSpec((1, nh, BLK, 2 * BLK), lambda s: (s, 0, 0, 0)),
        out_shape=jax.ShapeDtypeStruct((len(DIL_PAIRS), nh, BLK, 2 * BLK), F32),
        compiler_params=_params("arbitrary"),
        name="bias_tiles",
    )(buckets, rel_bias)


def _short_conv(b_ref, c_ref, u_ref, w_ref, o_ref):
    lanes = 128
    for j in range(b_ref.shape[2] // lanes):
        cols = slice(j * lanes, (j + 1) * lanes)
        z = c_ref[0, :, cols].astype(F32) * u_ref[0, :, cols].astype(F32)
        row = lax.broadcasted_iota(jnp.int32, z.shape, 0)
        z1 = jnp.where(row >= 1, pltpu.roll(z, 1, axis=0), 0.0)
        z2 = jnp.where(row >= 2, pltpu.roll(z, 2, axis=0), 0.0)
        w = w_ref[:, cols]
        y = z2 * w[0:1, :] + z1 * w[1:2, :] + z * w[2:3, :]
        o_ref[0, :, cols] = (b_ref[0, :, cols].astype(F32) * y).astype(o_ref.dtype)


def _band_mask(max_dist):
    a = lax.broadcasted_iota(jnp.int32, (BLK, 2 * BLK), 0)
    b = lax.broadcasted_iota(jnp.int32, (BLK, 2 * BLK), 1)
    dist = BLK + a - b
    return (dist >= 0) & (dist <= max_dist), b


def _attend(qs, kws, vws, biases, normalize=True):
    idx = range(len(qs))
    s = [lax.dot_general(qs[i], kws[i], (((1,), (1,)), ((), ())), preferred_element_type=F32)
         + biases[i] for i in idx]
    m = [jnp.max(s[i], axis=-1, keepdims=True) for i in idx]
    p = [jnp.exp2(s[i] - m[i]) for i in idx]
    den = [jnp.sum(p[i], axis=-1, keepdims=True) for i in idx]
    o = [jnp.dot(p[i].astype(BF16), vws[i], preferred_element_type=F32) for i in idx]
    if normalize:
        o = [o[i] / den[i] for i in idx]
    return o, m, den


def _head_rms(x, gain):
    ms = _head_sums(x * x, split=False) * (1.0 / HEAD_DIM)
    return x * lax.rsqrt(ms + RMS_EPS) * gain


def _swa_kernel(q_ref, k_ref, v_ref, qg_ref, kg_ref, sink_ref, bias_ref,
                cb_ref, cc_ref, cu_ref, cw_ref, o_ref, oc_ref,
                qn_ref, kn_ref, vb_ref, bm_ref):
    seq = q_ref.shape[1]
    nb = seq // BLK
    kvw = SWA_KV_HEADS * HEAD_DIM
    scale = HEAD_DIM ** -0.5 * LOG2E

    _short_conv(cb_ref, cc_ref, cu_ref, cw_ref, oc_ref)

    kn_ref[0:BLK, :] = jnp.zeros((BLK, 2 * kvw), BF16)
    vb_ref[0:BLK, :] = jnp.zeros((BLK, 2 * kvw), BF16)
    lane_half = lax.broadcasted_iota(jnp.int32, (1, kvw), 1) // HEAD_DIM

    def both_halves(x):
        xr = pltpu.roll(x, HEAD_DIM, axis=1)
        return jnp.concatenate([jnp.where(lane_half == 0, x, xr),
                                jnp.where(lane_half == 0, xr, x)], axis=-1)

    def prep(i, carry):
        r0 = pl.multiple_of(i * BLK, BLK)
        q = q_ref[0, pl.ds(r0, BLK), :].astype(F32)
        qn_ref[pl.ds(r0, BLK), :] = (_head_rms(q, qg_ref[...]) * scale).astype(BF16)
        k = _head_rms(k_ref[0, pl.ds(r0, BLK), :].astype(F32), kg_ref[...])
        kn_ref[pl.ds(r0 + BLK, BLK), :] = both_halves(k).astype(BF16)
        v = v_ref[0, pl.ds(r0, BLK), :].astype(F32)
        vb_ref[pl.ds(r0 + BLK, BLK), :] = both_halves(v).astype(BF16)
        return carry

    lax.fori_loop(0, nb, prep, 0, unroll=PREP_UNROLL)

    assert SWA_WINDOW <= BLK
    band, kcol = _band_mask(SWA_WINDOW - 1)
    for h in range(N_HEADS):
        sink = sink_ref[h] * LOG2E
        bias = bias_ref[h] * LOG2E
        bm_ref[h] = jnp.where(kcol == 0, sink, jnp.where(band & (kcol >= BLK), bias, NEG))
        bm_ref[N_HEADS + h] = jnp.where(kcol == 0, sink, jnp.where(band, bias, NEG))
    first_key = lax.broadcasted_iota(jnp.int32, (2 * BLK, kvw), 0) == 0

    def block(i, carry):
        r0 = pl.multiple_of(i * BLK, BLK)
        later = jnp.minimum(i, 1) * N_HEADS
        qs, kws, vws, bms = [], [], [], []
        for hk in range(SWA_KV_HEADS):
            kw = kn_ref[pl.ds(r0, 2 * BLK), hk * kvw:(hk + 1) * kvw]
            vw = vb_ref[pl.ds(r0, 2 * BLK), hk * kvw:(hk + 1) * kvw]
            kw = jnp.where(first_key, jnp.zeros_like(kw), kw)
            vw = jnp.where(first_key, jnp.zeros_like(vw), vw)
            for g in range(SWA_GROUP):
                h = hk * SWA_GROUP + g
                q2 = qn_ref[pl.ds(r0, BLK), (h // 2) * kvw:(h // 2 + 1) * kvw]
                qs.append(jnp.where(lane_half == h % 2, q2, jnp.zeros_like(q2)))
                kws.append(kw)
                vws.append(vw)
                bms.append(bm_ref[later + h])
        outs, _, _ = _attend(qs, kws, vws, bms)
        pairs = [jnp.where(lane_half == 0, outs[h], outs[h + 1]) for h in range(0, N_HEADS, 2)]
        o_ref[0, pl.ds(r0, BLK), :] = jnp.concatenate(pairs, axis=-1).astype(o_ref.dtype)
        return carry

    lax.fori_loop(0, nb, block, 0)


def _swa_conv_mixers(proj, q_gain, k_gain, sink, bias, conv_w):
    bsz, seq, _ = proj.shape
    kvw = SWA_KV_HEADS * HEAD_DIM
    q_gain_t = jnp.tile(q_gain, N_HEADS).reshape(1, GROUP_WIDTH)
    k_gain_t = jnp.tile(k_gain, SWA_KV_HEADS).reshape(1, kvw)
    wide = lambda off: pl.BlockSpec((1, seq, GROUP_WIDTH), lambda b: (b, 0, off // GROUP_WIDTH))
    out = jax.ShapeDtypeStruct((bsz, seq, GROUP_WIDTH), BF16)
    return pl.pallas_call(
        _swa_kernel,
        grid=(bsz,),
        in_specs=[
            wide(OFF_SWA_Q),
            pl.BlockSpec((1, seq, kvw), lambda b: (b, 0, OFF_SWA_K // kvw)),
            pl.BlockSpec((1, seq, kvw), lambda b: (b, 0, OFF_SWA_V // kvw)),
            pl.BlockSpec((1, GROUP_WIDTH), lambda b: (0, 0)),
            pl.BlockSpec((1, kvw), lambda b: (0, 0)),
            pl.BlockSpec(memory_space=pltpu.SMEM),
            pl.BlockSpec((N_HEADS, BLK, 2 * BLK), lambda b: (0, 0, 0)),
            wide(OFF_CONV), wide(OFF_CONV + GROUP_WIDTH), wide(OFF_CONV + 2 * GROUP_WIDTH),
            pl.BlockSpec((3, GROUP_WIDTH), lambda b: (0, 0)),
        ],
        out_specs=[pl.BlockSpec((1, seq, GROUP_WIDTH), lambda b: (b, 0, 0))] * 2,
        out_shape=[out, out],
        scratch_shapes=[
            pltpu.VMEM((seq, GROUP_WIDTH), BF16),
            pltpu.VMEM((seq + BLK, 2 * kvw), BF16),
            pltpu.VMEM((seq + BLK, 2 * kvw), BF16),
            pltpu.VMEM((2 * N_HEADS, BLK, 2 * BLK), F32),
        ],
        compiler_params=_params("parallel"),
        name="swa_conv_mixers",
    )(proj, proj, proj, q_gain_t, k_gain_t, sink, bias, proj, proj, proj, conv_w)


def _dil_kernel(q_ref, k_ref, v_ref, qg_ref, kg_ref, bias_ref, wu_ref, wd_ref,
                o_ref, wu_out, wd_out,
                qn_ref, kn_ref, vn_ref, ob_ref, mb_ref, db_ref, bm_ref):
    wu_out[...] = wu_ref[0].astype(BF16)
    wd_out[...] = wd_ref[0].astype(BF16)
    seq = q_ref.shape[1]
    lanes = q_ref.shape[2]
    heads = lanes // HEAD_DIM
    scale = HEAD_DIM ** -0.5 * LOG2E

    def prep(i, carry):
        r0 = pl.multiple_of(i * BLK, BLK)
        q = q_ref[0, pl.ds(r0, BLK), :].astype(F32)
        k = k_ref[0, pl.ds(r0, BLK), :].astype(F32)
        qn_ref[pl.ds(r0, BLK), :] = _head_rms(q, qg_ref[...]) * scale
        kn_ref[pl.ds(r0, BLK), :] = _head_rms(k, kg_ref[...])
        vn_ref[pl.ds(r0, BLK), :] = v_ref[0, pl.ds(r0, BLK), :].astype(F32)
        return carry

    lax.fori_loop(0, seq // BLK, prep, 0, unroll=PREP_UNROLL)

    for br, (window, r) in enumerate(DIL_PAIRS):
        band, kcol = _band_mask(window // r)
        for h in range(heads):
            bias = bias_ref[br, h] * LOG2E
            bm_ref[(br * heads + h) * 2] = jnp.where(band & (kcol >= BLK), bias, NEG)
            bm_ref[(br * heads + h) * 2 + 1] = jnp.where(band, bias, NEG)

    lane_head = lax.broadcasted_iota(jnp.int32, (1, lanes), 1) // HEAD_DIM

    for br, (window, r) in enumerate(DIL_PAIRS):
        nb = seq // r // BLK

        def blocks(it, carry, br=br, r=r, nb=nb):
            qs, kws, vws, bms, curs = [], [], [], [], []
            for u in range(DIL_TILES):
                t = it * DIL_TILES + u
                c = t // nb
                i = t - c * nb
                cur = c + i * (BLK * r)
                prev = jnp.maximum(cur - BLK * r, c)
                later = jnp.minimum(i, 1)

                def rows(ref, start):
                    if r == 1:
                        return ref[pl.ds(start, BLK), :]
                    return ref[pl.ds(start, BLK, stride=r), :]

                q = rows(qn_ref, cur).astype(BF16)
                if nb == 1:
                    kw = rows(kn_ref, cur).astype(BF16)
                    vw = rows(vn_ref, cur).astype(BF16)
                else:
                    kw = jnp.concatenate([rows(kn_ref, prev), rows(kn_ref, cur)],
                                         axis=0).astype(BF16)
                    vw = jnp.concatenate([rows(vn_ref, prev), rows(vn_ref, cur)],
                                         axis=0).astype(BF16)
                curs.append(cur)
                for h in range(heads):
                    qs.append(jnp.where(lane_head == h, q, jnp.zeros_like(q)))
                    kws.append(kw)
                    vws.append(vw)
                    if nb == 1:
                        bms.append(bm_ref[(br * heads + h) * 2, :, BLK:])
                    else:
                        bms.append(bm_ref[(br * heads + h) * 2 + later])
            outs, ms, dens = _attend(qs, kws, vws, bms, normalize=False)
            for u in range(DIL_TILES):
                o_all, m_all, d_all = outs[u * heads], ms[u * heads], dens[u * heads]
                for h in range(1, heads):
                    o_all = jnp.where(lane_head == h, outs[u * heads + h], o_all)
                    m_all = jnp.where(lane_head == h, ms[u * heads + h], m_all)
                    d_all = jnp.where(lane_head == h, dens[u * heads + h], d_all)
                if r == 1:
                    rows = pl.ds(curs[u], BLK)
                else:
                    rows = pl.ds(curs[u], BLK, stride=r)
                ob_ref[br, rows, :] = o_all
                mb_ref[br, rows, :] = m_all
                db_ref[br, rows, :] = d_all
            return carry

        lax.fori_loop(0, r * nb // DIL_TILES, blocks, 0)

    def combine(i, carry):
        r0 = pl.multiple_of(i * BLK, BLK)
        rows = pl.ds(r0, BLK)
        m0, m1, m2 = mb_ref[0, rows, :], mb_ref[1, rows, :], mb_ref[2, rows, :]
        m = jnp.maximum(jnp.maximum(m0, m1), m2)
        e0, e1, e2 = jnp.exp2(m0 - m), jnp.exp2(m1 - m), jnp.exp2(m2 - m)
        num = e0 * ob_ref[0, rows, :] + e1 * ob_ref[1, rows, :] + e2 * ob_ref[2, rows, :]
        den = e0 * db_ref[0, rows, :] + e1 * db_ref[1, rows, :] + e2 * db_ref[2, rows, :]
        o_ref[0, rows, :] = (num / den).astype(o_ref.dtype)
        return carry

    lax.fori_loop(0, seq // BLK, combine, 0, unroll=PREP_UNROLL)


def _dil_mixer(proj, q_gain, k_gain, bias, w_up, w_down, layer):
    bsz, seq, _ = proj.shape
    lanes = 128
    heads = lanes // HEAD_DIM
    nblk = GROUP_WIDTH // lanes
    base = OFF_DIL // lanes
    gq = jnp.tile(q_gain, heads).reshape(1, lanes)
    gk = jnp.tile(k_gain, heads).reshape(1, lanes)
    _, d, f = w_up.shape
    steps = bsz * nblk
    assert d % steps == 0 and f % steps == 0

    def col(seg):
        return pl.BlockSpec((1, seq, lanes), lambda b, j: (b, 0, base + seg * nblk + j))

    return pl.pallas_call(
        _dil_kernel,
        grid=(bsz, nblk),
        in_specs=[
            col(0), col(1), col(2),
            pl.BlockSpec((1, lanes), lambda b, j: (0, 0)),
            pl.BlockSpec((1, lanes), lambda b, j: (0, 0)),
            pl.BlockSpec((len(DIL_PAIRS), heads, BLK, 2 * BLK), lambda b, j: (0, j, 0, 0)),
            pl.BlockSpec((1, d // steps, f), lambda b, j: (layer, b * nblk + j, 0)),
            pl.BlockSpec((1, f // steps, d), lambda b, j: (layer, b * nblk + j, 0)),
        ],
        out_specs=[
            pl.BlockSpec((1, seq, lanes), lambda b, j: (b, 0, j)),
            pl.BlockSpec((d // steps, f), lambda b, j: (b * nblk + j, 0)),
            pl.BlockSpec((f // steps, d), lambda b, j: (b * nblk + j, 0)),
        ],
        out_shape=[
            jax.ShapeDtypeStruct((bsz, seq, GROUP_WIDTH), BF16),
            jax.ShapeDtypeStruct((d, f), BF16),
            jax.ShapeDtypeStruct((f, d), BF16),
        ],
        scratch_shapes=[
            pltpu.VMEM((seq, lanes), F32),
            pltpu.VMEM((seq, lanes), F32),
            pltpu.VMEM((seq, lanes), F32),
            pltpu.VMEM((len(DIL_PAIRS), seq, lanes), F32),
            pltpu.VMEM((len(DIL_PAIRS), seq, lanes), F32),
            pltpu.VMEM((len(DIL_PAIRS), seq, lanes), F32),
            pltpu.VMEM((len(DIL_PAIRS) * heads * 2, BLK, 2 * BLK), F32),
        ],
        compiler_params=_params("parallel", "parallel"),
        name="dil_mixer",
    )(proj, proj, proj, gq, gk, bias, w_up, w_down)


def _rwkv_maps_kernel(*refs, has_vres):
    if has_vres:
        (p_ref, pp_ref, mu_ref, w0_ref, w2_ref, a0_ref, a2_ref, g2_ref, kk_ref, ka_ref, rk_ref,
         vf_ref, v0_ref, v1_ref, v2_ref,
         q_out, y1_out, m_out, gm_out, bonus_out, gate_out, *scratch) = refs
    else:
        (p_ref, pp_ref, mu_ref, w0_ref, w2_ref, a0_ref, a2_ref, g2_ref, kk_ref, ka_ref, rk_ref,
         q_out, y1_out, m_out, gm_out, bonus_out, gate_out, v_out, *scratch) = refs
    w = GROUP_WIDTH
    p = p_ref[0].astype(F32)
    row = lax.broadcasted_iota(jnp.int32, p.shape, 0)
    last_prev = pp_ref[0, PREV_ROWS - 1:PREV_ROWS, :].astype(F32)
    last_prev = jnp.where(pl.program_id(1) > 0, last_prev, 0.0)
    prev = jnp.where(row >= 1, pltpu.roll(p, 1, axis=0), last_prev)
    xs = p + (prev - p) * mu_ref[...]
    grows = WKV_GROUP * WKV_CHUNK
    for grp in range(p.shape[0] // grows):
        rs = slice(grp * grows, (grp + 1) * grows)
        x = xs[rs]
        r = x[:, 0:w]
        k = x[:, w:2 * w]
        v = x[:, 2 * w:3 * w]
        o = 3 * w
        wd = x[:, o:o + DECAY_LORA]
        ad = x[:, o + DECAY_LORA:o + DECAY_LORA + ICLR_LORA]
        gd = x[:, o + DECAY_LORA + ICLR_LORA:]

        z = -(w0_ref[...] + _mm(jnp.tanh(wd), w2_ref[...]))
        softplus = jnp.maximum(z, 0.0) + jnp.log(1.0 + jnp.exp(-jnp.abs(z)))
        logw = -softplus - 0.5
        lw = -jnp.exp(logw)
        a = _sigmoid(a0_ref[...] + _mm(ad, a2_ref[...]))
        gate_out[0, rs] = _mm(_sigmoid(gd), g2_ref[...])
        if has_vres:
            mix = _sigmoid(v0_ref[...] + _mm(_mm(v, v1_ref[...]), v2_ref[...]))
            v = v + (vf_ref[0, rs] - v) * mix
        else:
            v_out[0, rs] = v
        kk = k * kk_ref[...]
        ss = _head_sums(kk * kk, split=False)
        kk = kk * lax.rsqrt(jnp.maximum(ss, 1e-24))
        k = k * (1.0 + (a - 1.0) * ka_ref[...])
        bonus_out[0, rs] = _head_sums(r * k * rk_ref[...]) * v
        _chunk_maps(grp, r, lw, k, v, kk, kk * a, q_out.at[0], y1_out.at[0], m_out.at[0],
                    gm_out.at[0], *scratch)


def _rwkv_maps(proj, mu, w0, w2, a0, a2, g2, k_k, k_a, r_k, v_first, vres, ts=512):
    bsz, seq, _ = proj.shape
    w = GROUP_WIDTH
    nt = seq // ts
    has_vres = vres is not None

    def full(shape):
        return pl.BlockSpec(shape, lambda b, i: (0,) * len(shape))

    row = lambda a: a.reshape(1, -1)
    tile = pl.BlockSpec((1, ts, w), lambda b, i: (b, i, 0))
    in_specs = [
        pl.BlockSpec((pl.Element(1), pl.Element(ts), pl.Element(RWKV_IN_WIDTH)),
                     lambda b, i: (b, i * ts, OFF_RW)),
        pl.BlockSpec((pl.Element(1), pl.Element(PREV_ROWS), pl.Element(RWKV_IN_WIDTH)),
                     lambda b, i: (b, jnp.maximum(i * (ts // PREV_ROWS) - 1, 0) * PREV_ROWS, OFF_RW)),
        full((1, RWKV_IN_WIDTH)), full((1, w)), full((DECAY_LORA, w)), full((1, w)),
        full((ICLR_LORA, w)), full((GATE_LORA, w)), full((1, w)), full((1, w)), full((1, w)),
    ]
    args = [proj, proj, row(mu), row(w0), w2, row(a0), a2, g2, row(k_k), row(k_a), row(r_k)]
    if has_vres:
        v0, v1, v2 = vres
        in_specs += [tile, full((1, w)), full(v1.shape), full(v2.shape)]
        args += [v_first, row(v0), v1, v2]
    f32 = jax.ShapeDtypeStruct((bsz, seq, w), F32)
    bf16 = jax.ShapeDtypeStruct((bsz, seq, w), BF16)
    out_shape = [bf16, f32, bf16, f32, f32, f32] + ([] if has_vres else [f32])
    return pl.pallas_call(
        functools.partial(_rwkv_maps_kernel, has_vres=has_vres),
        grid=(bsz, nt),
        in_specs=in_specs,
        out_specs=[tile] * len(out_shape),
        out_shape=out_shape,
        scratch_shapes=[pltpu.VMEM((ts, w), dt) for dt in (BF16, F32, BF16, BF16, F32, BF16, BF16)],
        compiler_params=_params("parallel", "parallel"),
        name="rwkv_chunk_maps",
    )(*args)


def _chunk_maps(grp, r, lw, k, v, kk, b, q_out, y1_out, m_out, g_out,
                at_ref, rt_ref, bt_ref, kt_ref, dec_ref, vb_ref, rb_ref):
    c = WKV_CHUNK
    n = HEAD_DIM
    rows = r.shape[0]
    rs = slice(grp * rows, (grp + 1) * rows)

    row = lax.broadcasted_iota(jnp.int32, (rows, rows), 0)
    col = lax.broadcasted_iota(jnp.int32, (rows, rows), 1)
    tri = ((row >= col) & ((row // c) == (col // c))).astype(BF16)
    lw_hi, lw_lo = _split_bf16(lw)
    cum = (jnp.dot(tri, lw_hi, preferred_element_type=F32)
           + jnp.dot(tri, lw_lo, preferred_element_type=F32))
    e_pos = jnp.exp(cum)
    e_neg = jnp.exp(-cum)
    r_t = r * e_pos
    at_ref[rs] = (-kk * jnp.exp(cum - lw)).astype(BF16)
    rt_ref[rs] = r_t
    rb_ref[rs] = r_t.astype(BF16)
    bt_ref[rs] = (b * e_neg).astype(BF16)
    kt_ref[rs] = (k * e_neg).astype(BF16)
    dec_ref[rs] = e_pos
    vb_ref[rs] = v.astype(BF16)

    crow = lax.broadcasted_iota(jnp.int32, (c, c), 0)
    ccol = lax.broadcasted_iota(jnp.int32, (c, c), 1)
    lower = crow >= ccol
    strict = crow > ccol
    eye = (crow == ccol).astype(F32)
    wrow = lax.broadcasted_iota(jnp.int32, (c, 2 * c), 0)
    wcol = lax.broadcasted_iota(jnp.int32, (c, 2 * c), 1)
    strict_left = wrow > wcol
    right = wcol >= c
    eye_right = (wcol == wrow + c).astype(F32)

    def tile(ref, j, h):
        return ref[j * c:(j + 1) * c, h * n:(h + 1) * n]

    _chunk_group(range(grp * WKV_GROUP, (grp + 1) * WKV_GROUP), tile, at_ref, rt_ref, bt_ref,
                 kt_ref, dec_ref, vb_ref, rb_ref, q_out, y1_out, m_out, g_out,
                 lower, strict, eye, strict_left, right, eye_right)


def _chunk_group(chunks, tile, at_ref, rt_ref, bt_ref, kt_ref, dec_ref, vb_ref, rb_ref,
                 q_out, y1_out, m_out, g_out, lower, strict, eye, strict_left, right, eye_right):
    c = WKV_CHUNK
    n = HEAD_DIM
    pairs = [(j, h) for j in chunks for h in range(N_HEADS)]
    ah = [tile(at_ref, j, h) for j, h in pairs]
    rh = [tile(rt_ref, j, h) for j, h in pairs]
    bh = [tile(bt_ref, j, h) for j, h in pairs]
    kh = [tile(kt_ref, j, h) for j, h in pairs]
    vh = [tile(vb_ref, j, h) for j, h in pairs]
    dh = [dec_ref[(j + 1) * c - 1:(j + 1) * c, h * n:(h + 1) * n] for j, h in pairs]
    idx = range(len(pairs))
    ar = [jnp.concatenate([ah[i], tile(rb_ref, *pairs[i])], axis=0) for i in idx]
    bk = [jnp.concatenate([bh[i], kh[i]], axis=0) for i in idx]
    aa = [_mm_nt(ar[i], bk[i]) for i in idx]
    a_kr = [jnp.concatenate([jnp.where(strict, aa[i][:c, c:], 0.0),
                             jnp.where(lower, aa[i][c:, c:], 0.0)], axis=0).astype(BF16)
            for i in idx]
    a_rb = [jnp.where(lower, aa[i][c:, :c], 0.0).astype(BF16) for i in idx]
    ps = [jnp.where(strict_left, aa[i][:c, :], 0.0) + eye_right for i in idx]
    for _ in range(int(math.log2(c))):
        ps = [_mm(ps[i][:, :c], ps[i]) + jnp.where(right, ps[i], 0.0) for i in idx]
    t = [ps[i][:, c:].astype(BF16) for i in idx]
    av = [_mm(a_kr[i], vh[i]) for i in idx]
    wu = [_mm(t[i], jnp.concatenate([ah[i], av[i][:c].astype(BF16)], axis=1)).astype(BF16)
          for i in idx]
    aw = [_mm(a_rb[i], wu[i]) for i in idx]
    q = [rh[i] + aw[i][:, :n] for i in idx]
    y1 = [aw[i][:, n:] + av[i][c:] for i in idx]
    zero = jnp.zeros((c, n), BF16)
    mg = [_mm_tn(jnp.concatenate([wu[i], jnp.concatenate([zero, vh[i]], axis=1)], axis=0), bk[i])
          for i in idx]
    m = [(eye + mg[i][:n]) * dh[i] for i in idx]
    g = [mg[i][n:] * dh[i] for i in idx]
    for j in chunks:
        sel = slice((j - chunks[0]) * N_HEADS, (j - chunks[0] + 1) * N_HEADS)
        q_out[j * c:(j + 1) * c, :] = jnp.concatenate(q[sel], axis=-1).astype(q_out.dtype)
        y1_out[j * c:(j + 1) * c, :] = jnp.concatenate(y1[sel], axis=-1)
        m_out[j * c:(j + 1) * c, :] = jnp.concatenate(m[sel], axis=-1).astype(m_out.dtype)
        g_out[j * c:(j + 1) * c, :] = jnp.concatenate(g[sel], axis=-1)


def _wkv_scan_kernel(q_ref, y1_ref, m_ref, gm_ref, bonus_ref, gate_ref, lng_ref, lnb_ref,
                     o_ref, st_ref):
    n = HEAD_DIM
    bsz = q_ref.shape[0]

    @pl.when(pl.program_id(0) == 0)
    def _():
        st_ref[...] = jnp.zeros(st_ref.shape, F32)

    c = WKV_CHUNK
    rows = q_ref.shape[1]
    chains = [(bi, h) for bi in range(bsz) for h in range(N_HEADS)]
    state = [st_ref[bi * N_HEADS + h] for bi, h in chains]
    entering = []
    for j in range(rows // c):
        rs = slice(j * c, (j + 1) * c)
        s_in = [s.astype(BF16) for s in state]
        entering.append(s_in)
        state = [_mm(s_in[i], m_ref[bi, rs, h * n:(h + 1) * n]) + gm_ref[bi, rs, h * n:(h + 1) * n]
                 for i, (bi, h) in enumerate(chains)]
    for i, (bi, h) in enumerate(chains):
        st_ref[bi * N_HEADS + h] = state[i]
    ys = [[_mm_nt(q_ref[bi, j * c:(j + 1) * c, h * n:(h + 1) * n], entering[j][i])
           for i, (bi, h) in enumerate(chains)] for j in range(rows // c)]
    y = jnp.concatenate(
        [jnp.concatenate([jnp.concatenate(ys[j][bi * N_HEADS:(bi + 1) * N_HEADS], axis=-1)
                          for j in range(rows // c)], axis=0) + y1_ref[bi]
         for bi in range(bsz)], axis=0)
    mean = _head_sums(y) * (1.0 / n)
    yc = y - mean
    var = _head_sums(yc * yc) * (1.0 / n)
    yn = yc * lax.rsqrt(var + LN_X_EPS) * lng_ref[...] + lnb_ref[...]
    for bi in range(bsz):
        o_ref[bi] = ((yn[bi * rows:(bi + 1) * rows] + bonus_ref[bi]) * gate_ref[bi]).astype(o_ref.dtype)


def _wkv_scan(q, y1, m, gm, bonus, gate, ln_g, ln_b, rows=4 * WKV_CHUNK):
    bsz, seq, w = y1.shape
    tile = pl.BlockSpec((bsz, rows, w), lambda ci: (0, ci, 0))
    vec = pl.BlockSpec((1, w), lambda ci: (0, 0))
    return pl.pallas_call(
        _wkv_scan_kernel,
        grid=(seq // rows,),
        in_specs=[tile] * 6 + [vec] * 2,
        out_specs=tile,
        out_shape=jax.ShapeDtypeStruct((bsz, seq, w), BF16),
        scratch_shapes=[pltpu.VMEM((bsz * N_HEADS, HEAD_DIM, HEAD_DIM), F32)],
        compiler_params=_params("arbitrary"),
        name="wkv_state_scan",
    )(q, y1, m, gm, bonus, gate, ln_g.reshape(1, w), ln_b.reshape(1, w))


def _wout_kernel(x_ref, y0_ref, y1_ref, y2_ref, y3_ref, w_ref, o_ref):
    acc = x_ref[...]
    for idx, y_ref in enumerate((y0_ref, y1_ref, y2_ref, y3_ref)):
        w = w_ref[0, idx * GROUP_WIDTH:(idx + 1) * GROUP_WIDTH, :].astype(BF16)
        acc = acc + jnp.dot(y_ref[...], w, preferred_element_type=F32)
    o_ref[...] = acc


def _wout(x2d, ys, w_stack, layer, tm=512):
    m, d = x2d.shape
    ytile = pl.BlockSpec((tm, GROUP_WIDTH), lambda i: (i, 0))
    xtile = pl.BlockSpec((tm, d), lambda i: (i, 0))
    wspec = pl.BlockSpec((1,) + w_stack.shape[1:], lambda i: (layer, 0, 0))
    return pl.pallas_call(
        _wout_kernel,
        grid=(m // tm,),
        in_specs=[xtile] + [ytile] * 4 + [wspec],
        out_specs=xtile,
        out_shape=jax.ShapeDtypeStruct((m, d), F32),
        compiler_params=_params("parallel", vmem_limit=FFN_VMEM_LIMIT),
        name="wout_residual",
    )(x2d, *ys, w_stack)


def _ffn_kernel(x_ref, g_ref, wu_ref, wd_ref, o_ref, h_ref):
    def down(u):
        act = jnp.square(jnp.maximum(u, 0.0)).astype(BF16)
        return jnp.dot(act, wd_ref[0].astype(BF16), preferred_element_type=F32)

    @pl.when(pl.program_id(1) == 0)
    def _():
        u = None
        for cols, h in _norm_chunks(x_ref, g_ref, h_ref, _rms_scale(x_ref)):
            part = jnp.dot(h, wu_ref[0, cols, :].astype(BF16), preferred_element_type=F32)
            u = part if u is None else u + part
        o_ref[...] = x_ref[...] + down(u)

    @pl.when(pl.program_id(1) > 0)
    def _():
        u = jnp.dot(h_ref[...], wu_ref[0].astype(BF16), preferred_element_type=F32)
        o_ref[...] += down(u)


def _ffn(x2d, gains, layer, wu, wd, tm=1024, tf=512):
    m, d = x2d.shape
    f = wu.shape[1]
    wu_stack = wu.reshape(1, d, f)
    wd_stack = wd.reshape(1, f, d)
    xtile = pl.BlockSpec((tm, d), lambda i, j: (i, 0))
    return pl.pallas_call(
        _ffn_kernel,
        grid=(m // tm, f // tf),
        in_specs=[
            xtile,
            pl.BlockSpec((1, 1, d), lambda i, j: (layer, 0, 0)),
            pl.BlockSpec((1, d, tf), lambda i, j: (0, 0, j)),
            pl.BlockSpec((1, tf, d), lambda i, j: (0, j, 0)),
        ],
        out_specs=xtile,
        out_shape=jax.ShapeDtypeStruct((m, d), F32),
        scratch_shapes=[pltpu.VMEM((tm, d), BF16)],
        compiler_params=_params("parallel", "arbitrary", vmem_limit=FFN_VMEM_LIMIT),
        name="ffn",
    )(x2d, gains.reshape(-1, 1, d), wu_stack, wd_stack)


def kernel(x, norm_mix, w_in, conv_w, swa_q_norm, swa_k_norm, swa_sink, dil_q_norm, dil_k_norm,
           rwkv_mu, decay_w0, decay_w2, iclr_a0, iclr_a2, gate_g2, k_k, k_a, r_k, ln_x_g, ln_x_b,
           vres_v0, vres_v1, vres_v2, w_out, norm_ffn, w_up, w_down, rel_bias):
    bsz, seq, d = x.shape
    depth = w_in.shape[0]
    bias = _bias_tiles(rel_bias)
    swa_bias = bias[0, :N_HEADS]
    dil_bias = bias[:, N_HEADS:]
    x2d = x.reshape(bsz * seq, d)
    v_first = None
    for layer in range(depth):
        proj = _norm_matmul(x2d, norm_mix, w_in, layer).reshape(bsz, seq, IN_WIDTH)
        y_swa, y_conv = _swa_conv_mixers(proj, swa_q_norm[layer], swa_k_norm[layer],
                                         swa_sink[layer], swa_bias, conv_w[layer])
        y_dil, wu_bf16, wd_bf16 = _dil_mixer(proj, dil_q_norm[layer], dil_k_norm[layer], dil_bias,
                                             w_up, w_down, layer)
        vres = None if layer == 0 else (vres_v0[layer - 1], vres_v1[layer - 1], vres_v2[layer - 1])
        maps = _rwkv_maps(
            proj, rwkv_mu[layer], decay_w0[layer], decay_w2[layer], iclr_a0[layer], iclr_a2[layer],
            gate_g2[layer], k_k[layer], k_a[layer], r_k[layer], v_first, vres)
        if layer == 0:
            v_first = maps[6]
        y_rwkv = _wkv_scan(*maps[:6], ln_x_g[layer], ln_x_b[layer])
        ys = [y.reshape(bsz * seq, GROUP_WIDTH) for y in (y_conv, y_swa, y_dil, y_rwkv)]
        x2d = _wout(x2d, ys, w_out, layer)
        x2d = _ffn(x2d, norm_ffn, layer, wu_bf16, wd_bf16)
    return x2d.reshape(bsz, seq, d)
```

```python
import functools
import math

import jax
import jax.numpy as jnp
from jax import lax
from jax.experimental import pallas as pl
from jax.experimental.pallas import tpu as pltpu

F32 = jnp.float32
BF16 = jnp.bfloat16

HEAD_DIM = 64
GROUP_WIDTH = 512
N_HEADS = GROUP_WIDTH // HEAD_DIM
SWA_KV_HEADS = 2
SWA_GROUP = N_HEADS // SWA_KV_HEADS
SWA_WINDOW = 128
DIL_PAIRS = ((128, 1), (512, 4), (2048, 16))
DECAY_LORA = 64
ICLR_LORA = 64
GATE_LORA = 128
RWKV_IN_WIDTH = 3 * GROUP_WIDTH + DECAY_LORA + ICLR_LORA + GATE_LORA
BLK = 128
NUM_BUCKETS = 32
BUCKET_MAX_DIST = 128
RMS_EPS = 1e-6
LN_X_EPS = 64e-5
NEG = -1e30
LOG2E = math.log2(math.e)
WKV_CHUNK = 64
WKV_GROUP = 4
DIL_TILES = 4
PREV_ROWS = 16
NORM_CHUNKS = 4
PREP_UNROLL = 4

OFF_CONV = 0
OFF_SWA_Q = OFF_CONV + 3 * GROUP_WIDTH
OFF_SWA_K = OFF_SWA_Q + GROUP_WIDTH
OFF_SWA_V = OFF_SWA_K + SWA_KV_HEADS * HEAD_DIM
OFF_DIL = OFF_SWA_V + SWA_KV_HEADS * HEAD_DIM
OFF_RW = OFF_DIL + 3 * GROUP_WIDTH
IN_WIDTH = OFF_RW + RWKV_IN_WIDTH

V7X_VMEM_BYTES = 64 * 1024 * 1024
VMEM_LIMIT = 48 * 1024 * 1024
FFN_VMEM_LIMIT = V7X_VMEM_BYTES - 6 * 1024 * 1024


def _params(*sem, vmem_limit=VMEM_LIMIT):
    return pltpu.CompilerParams(dimension_semantics=sem, vmem_limit_bytes=vmem_limit)


def _mm(a, b):
    return jnp.dot(a.astype(BF16), b.astype(BF16), preferred_element_type=F32)


def _mm_nt(a, b):
    return lax.dot_general(a.astype(BF16), b.astype(BF16), (((1,), (1,)), ((), ())),
                           preferred_element_type=F32)


def _mm_tn(a, b):
    return lax.dot_general(a.astype(BF16), b.astype(BF16), (((0,), (0,)), ((), ())),
                           preferred_element_type=F32)


def _split_bf16(x):
    hi = x.astype(BF16)
    lo = (x - hi.astype(F32)).astype(BF16)
    return hi, lo


def _head_sums(x, split=True):
    lanes = 128
    r = lax.broadcasted_iota(jnp.int32, (lanes, lanes), 0) // HEAD_DIM
    c = lax.broadcasted_iota(jnp.int32, (lanes, lanes), 1) // HEAD_DIM
    bd = (r == c).astype(BF16)
    parts = _split_bf16(x) if split else (x.astype(BF16),)
    cols = []
    for j in range(x.shape[-1] // lanes):
        sl = slice(j * lanes, (j + 1) * lanes)
        cols.append(sum(jnp.dot(part[:, sl], bd, preferred_element_type=F32) for part in parts))
    return cols[0] if len(cols) == 1 else jnp.concatenate(cols, axis=-1)


def _sigmoid(z):
    return 1.0 / (1.0 + jnp.exp(-z))


def _rms_scale(x_ref):
    x = x_ref[...]
    return lax.rsqrt(jnp.mean(x * x, axis=-1, keepdims=True) + RMS_EPS)


def _norm_chunks(x_ref, g_ref, h_ref, scale):
    k = x_ref.shape[1]
    kc = k // NORM_CHUNKS
    for c in range(NORM_CHUNKS):
        cols = slice(c * kc, (c + 1) * kc)
        h = (x_ref[:, cols] * scale * g_ref[0, :, cols]).astype(BF16)
        h_ref[:, cols] = h
        yield cols, h


def _norm_matmul_kernel(x_ref, g_ref, w_ref, o_ref, h_ref):
    @pl.when(pl.program_id(1) == 0)
    def _():
        for _ in _norm_chunks(x_ref, g_ref, h_ref, _rms_scale(x_ref)):
            pass

    o_ref[...] = jnp.dot(h_ref[...], w_ref[0].astype(BF16),
                         preferred_element_type=F32).astype(o_ref.dtype)


def _norm_matmul(x2d, gains, w_stack, layer, tm=2048, tn=512):
    m, k = x2d.shape
    n = w_stack.shape[2]
    assert n % tn == 0 and m % tm == 0
    return pl.pallas_call(
        _norm_matmul_kernel,
        grid=(m // tm, n // tn),
        in_specs=[
            pl.BlockSpec((tm, k), lambda i, j: (i, 0)),
            pl.BlockSpec((1, 1, k), lambda i, j: (layer, 0, 0)),
            pl.BlockSpec((1, k, tn), lambda i, j: (layer, 0, j)),
        ],
        out_specs=pl.BlockSpec((tm, tn), lambda i, j: (i, j)),
        out_shape=jax.ShapeDtypeStruct((m, n), BF16),
        scratch_shapes=[pltpu.VMEM((tm, k), BF16)],
        compiler_params=_params("parallel", "arbitrary", vmem_limit=FFN_VMEM_LIMIT),
        name="norm_matmul",
    )(x2d, gains.reshape(-1, 1, k), w_stack)


def _t5_bucket(dist):
    dist = jnp.maximum(dist, 0)
    max_exact = NUM_BUCKETS // 2
    scaled = (jnp.log(jnp.maximum(dist, 1).astype(F32) / max_exact)
              / math.log(BUCKET_MAX_DIST / max_exact))
    large = max_exact + (scaled * (NUM_BUCKETS - max_exact)).astype(jnp.int32)
    large = jnp.minimum(large, NUM_BUCKETS - 1)
    return jnp.where(dist < max_exact, dist, large)


def _bias_kernel(bucket_ref, table_ref, o_ref):
    bucket = bucket_ref[0]

    def head(h, carry):
        g = jnp.zeros(bucket.shape, F32)
        for b in range(NUM_BUCKETS):
            g = jnp.where(bucket == b, table_ref[b, h], g)
        rows = jnp.broadcast_to(g[0:1], (BLK, 2 * BLK))
        o_ref[0, h] = pltpu.roll(rows, 0, axis=1, stride=1, stride_axis=0)
        return carry

    lax.fori_loop(0, o_ref.shape[1], head, 0)


def _bias_tiles(rel_bias):
    lag_dist = BLK - jnp.arange(2 * BLK)
    buckets = jnp.stack([_t5_bucket(lag_dist * r) for _, r in DIL_PAIRS]).astype(jnp.int32)
    buckets = jnp.broadcast_to(buckets[:, None, :], (len(DIL_PAIRS), 8, 2 * BLK))
    nh = rel_bias.shape[1]
    return pl.pallas_call(
        _bias_kernel,
        grid=(len(DIL_PAIRS),),
        in_specs=[
            pl.BlockSpec((1, 8, 2 * BLK), lambda s: (s, 0, 0)),
            pl.BlockSpec(memory_space=pltpu.SMEM),
        ],
        out_specs=pl.BlockSpec((1, nh, BLK, 2 * BLK), lambda s: (s, 0, 0, 0)),
        out_shape=jax.ShapeDtypeStruct((len(DIL_PAIRS), nh, BLK, 2 * BLK), F32),
        compiler_params=_params("arbitrary"),
        name="bias_tiles",
    )(buckets, rel_bias)


def _short_conv(b_ref, c_ref, u_ref, w_ref, o_ref):
    lanes = 128
    for j in range(b_ref.shape[2] // lanes):
        cols = slice(j * lanes, (j + 1) * lanes)
        z = c_ref[0, :, cols].astype(F32) * u_ref[0, :, cols].astype(F32)
        row = lax.broadcasted_iota(jnp.int32, z.shape, 0)
        z1 = jnp.where(row >= 1, pltpu.roll(z, 1, axis=0), 0.0)
        z2 = jnp.where(row >= 2, pltpu.roll(z, 2, axis=0), 0.0)
        w = w_ref[:, cols]
        y = z2 * w[0:1, :] + z1 * w[1:2, :] + z * w[2:3, :]
        o_ref[0, :, cols] = (b_ref[0, :, cols].astype(F32) * y).astype(o_ref.dtype)


def _band_mask(max_dist):
    a = lax.broadcasted_iota(jnp.int32, (BLK, 2 * BLK), 0)
    b = lax.broadcasted_iota(jnp.int32, (BLK, 2 * BLK), 1)
    dist = BLK + a - b
    return (dist >= 0) & (dist <= max_dist), b


def _attend(qs, kws, vws, biases, normalize=True):
    idx = range(len(qs))
    s = [lax.dot_general(qs[i], kws[i], (((1,), (1,)), ((), ())), preferred_element_type=F32)
         + biases[i] for i in idx]
    m = [jnp.max(s[i], axis=-1, keepdims=True) for i in idx]
    p = [jnp.exp2(s[i] - m[i]) for i in idx]
    den = [jnp.sum(p[i], axis=-1, keepdims=True) for i in idx]
    o = [jnp.dot(p[i].astype(BF16), vws[i], preferred_element_type=F32) for i in idx]
    if normalize:
        o = [o[i] / den[i] for i in idx]
    return o, m, den


def _head_rms(x, gain):
    ms = _head_sums(x * x, split=False) * (1.0 / HEAD_DIM)
    return x * lax.rsqrt(ms + RMS_EPS) * gain


def _swa_kernel(q_ref, k_ref, v_ref, qg_ref, kg_ref, sink_ref, bias_ref,
                cb_ref, cc_ref, cu_ref, cw_ref, o_ref, oc_ref,
                qn_ref, kn_ref, vb_ref, bm_ref):
    seq = q_ref.shape[1]
    nb = seq // BLK
    kvw = SWA_KV_HEADS * HEAD_DIM
    scale = HEAD_DIM ** -0.5 * LOG2E

    _short_conv(cb_ref, cc_ref, cu_ref, cw_ref, oc_ref)

    kn_ref[0:BLK, :] = jnp.zeros((BLK, 2 * kvw), BF16)
    vb_ref[0:BLK, :] = jnp.zeros((BLK, 2 * kvw), BF16)
    lane_half = lax.broadcasted_iota(jnp.int32, (1, kvw), 1) // HEAD_DIM

    def both_halves(x):
        xr = pltpu.roll(x, HEAD_DIM, axis=1)
        return jnp.concatenate([jnp.where(lane_half == 0, x, xr),
                                jnp.where(lane_half == 0, xr, x)], axis=-1)

    def prep(i, carry):
        r0 = pl.multiple_of(i * BLK, BLK)
        q = q_ref[0, pl.ds(r0, BLK), :].astype(F32)
        qn_ref[pl.ds(r0, BLK), :] = (_head_rms(q, qg_ref[...]) * scale).astype(BF16)
        k = _head_rms(k_ref[0, pl.ds(r0, BLK), :].astype(F32), kg_ref[...])
        kn_ref[pl.ds(r0 + BLK, BLK), :] = both_halves(k).astype(BF16)
        v = v_ref[0, pl.ds(r0, BLK), :].astype(F32)
        vb_ref[pl.ds(r0 + BLK, BLK), :] = both_halves(v).astype(BF16)
        return carry

    lax.fori_loop(0, nb, prep, 0, unroll=PREP_UNROLL)

    assert SWA_WINDOW <= BLK
    band, kcol = _band_mask(SWA_WINDOW - 1)
    for h in range(N_HEADS):
        sink = sink_ref[h] * LOG2E
        bias = bias_ref[h] * LOG2E
        bm_ref[h] = jnp.where(kcol == 0, sink, jnp.where(band & (kcol >= BLK), bias, NEG))
        bm_ref[N_HEADS + h] = jnp.where(kcol == 0, sink, jnp.where(band, bias, NEG))
    first_key = lax.broadcasted_iota(jnp.int32, (2 * BLK, kvw), 0) == 0

    def block(i, carry):
        r0 = pl.multiple_of(i * BLK, BLK)
        later = jnp.minimum(i, 1) * N_HEADS
        qs, kws, vws, bms = [], [], [], []
        for hk in range(SWA_KV_HEADS):
            kw = kn_ref[pl.ds(r0, 2 * BLK), hk * kvw:(hk + 1) * kvw]
            vw = vb_ref[pl.ds(r0, 2 * BLK), hk * kvw:(hk + 1) * kvw]
            kw = jnp.where(first_key, jnp.zeros_like(kw), kw)
            vw = jnp.where(first_key, jnp.zeros_like(vw), vw)
            for g in range(SWA_GROUP):
                h = hk * SWA_GROUP + g
                q2 = qn_ref[pl.ds(r0, BLK), (h // 2) * kvw:(h // 2 + 1) * kvw]
                qs.append(jnp.where(lane_half == h % 2, q2, jnp.zeros_like(q2)))
                kws.append(kw)
                vws.append(vw)
                bms.append(bm_ref[later + h])
        outs, _, _ = _attend(qs, kws, vws, bms)
        pairs = [jnp.where(lane_half == 0, outs[h], outs[h + 1]) for h in range(0, N_HEADS, 2)]
        o_ref[0, pl.ds(r0, BLK), :] = jnp.concatenate(pairs, axis=-1).astype(o_ref.dtype)
        return carry

    lax.fori_loop(0, nb, block, 0)


def _swa_conv_mixers(proj, q_gain, k_gain, sink, bias, conv_w):
    bsz, seq, _ = proj.shape
    kvw = SWA_KV_HEADS * HEAD_DIM
    q_gain_t = jnp.tile(q_gain, N_HEADS).reshape(1, GROUP_WIDTH)
    k_gain_t = jnp.tile(k_gain, SWA_KV_HEADS).reshape(1, kvw)
    wide = lambda off: pl.BlockSpec((1, seq, GROUP_WIDTH), lambda b: (b, 0, off // GROUP_WIDTH))
    out = jax.ShapeDtypeStruct((bsz, seq, GROUP_WIDTH), BF16)
    return pl.pallas_call(
        _swa_kernel,
        grid=(bsz,),
        in_specs=[
            wide(OFF_SWA_Q),
            pl.BlockSpec((1, seq, kvw), lambda b: (b, 0, OFF_SWA_K // kvw)),
            pl.BlockSpec((1, seq, kvw), lambda b: (b, 0, OFF_SWA_V // kvw)),
            pl.BlockSpec((1, GROUP_WIDTH), lambda b: (0, 0)),
            pl.BlockSpec((1, kvw), lambda b: (0, 0)),
            pl.BlockSpec(memory_space=pltpu.SMEM),
            pl.BlockSpec((N_HEADS, BLK, 2 * BLK), lambda b: (0, 0, 0)),
            wide(OFF_CONV), wide(OFF_CONV + GROUP_WIDTH), wide(OFF_CONV + 2 * GROUP_WIDTH),
            pl.BlockSpec((3, GROUP_WIDTH), lambda b: (0, 0)),
        ],
        out_specs=[pl.BlockSpec((1, seq, GROUP_WIDTH), lambda b: (b, 0, 0))] * 2,
        out_shape=[out, out],
        scratch_shapes=[
            pltpu.VMEM((seq, GROUP_WIDTH), BF16),
            pltpu.VMEM((seq + BLK, 2 * kvw), BF16),
            pltpu.VMEM((seq + BLK, 2 * kvw), BF16),
            pltpu.VMEM((2 * N_HEADS, BLK, 2 * BLK), F32),
        ],
        compiler_params=_params("parallel"),
        name="swa_conv_mixers",
    )(proj, proj, proj, q_gain_t, k_gain_t, sink, bias, proj, proj, proj, conv_w)


def _dil_kernel(q_ref, k_ref, v_ref, qg_ref, kg_ref, bias_ref, wu_ref, wd_ref,
                o_ref, wu_out, wd_out,
                qn_ref, kn_ref, vn_ref, ob_ref, mb_ref, db_ref, bm_ref):
    wu_out[...] = wu_ref[0].astype(BF16)
    wd_out[...] = wd_ref[0].astype(BF16)
    seq = q_ref.shape[1]
    lanes = q_ref.shape[2]
    heads = lanes // HEAD_DIM
    scale = HEAD_DIM ** -0.5 * LOG2E

    def prep(i, carry):
        r0 = pl.multiple_of(i * BLK, BLK)
        q = q_ref[0, pl.ds(r0, BLK), :].astype(F32)
        k = k_ref[0, pl.ds(r0, BLK), :].astype(F32)
        qn_ref[pl.ds(r0, BLK), :] = _head_rms(q, qg_ref[...]) * scale
        kn_ref[pl.ds(r0, BLK), :] = _head_rms(k, kg_ref[...])
        vn_ref[pl.ds(r0, BLK), :] = v_ref[0, pl.ds(r0, BLK), :].astype(F32)
        return carry

    lax.fori_loop(0, seq // BLK, prep, 0, unroll=PREP_UNROLL)

    for br, (window, r) in enumerate(DIL_PAIRS):
        band, kcol = _band_mask(window // r)
        for h in range(heads):
            bias = bias_ref[br, h] * LOG2E
            bm_ref[(br * heads + h) * 2] = jnp.where(band & (kcol >= BLK), bias, NEG)
            bm_ref[(br * heads + h) * 2 + 1] = jnp.where(band, bias, NEG)

    lane_head = lax.broadcasted_iota(jnp.int32, (1, lanes), 1) // HEAD_DIM

    for br, (window, r) in enumerate(DIL_PAIRS):
        nb = seq // r // BLK

        def blocks(it, carry, br=br, r=r, nb=nb):
            qs, kws, vws, bms, curs = [], [], [], [], []
            for u in range(DIL_TILES):
                t = it * DIL_TILES + u
                c = t // nb
                i = t - c * nb
                cur = c + i * (BLK * r)
                prev = jnp.maximum(cur - BLK * r, c)
                later = jnp.minimum(i, 1)

                def rows(ref, start):
                    if r == 1:
                        return ref[pl.ds(start, BLK), :]
                    return ref[pl.ds(start, BLK, stride=r), :]

                q = rows(qn_ref, cur).astype(BF16)
                if nb == 1:
                    kw = rows(kn_ref, cur).astype(BF16)
                    vw = rows(vn_ref, cur).astype(BF16)
                else:
                    kw = jnp.concatenate([rows(kn_ref, prev), rows(kn_ref, cur)],
                                         axis=0).astype(BF16)
                    vw = jnp.concatenate([rows(vn_ref, prev), rows(vn_ref, cur)],
                                         axis=0).astype(BF16)
                curs.append(cur)
                for h in range(heads):
                    qs.append(jnp.where(lane_head == h, q, jnp.zeros_like(q)))
                    kws.append(kw)
                    vws.append(vw)
                    if nb == 1:
                        bms.append(bm_ref[(br * heads + h) * 2, :, BLK:])
                    else:
                        bms.append(bm_ref[(br * heads + h) * 2 + later])
            outs, ms, dens = _attend(qs, kws, vws, bms, normalize=False)
            for u in range(DIL_TILES):
                o_all, m_all, d_all = outs[u * heads], ms[u * heads], dens[u * heads]
                for h in range(1, heads):
                    o_all = jnp.where(lane_head == h, outs[u * heads + h], o_all)
                    m_all = jnp.where(lane_head == h, ms[u * heads + h], m_all)
                    d_all = jnp.where(lane_head == h, dens[u * heads + h], d_all)
                if r == 1:
                    rows = pl.ds(curs[u], BLK)
                else:
                    rows = pl.ds(curs[u], BLK, stride=r)
                ob_ref[br, rows, :] = o_all
                mb_ref[br, rows, :] = m_all
                db_ref[br, rows, :] = d_all
            return carry

        lax.fori_loop(0, r * nb // DIL_TILES, blocks, 0)

    def combine(i, carry):
        r0 = pl.multiple_of(i * BLK, BLK)
        rows = pl.ds(r0, BLK)
        m0, m1, m2 = mb_ref[0, rows, :], mb_ref[1, rows, :], mb_ref[2, rows, :]
        m = jnp.maximum(jnp.maximum(m0, m1), m2)
        e0, e1, e2 = jnp.exp2(m0 - m), jnp.exp2(m1 - m), jnp.exp2(m2 - m)
        num = e0 * ob_ref[0, rows, :] + e1 * ob_ref[1, rows, :] + e2 * ob_ref[2, rows, :]
        den = e0 * db_ref[0, rows, :] + e1 * db_ref[1, rows, :] + e2 * db_ref[2, rows, :]
        o_ref[0, rows, :] = (num / den).astype(o_ref.dtype)
        return carry

    lax.fori_loop(0, seq // BLK, combine, 0, unroll=PREP_UNROLL)


def _dil_mixer(proj, q_gain, k_gain, bias, w_up, w_down, layer):
    bsz, seq, _ = proj.shape
    lanes = 128
    heads = lanes // HEAD_DIM
    nblk = GROUP_WIDTH // lanes
    base = OFF_DIL // lanes
    gq = jnp.tile(q_gain, heads).reshape(1, lanes)
    gk = jnp.tile(k_gain, heads).reshape(1, lanes)
    _, d, f = w_up.shape
    steps = bsz * nblk
    assert d % steps == 0 and f % steps == 0

    def col(seg):
        return pl.BlockSpec((1, seq, lanes), lambda b, j: (b, 0, base + seg * nblk + j))

    return pl.pallas_call(
        _dil_kernel,
        grid=(bsz, nblk),
        in_specs=[
            col(0), col(1), col(2),
            pl.BlockSpec((1, lanes), lambda b, j: (0, 0)),
            pl.BlockSpec((1, lanes), lambda b, j: (0, 0)),
            pl.BlockSpec((len(DIL_PAIRS), heads, BLK, 2 * BLK), lambda b, j: (0, j, 0, 0)),
            pl.BlockSpec((1, d // steps, f), lambda b, j: (layer, b * nblk + j, 0)),
            pl.BlockSpec((1, f // steps, d), lambda b, j: (layer, b * nblk + j, 0)),
        ],
        out_specs=[
            pl.BlockSpec((1, seq, lanes), lambda b, j: (b, 0, j)),
            pl.BlockSpec((d // steps, f), lambda b, j: (b * nblk + j, 0)),
            pl.BlockSpec((f // steps, d), lambda b, j: (b * nblk + j, 0)),
        ],
        out_shape=[
            jax.ShapeDtypeStruct((bsz, seq, GROUP_WIDTH), BF16),
            jax.ShapeDtypeStruct((d, f), BF16),
            jax.ShapeDtypeStruct((f, d), BF16),
        ],
        scratch_shapes=[
            pltpu.VMEM((seq, lanes), F32),
            pltpu.VMEM((seq, lanes), F32),
            pltpu.VMEM((seq, lanes), F32),
            pltpu.VMEM((len(DIL_PAIRS), seq, lanes), F32),
            pltpu.VMEM((len(DIL_PAIRS), seq, lanes), F32),
            pltpu.VMEM((len(DIL_PAIRS), seq, lanes), F32),
            pltpu.VMEM((len(DIL_PAIRS) * heads * 2, BLK, 2 * BLK), F32),
        ],
        compiler_params=_params("parallel", "parallel"),
        name="dil_mixer",
    )(proj, proj, proj, gq, gk, bias, w_up, w_down)


def _rwkv_maps_kernel(*refs, has_vres):
    if has_vres:
        (p_ref, pp_ref, mu_ref, w0_ref, w2_ref, a0_ref, a2_ref, g2_ref, kk_ref, ka_ref, rk_ref,
         vf_ref, v0_ref, v1_ref, v2_ref,
         q_out, y1_out, m_out, gm_out, bonus_out, gate_out, *scratch) = refs
    else:
        (p_ref, pp_ref, mu_ref, w0_ref, w2_ref, a0_ref, a2_ref, g2_ref, kk_ref, ka_ref, rk_ref,
         q_out, y1_out, m_out, gm_out, bonus_out, gate_out, v_out, *scratch) = refs
    w = GROUP_WIDTH
    p = p_ref[0].astype(F32)
    row = lax.broadcasted_iota(jnp.int32, p.shape, 0)
    last_prev = pp_ref[0, PREV_ROWS - 1:PREV_ROWS, :].astype(F32)
    last_prev = jnp.where(pl.program_id(1) > 0, last_prev, 0.0)
    prev = jnp.where(row >= 1, pltpu.roll(p, 1, axis=0), last_prev)
    xs = p + (prev - p) * mu_ref[...]
    grows = WKV_GROUP * WKV_CHUNK
    for grp in range(p.shape[0] // grows):
        rs = slice(grp * grows, (grp + 1) * grows)
        x = xs[rs]
        r = x[:, 0:w]
        k = x[:, w:2 * w]
        v = x[:, 2 * w:3 * w]
        o = 3 * w
        wd = x[:, o:o + DECAY_LORA]
        ad = x[:, o + DECAY_LORA:o + DECAY_LORA + ICLR_LORA]
        gd = x[:, o + DECAY_LORA + ICLR_LORA:]

        z = -(w0_ref[...] + _mm(jnp.tanh(wd), w2_ref[...]))
        softplus = jnp.maximum(z, 0.0) + jnp.log(1.0 + jnp.exp(-jnp.abs(z)))
        logw = -softplus - 0.5
        lw = -jnp.exp(logw)
        a = _sigmoid(a0_ref[...] + _mm(ad, a2_ref[...]))
        gate_out[0, rs] = _mm(_sigmoid(gd), g2_ref[...])
        if has_vres:
            mix = _sigmoid(v0_ref[...] + _mm(_mm(v, v1_ref[...]), v2_ref[...]))
            v = v + (vf_ref[0, rs] - v) * mix
        else:
            v_out[0, rs] = v
        kk = k * kk_ref[...]
        ss = _head_sums(kk * kk, split=False)
        kk = kk * lax.rsqrt(jnp.maximum(ss, 1e-24))
        k = k * (1.0 + (a - 1.0) * ka_ref[...])
        bonus_out[0, rs] = _head_sums(r * k * rk_ref[...]) * v
        _chunk_maps(grp, r, lw, k, v, kk, kk * a, q_out.at[0], y1_out.at[0], m_out.at[0],
                    gm_out.at[0], *scratch)


def _rwkv_maps(proj, mu, w0, w2, a0, a2, g2, k_k, k_a, r_k, v_first, vres, ts=512):
    bsz, seq, _ = proj.shape
    w = GROUP_WIDTH
    nt = seq // ts
    has_vres = vres is not None

    def full(shape):
        return pl.BlockSpec(shape, lambda b, i: (0,) * len(shape))

    row = lambda a: a.reshape(1, -1)
    tile = pl.BlockSpec((1, ts, w), lambda b, i: (b, i, 0))
    in_specs = [
        pl.BlockSpec((pl.Element(1), pl.Element(ts), pl.Element(RWKV_IN_WIDTH)),
                     lambda b, i: (b, i * ts, OFF_RW)),
        pl.BlockSpec((pl.Element(1), pl.Element(PREV_ROWS), pl.Element(RWKV_IN_WIDTH)),
                     lambda b, i: (b, jnp.maximum(i * (ts // PREV_ROWS) - 1, 0) * PREV_ROWS, OFF_RW)),
        full((1, RWKV_IN_WIDTH)), full((1, w)), full((DECAY_LORA, w)), full((1, w)),
        full((ICLR_LORA, w)), full((GATE_LORA, w)), full((1, w)), full((1, w)), full((1, w)),
    ]
    args = [proj, proj, row(mu), row(w0), w2, row(a0), a2, g2, row(k_k), row(k_a), row(r_k)]
    if has_vres:
        v0, v1, v2 = vres
        in_specs += [tile, full((1, w)), full(v1.shape), full(v2.shape)]
        args += [v_first, row(v0), v1, v2]
    f32 = jax.ShapeDtypeStruct((bsz, seq, w), F32)
    bf16 = jax.ShapeDtypeStruct((bsz, seq, w), BF16)
    out_shape = [bf16, f32, bf16, f32, f32, f32] + ([] if has_vres else [f32])
    return pl.pallas_call(
        functools.partial(_rwkv_maps_kernel, has_vres=has_vres),
        grid=(bsz, nt),
        in_specs=in_specs,
        out_specs=[tile] * len(out_shape),
        out_shape=out_shape,
        scratch_shapes=[pltpu.VMEM((ts, w), dt) for dt in (BF16, F32, BF16, BF16, F32, BF16, BF16)],
        compiler_params=_params("parallel", "parallel"),
        name="rwkv_chunk_maps",
    )(*args)


def _chunk_maps(grp, r, lw, k, v, kk, b, q_out, y1_out, m_out, g_out,
                at_ref, rt_ref, bt_ref, kt_ref, dec_ref, vb_ref, rb_ref):
    c = WKV_CHUNK
    n = HEAD_DIM
    rows = r.shape[0]
    rs = slice(grp * rows, (grp + 1) * rows)

    row = lax.broadcasted_iota(jnp.int32, (rows, rows), 0)
    col = lax.broadcasted_iota(jnp.int32, (rows, rows), 1)
    tri = ((row >= col) & ((row // c) == (col // c))).astype(BF16)
    lw_hi, lw_lo = _split_bf16(lw)
    cum = (jnp.dot(tri, lw_hi, preferred_element_type=F32)
           + jnp.dot(tri, lw_lo, preferred_element_type=F32))
    e_pos = jnp.exp(cum)
    e_neg = jnp.exp(-cum)
    r_t = r * e_pos
    at_ref[rs] = (-kk * jnp.exp(cum - lw)).astype(BF16)
    rt_ref[rs] = r_t
    rb_ref[rs] = r_t.astype(BF16)
    bt_ref[rs] = (b * e_neg).astype(BF16)
    kt_ref[rs] = (k * e_neg).astype(BF16)
    dec_ref[rs] = e_pos
    vb_ref[rs] = v.astype(BF16)

    crow = lax.broadcasted_iota(jnp.int32, (c, c), 0)
    ccol = lax.broadcasted_iota(jnp.int32, (c, c), 1)
    lower = crow >= ccol
    strict = crow > ccol
    eye = (crow == ccol).astype(F32)
    wrow = lax.broadcasted_iota(jnp.int32, (c, 2 * c), 0)
    wcol = lax.broadcasted_iota(jnp.int32, (c, 2 * c), 1)
    strict_left = wrow > wcol
    right = wcol >= c
    eye_right = (wcol == wrow + c).astype(F32)

    def tile(ref, j, h):
        return ref[j * c:(j + 1) * c, h * n:(h + 1) * n]

    _chunk_group(range(grp * WKV_GROUP, (grp + 1) * WKV_GROUP), tile, at_ref, rt_ref, bt_ref,
                 kt_ref, dec_ref, vb_ref, rb_ref, q_out, y1_out, m_out, g_out,
                 lower, strict, eye, strict_left, right, eye_right)


def _chunk_group(chunks, tile, at_ref, rt_ref, bt_ref, kt_ref, dec_ref, vb_ref, rb_ref,
                 q_out, y1_out, m_out, g_out, lower, strict, eye, strict_left, right, eye_right):
    c = WKV_CHUNK
    n = HEAD_DIM
    pairs = [(j, h) for j in chunks for h in range(N_HEADS)]
    ah = [tile(at_ref, j, h) for j, h in pairs]
    rh = [tile(rt_ref, j, h) for j, h in pairs]
    bh = [tile(bt_ref, j, h) for j, h in pairs]
    kh = [tile(kt_ref, j, h) for j, h in pairs]
    vh = [tile(vb_ref, j, h) for j, h in pairs]
    dh = [dec_ref[(j + 1) * c - 1:(j + 1) * c, h * n:(h + 1) * n] for j, h in pairs]
    idx = range(len(pairs))
    ar = [jnp.concatenate([ah[i], tile(rb_ref, *pairs[i])], axis=0) for i in idx]
    bk = [jnp.concatenate([bh[i], kh[i]], axis=0) for i in idx]
    odd = [i % 2 == 1 for i in idx]
    aa = [_mm_nt(ar[i], jnp.concatenate([kh[i], bh[i]], axis=0) if odd[i] else bk[i])
          for i in idx]
    ab = [slice(c, 2 * c) if odd[i] else slice(0, c) for i in idx]
    ak = [slice(0, c) if odd[i] else slice(c, 2 * c) for i in idx]
    a_kr = [jnp.concatenate([jnp.where(strict, aa[i][:c, ak[i]], 0.0),
                             jnp.where(lower, aa[i][c:, ak[i]], 0.0)], axis=0).astype(BF16)
            for i in idx]
    a_rb = [jnp.where(lower, aa[i][c:, ab[i]], 0.0).astype(BF16) for i in idx]
    wrow = lax.broadcasted_iota(jnp.int32, (c, 2 * c), 0)
    wcol = lax.broadcasted_iota(jnp.int32, (c, 2 * c), 1)
    strict_right = right & (wrow > wcol - c)
    eye_left = (wcol == wrow).astype(F32)
    ps = [jnp.where(strict_right, aa[i][:c, :], 0.0) + eye_left if odd[i]
          else jnp.where(strict_left, aa[i][:c, :], 0.0) + eye_right for i in idx]
    zeros = jnp.zeros((c, 2 * c), BF16)
    for _ in range(int(math.log2(c))):
        nxt = []
        for e in range(0, len(pairs), 2):
            pe, po = ps[e].astype(BF16), ps[e + 1].astype(BF16)
            lhs = jnp.where(right, po, pe)
            rhs = jnp.concatenate([jnp.concatenate([pe, zeros], axis=1),
                                   jnp.concatenate([zeros, po], axis=1)], axis=0)
            prod = jnp.dot(lhs, rhs, preferred_element_type=F32)
            nxt.append(prod[:, :2 * c] + jnp.where(right, ps[e], 0.0))
            nxt.append(prod[:, 2 * c:] + jnp.where(right, 0.0, ps[e + 1]))
        ps = nxt
    t = [(ps[i][:, :c] if odd[i] else ps[i][:, c:]).astype(BF16) for i in idx]
    av = [_mm(a_kr[i], vh[i]) for i in idx]
    wu = [_mm(t[i], jnp.concatenate([ah[i], av[i][:c].astype(BF16)], axis=1)).astype(BF16)
          for i in idx]
    aw = [_mm(a_rb[i], wu[i]) for i in idx]
    q = [rh[i] + aw[i][:, :n] for i in idx]
    y1 = [aw[i][:, n:] + av[i][c:] for i in idx]
    zero = jnp.zeros((c, n), BF16)
    mg = [_mm_tn(jnp.concatenate([wu[i], jnp.concatenate([zero, vh[i]], axis=1)], axis=0), bk[i])
          for i in idx]
    m = [(eye + mg[i][:n]) * dh[i] for i in idx]
    g = [mg[i][n:] * dh[i] for i in idx]
    for j in chunks:
        sel = slice((j - chunks[0]) * N_HEADS, (j - chunks[0] + 1) * N_HEADS)
        q_out[j * c:(j + 1) * c, :] = jnp.concatenate(q[sel], axis=-1).astype(q_out.dtype)
        y1_out[j * c:(j + 1) * c, :] = jnp.concatenate(y1[sel], axis=-1)
        m_out[j * c:(j + 1) * c, :] = jnp.concatenate(m[sel], axis=-1).astype(m_out.dtype)
        g_out[j * c:(j + 1) * c, :] = jnp.concatenate(g[sel], axis=-1)


def _wkv_scan_kernel(q_ref, y1_ref, m_ref, gm_ref, bonus_ref, gate_ref, lng_ref, lnb_ref,
                     o_ref, st_ref):
    n = HEAD_DIM
    bsz = q_ref.shape[0]

    @pl.when(pl.program_id(0) == 0)
    def _():
        st_ref[...] = jnp.zeros(st_ref.shape, F32)

    c = WKV_CHUNK
    rows = q_ref.shape[1]
    chains = [(bi, h) for bi in range(bsz) for h in range(N_HEADS)]
    state = [st_ref[bi * N_HEADS + h] for bi, h in chains]
    entering = []
    for j in range(rows // c):
        rs = slice(j * c, (j + 1) * c)
        s_in = [s.astype(BF16) for s in state]
        entering.append(s_in)
        state = [_mm(s_in[i], m_ref[bi, rs, h * n:(h + 1) * n]) + gm_ref[bi, rs, h * n:(h + 1) * n]
                 for i, (bi, h) in enumerate(chains)]
    for i, (bi, h) in enumerate(chains):
        st_ref[bi * N_HEADS + h] = state[i]
    ys = [[_mm_nt(q_ref[bi, j * c:(j + 1) * c, h * n:(h + 1) * n], entering[j][i])
           for i, (bi, h) in enumerate(chains)] for j in range(rows // c)]
    y = jnp.concatenate(
        [jnp.concatenate([jnp.concatenate(ys[j][bi * N_HEADS:(bi + 1) * N_HEADS], axis=-1)
                          for j in range(rows // c)], axis=0) + y1_ref[bi]
         for bi in range(bsz)], axis=0)
    mean = _head_sums(y) * (1.0 / n)
    yc = y - mean
    var = _head_sums(yc * yc) * (1.0 / n)
    yn = yc * lax.rsqrt(var + LN_X_EPS) * lng_ref[...] + lnb_ref[...]
    for bi in range(bsz):
        o_ref[bi] = ((yn[bi * rows:(bi + 1) * rows] + bonus_ref[bi]) * gate_ref[bi]).astype(o_ref.dtype)


def _wkv_scan(q, y1, m, gm, bonus, gate, ln_g, ln_b, rows=4 * WKV_CHUNK):
    bsz, seq, w = y1.shape
    tile = pl.BlockSpec((bsz, rows, w), lambda ci: (0, ci, 0))
    vec = pl.BlockSpec((1, w), lambda ci: (0, 0))
    return pl.pallas_call(
        _wkv_scan_kernel,
        grid=(seq // rows,),
        in_specs=[tile] * 6 + [vec] * 2,
        out_specs=tile,
        out_shape=jax.ShapeDtypeStruct((bsz, seq, w), BF16),
        scratch_shapes=[pltpu.VMEM((bsz * N_HEADS, HEAD_DIM, HEAD_DIM), F32)],
        compiler_params=_params("arbitrary"),
        name="wkv_state_scan",
    )(q, y1, m, gm, bonus, gate, ln_g.reshape(1, w), ln_b.reshape(1, w))


def _wout_kernel(x_ref, y0_ref, y1_ref, y2_ref, y3_ref, w_ref, o_ref):
    acc = x_ref[...]
    for idx, y_ref in enumerate((y0_ref, y1_ref, y2_ref, y3_ref)):
        w = w_ref[0, idx * GROUP_WIDTH:(idx + 1) * GROUP_WIDTH, :].astype(BF16)
        acc = acc + jnp.dot(y_ref[...], w, preferred_element_type=F32)
    o_ref[...] = acc


def _wout(x2d, ys, w_stack, layer, tm=512):
    m, d = x2d.shape
    ytile = pl.BlockSpec((tm, GROUP_WIDTH), lambda i: (i, 0))
    xtile = pl.BlockSpec((tm, d), lambda i: (i, 0))
    wspec = pl.BlockSpec((1,) + w_stack.shape[1:], lambda i: (layer, 0, 0))
    return pl.pallas_call(
        _wout_kernel,
        grid=(m // tm,),
        in_specs=[xtile] + [ytile] * 4 + [wspec],
        out_specs=xtile,
        out_shape=jax.ShapeDtypeStruct((m, d), F32),
        compiler_params=_params("parallel", vmem_limit=FFN_VMEM_LIMIT),
        name="wout_residual",
    )(x2d, *ys, w_stack)


def _ffn_kernel(x_ref, g_ref, wu_ref, wd_ref, o_ref, h_ref):
    def down(u):
        act = jnp.square(jnp.maximum(u, 0.0)).astype(BF16)
        return jnp.dot(act, wd_ref[0].astype(BF16), preferred_element_type=F32)

    @pl.when(pl.program_id(1) == 0)
    def _():
        u = None
        for cols, h in _norm_chunks(x_ref, g_ref, h_ref, _rms_scale(x_ref)):
            part = jnp.dot(h, wu_ref[0, cols, :].astype(BF16), preferred_element_type=F32)
            u = part if u is None else u + part
        o_ref[...] = x_ref[...] + down(u)

    @pl.when(pl.program_id(1) > 0)
    def _():
        u = jnp.dot(h_ref[...], wu_ref[0].astype(BF16), preferred_element_type=F32)
        o_ref[...] += down(u)


def _ffn(x2d, gains, layer, wu, wd, tm=1024, tf=1024):
    m, d = x2d.shape
    f = wu.shape[1]
    wu_stack = wu.reshape(1, d, f)
    wd_stack = wd.reshape(1, f, d)
    xtile = pl.BlockSpec((tm, d), lambda i, j: (i, 0))
    return pl.pallas_call(
        _ffn_kernel,
        grid=(m // tm, f // tf),
        in_specs=[
            xtile,
            pl.BlockSpec((1, 1, d), lambda i, j: (layer, 0, 0)),
            pl.BlockSpec((1, d, tf), lambda i, j: (0, 0, j)),
            pl.BlockSpec((1, tf, d), lambda i, j: (0, j, 0)),
        ],
        out_specs=xtile,
        out_shape=jax.ShapeDtypeStruct((m, d), F32),
        scratch_shapes=[pltpu.VMEM((tm, d), BF16)],
        compiler_params=_params("parallel", "arbitrary",
                                vmem_limit=V7X_VMEM_BYTES - 3 * 1024 * 1024),
        name="ffn",
    )(x2d, gains.reshape(-1, 1, d), wu_stack, wd_stack)


def kernel(x, norm_mix, w_in, conv_w, swa_q_norm, swa_k_norm, swa_sink, dil_q_norm, dil_k_norm,
           rwkv_mu, decay_w0, decay_w2, iclr_a0, iclr_a2, gate_g2, k_k, k_a, r_k, ln_x_g, ln_x_b,
           vres_v0, vres_v1, vres_v2, w_out, norm_ffn, w_up, w_down, rel_bias):
    bsz, seq, d = x.shape
    depth = w_in.shape[0]
    bias = _bias_tiles(rel_bias)
    swa_bias = bias[0, :N_HEADS]
    dil_bias = bias[:, N_HEADS:]
    x2d = x.reshape(bsz * seq, d)
    v_first = None
    for layer in range(depth):
        proj = _norm_matmul(x2d, norm_mix, w_in, layer).reshape(bsz, seq, IN_WIDTH)
        y_swa, y_conv = _swa_conv_mixers(proj, swa_q_norm[layer], swa_k_norm[layer],
                                         swa_sink[layer], swa_bias, conv_w[layer])
        y_dil, wu_bf16, wd_bf16 = _dil_mixer(proj, dil_q_norm[layer], dil_k_norm[layer], dil_bias,
                                             w_up, w_down, layer)
        vres = None if layer == 0 else (vres_v0[layer - 1], vres_v1[layer - 1], vres_v2[layer - 1])
        maps = _rwkv_maps(
            proj, rwkv_mu[layer], decay_w0[layer], decay_w2[layer], iclr_a0[layer], iclr_a2[layer],
            gate_g2[layer], k_k[layer], k_a[layer], r_k[layer], v_first, vres)
        if layer == 0:
            v_first = maps[6]
        y_rwkv = _wkv_scan(*maps[:6], ln_x_g[layer], ln_x_b[layer])
        ys = [y.reshape(bsz * seq, GROUP_WIDTH) for y in (y_conv, y_swa, y_dil, y_rwkv)]
        x2d = _wout(x2d, ys, w_out, layer)
        x2d = _ffn(x2d, norm_ffn, layer, wu_bf16, wd_bf16)
    return x2d.reshape(bsz, seq, d)
```

```python
import functools
import math

import jax
import jax.numpy as jnp
from jax import lax
from jax.experimental import pallas as pl
from jax.experimental.pallas import tpu as pltpu

F32 = jnp.float32
BF16 = jnp.bfloat16

HEAD_DIM = 64
GROUP_WIDTH = 512
N_HEADS = GROUP_WIDTH // HEAD_DIM
SWA_KV_HEADS = 2
SWA_GROUP = N_HEADS // SWA_KV_HEADS
SWA_WINDOW = 128
DIL_PAIRS = ((128, 1), (512, 4), (2048, 16))
DECAY_LORA = 64
ICLR_LORA = 64
GATE_LORA = 128
RWKV_IN_WIDTH = 3 * GROUP_WIDTH + DECAY_LORA + ICLR_LORA + GATE_LORA
BLK = 128
NUM_BUCKETS = 32
BUCKET_MAX_DIST = 128
RMS_EPS = 1e-6
LN_X_EPS = 64e-5
NEG = -1e30
LOG2E = math.log2(math.e)
WKV_CHUNK = 64
WKV_GROUP = 4
DIL_TILES = 4
PREV_ROWS = 16
NORM_CHUNKS = 4
PREP_UNROLL = 4

OFF_CONV = 0
OFF_SWA_Q = OFF_CONV + 3 * GROUP_WIDTH
OFF_SWA_K = OFF_SWA_Q + GROUP_WIDTH
OFF_SWA_V = OFF_SWA_K + SWA_KV_HEADS * HEAD_DIM
OFF_DIL = OFF_SWA_V + SWA_KV_HEADS * HEAD_DIM
OFF_RW = OFF_DIL + 3 * GROUP_WIDTH
IN_WIDTH = OFF_RW + RWKV_IN_WIDTH

V7X_VMEM_BYTES = 64 * 1024 * 1024
VMEM_LIMIT = 48 * 1024 * 1024
FFN_VMEM_LIMIT = V7X_VMEM_BYTES - 6 * 1024 * 1024


def _params(*sem, vmem_limit=VMEM_LIMIT):
    return pltpu.CompilerParams(dimension_semantics=sem, vmem_limit_bytes=vmem_limit)


def _mm(a, b):
    return jnp.dot(a.astype(BF16), b.astype(BF16), preferred_element_type=F32)


def _mm_nt(a, b):
    return lax.dot_general(a.astype(BF16), b.astype(BF16), (((1,), (1,)), ((), ())),
                           preferred_element_type=F32)


def _mm_tn(a, b):
    return lax.dot_general(a.astype(BF16), b.astype(BF16), (((0,), (0,)), ((), ())),
                           preferred_element_type=F32)


def _split_bf16(x):
    hi = x.astype(BF16)
    lo = (x - hi.astype(F32)).astype(BF16)
    return hi, lo


def _head_sums(x, split=True):
    lanes = 128
    r = lax.broadcasted_iota(jnp.int32, (lanes, lanes), 0) // HEAD_DIM
    c = lax.broadcasted_iota(jnp.int32, (lanes, lanes), 1) // HEAD_DIM
    bd = (r == c).astype(BF16)
    parts = _split_bf16(x) if split else (x.astype(BF16),)
    cols = []
    for j in range(x.shape[-1] // lanes):
        sl = slice(j * lanes, (j + 1) * lanes)
        cols.append(sum(jnp.dot(part[:, sl], bd, preferred_element_type=F32) for part in parts))
    return cols[0] if len(cols) == 1 else jnp.concatenate(cols, axis=-1)


def _sigmoid(z):
    return 1.0 / (1.0 + jnp.exp(-z))


def _rms_scale(x_ref):
    x = x_ref[...]
    return lax.rsqrt(jnp.mean(x * x, axis=-1, keepdims=True) + RMS_EPS)


def _norm_chunks(x_ref, g_ref, h_ref, scale):
    k = x_ref.shape[1]
    kc = k // NORM_CHUNKS
    for c in range(NORM_CHUNKS):
        cols = slice(c * kc, (c + 1) * kc)
        h = (x_ref[:, cols] * scale * g_ref[0, :, cols]).astype(BF16)
        h_ref[:, cols] = h
        yield cols, h


def _norm_matmul_kernel(x_ref, g_ref, w_ref, o_ref, h_ref):
    @pl.when(pl.program_id(1) == 0)
    def _():
        for _ in _norm_chunks(x_ref, g_ref, h_ref, _rms_scale(x_ref)):
            pass

    o_ref[...] = jnp.dot(h_ref[...], w_ref[0].astype(BF16),
                         preferred_element_type=F32).astype(o_ref.dtype)


def _norm_matmul(x2d, gains, w_stack, layer, tm=2048, tn=512):
    m, k = x2d.shape
    n = w_stack.shape[2]
    assert n % tn == 0 and m % tm == 0
    return pl.pallas_call(
        _norm_matmul_kernel,
        grid=(m // tm, n // tn),
        in_specs=[
            pl.BlockSpec((tm, k), lambda i, j: (i, 0)),
            pl.BlockSpec((1, 1, k), lambda i, j: (layer, 0, 0)),
            pl.BlockSpec((1, k, tn), lambda i, j: (layer, 0, j)),
        ],
        out_specs=pl.BlockSpec((tm, tn), lambda i, j: (i, j)),
        out_shape=jax.ShapeDtypeStruct((m, n), BF16),
        scratch_shapes=[pltpu.VMEM((tm, k), BF16)],
        compiler_params=_params("parallel", "arbitrary", vmem_limit=FFN_VMEM_LIMIT),
        name="norm_matmul",
    )(x2d, gains.reshape(-1, 1, k), w_stack)


def _t5_bucket(dist):
    dist = jnp.maximum(dist, 0)
    max_exact = NUM_BUCKETS // 2
    scaled = (jnp.log(jnp.maximum(dist, 1).astype(F32) / max_exact)
              / math.log(BUCKET_MAX_DIST / max_exact))
    large = max_exact + (scaled * (NUM_BUCKETS - max_exact)).astype(jnp.int32)
    large = jnp.minimum(large, NUM_BUCKETS - 1)
    return jnp.where(dist < max_exact, dist, large)


def _bias_kernel(bucket_ref, table_ref, o_ref):
    bucket = bucket_ref[0]

    def head(h, carry):
        g = jnp.zeros(bucket.shape, F32)
        for b in range(NUM_BUCKETS):
            g = jnp.where(bucket == b, table_ref[b, h], g)
        rows = jnp.broadcast_to(g[0:1], (BLK, 2 * BLK))
        o_ref[0, h] = pltpu.roll(rows, 0, axis=1, stride=1, stride_axis=0)
        return carry

    lax.fori_loop(0, o_ref.shape[1], head, 0)


def _bias_tiles(rel_bias):
    lag_dist = BLK - jnp.arange(2 * BLK)
    buckets = jnp.stack([_t5_bucket(lag_dist * r) for _, r in DIL_PAIRS]).astype(jnp.int32)
    buckets = jnp.broadcast_to(buckets[:, None, :], (len(DIL_PAIRS), 8, 2 * BLK))
    nh = rel_bias.shape[1]
    return pl.pallas_call(
        _bias_kernel,
        grid=(len(DIL_PAIRS),),
        in_specs=[
            pl.BlockSpec((1, 8, 2 * BLK), lambda s: (s, 0, 0)),
            pl.BlockSpec(memory_space=pltpu.SMEM),
        ],
        out_specs=pl.BlockSpec((1, nh, BLK, 2 * BLK), lambda s: (s, 0, 0, 0)),
        out_shape=jax.ShapeDtypeStruct((len(DIL_PAIRS), nh, BLK, 2 * BLK), F32),
        compiler_params=_params("arbitrary"),
        name="bias_tiles",
    )(buckets, rel_bias)


def _short_conv(b_ref, c_ref, u_ref, w_ref, o_ref):
    lanes = 128
    for j in range(b_ref.shape[2] // lanes):
        cols = slice(j * lanes, (j + 1) * lanes)
        z = c_ref[0, :, cols].astype(F32) * u_ref[0, :, cols].astype(F32)
        row = lax.broadcasted_iota(jnp.int32, z.shape, 0)
        z1 = jnp.where(row >= 1, pltpu.roll(z, 1, axis=0), 0.0)
        z2 = jnp.where(row >= 2, pltpu.roll(z, 2, axis=0), 0.0)
        w = w_ref[:, cols]
        y = z2 * w[0:1, :] + z1 * w[1:2, :] + z * w[2:3, :]
        o_ref[0, :, cols] = (b_ref[0, :, cols].astype(F32) * y).astype(o_ref.dtype)


def _band_mask(max_dist):
    a = lax.broadcasted_iota(jnp.int32, (BLK, 2 * BLK), 0)
    b = lax.broadcasted_iota(jnp.int32, (BLK, 2 * BLK), 1)
    dist = BLK + a - b
    return (dist >= 0) & (dist <= max_dist), b


def _attend(qs, kws, vws, biases, normalize=True):
    idx = range(len(qs))
    s = [lax.dot_general(qs[i], kws[i], (((1,), (1,)), ((), ())), preferred_element_type=F32)
         + biases[i] for i in idx]
    m = [jnp.max(s[i], axis=-1, keepdims=True) for i in idx]
    p = [jnp.exp2(s[i] - m[i]) for i in idx]
    den = [jnp.sum(p[i], axis=-1, keepdims=True) for i in idx]
    o = [jnp.dot(p[i].astype(BF16), vws[i], preferred_element_type=F32) for i in idx]
    if normalize:
        o = [o[i] / den[i] for i in idx]
    return o, m, den


def _head_rms(x, gain):
    ms = _head_sums(x * x, split=False) * (1.0 / HEAD_DIM)
    return x * lax.rsqrt(ms + RMS_EPS) * gain


def _swa_kernel(q_ref, k_ref, v_ref, qg_ref, kg_ref, sink_ref, bias_ref,
                cb_ref, cc_ref, cu_ref, cw_ref, o_ref, oc_ref,
                qn_ref, kn_ref, vb_ref, bm_ref):
    seq = q_ref.shape[1]
    nb = seq // BLK
    kvw = SWA_KV_HEADS * HEAD_DIM
    scale = HEAD_DIM ** -0.5 * LOG2E

    _short_conv(cb_ref, cc_ref, cu_ref, cw_ref, oc_ref)

    kn_ref[0:BLK, :] = jnp.zeros((BLK, 2 * kvw), BF16)
    vb_ref[0:BLK, :] = jnp.zeros((BLK, 2 * kvw), BF16)
    lane_half = lax.broadcasted_iota(jnp.int32, (1, kvw), 1) // HEAD_DIM

    def both_halves(x):
        xr = pltpu.roll(x, HEAD_DIM, axis=1)
        return jnp.concatenate([jnp.where(lane_half == 0, x, xr),
                                jnp.where(lane_half == 0, xr, x)], axis=-1)

    def prep(i, carry):
        r0 = pl.multiple_of(i * BLK, BLK)
        q = q_ref[0, pl.ds(r0, BLK), :].astype(F32)
        qn_ref[pl.ds(r0, BLK), :] = (_head_rms(q, qg_ref[...]) * scale).astype(BF16)
        k = _head_rms(k_ref[0, pl.ds(r0, BLK), :].astype(F32), kg_ref[...])
        kn_ref[pl.ds(r0 + BLK, BLK), :] = both_halves(k).astype(BF16)
        v = v_ref[0, pl.ds(r0, BLK), :].astype(F32)
        vb_ref[pl.ds(r0 + BLK, BLK), :] = both_halves(v).astype(BF16)
        return carry

    lax.fori_loop(0, nb, prep, 0, unroll=PREP_UNROLL)

    assert SWA_WINDOW <= BLK
    band, kcol = _band_mask(SWA_WINDOW - 1)
    for h in range(N_HEADS):
        sink = sink_ref[h] * LOG2E
        bias = bias_ref[h] * LOG2E
        bm_ref[h] = jnp.where(kcol == 0, sink, jnp.where(band & (kcol >= BLK), bias, NEG))
        bm_ref[N_HEADS + h] = jnp.where(kcol == 0, sink, jnp.where(band, bias, NEG))
    first_key = lax.broadcasted_iota(jnp.int32, (2 * BLK, kvw), 0) == 0

    def block(i, carry):
        r0 = pl.multiple_of(i * BLK, BLK)
        later = jnp.minimum(i, 1) * N_HEADS
        qs, kws, vws, bms = [], [], [], []
        for hk in range(SWA_KV_HEADS):
            kw = kn_ref[pl.ds(r0, 2 * BLK), hk * kvw:(hk + 1) * kvw]
            vw = vb_ref[pl.ds(r0, 2 * BLK), hk * kvw:(hk + 1) * kvw]
            kw = jnp.where(first_key, jnp.zeros_like(kw), kw)
            vw = jnp.where(first_key, jnp.zeros_like(vw), vw)
            for g in range(SWA_GROUP):
                h = hk * SWA_GROUP + g
                q2 = qn_ref[pl.ds(r0, BLK), (h // 2) * kvw:(h // 2 + 1) * kvw]
                qs.append(jnp.where(lane_half == h % 2, q2, jnp.zeros_like(q2)))
                kws.append(kw)
                vws.append(vw)
                bms.append(bm_ref[later + h])
        outs, _, _ = _attend(qs, kws, vws, bms)
        pairs = [jnp.where(lane_half == 0, outs[h], outs[h + 1]) for h in range(0, N_HEADS, 2)]
        o_ref[0, pl.ds(r0, BLK), :] = jnp.concatenate(pairs, axis=-1).astype(o_ref.dtype)
        return carry

    lax.fori_loop(0, nb, block, 0)


def _swa_conv_mixers(proj, q_gain, k_gain, sink, bias, conv_w):
    bsz, seq, _ = proj.shape
    kvw = SWA_KV_HEADS * HEAD_DIM
    q_gain_t = jnp.tile(q_gain, N_HEADS).reshape(1, GROUP_WIDTH)
    k_gain_t = jnp.tile(k_gain, SWA_KV_HEADS).reshape(1, kvw)
    wide = lambda off: pl.BlockSpec((1, seq, GROUP_WIDTH), lambda b: (b, 0, off // GROUP_WIDTH))
    out = jax.ShapeDtypeStruct((bsz, seq, GROUP_WIDTH), BF16)
    return pl.pallas_call(
        _swa_kernel,
        grid=(bsz,),
        in_specs=[
            wide(OFF_SWA_Q),
            pl.BlockSpec((1, seq, kvw), lambda b: (b, 0, OFF_SWA_K // kvw)),
            pl.BlockSpec((1, seq, kvw), lambda b: (b, 0, OFF_SWA_V // kvw)),
            pl.BlockSpec((1, GROUP_WIDTH), lambda b: (0, 0)),
            pl.BlockSpec((1, kvw), lambda b: (0, 0)),
            pl.BlockSpec(memory_space=pltpu.SMEM),
            pl.BlockSpec((N_HEADS, BLK, 2 * BLK), lambda b: (0, 0, 0)),
            wide(OFF_CONV), wide(OFF_CONV + GROUP_WIDTH), wide(OFF_CONV + 2 * GROUP_WIDTH),
            pl.BlockSpec((3, GROUP_WIDTH), lambda b: (0, 0)),
        ],
        out_specs=[pl.BlockSpec((1, seq, GROUP_WIDTH), lambda b: (b, 0, 0))] * 2,
        out_shape=[out, out],
        scratch_shapes=[
            pltpu.VMEM((seq, GROUP_WIDTH), BF16),
            pltpu.VMEM((seq + BLK, 2 * kvw), BF16),
            pltpu.VMEM((seq + BLK, 2 * kvw), BF16),
            pltpu.VMEM((2 * N_HEADS, BLK, 2 * BLK), F32),
        ],
        compiler_params=_params("parallel"),
        name="swa_conv_mixers",
    )(proj, proj, proj, q_gain_t, k_gain_t, sink, bias, proj, proj, proj, conv_w)


def _dil_kernel(q_ref, k_ref, v_ref, qg_ref, kg_ref, bias_ref, wu_ref, wd_ref,
                o_ref, wu_out, wd_out,
                qn_ref, kn_ref, vn_ref, ob_ref, mb_ref, db_ref, bm_ref):
    wu_out[...] = wu_ref[0].astype(BF16)
    wd_out[...] = wd_ref[0].astype(BF16)
    seq = q_ref.shape[1]
    lanes = q_ref.shape[2]
    heads = lanes // HEAD_DIM
    scale = HEAD_DIM ** -0.5 * LOG2E

    def prep(i, carry):
        r0 = pl.multiple_of(i * BLK, BLK)
        q = q_ref[0, pl.ds(r0, BLK), :].astype(F32)
        k = k_ref[0, pl.ds(r0, BLK), :].astype(F32)
        qn_ref[pl.ds(r0, BLK), :] = _head_rms(q, qg_ref[...]) * scale
        kn_ref[pl.ds(r0, BLK), :] = _head_rms(k, kg_ref[...])
        vn_ref[pl.ds(r0, BLK), :] = v_ref[0, pl.ds(r0, BLK), :].astype(F32)
        return carry

    lax.fori_loop(0, seq // BLK, prep, 0, unroll=PREP_UNROLL)

    for br, (window, r) in enumerate(DIL_PAIRS):
        band, kcol = _band_mask(window // r)
        for h in range(heads):
            bias = bias_ref[br, h] * LOG2E
            bm_ref[(br * heads + h) * 2] = jnp.where(band & (kcol >= BLK), bias, NEG)
            bm_ref[(br * heads + h) * 2 + 1] = jnp.where(band, bias, NEG)

    lane_head = lax.broadcasted_iota(jnp.int32, (1, lanes), 1) // HEAD_DIM

    for br, (window, r) in enumerate(DIL_PAIRS):
        nb = seq // r // BLK

        def blocks(it, carry, br=br, r=r, nb=nb):
            qs, kws, vws, bms, curs = [], [], [], [], []
            for u in range(DIL_TILES):
                t = it * DIL_TILES + u
                c = t // nb
                i = t - c * nb
                cur = c + i * (BLK * r)
                prev = jnp.maximum(cur - BLK * r, c)
                later = jnp.minimum(i, 1)

                def rows(ref, start):
                    if r == 1:
                        return ref[pl.ds(start, BLK), :]
                    return ref[pl.ds(start, BLK, stride=r), :]

                q = rows(qn_ref, cur).astype(BF16)
                if nb == 1:
                    kw = rows(kn_ref, cur).astype(BF16)
                    vw = rows(vn_ref, cur).astype(BF16)
                else:
                    kw = jnp.concatenate([rows(kn_ref, prev), rows(kn_ref, cur)],
                                         axis=0).astype(BF16)
                    vw = jnp.concatenate([rows(vn_ref, prev), rows(vn_ref, cur)],
                                         axis=0).astype(BF16)
                curs.append(cur)
                for h in range(heads):
                    qs.append(jnp.where(lane_head == h, q, jnp.zeros_like(q)))
                    kws.append(kw)
                    vws.append(vw)
                    if nb == 1:
                        bms.append(bm_ref[(br * heads + h) * 2, :, BLK:])
                    else:
                        bms.append(bm_ref[(br * heads + h) * 2 + later])
            outs, ms, dens = _attend(qs, kws, vws, bms, normalize=False)
            for u in range(DIL_TILES):
                o_all, m_all, d_all = outs[u * heads], ms[u * heads], dens[u * heads]
                for h in range(1, heads):
                    o_all = jnp.where(lane_head == h, outs[u * heads + h], o_all)
                    m_all = jnp.where(lane_head == h, ms[u * heads + h], m_all)
                    d_all = jnp.where(lane_head == h, dens[u * heads + h], d_all)
                if r == 1:
                    rows = pl.ds(curs[u], BLK)
                else:
                    rows = pl.ds(curs[u], BLK, stride=r)
                ob_ref[br, rows, :] = o_all
                mb_ref[br, rows, :] = m_all
                db_ref[br, rows, :] = d_all
            return carry

        lax.fori_loop(0, r * nb // DIL_TILES, blocks, 0)

    def combine(i, carry):
        r0 = pl.multiple_of(i * BLK, BLK)
        rows = pl.ds(r0, BLK)
        m0, m1, m2 = mb_ref[0, rows, :], mb_ref[1, rows, :], mb_ref[2, rows, :]
        m = jnp.maximum(jnp.maximum(m0, m1), m2)
        e0, e1, e2 = jnp.exp2(m0 - m), jnp.exp2(m1 - m), jnp.exp2(m2 - m)
        num = e0 * ob_ref[0, rows, :] + e1 * ob_ref[1, rows, :] + e2 * ob_ref[2, rows, :]
        den = e0 * db_ref[0, rows, :] + e1 * db_ref[1, rows, :] + e2 * db_ref[2, rows, :]
        o_ref[0, rows, :] = (num / den).astype(o_ref.dtype)
        return carry

    lax.fori_loop(0, seq // BLK, combine, 0, unroll=PREP_UNROLL)


def _dil_mixer(proj, q_gain, k_gain, bias, w_up, w_down, layer):
    bsz, seq, _ = proj.shape
    lanes = 128
    heads = lanes // HEAD_DIM
    nblk = GROUP_WIDTH // lanes
    base = OFF_DIL // lanes
    gq = jnp.tile(q_gain, heads).reshape(1, lanes)
    gk = jnp.tile(k_gain, heads).reshape(1, lanes)
    _, d, f = w_up.shape
    steps = bsz * nblk
    assert d % steps == 0 and f % steps == 0

    def col(seg):
        return pl.BlockSpec((1, seq, lanes), lambda b, j: (b, 0, base + seg * nblk + j))

    return pl.pallas_call(
        _dil_kernel,
        grid=(bsz, nblk),
        in_specs=[
            col(0), col(1), col(2),
            pl.BlockSpec((1, lanes), lambda b, j: (0, 0)),
            pl.BlockSpec((1, lanes), lambda b, j: (0, 0)),
            pl.BlockSpec((len(DIL_PAIRS), heads, BLK, 2 * BLK), lambda b, j: (0, j, 0, 0)),
            pl.BlockSpec((1, d // steps, f), lambda b, j: (layer, b * nblk + j, 0)),
            pl.BlockSpec((1, f // steps, d), lambda b, j: (layer, b * nblk + j, 0)),
        ],
        out_specs=[
            pl.BlockSpec((1, seq, lanes), lambda b, j: (b, 0, j)),
            pl.BlockSpec((d // steps, f), lambda b, j: (b * nblk + j, 0)),
            pl.BlockSpec((f // steps, d), lambda b, j: (b * nblk + j, 0)),
        ],
        out_shape=[
            jax.ShapeDtypeStruct((bsz, seq, GROUP_WIDTH), BF16),
            jax.ShapeDtypeStruct((d, f), BF16),
            jax.ShapeDtypeStruct((f, d), BF16),
        ],
        scratch_shapes=[
            pltpu.VMEM((seq, lanes), F32),
            pltpu.VMEM((seq, lanes), F32),
            pltpu.VMEM((seq, lanes), F32),
            pltpu.VMEM((len(DIL_PAIRS), seq, lanes), F32),
            pltpu.VMEM((len(DIL_PAIRS), seq, lanes), F32),
            pltpu.VMEM((len(DIL_PAIRS), seq, lanes), F32),
            pltpu.VMEM((len(DIL_PAIRS) * heads * 2, BLK, 2 * BLK), F32),
        ],
        compiler_params=_params("parallel", "parallel"),
        name="dil_mixer",
    )(proj, proj, proj, gq, gk, bias, w_up, w_down)


def _rwkv_maps_kernel(*refs, has_vres):
    if has_vres:
        (p_ref, pp_ref, mu_ref, w0_ref, w2_ref, a0_ref, a2_ref, g2_ref, kk_ref, ka_ref, rk_ref,
         vf_ref, v0_ref, v1_ref, v2_ref,
         q_out, y1_out, m_out, gm_out, bonus_out, gate_out, *scratch) = refs
    else:
        (p_ref, pp_ref, mu_ref, w0_ref, w2_ref, a0_ref, a2_ref, g2_ref, kk_ref, ka_ref, rk_ref,
         q_out, y1_out, m_out, gm_out, bonus_out, gate_out, v_out, *scratch) = refs
    w = GROUP_WIDTH
    p = p_ref[0].astype(F32)
    row = lax.broadcasted_iota(jnp.int32, p.shape, 0)
    last_prev = pp_ref[0, PREV_ROWS - 1:PREV_ROWS, :].astype(F32)
    last_prev = jnp.where(pl.program_id(1) > 0, last_prev, 0.0)
    prev = jnp.where(row >= 1, pltpu.roll(p, 1, axis=0), last_prev)
    xs = p + (prev - p) * mu_ref[...]
    grows = WKV_GROUP * WKV_CHUNK
    for grp in range(p.shape[0] // grows):
        rs = slice(grp * grows, (grp + 1) * grows)
        x = xs[rs]
        r = x[:, 0:w]
        k = x[:, w:2 * w]
        v = x[:, 2 * w:3 * w]
        o = 3 * w
        wd = x[:, o:o + DECAY_LORA]
        ad = x[:, o + DECAY_LORA:o + DECAY_LORA + ICLR_LORA]
        gd = x[:, o + DECAY_LORA + ICLR_LORA:]

        z = -(w0_ref[...] + _mm(jnp.tanh(wd), w2_ref[...]))
        softplus = jnp.maximum(z, 0.0) + jnp.log(1.0 + jnp.exp(-jnp.abs(z)))
        logw = -softplus - 0.5
        lw = -jnp.exp(logw)
        a = _sigmoid(a0_ref[...] + _mm(ad, a2_ref[...]))
        gate_out[0, rs] = _mm(_sigmoid(gd), g2_ref[...])
        if has_vres:
            mix = _sigmoid(v0_ref[...] + _mm(_mm(v, v1_ref[...]), v2_ref[...]))
            v = v + (vf_ref[0, rs] - v) * mix
        else:
            v_out[0, rs] = v
        kk = k * kk_ref[...]
        ss = _head_sums(kk * kk, split=False)
        kk = kk * lax.rsqrt(jnp.maximum(ss, 1e-24))
        k = k * (1.0 + (a - 1.0) * ka_ref[...])
        bonus_out[0, rs] = _head_sums(r * k * rk_ref[...]) * v
        _chunk_maps(grp, r, lw, k, v, kk, kk * a, q_out.at[0], y1_out.at[0], m_out.at[0],
                    gm_out.at[0], *scratch)


def _rwkv_maps(proj, mu, w0, w2, a0, a2, g2, k_k, k_a, r_k, v_first, vres, ts=512):
    bsz, seq, _ = proj.shape
    w = GROUP_WIDTH
    nt = seq // ts
    has_vres = vres is not None

    def full(shape):
        return pl.BlockSpec(shape, lambda b, i: (0,) * len(shape))

    row = lambda a: a.reshape(1, -1)
    tile = pl.BlockSpec((1, ts, w), lambda b, i: (b, i, 0))
    in_specs = [
        pl.BlockSpec((pl.Element(1), pl.Element(ts), pl.Element(RWKV_IN_WIDTH)),
                     lambda b, i: (b, i * ts, OFF_RW)),
        pl.BlockSpec((pl.Element(1), pl.Element(PREV_ROWS), pl.Element(RWKV_IN_WIDTH)),
                     lambda b, i: (b, jnp.maximum(i * (ts // PREV_ROWS) - 1, 0) * PREV_ROWS, OFF_RW)),
        full((1, RWKV_IN_WIDTH)), full((1, w)), full((DECAY_LORA, w)), full((1, w)),
        full((ICLR_LORA, w)), full((GATE_LORA, w)), full((1, w)), full((1, w)), full((1, w)),
    ]
    args = [proj, proj, row(mu), row(w0), w2, row(a0), a2, g2, row(k_k), row(k_a), row(r_k)]
    if has_vres:
        v0, v1, v2 = vres
        in_specs += [tile, full((1, w)), full(v1.shape), full(v2.shape)]
        args += [v_first, row(v0), v1, v2]
    f32 = jax.ShapeDtypeStruct((bsz, seq, w), F32)
    bf16 = jax.ShapeDtypeStruct((bsz, seq, w), BF16)
    out_shape = [bf16, f32, bf16, f32, f32, f32] + ([] if has_vres else [f32])
    return pl.pallas_call(
        functools.partial(_rwkv_maps_kernel, has_vres=has_vres),
        grid=(bsz, nt),
        in_specs=in_specs,
        out_specs=[tile] * len(out_shape),
        out_shape=out_shape,
        scratch_shapes=[pltpu.VMEM((ts, w), dt) for dt in (BF16, F32, BF16, BF16, F32, BF16, BF16)],
        compiler_params=_params("parallel", "parallel"),
        name="rwkv_chunk_maps",
    )(*args)


def _chunk_maps(grp, r, lw, k, v, kk, b, q_out, y1_out, m_out, g_out,
                at_ref, rt_ref, bt_ref, kt_ref, dec_ref, vb_ref, rb_ref):
    c = WKV_CHUNK
    n = HEAD_DIM
    rows = r.shape[0]
    rs = slice(grp * rows, (grp + 1) * rows)

    row = lax.broadcasted_iota(jnp.int32, (rows, rows), 0)
    col = lax.broadcasted_iota(jnp.int32, (rows, rows), 1)
    tri = ((row >= col) & ((row // c) == (col // c))).astype(BF16)
    lw_hi, lw_lo = _split_bf16(lw)
    cum = (jnp.dot(tri, lw_hi, preferred_element_type=F32)
           + jnp.dot(tri, lw_lo, preferred_element_type=F32))
    e_pos = jnp.exp(cum)
    e_neg = jnp.exp(-cum)
    r_t = r * e_pos
    at_ref[rs] = (-kk * jnp.exp(cum - lw)).astype(BF16)
    rt_ref[rs] = r_t
    rb_ref[rs] = r_t.astype(BF16)
    bt_ref[rs] = (b * e_neg).astype(BF16)
    kt_ref[rs] = (k * e_neg).astype(BF16)
    dec_ref[rs] = e_pos
    vb_ref[rs] = v.astype(BF16)

    _chunk_group_pairs(range(grp * WKV_GROUP, (grp + 1) * WKV_GROUP), at_ref, rt_ref, bt_ref,
                       kt_ref, dec_ref, vb_ref, rb_ref, q_out, y1_out, m_out, g_out)


def _chunk_group_pairs(chunks, at_ref, rt_ref, bt_ref, kt_ref, dec_ref, vb_ref, rb_ref,
                       q_out, y1_out, m_out, g_out):
    c = WKV_CHUNK
    n = HEAD_DIM
    assert c == n
    w2 = 2 * n
    units = [(j, p) for j in chunks for p in range(N_HEADS // 2)]
    idx = range(len(units))

    def tile(ref, i):
        j, p = units[i]
        return ref[j * c:(j + 1) * c, p * w2:(p + 1) * w2]

    left = lax.broadcasted_iota(jnp.int32, (1, w2), 1) < n
    row2 = lax.broadcasted_iota(jnp.int32, (2 * c, 2 * c), 0)
    col2 = lax.broadcasted_iota(jnp.int32, (2 * c, 2 * c), 1)
    keep = (row2 % c > col2 % c) | ((row2 >= c) & (row2 % c == col2 % c))
    wrow = lax.broadcasted_iota(jnp.int32, (c, w2), 0)
    wcol = lax.broadcasted_iota(jnp.int32, (c, w2), 1)
    eye_left = (wcol == wrow).astype(F32)
    eye_right = (wcol == wrow + c).astype(F32)

    def only(mask, x):
        return jnp.where(mask, x, jnp.zeros_like(x))

    def rows_ba(x):
        return jnp.concatenate([only(~left, x), only(left, x)], axis=0)

    def rows_ab(x):
        return jnp.concatenate([only(left, x), only(~left, x)], axis=0)

    a = [tile(at_ref, i) for i in idx]
    b = [tile(bt_ref, i) for i in idx]
    k = [tile(kt_ref, i) for i in idx]
    v = [tile(vb_ref, i) for i in idx]
    ar = [jnp.concatenate([a[i], tile(rb_ref, i)], axis=0) for i in idx]
    bk = [jnp.concatenate([b[i], k[i]], axis=0) for i in idx]
    kb = [jnp.concatenate([k[i], b[i]], axis=0) for i in idx]
    aa_a = [jnp.where(keep, _mm_nt(only(left, ar[i]), bk[i]), 0.0) for i in idx]
    aa_b = [jnp.where(keep, _mm_nt(only(~left, ar[i]), kb[i]), 0.0) for i in idx]
    ps_a = [only(left, aa_a[i][:c]) + eye_right for i in idx]
    ps_b = [only(~left, aa_b[i][:c]) + eye_left for i in idx]
    zeros = jnp.zeros((c, w2), BF16)
    for _ in range(int(math.log2(c))):
        nxt_a, nxt_b = [], []
        for i in idx:
            pa, pb = ps_a[i].astype(BF16), ps_b[i].astype(BF16)
            rhs = jnp.concatenate([jnp.concatenate([pa, zeros], axis=1),
                                   jnp.concatenate([zeros, pb], axis=1)], axis=0)
            prod = jnp.dot(jnp.where(left, pa, pb), rhs, preferred_element_type=F32)
            nxt_a.append(prod[:, :w2] + only(~left, ps_a[i]))
            nxt_b.append(prod[:, w2:] + only(left, ps_b[i]))
        ps_a, ps_b = nxt_a, nxt_b
    t_ba = [jnp.where(left, ps_b[i], ps_a[i]).astype(BF16) for i in idx]
    av = [_mm(jnp.where(left, aa_b[i], aa_a[i]), rows_ba(v[i])) for i in idx]
    av_u = [av[i][:c].astype(BF16) for i in idx]
    wu = [_mm(t_ba[i], jnp.concatenate([rows_ba(a[i]), rows_ba(av_u[i])], axis=1)).astype(BF16)
          for i in idx]
    a_rb = [jnp.where(left, aa_a[i][c:], aa_b[i][c:]) for i in idx]
    aw = [_mm(a_rb[i], jnp.concatenate([rows_ab(wu[i][:, :w2]), rows_ab(wu[i][:, w2:])], axis=1))
          for i in idx]
    q = [tile(rt_ref, i) + aw[i][:, :w2] for i in idx]
    y1 = [aw[i][:, w2:] + av[i][c:] for i in idx]
    x1 = [_mm_tn(wu[i][:, :w2], b[i]) for i in idx]
    x2 = [_mm_tn(jnp.concatenate([wu[i][:, w2:], v[i]], axis=0), bk[i]) for i in idx]
    eye2 = eye_left + eye_right
    m, g = [], []
    for i in idx:
        j, p = units[i]
        d = dec_ref[(j + 1) * c - 1:(j + 1) * c, p * w2:(p + 1) * w2]
        m.append((eye2 + jnp.where(left, x1[i][:n], x1[i][n:])) * d)
        g.append(jnp.where(left, x2[i][:n], x2[i][n:]) * d)
    per = N_HEADS // 2
    for ci, j in enumerate(chunks):
        sel = slice(ci * per, (ci + 1) * per)
        q_out[j * c:(j + 1) * c, :] = jnp.concatenate(q[sel], axis=-1).astype(q_out.dtype)
        y1_out[j * c:(j + 1) * c, :] = jnp.concatenate(y1[sel], axis=-1)
        m_out[j * c:(j + 1) * c, :] = jnp.concatenate(m[sel], axis=-1).astype(m_out.dtype)
        g_out[j * c:(j + 1) * c, :] = jnp.concatenate(g[sel], axis=-1)


def _wkv_scan_kernel(q_ref, y1_ref, m_ref, gm_ref, bonus_ref, gate_ref, lng_ref, lnb_ref,
                     o_ref, st_ref):
    n = HEAD_DIM
    bsz = q_ref.shape[0]

    @pl.when(pl.program_id(0) == 0)
    def _():
        st_ref[...] = jnp.zeros(st_ref.shape, F32)

    c = WKV_CHUNK
    rows = q_ref.shape[1]
    chains = [(bi, h) for bi in range(bsz) for h in range(N_HEADS)]
    state = [st_ref[bi * N_HEADS + h] for bi, h in chains]
    entering = []
    for j in range(rows // c):
        rs = slice(j * c, (j + 1) * c)
        s_in = [s.astype(BF16) for s in state]
        entering.append(s_in)
        state = [_mm(s_in[i], m_ref[bi, rs, h * n:(h + 1) * n]) + gm_ref[bi, rs, h * n:(h + 1) * n]
                 for i, (bi, h) in enumerate(chains)]
    for i, (bi, h) in enumerate(chains):
        st_ref[bi * N_HEADS + h] = state[i]
    ys = [[_mm_nt(q_ref[bi, j * c:(j + 1) * c, h * n:(h + 1) * n], entering[j][i])
           for i, (bi, h) in enumerate(chains)] for j in range(rows // c)]
    y = jnp.concatenate(
        [jnp.concatenate([jnp.concatenate(ys[j][bi * N_HEADS:(bi + 1) * N_HEADS], axis=-1)
                          for j in range(rows // c)], axis=0) + y1_ref[bi]
         for bi in range(bsz)], axis=0)
    mean = _head_sums(y) * (1.0 / n)
    yc = y - mean
    var = _head_sums(yc * yc) * (1.0 / n)
    yn = yc * lax.rsqrt(var + LN_X_EPS) * lng_ref[...] + lnb_ref[...]
    for bi in range(bsz):
        o_ref[bi] = ((yn[bi * rows:(bi + 1) * rows] + bonus_ref[bi]) * gate_ref[bi]).astype(o_ref.dtype)


def _wkv_scan(q, y1, m, gm, bonus, gate, ln_g, ln_b, rows=4 * WKV_CHUNK):
    bsz, seq, w = y1.shape
    tile = pl.BlockSpec((bsz, rows, w), lambda ci: (0, ci, 0))
    vec = pl.BlockSpec((1, w), lambda ci: (0, 0))
    return pl.pallas_call(
        _wkv_scan_kernel,
        grid=(seq // rows,),
        in_specs=[tile] * 6 + [vec] * 2,
        out_specs=tile,
        out_shape=jax.ShapeDtypeStruct((bsz, seq, w), BF16),
        scratch_shapes=[pltpu.VMEM((bsz * N_HEADS, HEAD_DIM, HEAD_DIM), F32)],
        compiler_params=_params("arbitrary"),
        name="wkv_state_scan",
    )(q, y1, m, gm, bonus, gate, ln_g.reshape(1, w), ln_b.reshape(1, w))


def _wout_kernel(x_ref, y0_ref, y1_ref, y2_ref, y3_ref, w_ref, o_ref):
    acc = x_ref[...]
    for idx, y_ref in enumerate((y0_ref, y1_ref, y2_ref, y3_ref)):
        w = w_ref[0, idx * GROUP_WIDTH:(idx + 1) * GROUP_WIDTH, :].astype(BF16)
        acc = acc + jnp.dot(y_ref[...], w, preferred_element_type=F32)
    o_ref[...] = acc


def _wout(x2d, ys, w_stack, layer, tm=512):
    m, d = x2d.shape
    ytile = pl.BlockSpec((tm, GROUP_WIDTH), lambda i: (i, 0))
    xtile = pl.BlockSpec((tm, d), lambda i: (i, 0))
    wspec = pl.BlockSpec((1,) + w_stack.shape[1:], lambda i: (layer, 0, 0))
    return pl.pallas_call(
        _wout_kernel,
        grid=(m // tm,),
        in_specs=[xtile] + [ytile] * 4 + [wspec],
        out_specs=xtile,
        out_shape=jax.ShapeDtypeStruct((m, d), F32),
        compiler_params=_params("parallel", vmem_limit=FFN_VMEM_LIMIT),
        name="wout_residual",
    )(x2d, *ys, w_stack)


def _ffn_kernel(x_ref, g_ref, wu_ref, wd_ref, o_ref, h_ref):
    def down(u):
        act = jnp.square(jnp.maximum(u, 0.0)).astype(BF16)
        return jnp.dot(act, wd_ref[0].astype(BF16), preferred_element_type=F32)

    @pl.when(pl.program_id(1) == 0)
    def _():
        u = None
        for cols, h in _norm_chunks(x_ref, g_ref, h_ref, _rms_scale(x_ref)):
            part = jnp.dot(h, wu_ref[0, cols, :].astype(BF16), preferred_element_type=F32)
            u = part if u is None else u + part
        o_ref[...] = x_ref[...] + down(u)

    @pl.when(pl.program_id(1) > 0)
    def _():
        u = jnp.dot(h_ref[...], wu_ref[0].astype(BF16), preferred_element_type=F32)
        o_ref[...] += down(u)


def _ffn(x2d, gains, layer, wu, wd, tm=1024, tf=1024):
    m, d = x2d.shape
    f = wu.shape[1]
    wu_stack = wu.reshape(1, d, f)
    wd_stack = wd.reshape(1, f, d)
    xtile = pl.BlockSpec((tm, d), lambda i, j: (i, 0))
    return pl.pallas_call(
        _ffn_kernel,
        grid=(m // tm, f // tf),
        in_specs=[
            xtile,
            pl.BlockSpec((1, 1, d), lambda i, j: (layer, 0, 0)),
            pl.BlockSpec((1, d, tf), lambda i, j: (0, 0, j)),
            pl.BlockSpec((1, tf, d), lambda i, j: (0, j, 0)),
        ],
        out_specs=xtile,
        out_shape=jax.ShapeDtypeStruct((m, d), F32),
        scratch_shapes=[pltpu.VMEM((tm, d), BF16)],
        compiler_params=_params("parallel", "arbitrary",
                                vmem_limit=V7X_VMEM_BYTES - 3 * 1024 * 1024),
        name="ffn",
    )(x2d, gains.reshape(-1, 1, d), wu_stack, wd_stack)


def kernel(x, norm_mix, w_in, conv_w, swa_q_norm, swa_k_norm, swa_sink, dil_q_norm, dil_k_norm,
           rwkv_mu, decay_w0, decay_w2, iclr_a0, iclr_a2, gate_g2, k_k, k_a, r_k, ln_x_g, ln_x_b,
           vres_v0, vres_v1, vres_v2, w_out, norm_ffn, w_up, w_down, rel_bias):
    bsz, seq, d = x.shape
    depth = w_in.shape[0]
    bias = _bias_tiles(rel_bias)
    swa_bias = bias[0, :N_HEADS]
    dil_bias = bias[:, N_HEADS:]
    x2d = x.reshape(bsz * seq, d)
    v_first = None
    for layer in range(depth):
        proj = _norm_matmul(x2d, norm_mix, w_in, layer).reshape(bsz, seq, IN_WIDTH)
        y_swa, y_conv = _swa_conv_mixers(proj, swa_q_norm[layer], swa_k_norm[layer],
                                         swa_sink[layer], swa_bias, conv_w[layer])
        y_dil, wu_bf16, wd_bf16 = _dil_mixer(proj, dil_q_norm[layer], dil_k_norm[layer], dil_bias,
                                             w_up, w_down, layer)
        vres = None if layer == 0 else (vres_v0[layer - 1], vres_v1[layer - 1], vres_v2[layer - 1])
        maps = _rwkv_maps(
            proj, rwkv_mu[layer], decay_w0[layer], decay_w2[layer], iclr_a0[layer], iclr_a2[layer],
            gate_g2[layer], k_k[layer], k_a[layer], r_k[layer], v_first, vres)
        if layer == 0:
            v_first = maps[6]
        y_rwkv = _wkv_scan(*maps[:6], ln_x_g[layer], ln_x_b[layer])
        ys = [y.reshape(bsz * seq, GROUP_WIDTH) for y in (y_conv, y_swa, y_dil, y_rwkv)]
        x2d = _wout(x2d, ys, w_out, layer)
        x2d = _ffn(x2d, norm_ffn, layer, wu_bf16, wd_bf16)
    return x2d.reshape(bsz, seq, d)
```

```python
import functools
import math

import jax
import jax.numpy as jnp
from jax import lax
from jax.experimental import pallas as pl
from jax.experimental.pallas import tpu as pltpu

F32 = jnp.float32
BF16 = jnp.bfloat16

HEAD_DIM = 64
GROUP_WIDTH = 512
N_HEADS = GROUP_WIDTH // HEAD_DIM
SWA_KV_HEADS = 2
SWA_GROUP = N_HEADS // SWA_KV_HEADS
SWA_WINDOW = 128
DIL_PAIRS = ((128, 1), (512, 4), (2048, 16))
DECAY_LORA = 64
ICLR_LORA = 64
GATE_LORA = 128
RWKV_IN_WIDTH = 3 * GROUP_WIDTH + DECAY_LORA + ICLR_LORA + GATE_LORA
BLK = 128
NUM_BUCKETS = 32
BUCKET_MAX_DIST = 128
RMS_EPS = 1e-6
LN_X_EPS = 64e-5
NEG = -1e30
LOG2E = math.log2(math.e)
WKV_CHUNK = 64
WKV_GROUP = 4
DIL_TILES = 4
PREV_ROWS = 16
NORM_CHUNKS = 4
PREP_UNROLL = 4

OFF_CONV = 0
OFF_SWA_Q = OFF_CONV + 3 * GROUP_WIDTH
OFF_SWA_K = OFF_SWA_Q + GROUP_WIDTH
OFF_SWA_V = OFF_SWA_K + SWA_KV_HEADS * HEAD_DIM
OFF_DIL = OFF_SWA_V + SWA_KV_HEADS * HEAD_DIM
OFF_RW = OFF_DIL + 3 * GROUP_WIDTH
IN_WIDTH = OFF_RW + RWKV_IN_WIDTH

V7X_VMEM_BYTES = 64 * 1024 * 1024
VMEM_LIMIT = 48 * 1024 * 1024
FFN_VMEM_LIMIT = V7X_VMEM_BYTES - 6 * 1024 * 1024


def _params(*sem, vmem_limit=VMEM_LIMIT):
    return pltpu.CompilerParams(dimension_semantics=sem, vmem_limit_bytes=vmem_limit)


def _mm(a, b):
    return jnp.dot(a.astype(BF16), b.astype(BF16), preferred_element_type=F32)


def _mm_nt(a, b):
    return lax.dot_general(a.astype(BF16), b.astype(BF16), (((1,), (1,)), ((), ())),
                           preferred_element_type=F32)


def _mm_tn(a, b):
    return lax.dot_general(a.astype(BF16), b.astype(BF16), (((0,), (0,)), ((), ())),
                           preferred_element_type=F32)


def _split_bf16(x):
    hi = x.astype(BF16)
    lo = (x - hi.astype(F32)).astype(BF16)
    return hi, lo


def _head_sums(x, split=True):
    lanes = 128
    r = lax.broadcasted_iota(jnp.int32, (lanes, lanes), 0) // HEAD_DIM
    c = lax.broadcasted_iota(jnp.int32, (lanes, lanes), 1) // HEAD_DIM
    bd = (r == c).astype(BF16)
    parts = _split_bf16(x) if split else (x.astype(BF16),)
    cols = []
    for j in range(x.shape[-1] // lanes):
        sl = slice(j * lanes, (j + 1) * lanes)
        cols.append(sum(jnp.dot(part[:, sl], bd, preferred_element_type=F32) for part in parts))
    return cols[0] if len(cols) == 1 else jnp.concatenate(cols, axis=-1)


def _sigmoid(z):
    return 1.0 / (1.0 + jnp.exp(-z))


def _rms_scale(x_ref):
    x = x_ref[...]
    return lax.rsqrt(jnp.mean(x * x, axis=-1, keepdims=True) + RMS_EPS)


def _norm_chunks(x_ref, g_ref, h_ref, scale):
    k = x_ref.shape[1]
    kc = k // NORM_CHUNKS
    for c in range(NORM_CHUNKS):
        cols = slice(c * kc, (c + 1) * kc)
        h = (x_ref[:, cols] * scale * g_ref[0, :, cols]).astype(BF16)
        h_ref[:, cols] = h
        yield cols, h


def _norm_matmul_kernel(x_ref, g_ref, w_ref, o_ref, h_ref):
    @pl.when(pl.program_id(1) == 0)
    def _():
        for _ in _norm_chunks(x_ref, g_ref, h_ref, _rms_scale(x_ref)):
            pass

    o_ref[...] = jnp.dot(h_ref[...], w_ref[0].astype(BF16),
                         preferred_element_type=F32).astype(o_ref.dtype)


def _norm_matmul(x2d, gains, w_stack, layer, tm=2048, tn=512):
    m, k = x2d.shape
    n = w_stack.shape[2]
    assert n % tn == 0 and m % tm == 0
    return pl.pallas_call(
        _norm_matmul_kernel,
        grid=(m // tm, n // tn),
        in_specs=[
            pl.BlockSpec((tm, k), lambda i, j: (i, 0)),
            pl.BlockSpec((1, 1, k), lambda i, j: (layer, 0, 0)),
            pl.BlockSpec((1, k, tn), lambda i, j: (layer, 0, j)),
        ],
        out_specs=pl.BlockSpec((tm, tn), lambda i, j: (i, j)),
        out_shape=jax.ShapeDtypeStruct((m, n), BF16),
        scratch_shapes=[pltpu.VMEM((tm, k), BF16)],
        compiler_params=_params("parallel", "arbitrary", vmem_limit=FFN_VMEM_LIMIT),
        name="norm_matmul",
    )(x2d, gains.reshape(-1, 1, k), w_stack)


def _t5_bucket(dist):
    dist = jnp.maximum(dist, 0)
    max_exact = NUM_BUCKETS // 2
    scaled = (jnp.log(jnp.maximum(dist, 1).astype(F32) / max_exact)
              / math.log(BUCKET_MAX_DIST / max_exact))
    large = max_exact + (scaled * (NUM_BUCKETS - max_exact)).astype(jnp.int32)
    large = jnp.minimum(large, NUM_BUCKETS - 1)
    return jnp.where(dist < max_exact, dist, large)


def _bias_kernel(bucket_ref, table_ref, o_ref):
    bucket = bucket_ref[0]

    def head(h, carry):
        g = jnp.zeros(bucket.shape, F32)
        for b in range(NUM_BUCKETS):
            g = jnp.where(bucket == b, table_ref[b, h], g)
        rows = jnp.broadcast_to(g[0:1], (BLK, 2 * BLK))
        o_ref[0, h] = pltpu.roll(rows, 0, axis=1, stride=1, stride_axis=0)
        return carry

    lax.fori_loop(0, o_ref.shape[1], head, 0)


def _bias_tiles(rel_bias):
    lag_dist = BLK - jnp.arange(2 * BLK)
    buckets = jnp.stack([_t5_bucket(lag_dist * r) for _, r in DIL_PAIRS]).astype(jnp.int32)
    buckets = jnp.broadcast_to(buckets[:, None, :], (len(DIL_PAIRS), 8, 2 * BLK))
    nh = rel_bias.shape[1]
    return pl.pallas_call(
        _bias_kernel,
        grid=(len(DIL_PAIRS),),
        in_specs=[
            pl.BlockSpec((1, 8, 2 * BLK), lambda s: (s, 0, 0)),
            pl.BlockSpec(memory_space=pltpu.SMEM),
        ],
        out_specs=pl.BlockSpec((1, nh, BLK, 2 * BLK), lambda s: (s, 0, 0, 0)),
        out_shape=jax.ShapeDtypeStruct((len(DIL_PAIRS), nh, BLK, 2 * BLK), F32),
        compiler_params=_params("arbitrary"),
        name="bias_tiles",
    )(buckets, rel_bias)


def _short_conv(b_ref, c_ref, u_ref, w_ref, o_ref):
    lanes = 128
    for j in range(b_ref.shape[2] // lanes):
        cols = slice(j * lanes, (j + 1) * lanes)
        z = c_ref[0, :, cols].astype(F32) * u_ref[0, :, cols].astype(F32)
        row = lax.broadcasted_iota(jnp.int32, z.shape, 0)
        z1 = jnp.where(row >= 1, pltpu.roll(z, 1, axis=0), 0.0)
        z2 = jnp.where(row >= 2, pltpu.roll(z, 2, axis=0), 0.0)
        w = w_ref[:, cols]
        y = z2 * w[0:1, :] + z1 * w[1:2, :] + z * w[2:3, :]
        o_ref[0, :, cols] = (b_ref[0, :, cols].astype(F32) * y).astype(o_ref.dtype)


def _band_mask(max_dist):
    a = lax.broadcasted_iota(jnp.int32, (BLK, 2 * BLK), 0)
    b = lax.broadcasted_iota(jnp.int32, (BLK, 2 * BLK), 1)
    dist = BLK + a - b
    return (dist >= 0) & (dist <= max_dist), b


def _attend(qs, kws, vws, biases, normalize=True):
    idx = range(len(qs))
    s = [lax.dot_general(qs[i], kws[i], (((1,), (1,)), ((), ())), preferred_element_type=F32)
         + biases[i] for i in idx]
    m = [jnp.max(s[i], axis=-1, keepdims=True) for i in idx]
    p = [jnp.exp2(s[i] - m[i]) for i in idx]
    den = [jnp.sum(p[i], axis=-1, keepdims=True) for i in idx]
    o = [jnp.dot(p[i].astype(BF16), vws[i], preferred_element_type=F32) for i in idx]
    if normalize:
        o = [o[i] / den[i] for i in idx]
    return o, m, den


def _head_rms(x, gain):
    ms = _head_sums(x * x, split=False) * (1.0 / HEAD_DIM)
    return x * lax.rsqrt(ms + RMS_EPS) * gain


def _swa_kernel(q_ref, k_ref, v_ref, qg_ref, kg_ref, sink_ref, bias_ref,
                cb_ref, cc_ref, cu_ref, cw_ref, o_ref, oc_ref,
                qn_ref, kn_ref, vb_ref, bm_ref):
    seq = q_ref.shape[1]
    nb = seq // BLK
    kvw = SWA_KV_HEADS * HEAD_DIM
    scale = HEAD_DIM ** -0.5 * LOG2E

    _short_conv(cb_ref, cc_ref, cu_ref, cw_ref, oc_ref)

    kn_ref[0:BLK, :] = jnp.zeros((BLK, 2 * kvw), BF16)
    vb_ref[0:BLK, :] = jnp.zeros((BLK, 2 * kvw), BF16)
    lane_half = lax.broadcasted_iota(jnp.int32, (1, kvw), 1) // HEAD_DIM

    def both_halves(x):
        xr = pltpu.roll(x, HEAD_DIM, axis=1)
        return jnp.concatenate([jnp.where(lane_half == 0, x, xr),
                                jnp.where(lane_half == 0, xr, x)], axis=-1)

    def prep(i, carry):
        r0 = pl.multiple_of(i * BLK, BLK)
        q = q_ref[0, pl.ds(r0, BLK), :].astype(F32)
        qn_ref[pl.ds(r0, BLK), :] = (_head_rms(q, qg_ref[...]) * scale).astype(BF16)
        k = _head_rms(k_ref[0, pl.ds(r0, BLK), :].astype(F32), kg_ref[...])
        kn_ref[pl.ds(r0 + BLK, BLK), :] = both_halves(k).astype(BF16)
        v = v_ref[0, pl.ds(r0, BLK), :].astype(F32)
        vb_ref[pl.ds(r0 + BLK, BLK), :] = both_halves(v).astype(BF16)
        return carry

    lax.fori_loop(0, nb, prep, 0, unroll=PREP_UNROLL)

    assert SWA_WINDOW <= BLK
    band, kcol = _band_mask(SWA_WINDOW - 1)
    for h in range(N_HEADS):
        sink = sink_ref[h] * LOG2E
        bias = bias_ref[h] * LOG2E
        bm_ref[h] = jnp.where(kcol == 0, sink, jnp.where(band & (kcol >= BLK), bias, NEG))
        bm_ref[N_HEADS + h] = jnp.where(kcol == 0, sink, jnp.where(band, bias, NEG))
    first_key = lax.broadcasted_iota(jnp.int32, (2 * BLK, kvw), 0) == 0

    def block(i, carry):
        r0 = pl.multiple_of(i * BLK, BLK)
        later = jnp.minimum(i, 1) * N_HEADS
        qs, kws, vws, bms = [], [], [], []
        for hk in range(SWA_KV_HEADS):
            kw = kn_ref[pl.ds(r0, 2 * BLK), hk * kvw:(hk + 1) * kvw]
            vw = vb_ref[pl.ds(r0, 2 * BLK), hk * kvw:(hk + 1) * kvw]
            kw = jnp.where(first_key, jnp.zeros_like(kw), kw)
            vw = jnp.where(first_key, jnp.zeros_like(vw), vw)
            for g in range(SWA_GROUP):
                h = hk * SWA_GROUP + g
                q2 = qn_ref[pl.ds(r0, BLK), (h // 2) * kvw:(h // 2 + 1) * kvw]
                qs.append(jnp.where(lane_half == h % 2, q2, jnp.zeros_like(q2)))
                kws.append(kw)
                vws.append(vw)
                bms.append(bm_ref[later + h])
        outs, _, _ = _attend(qs, kws, vws, bms)
        pairs = [jnp.where(lane_half == 0, outs[h], outs[h + 1]) for h in range(0, N_HEADS, 2)]
        o_ref[0, pl.ds(r0, BLK), :] = jnp.concatenate(pairs, axis=-1).astype(o_ref.dtype)
        return carry

    lax.fori_loop(0, nb, block, 0)


def _swa_conv_mixers(proj, q_gain, k_gain, sink, bias, conv_w):
    bsz, seq, _ = proj.shape
    kvw = SWA_KV_HEADS * HEAD_DIM
    q_gain_t = jnp.tile(q_gain, N_HEADS).reshape(1, GROUP_WIDTH)
    k_gain_t = jnp.tile(k_gain, SWA_KV_HEADS).reshape(1, kvw)
    wide = lambda off: pl.BlockSpec((1, seq, GROUP_WIDTH), lambda b: (b, 0, off // GROUP_WIDTH))
    out = jax.ShapeDtypeStruct((bsz, seq, GROUP_WIDTH), BF16)
    return pl.pallas_call(
        _swa_kernel,
        grid=(bsz,),
        in_specs=[
            wide(OFF_SWA_Q),
            pl.BlockSpec((1, seq, kvw), lambda b: (b, 0, OFF_SWA_K // kvw)),
            pl.BlockSpec((1, seq, kvw), lambda b: (b, 0, OFF_SWA_V // kvw)),
            pl.BlockSpec((1, GROUP_WIDTH), lambda b: (0, 0)),
            pl.BlockSpec((1, kvw), lambda b: (0, 0)),
            pl.BlockSpec(memory_space=pltpu.SMEM),
            pl.BlockSpec((N_HEADS, BLK, 2 * BLK), lambda b: (0, 0, 0)),
            wide(OFF_CONV), wide(OFF_CONV + GROUP_WIDTH), wide(OFF_CONV + 2 * GROUP_WIDTH),
            pl.BlockSpec((3, GROUP_WIDTH), lambda b: (0, 0)),
        ],
        out_specs=[pl.BlockSpec((1, seq, GROUP_WIDTH), lambda b: (b, 0, 0))] * 2,
        out_shape=[out, out],
        scratch_shapes=[
            pltpu.VMEM((seq, GROUP_WIDTH), BF16),
            pltpu.VMEM((seq + BLK, 2 * kvw), BF16),
            pltpu.VMEM((seq + BLK, 2 * kvw), BF16),
            pltpu.VMEM((2 * N_HEADS, BLK, 2 * BLK), F32),
        ],
        compiler_params=_params("parallel"),
        name="swa_conv_mixers",
    )(proj, proj, proj, q_gain_t, k_gain_t, sink, bias, proj, proj, proj, conv_w)


def _dil_kernel(q_ref, k_ref, v_ref, qg_ref, kg_ref, bias_ref, wu_ref, wd_ref,
                o_ref, wu_out, wd_out,
                qn_ref, kn_ref, vn_ref, ob_ref, mb_ref, db_ref, bm_ref):
    wu_out[...] = wu_ref[0].astype(BF16)
    wd_out[...] = wd_ref[0].astype(BF16)
    seq = q_ref.shape[1]
    lanes = q_ref.shape[2]
    heads = lanes // HEAD_DIM
    scale = HEAD_DIM ** -0.5 * LOG2E

    def prep(i, carry):
        r0 = pl.multiple_of(i * BLK, BLK)
        q = q_ref[0, pl.ds(r0, BLK), :].astype(F32)
        k = k_ref[0, pl.ds(r0, BLK), :].astype(F32)
        qn_ref[pl.ds(r0, BLK), :] = _head_rms(q, qg_ref[...]) * scale
        kn_ref[pl.ds(r0, BLK), :] = _head_rms(k, kg_ref[...])
        vn_ref[pl.ds(r0, BLK), :] = v_ref[0, pl.ds(r0, BLK), :].astype(F32)
        return carry

    lax.fori_loop(0, seq // BLK, prep, 0, unroll=PREP_UNROLL)

    for br, (window, r) in enumerate(DIL_PAIRS):
        band, kcol = _band_mask(window // r)
        for h in range(heads):
            bias = bias_ref[br, h] * LOG2E
            bm_ref[(br * heads + h) * 2] = jnp.where(band & (kcol >= BLK), bias, NEG)
            bm_ref[(br * heads + h) * 2 + 1] = jnp.where(band, bias, NEG)

    lane_head = lax.broadcasted_iota(jnp.int32, (1, lanes), 1) // HEAD_DIM

    for br, (window, r) in enumerate(DIL_PAIRS):
        nb = seq // r // BLK

        def blocks(it, carry, br=br, r=r, nb=nb):
            qs, kws, vws, bms, curs = [], [], [], [], []
            for u in range(DIL_TILES):
                t = it * DIL_TILES + u
                c = t // nb
                i = t - c * nb
                cur = c + i * (BLK * r)
                prev = jnp.maximum(cur - BLK * r, c)
                later = jnp.minimum(i, 1)

                def rows(ref, start):
                    if r == 1:
                        return ref[pl.ds(start, BLK), :]
                    return ref[pl.ds(start, BLK, stride=r), :]

                q = rows(qn_ref, cur).astype(BF16)
                if nb == 1:
                    kw = rows(kn_ref, cur).astype(BF16)
                    vw = rows(vn_ref, cur).astype(BF16)
                else:
                    kw = jnp.concatenate([rows(kn_ref, prev), rows(kn_ref, cur)],
                                         axis=0).astype(BF16)
                    vw = jnp.concatenate([rows(vn_ref, prev), rows(vn_ref, cur)],
                                         axis=0).astype(BF16)
                curs.append(cur)
                for h in range(heads):
                    qs.append(jnp.where(lane_head == h, q, jnp.zeros_like(q)))
                    kws.append(kw)
                    vws.append(vw)
                    if nb == 1:
                        bms.append(bm_ref[(br * heads + h) * 2, :, BLK:])
                    else:
                        bms.append(bm_ref[(br * heads + h) * 2 + later])
            outs, ms, dens = _attend(qs, kws, vws, bms, normalize=False)
            for u in range(DIL_TILES):
                o_all, m_all, d_all = outs[u * heads], ms[u * heads], dens[u * heads]
                for h in range(1, heads):
                    o_all = jnp.where(lane_head == h, outs[u * heads + h], o_all)
                    m_all = jnp.where(lane_head == h, ms[u * heads + h], m_all)
                    d_all = jnp.where(lane_head == h, dens[u * heads + h], d_all)
                if r == 1:
                    rows = pl.ds(curs[u], BLK)
                else:
                    rows = pl.ds(curs[u], BLK, stride=r)
                ob_ref[br, rows, :] = o_all
                mb_ref[br, rows, :] = m_all
                db_ref[br, rows, :] = d_all
            return carry

        lax.fori_loop(0, r * nb // DIL_TILES, blocks, 0)

    def combine(i, carry):
        r0 = pl.multiple_of(i * BLK, BLK)
        rows = pl.ds(r0, BLK)
        m0, m1, m2 = mb_ref[0, rows, :], mb_ref[1, rows, :], mb_ref[2, rows, :]
        m = jnp.maximum(jnp.maximum(m0, m1), m2)
        e0, e1, e2 = jnp.exp2(m0 - m), jnp.exp2(m1 - m), jnp.exp2(m2 - m)
        num = e0 * ob_ref[0, rows, :] + e1 * ob_ref[1, rows, :] + e2 * ob_ref[2, rows, :]
        den = e0 * db_ref[0, rows, :] + e1 * db_ref[1, rows, :] + e2 * db_ref[2, rows, :]
        o_ref[0, rows, :] = (num / den).astype(o_ref.dtype)
        return carry

    lax.fori_loop(0, seq // BLK, combine, 0, unroll=PREP_UNROLL)


def _dil_mixer(proj, q_gain, k_gain, bias, w_up, w_down, layer):
    bsz, seq, _ = proj.shape
    lanes = 128
    heads = lanes // HEAD_DIM
    nblk = GROUP_WIDTH // lanes
    base = OFF_DIL // lanes
    gq = jnp.tile(q_gain, heads).reshape(1, lanes)
    gk = jnp.tile(k_gain, heads).reshape(1, lanes)
    _, d, f = w_up.shape
    steps = bsz * nblk
    assert d % steps == 0 and f % steps == 0

    def col(seg):
        return pl.BlockSpec((1, seq, lanes), lambda b, j: (b, 0, base + seg * nblk + j))

    return pl.pallas_call(
        _dil_kernel,
        grid=(bsz, nblk),
        in_specs=[
            col(0), col(1), col(2),
            pl.BlockSpec((1, lanes), lambda b, j: (0, 0)),
            pl.BlockSpec((1, lanes), lambda b, j: (0, 0)),
            pl.BlockSpec((len(DIL_PAIRS), heads, BLK, 2 * BLK), lambda b, j: (0, j, 0, 0)),
            pl.BlockSpec((1, d // steps, f), lambda b, j: (layer, b * nblk + j, 0)),
            pl.BlockSpec((1, f // steps, d), lambda b, j: (layer, b * nblk + j, 0)),
        ],
        out_specs=[
            pl.BlockSpec((1, seq, lanes), lambda b, j: (b, 0, j)),
            pl.BlockSpec((d // steps, f), lambda b, j: (b * nblk + j, 0)),
            pl.BlockSpec((f // steps, d), lambda b, j: (b * nblk + j, 0)),
        ],
        out_shape=[
            jax.ShapeDtypeStruct((bsz, seq, GROUP_WIDTH), BF16),
            jax.ShapeDtypeStruct((d, f), BF16),
            jax.ShapeDtypeStruct((f, d), BF16),
        ],
        scratch_shapes=[
            pltpu.VMEM((seq, lanes), F32),
            pltpu.VMEM((seq, lanes), F32),
            pltpu.VMEM((seq, lanes), F32),
            pltpu.VMEM((len(DIL_PAIRS), seq, lanes), F32),
            pltpu.VMEM((len(DIL_PAIRS), seq, lanes), F32),
            pltpu.VMEM((len(DIL_PAIRS), seq, lanes), F32),
            pltpu.VMEM((len(DIL_PAIRS) * heads * 2, BLK, 2 * BLK), F32),
        ],
        compiler_params=_params("parallel", "parallel"),
        name="dil_mixer",
    )(proj, proj, proj, gq, gk, bias, w_up, w_down)


def _rwkv_maps_kernel(*refs, has_vres):
    if has_vres:
        (p_ref, pp_ref, mu_ref, w0_ref, w2_ref, a0_ref, a2_ref, g2_ref, kk_ref, ka_ref, rk_ref,
         lng_ref, lnb_ref, vf_ref, v0_ref, v1_ref, v2_ref, o_ref, *scratch) = refs
    else:
        (p_ref, pp_ref, mu_ref, w0_ref, w2_ref, a0_ref, a2_ref, g2_ref, kk_ref, ka_ref, rk_ref,
         lng_ref, lnb_ref, o_ref, v_out, *scratch) = refs
    *scratch, q_s, y1_s, m_s, gm_s, st_ref = scratch

    @pl.when(pl.program_id(1) == 0)
    def _():
        st_ref[...] = jnp.zeros(st_ref.shape, F32)

    w = GROUP_WIDTH
    p = p_ref[0].astype(F32)
    row = lax.broadcasted_iota(jnp.int32, p.shape, 0)
    last_prev = pp_ref[0, PREV_ROWS - 1:PREV_ROWS, :].astype(F32)
    last_prev = jnp.where(pl.program_id(1) > 0, last_prev, 0.0)
    prev = jnp.where(row >= 1, pltpu.roll(p, 1, axis=0), last_prev)
    xs = p + (prev - p) * mu_ref[...]
    grows = WKV_GROUP * WKV_CHUNK
    for grp in range(p.shape[0] // grows):
        rs = slice(grp * grows, (grp + 1) * grows)
        x = xs[rs]
        r = x[:, 0:w]
        k = x[:, w:2 * w]
        v = x[:, 2 * w:3 * w]
        o = 3 * w
        wd = x[:, o:o + DECAY_LORA]
        ad = x[:, o + DECAY_LORA:o + DECAY_LORA + ICLR_LORA]
        gd = x[:, o + DECAY_LORA + ICLR_LORA:]

        z = -(w0_ref[...] + _mm(jnp.tanh(wd), w2_ref[...]))
        softplus = jnp.maximum(z, 0.0) + jnp.log(1.0 + jnp.exp(-jnp.abs(z)))
        logw = -softplus - 0.5
        lw = -jnp.exp(logw)
        a = _sigmoid(a0_ref[...] + _mm(ad, a2_ref[...]))
        gate = _mm(_sigmoid(gd), g2_ref[...])
        if has_vres:
            mix = _sigmoid(v0_ref[...] + _mm(_mm(v, v1_ref[...]), v2_ref[...]))
            v = v + (vf_ref[0, rs] - v) * mix
        else:
            v_out[0, rs] = v
        kk = k * kk_ref[...]
        ss = _head_sums(kk * kk, split=False)
        kk = kk * lax.rsqrt(jnp.maximum(ss, 1e-24))
        k = k * (1.0 + (a - 1.0) * ka_ref[...])
        bonus = _head_sums(r * k * rk_ref[...]) * v
        _chunk_maps(grp, r, lw, k, v, kk, kk * a, q_s, y1_s, m_s, gm_s, *scratch)
        y = _walk_chunks(range(grp * WKV_GROUP, (grp + 1) * WKV_GROUP), q_s, y1_s, m_s, gm_s, st_ref)
        mean = _head_sums(y) * (1.0 / HEAD_DIM)
        yc = y - mean
        var = _head_sums(yc * yc) * (1.0 / HEAD_DIM)
        yn = yc * lax.rsqrt(var + LN_X_EPS) * lng_ref[...] + lnb_ref[...]
        o_ref[0, rs] = ((yn + bonus) * gate).astype(o_ref.dtype)


def _walk_chunks(chunks, q_ref, y1_ref, m_ref, gm_ref, st_ref):
    c = WKV_CHUNK
    w2 = 2 * HEAD_DIM
    pairs = range(N_HEADS // 2)
    state = [st_ref[p] for p in pairs]
    entering = []
    for j in chunks:
        rs = slice(j * w2, (j + 1) * w2)
        s_in = [s.astype(BF16) for s in state]
        entering.append(s_in)
        state = [_mm(m_ref[rs, p * w2:(p + 1) * w2], s_in[p]) + gm_ref[rs, p * w2:(p + 1) * w2]
                 for p in pairs]
    for p in pairs:
        st_ref[p] = state[p]
    ys = [jnp.concatenate([_mm(q_ref[j * c:(j + 1) * c, p * w2:(p + 1) * w2], entering[ci][p])
                           for p in pairs], axis=-1) + y1_ref[j * c:(j + 1) * c, :]
          for ci, j in enumerate(chunks)]
    return jnp.concatenate(ys, axis=0)


def _rwkv_maps(proj, mu, w0, w2, a0, a2, g2, k_k, k_a, r_k, ln_g, ln_b, v_first, vres, ts=512):
    bsz, seq, _ = proj.shape
    w = GROUP_WIDTH
    nt = seq // ts
    has_vres = vres is not None

    def full(shape):
        return pl.BlockSpec(shape, lambda b, i: (0,) * len(shape))

    row = lambda a: a.reshape(1, -1)
    tile = pl.BlockSpec((1, ts, w), lambda b, i: (b, i, 0))
    in_specs = [
        pl.BlockSpec((pl.Element(1), pl.Element(ts), pl.Element(RWKV_IN_WIDTH)),
                     lambda b, i: (b, i * ts, OFF_RW)),
        pl.BlockSpec((pl.Element(1), pl.Element(PREV_ROWS), pl.Element(RWKV_IN_WIDTH)),
                     lambda b, i: (b, jnp.maximum(i * (ts // PREV_ROWS) - 1, 0) * PREV_ROWS, OFF_RW)),
        full((1, RWKV_IN_WIDTH)), full((1, w)), full((DECAY_LORA, w)), full((1, w)),
        full((ICLR_LORA, w)), full((GATE_LORA, w)), full((1, w)), full((1, w)), full((1, w)),
        full((1, w)), full((1, w)),
    ]
    args = [proj, proj, row(mu), row(w0), w2, row(a0), a2, g2, row(k_k), row(k_a), row(r_k),
            row(ln_g), row(ln_b)]
    if has_vres:
        v0, v1, v2 = vres
        in_specs += [tile, full((1, w)), full(v1.shape), full(v2.shape)]
        args += [v_first, row(v0), v1, v2]
    f32 = jax.ShapeDtypeStruct((bsz, seq, w), F32)
    bf16 = jax.ShapeDtypeStruct((bsz, seq, w), BF16)
    out_shape = [bf16] + ([] if has_vres else [f32])
    return pl.pallas_call(
        functools.partial(_rwkv_maps_kernel, has_vres=has_vres),
        grid=(bsz, nt),
        in_specs=in_specs,
        out_specs=[tile] * len(out_shape),
        out_shape=out_shape,
        scratch_shapes=[pltpu.VMEM((ts, w), dt) for dt in (BF16, F32, BF16, BF16, F32, BF16, BF16,
                                                           BF16, F32)]
        + [pltpu.VMEM((2 * ts, w), BF16), pltpu.VMEM((2 * ts, w), F32),
           pltpu.VMEM((N_HEADS // 2, 2 * HEAD_DIM, 2 * HEAD_DIM), F32)],
        compiler_params=_params("parallel", "arbitrary"),
        name="rwkv_chunk_maps",
    )(*args)


def _chunk_maps(grp, r, lw, k, v, kk, b, q_out, y1_out, m_out, g_out,
                at_ref, rt_ref, bt_ref, kt_ref, dec_ref, vb_ref, rb_ref):
    c = WKV_CHUNK
    n = HEAD_DIM
    rows = r.shape[0]
    rs = slice(grp * rows, (grp + 1) * rows)

    row = lax.broadcasted_iota(jnp.int32, (rows, rows), 0)
    col = lax.broadcasted_iota(jnp.int32, (rows, rows), 1)
    tri = ((row >= col) & ((row // c) == (col // c))).astype(BF16)
    lw_hi, lw_lo = _split_bf16(lw)
    cum = (jnp.dot(tri, lw_hi, preferred_element_type=F32)
           + jnp.dot(tri, lw_lo, preferred_element_type=F32))
    e_pos = jnp.exp(cum)
    e_neg = jnp.exp(-cum)
    r_t = r * e_pos
    at_ref[rs] = (-kk * jnp.exp(cum - lw)).astype(BF16)
    rt_ref[rs] = r_t
    rb_ref[rs] = r_t.astype(BF16)
    bt_ref[rs] = (b * e_neg).astype(BF16)
    kt_ref[rs] = (k * e_neg).astype(BF16)
    dec_ref[rs] = e_pos
    vb_ref[rs] = v.astype(BF16)

    _chunk_group_pairs(range(grp * WKV_GROUP, (grp + 1) * WKV_GROUP), at_ref, rt_ref, bt_ref,
                       kt_ref, dec_ref, vb_ref, rb_ref, q_out, y1_out, m_out, g_out)


def _chunk_group_pairs(chunks, at_ref, rt_ref, bt_ref, kt_ref, dec_ref, vb_ref, rb_ref,
                       q_out, y1_out, m_out, g_out):
    c = WKV_CHUNK
    n = HEAD_DIM
    assert c == n
    w2 = 2 * n
    units = [(j, p) for j in chunks for p in range(N_HEADS // 2)]
    idx = range(len(units))

    def tile(ref, i):
        j, p = units[i]
        return ref[j * c:(j + 1) * c, p * w2:(p + 1) * w2]

    left = lax.broadcasted_iota(jnp.int32, (1, w2), 1) < n
    row2 = lax.broadcasted_iota(jnp.int32, (2 * c, 2 * c), 0)
    col2 = lax.broadcasted_iota(jnp.int32, (2 * c, 2 * c), 1)
    keep = (row2 % c > col2 % c) | ((row2 >= c) & (row2 % c == col2 % c))
    wrow = lax.broadcasted_iota(jnp.int32, (c, w2), 0)
    wcol = lax.broadcasted_iota(jnp.int32, (c, w2), 1)
    eye_left = (wcol == wrow).astype(F32)
    eye_right = (wcol == wrow + c).astype(F32)

    def only(mask, x):
        return jnp.where(mask, x, jnp.zeros_like(x))

    def rows_ba(x):
        return jnp.concatenate([only(~left, x), only(left, x)], axis=0)

    def rows_ab(x):
        return jnp.concatenate([only(left, x), only(~left, x)], axis=0)

    a = [tile(at_ref, i) for i in idx]
    b = [tile(bt_ref, i) for i in idx]
    k = [tile(kt_ref, i) for i in idx]
    v = [tile(vb_ref, i) for i in idx]
    ar = [jnp.concatenate([a[i], tile(rb_ref, i)], axis=0) for i in idx]
    bk = [jnp.concatenate([b[i], k[i]], axis=0) for i in idx]
    kb = [jnp.concatenate([k[i], b[i]], axis=0) for i in idx]
    aa_a = [jnp.where(keep, _mm_nt(only(left, ar[i]), bk[i]), 0.0) for i in idx]
    aa_b = [jnp.where(keep, _mm_nt(only(~left, ar[i]), kb[i]), 0.0) for i in idx]
    ps_a = [only(left, aa_a[i][:c]) + eye_right for i in idx]
    ps_b = [only(~left, aa_b[i][:c]) + eye_left for i in idx]
    zeros = jnp.zeros((c, w2), BF16)
    for _ in range(int(math.log2(c))):
        nxt_a, nxt_b = [], []
        for i in idx:
            pa, pb = ps_a[i].astype(BF16), ps_b[i].astype(BF16)
            rhs = jnp.concatenate([jnp.concatenate([pa, zeros], axis=1),
                                   jnp.concatenate([zeros, pb], axis=1)], axis=0)
            prod = jnp.dot(jnp.where(left, pa, pb), rhs, preferred_element_type=F32)
            nxt_a.append(prod[:, :w2] + only(~left, ps_a[i]))
            nxt_b.append(prod[:, w2:] + only(left, ps_b[i]))
        ps_a, ps_b = nxt_a, nxt_b
    t_ba = [jnp.where(left, ps_b[i], ps_a[i]).astype(BF16) for i in idx]
    av = [_mm(jnp.where(left, aa_b[i], aa_a[i]), rows_ba(v[i])) for i in idx]
    av_u = [av[i][:c].astype(BF16) for i in idx]
    wu = [_mm(t_ba[i], jnp.concatenate([rows_ba(a[i]), rows_ba(av_u[i])], axis=1)).astype(BF16)
          for i in idx]
    a_rb = [jnp.where(left, aa_a[i][c:], aa_b[i][c:]) for i in idx]
    aw = [_mm(a_rb[i], jnp.concatenate([rows_ab(wu[i][:, :w2]), rows_ab(wu[i][:, w2:])], axis=1))
          for i in idx]
    q = [tile(rt_ref, i) + aw[i][:, :w2] for i in idx]
    y1 = [aw[i][:, w2:] + av[i][c:] for i in idx]
    x1 = [_mm_tn(b[i], wu[i][:, :w2]) for i in idx]
    x2 = [_mm_tn(bk[i], jnp.concatenate([wu[i][:, w2:], v[i]], axis=0)) for i in idx]
    diag = row2 == col2
    same_head = (row2 // n) == (col2 // n)
    eye2 = diag.astype(F32)
    m, g = [], []
    for i in idx:
        j, p = units[i]
        d = dec_ref[(j + 1) * c - 1:(j + 1) * c, p * w2:(p + 1) * w2]
        d_col = jnp.sum(jnp.where(diag, d, 0.0), axis=1, keepdims=True)
        m.append((eye2 + jnp.where(same_head, x1[i], 0.0)) * d_col)
        g.append(jnp.where(same_head, x2[i], 0.0) * d_col)
    per = N_HEADS // 2
    for ci, j in enumerate(chunks):
        sel = slice(ci * per, (ci + 1) * per)
        q_out[j * c:(j + 1) * c, :] = jnp.concatenate(q[sel], axis=-1).astype(q_out.dtype)
        y1_out[j * c:(j + 1) * c, :] = jnp.concatenate(y1[sel], axis=-1)
        m_out[j * w2:(j + 1) * w2, :] = jnp.concatenate(m[sel], axis=-1).astype(m_out.dtype)
        g_out[j * w2:(j + 1) * w2, :] = jnp.concatenate(g[sel], axis=-1)


def _wout_kernel(x_ref, y0_ref, y1_ref, y2_ref, y3_ref, w_ref, o_ref):
    acc = x_ref[...]
    for idx, y_ref in enumerate((y0_ref, y1_ref, y2_ref, y3_ref)):
        w = w_ref[0, idx * GROUP_WIDTH:(idx + 1) * GROUP_WIDTH, :].astype(BF16)
        acc = acc + jnp.dot(y_ref[...], w, preferred_element_type=F32)
    o_ref[...] = acc


def _wout(x2d, ys, w_stack, layer, tm=512):
    m, d = x2d.shape
    ytile = pl.BlockSpec((tm, GROUP_WIDTH), lambda i: (i, 0))
    xtile = pl.BlockSpec((tm, d), lambda i: (i, 0))
    wspec = pl.BlockSpec((1,) + w_stack.shape[1:], lambda i: (layer, 0, 0))
    return pl.pallas_call(
        _wout_kernel,
        grid=(m // tm,),
        in_specs=[xtile] + [ytile] * 4 + [wspec],
        out_specs=xtile,
        out_shape=jax.ShapeDtypeStruct((m, d), F32),
        compiler_params=_params("parallel", vmem_limit=FFN_VMEM_LIMIT),
        name="wout_residual",
    )(x2d, *ys, w_stack)


def _ffn_kernel(x_ref, g_ref, wu_ref, wd_ref, o_ref, h_ref):
    def down(u):
        act = jnp.square(jnp.maximum(u, 0.0)).astype(BF16)
        return jnp.dot(act, wd_ref[0].astype(BF16), preferred_element_type=F32)

    @pl.when(pl.program_id(1) == 0)
    def _():
        u = None
        for cols, h in _norm_chunks(x_ref, g_ref, h_ref, _rms_scale(x_ref)):
            part = jnp.dot(h, wu_ref[0, cols, :].astype(BF16), preferred_element_type=F32)
            u = part if u is None else u + part
        o_ref[...] = x_ref[...] + down(u)

    @pl.when(pl.program_id(1) > 0)
    def _():
        u = jnp.dot(h_ref[...], wu_ref[0].astype(BF16), preferred_element_type=F32)
        o_ref[...] += down(u)


def _ffn(x2d, gains, layer, wu, wd, tm=1024, tf=1024):
    m, d = x2d.shape
    f = wu.shape[1]
    wu_stack = wu.reshape(1, d, f)
    wd_stack = wd.reshape(1, f, d)
    xtile = pl.BlockSpec((tm, d), lambda i, j: (i, 0))
    return pl.pallas_call(
        _ffn_kernel,
        grid=(m // tm, f // tf),
        in_specs=[
            xtile,
            pl.BlockSpec((1, 1, d), lambda i, j: (layer, 0, 0)),
            pl.BlockSpec((1, d, tf), lambda i, j: (0, 0, j)),
            pl.BlockSpec((1, tf, d), lambda i, j: (0, j, 0)),
        ],
        out_specs=xtile,
        out_shape=jax.ShapeDtypeStruct((m, d), F32),
        scratch_shapes=[pltpu.VMEM((tm, d), BF16)],
        compiler_params=_params("parallel", "arbitrary",
                                vmem_limit=V7X_VMEM_BYTES - 3 * 1024 * 1024),
        name="ffn",
    )(x2d, gains.reshape(-1, 1, d), wu_stack, wd_stack)


def kernel(x, norm_mix, w_in, conv_w, swa_q_norm, swa_k_norm, swa_sink, dil_q_norm, dil_k_norm,
           rwkv_mu, decay_w0, decay_w2, iclr_a0, iclr_a2, gate_g2, k_k, k_a, r_k, ln_x_g, ln_x_b,
           vres_v0, vres_v1, vres_v2, w_out, norm_ffn, w_up, w_down, rel_bias):
    bsz, seq, d = x.shape
    depth = w_in.shape[0]
    bias = _bias_tiles(rel_bias)
    swa_bias = bias[0, :N_HEADS]
    dil_bias = bias[:, N_HEADS:]
    x2d = x.reshape(bsz * seq, d)
    v_first = None
    for layer in range(depth):
        proj = _norm_matmul(x2d, norm_mix, w_in, layer).reshape(bsz, seq, IN_WIDTH)
        y_swa, y_conv = _swa_conv_mixers(proj, swa_q_norm[layer], swa_k_norm[layer],
                                         swa_sink[layer], swa_bias, conv_w[layer])
        y_dil, wu_bf16, wd_bf16 = _dil_mixer(proj, dil_q_norm[layer], dil_k_norm[layer], dil_bias,
                                             w_up, w_down, layer)
        vres = None if layer == 0 else (vres_v0[layer - 1], vres_v1[layer - 1], vres_v2[layer - 1])
        outs = _rwkv_maps(
            proj, rwkv_mu[layer], decay_w0[layer], decay_w2[layer], iclr_a0[layer], iclr_a2[layer],
            gate_g2[layer], k_k[layer], k_a[layer], r_k[layer], ln_x_g[layer], ln_x_b[layer],
            v_first, vres)
        y_rwkv = outs[0]
        if layer == 0:
            v_first = outs[1]
        ys = [y.reshape(bsz * seq, GROUP_WIDTH) for y in (y_conv, y_swa, y_dil, y_rwkv)]
        x2d = _wout(x2d, ys, w_out, layer)
        x2d = _ffn(x2d, norm_ffn, layer, wu_bf16, wd_bf16)
    return x2d.reshape(bsz, seq, d)
```

```python
import functools
import math

import jax
import jax.numpy as jnp
from jax import lax
from jax.experimental import pallas as pl
from jax.experimental.pallas import tpu as pltpu

F32 = jnp.float32
BF16 = jnp.bfloat16

HEAD_DIM = 64
GROUP_WIDTH = 512
N_HEADS = GROUP_WIDTH // HEAD_DIM
SWA_KV_HEADS = 2
SWA_GROUP = N_HEADS // SWA_KV_HEADS
SWA_WINDOW = 128
DIL_PAIRS = ((128, 1), (512, 4), (2048, 16))
DECAY_LORA = 64
ICLR_LORA = 64
GATE_LORA = 128
RWKV_IN_WIDTH = 3 * GROUP_WIDTH + DECAY_LORA + ICLR_LORA + GATE_LORA
BLK = 128
NUM_BUCKETS = 32
BUCKET_MAX_DIST = 128
RMS_EPS = 1e-6
LN_X_EPS = 64e-5
NEG = -1e30
LOG2E = math.log2(math.e)
WKV_CHUNK = 64
WKV_GROUP = 4
DIL_TILES = 4
PREV_ROWS = 16
NORM_CHUNKS = 4
PREP_UNROLL = 4
ATTN_UNROLL = 4

OFF_CONV = 0
OFF_SWA_Q = OFF_CONV + 3 * GROUP_WIDTH
OFF_SWA_K = OFF_SWA_Q + GROUP_WIDTH
OFF_SWA_V = OFF_SWA_K + SWA_KV_HEADS * HEAD_DIM
OFF_DIL = OFF_SWA_V + SWA_KV_HEADS * HEAD_DIM
OFF_RW = OFF_DIL + 3 * GROUP_WIDTH
IN_WIDTH = OFF_RW + RWKV_IN_WIDTH

V7X_VMEM_BYTES = 64 * 1024 * 1024
VMEM_LIMIT = 48 * 1024 * 1024
FFN_VMEM_LIMIT = V7X_VMEM_BYTES - 6 * 1024 * 1024


def _params(*sem, vmem_limit=VMEM_LIMIT):
    return pltpu.CompilerParams(dimension_semantics=sem, vmem_limit_bytes=vmem_limit)


def _mm(a, b):
    return jnp.dot(a.astype(BF16), b.astype(BF16), preferred_element_type=F32)


def _mm_nt(a, b):
    return lax.dot_general(a.astype(BF16), b.astype(BF16), (((1,), (1,)), ((), ())),
                           preferred_element_type=F32)


def _mm_tn(a, b):
    return lax.dot_general(a.astype(BF16), b.astype(BF16), (((0,), (0,)), ((), ())),
                           preferred_element_type=F32)


def _split_bf16(x):
    hi = x.astype(BF16)
    lo = (x - hi.astype(F32)).astype(BF16)
    return hi, lo


def _head_sums(x, split=True):
    lanes = 128
    r = lax.broadcasted_iota(jnp.int32, (lanes, lanes), 0) // HEAD_DIM
    c = lax.broadcasted_iota(jnp.int32, (lanes, lanes), 1) // HEAD_DIM
    bd = (r == c).astype(BF16)
    parts = _split_bf16(x) if split else (x.astype(BF16),)
    cols = []
    for j in range(x.shape[-1] // lanes):
        sl = slice(j * lanes, (j + 1) * lanes)
        cols.append(sum(jnp.dot(part[:, sl], bd, preferred_element_type=F32) for part in parts))
    return cols[0] if len(cols) == 1 else jnp.concatenate(cols, axis=-1)


def _sigmoid(z):
    return 1.0 / (1.0 + jnp.exp(-z))


def _rms_scale(x_ref):
    x = x_ref[...]
    return lax.rsqrt(jnp.mean(x * x, axis=-1, keepdims=True) + RMS_EPS)


def _norm_chunks(x_ref, g_ref, h_ref, scale):
    k = x_ref.shape[1]
    kc = k // NORM_CHUNKS
    for c in range(NORM_CHUNKS):
        cols = slice(c * kc, (c + 1) * kc)
        h = (x_ref[:, cols] * scale * g_ref[0, :, cols]).astype(BF16)
        h_ref[:, cols] = h
        yield cols, h


def _norm_matmul_kernel(x_ref, g_ref, w_ref, o_ref, h_ref):
    @pl.when(pl.program_id(1) == 0)
    def _():
        for _ in _norm_chunks(x_ref, g_ref, h_ref, _rms_scale(x_ref)):
            pass

    o_ref[...] = jnp.dot(h_ref[...], w_ref[0].astype(BF16),
                         preferred_element_type=F32).astype(o_ref.dtype)


def _norm_matmul(x2d, gains, w_stack, layer, tm=2048, tn=512):
    m, k = x2d.shape
    n = w_stack.shape[2]
    assert n % tn == 0 and m % tm == 0
    return pl.pallas_call(
        _norm_matmul_kernel,
        grid=(m // tm, n // tn),
        in_specs=[
            pl.BlockSpec((tm, k), lambda i, j: (i, 0)),
            pl.BlockSpec((1, 1, k), lambda i, j: (layer, 0, 0)),
            pl.BlockSpec((1, k, tn), lambda i, j: (layer, 0, j)),
        ],
        out_specs=pl.BlockSpec((tm, tn), lambda i, j: (i, j)),
        out_shape=jax.ShapeDtypeStruct((m, n), BF16),
        scratch_shapes=[pltpu.VMEM((tm, k), BF16)],
        compiler_params=_params("parallel", "arbitrary", vmem_limit=FFN_VMEM_LIMIT),
        name="norm_matmul",
    )(x2d, gains.reshape(-1, 1, k), w_stack)


def _t5_bucket(dist):
    dist = jnp.maximum(dist, 0)
    max_exact = NUM_BUCKETS // 2
    scaled = (jnp.log(jnp.maximum(dist, 1).astype(F32) / max_exact)
              / math.log(BUCKET_MAX_DIST / max_exact))
    large = max_exact + (scaled * (NUM_BUCKETS - max_exact)).astype(jnp.int32)
    large = jnp.minimum(large, NUM_BUCKETS - 1)
    return jnp.where(dist < max_exact, dist, large)


def _bias_kernel(bucket_ref, table_ref, o_ref):
    bucket = bucket_ref[0]

    def head(h, carry):
        g = jnp.zeros(bucket.shape, F32)
        for b in range(NUM_BUCKETS):
            g = jnp.where(bucket == b, table_ref[b, h], g)
        rows = jnp.broadcast_to(g[0:1], (BLK, 2 * BLK))
        o_ref[0, h] = pltpu.roll(rows, 0, axis=1, stride=1, stride_axis=0)
        return carry

    lax.fori_loop(0, o_ref.shape[1], head, 0)


def _bias_tiles(rel_bias):
    lag_dist = BLK - jnp.arange(2 * BLK)
    buckets = jnp.stack([_t5_bucket(lag_dist * r) for _, r in DIL_PAIRS]).astype(jnp.int32)
    buckets = jnp.broadcast_to(buckets[:, None, :], (len(DIL_PAIRS), 8, 2 * BLK))
    nh = rel_bias.shape[1]
    return pl.pallas_call(
        _bias_kernel,
        grid=(len(DIL_PAIRS),),
        in_specs=[
            pl.BlockSpec((1, 8, 2 * BLK), lambda s: (s, 0, 0)),
            pl.BlockSpec(memory_space=pltpu.SMEM),
        ],
        out_specs=pl.BlockSpec((1, nh, BLK, 2 * BLK), lambda s: (s, 0, 0, 0)),
        out_shape=jax.ShapeDtypeStruct((len(DIL_PAIRS), nh, BLK, 2 * BLK), F32),
        compiler_params=_params("arbitrary"),
        name="bias_tiles",
    )(buckets, rel_bias)


def _short_conv(b_ref, c_ref, u_ref, w_ref, o_ref):
    lanes = 128
    for j in range(b_ref.shape[2] // lanes):
        cols = slice(j * lanes, (j + 1) * lanes)
        z = c_ref[0, :, cols].astype(F32) * u_ref[0, :, cols].astype(F32)
        row = lax.broadcasted_iota(jnp.int32, z.shape, 0)
        z1 = jnp.where(row >= 1, pltpu.roll(z, 1, axis=0), 0.0)
        z2 = jnp.where(row >= 2, pltpu.roll(z, 2, axis=0), 0.0)
        w = w_ref[:, cols]
        y = z2 * w[0:1, :] + z1 * w[1:2, :] + z * w[2:3, :]
        o_ref[0, :, cols] = (b_ref[0, :, cols].astype(F32) * y).astype(o_ref.dtype)


def _band_mask(max_dist):
    a = lax.broadcasted_iota(jnp.int32, (BLK, 2 * BLK), 0)
    b = lax.broadcasted_iota(jnp.int32, (BLK, 2 * BLK), 1)
    dist = BLK + a - b
    return (dist >= 0) & (dist <= max_dist), b


def _attend(qs, kws, vws, biases, normalize=True):
    idx = range(len(qs))
    s = [lax.dot_general(qs[i], kws[i], (((1,), (1,)), ((), ())), preferred_element_type=F32)
         + biases[i] for i in idx]
    m = [jnp.max(s[i], axis=-1, keepdims=True) for i in idx]
    p = [jnp.exp2(s[i] - m[i]) for i in idx]
    den = [jnp.sum(p[i], axis=-1, keepdims=True) for i in idx]
    o = [jnp.dot(p[i].astype(BF16), vws[i], preferred_element_type=F32) for i in idx]
    if normalize:
        o = [o[i] / den[i] for i in idx]
    return o, m, den


def _head_rms(x, gain):
    ms = _head_sums(x * x, split=False) * (1.0 / HEAD_DIM)
    return x * lax.rsqrt(ms + RMS_EPS) * gain


def _swa_kernel(q_ref, k_ref, v_ref, qg_ref, kg_ref, sink_ref, bias_ref,
                cb_ref, cc_ref, cu_ref, cw_ref, o_ref, oc_ref,
                qn_ref, kn_ref, vb_ref, bm_ref):
    seq = q_ref.shape[1]
    nb = seq // BLK
    kvw = SWA_KV_HEADS * HEAD_DIM
    scale = HEAD_DIM ** -0.5 * LOG2E

    _short_conv(cb_ref, cc_ref, cu_ref, cw_ref, oc_ref)

    kn_ref[0:BLK, :] = jnp.zeros((BLK, 2 * kvw), BF16)
    vb_ref[0:BLK, :] = jnp.zeros((BLK, 2 * kvw), BF16)
    lane_half = lax.broadcasted_iota(jnp.int32, (1, kvw), 1) // HEAD_DIM

    def both_halves(x):
        xr = pltpu.roll(x, HEAD_DIM, axis=1)
        return jnp.concatenate([jnp.where(lane_half == 0, x, xr),
                                jnp.where(lane_half == 0, xr, x)], axis=-1)

    def prep(i, carry):
        r0 = pl.multiple_of(i * BLK, BLK)
        q = q_ref[0, pl.ds(r0, BLK), :].astype(F32)
        qn_ref[pl.ds(r0, BLK), :] = (_head_rms(q, qg_ref[...]) * scale).astype(BF16)
        k = _head_rms(k_ref[0, pl.ds(r0, BLK), :].astype(F32), kg_ref[...])
        kn_ref[pl.ds(r0 + BLK, BLK), :] = both_halves(k).astype(BF16)
        v = v_ref[0, pl.ds(r0, BLK), :].astype(F32)
        vb_ref[pl.ds(r0 + BLK, BLK), :] = both_halves(v).astype(BF16)
        return carry

    lax.fori_loop(0, nb, prep, 0, unroll=PREP_UNROLL)

    assert SWA_WINDOW <= BLK
    band, kcol = _band_mask(SWA_WINDOW - 1)
    for h in range(N_HEADS):
        sink = sink_ref[h] * LOG2E
        bias = bias_ref[h] * LOG2E
        bm_ref[h] = jnp.where(kcol == 0, sink, jnp.where(band & (kcol >= BLK), bias, NEG))
        bm_ref[N_HEADS + h] = jnp.where(kcol == 0, sink, jnp.where(band, bias, NEG))
    first_key = lax.broadcasted_iota(jnp.int32, (2 * BLK, kvw), 0) == 0

    def block(i, carry):
        r0 = pl.multiple_of(i * BLK, BLK)
        later = jnp.minimum(i, 1) * N_HEADS
        qs, kws, vws, bms = [], [], [], []
        for hk in range(SWA_KV_HEADS):
            kw = kn_ref[pl.ds(r0, 2 * BLK), hk * kvw:(hk + 1) * kvw]
            vw = vb_ref[pl.ds(r0, 2 * BLK), hk * kvw:(hk + 1) * kvw]
            kw = jnp.where(first_key, jnp.zeros_like(kw), kw)
            vw = jnp.where(first_key, jnp.zeros_like(vw), vw)
            for g in range(SWA_GROUP):
                h = hk * SWA_GROUP + g
                q2 = qn_ref[pl.ds(r0, BLK), (h // 2) * kvw:(h // 2 + 1) * kvw]
                qs.append(jnp.where(lane_half == h % 2, q2, jnp.zeros_like(q2)))
                kws.append(kw)
                vws.append(vw)
                bms.append(bm_ref[later + h])
        outs, _, _ = _attend(qs, kws, vws, bms)
        pairs = [jnp.where(lane_half == 0, outs[h], outs[h + 1]) for h in range(0, N_HEADS, 2)]
        o_ref[0, pl.ds(r0, BLK), :] = jnp.concatenate(pairs, axis=-1).astype(o_ref.dtype)
        return carry

    lax.fori_loop(0, nb, block, 0, unroll=ATTN_UNROLL)


def _swa_conv_mixers(proj, q_gain, k_gain, sink, bias, conv_w):
    bsz, seq, _ = proj.shape
    kvw = SWA_KV_HEADS * HEAD_DIM
    q_gain_t = jnp.tile(q_gain, N_HEADS).reshape(1, GROUP_WIDTH)
    k_gain_t = jnp.tile(k_gain, SWA_KV_HEADS).reshape(1, kvw)
    wide = lambda off: pl.BlockSpec((1, seq, GROUP_WIDTH), lambda b: (b, 0, off // GROUP_WIDTH))
    out = jax.ShapeDtypeStruct((bsz, seq, GROUP_WIDTH), BF16)
    return pl.pallas_call(
        _swa_kernel,
        grid=(bsz,),
        in_specs=[
            wide(OFF_SWA_Q),
            pl.BlockSpec((1, seq, kvw), lambda b: (b, 0, OFF_SWA_K // kvw)),
            pl.BlockSpec((1, seq, kvw), lambda b: (b, 0, OFF_SWA_V // kvw)),
            pl.BlockSpec((1, GROUP_WIDTH), lambda b: (0, 0)),
            pl.BlockSpec((1, kvw), lambda b: (0, 0)),
            pl.BlockSpec(memory_space=pltpu.SMEM),
            pl.BlockSpec((N_HEADS, BLK, 2 * BLK), lambda b: (0, 0, 0)),
            wide(OFF_CONV), wide(OFF_CONV + GROUP_WIDTH), wide(OFF_CONV + 2 * GROUP_WIDTH),
            pl.BlockSpec((3, GROUP_WIDTH), lambda b: (0, 0)),
        ],
        out_specs=[pl.BlockSpec((1, seq, GROUP_WIDTH), lambda b: (b, 0, 0))] * 2,
        out_shape=[out, out],
        scratch_shapes=[
            pltpu.VMEM((seq, GROUP_WIDTH), BF16),
            pltpu.VMEM((seq + BLK, 2 * kvw), BF16),
            pltpu.VMEM((seq + BLK, 2 * kvw), BF16),
            pltpu.VMEM((2 * N_HEADS, BLK, 2 * BLK), F32),
        ],
        compiler_params=_params("parallel"),
        name="swa_conv_mixers",
    )(proj, proj, proj, q_gain_t, k_gain_t, sink, bias, proj, proj, proj, conv_w)


def _dil_kernel(q_ref, k_ref, v_ref, qg_ref, kg_ref, bias_ref, wu_ref, wd_ref,
                o_ref, wu_out, wd_out,
                qn_ref, kn_ref, vn_ref, ob_ref, mb_ref, db_ref, bm_ref):
    wu_out[...] = wu_ref[0].astype(BF16)
    wd_out[...] = wd_ref[0].astype(BF16)
    seq = q_ref.shape[1]
    lanes = q_ref.shape[2]
    heads = lanes // HEAD_DIM
    scale = HEAD_DIM ** -0.5 * LOG2E

    def prep(i, carry):
        r0 = pl.multiple_of(i * BLK, BLK)
        q = q_ref[0, pl.ds(r0, BLK), :].astype(F32)
        k = k_ref[0, pl.ds(r0, BLK), :].astype(F32)
        qn_ref[pl.ds(r0, BLK), :] = _head_rms(q, qg_ref[...]) * scale
        kn_ref[pl.ds(r0, BLK), :] = _head_rms(k, kg_ref[...])
        vn_ref[pl.ds(r0, BLK), :] = v_ref[0, pl.ds(r0, BLK), :].astype(F32)
        return carry

    lax.fori_loop(0, seq // BLK, prep, 0, unroll=PREP_UNROLL)

    for br, (window, r) in enumerate(DIL_PAIRS):
        band, kcol = _band_mask(window // r)
        for h in range(heads):
            bias = bias_ref[br, h] * LOG2E
            bm_ref[(br * heads + h) * 2] = jnp.where(band & (kcol >= BLK), bias, NEG)
            bm_ref[(br * heads + h) * 2 + 1] = jnp.where(band, bias, NEG)

    lane_head = lax.broadcasted_iota(jnp.int32, (1, lanes), 1) // HEAD_DIM

    for br, (window, r) in enumerate(DIL_PAIRS):
        nb = seq // r // BLK

        def blocks(it, carry, br=br, r=r, nb=nb):
            qs, kws, vws, bms, curs = [], [], [], [], []
            for u in range(DIL_TILES):
                t = it * DIL_TILES + u
                c = t // nb
                i = t - c * nb
                cur = c + i * (BLK * r)
                prev = jnp.maximum(cur - BLK * r, c)
                later = jnp.minimum(i, 1)

                def rows(ref, start):
                    if r == 1:
                        return ref[pl.ds(start, BLK), :]
                    return ref[pl.ds(start, BLK, stride=r), :]

                q = rows(qn_ref, cur).astype(BF16)
                if nb == 1:
                    kw = rows(kn_ref, cur).astype(BF16)
                    vw = rows(vn_ref, cur).astype(BF16)
                else:
                    kw = jnp.concatenate([rows(kn_ref, prev), rows(kn_ref, cur)],
                                         axis=0).astype(BF16)
                    vw = jnp.concatenate([rows(vn_ref, prev), rows(vn_ref, cur)],
                                         axis=0).astype(BF16)
                curs.append(cur)
                for h in range(heads):
                    qs.append(jnp.where(lane_head == h, q, jnp.zeros_like(q)))
                    kws.append(kw)
                    vws.append(vw)
                    if nb == 1:
                        bms.append(bm_ref[(br * heads + h) * 2, :, BLK:])
                    else:
                        bms.append(bm_ref[(br * heads + h) * 2 + later])
            outs, ms, dens = _attend(qs, kws, vws, bms, normalize=False)
            for u in range(DIL_TILES):
                o_all, m_all, d_all = outs[u * heads], ms[u * heads], dens[u * heads]
                for h in range(1, heads):
                    o_all = jnp.where(lane_head == h, outs[u * heads + h], o_all)
                    m_all = jnp.where(lane_head == h, ms[u * heads + h], m_all)
                    d_all = jnp.where(lane_head == h, dens[u * heads + h], d_all)
                if r == 1:
                    rows = pl.ds(curs[u], BLK)
                else:
                    rows = pl.ds(curs[u], BLK, stride=r)
                ob_ref[br, rows, :] = o_all
                mb_ref[br, rows, :] = m_all
                db_ref[br, rows, :] = d_all
            return carry

        lax.fori_loop(0, r * nb // DIL_TILES, blocks, 0, unroll=ATTN_UNROLL)

    def combine(i, carry):
        r0 = pl.multiple_of(i * BLK, BLK)
        rows = pl.ds(r0, BLK)
        m0, m1, m2 = mb_ref[0, rows, :], mb_ref[1, rows, :], mb_ref[2, rows, :]
        m = jnp.maximum(jnp.maximum(m0, m1), m2)
        e0, e1, e2 = jnp.exp2(m0 - m), jnp.exp2(m1 - m), jnp.exp2(m2 - m)
        num = e0 * ob_ref[0, rows, :] + e1 * ob_ref[1, rows, :] + e2 * ob_ref[2, rows, :]
        den = e0 * db_ref[0, rows, :] + e1 * db_ref[1, rows, :] + e2 * db_ref[2, rows, :]
        o_ref[0, rows, :] = (num / den).astype(o_ref.dtype)
        return carry

    lax.fori_loop(0, seq // BLK, combine, 0, unroll=PREP_UNROLL)


def _dil_mixer(proj, q_gain, k_gain, bias, w_up, w_down, layer):
    bsz, seq, _ = proj.shape
    lanes = 128
    heads = lanes // HEAD_DIM
    nblk = GROUP_WIDTH // lanes
    base = OFF_DIL // lanes
    gq = jnp.tile(q_gain, heads).reshape(1, lanes)
    gk = jnp.tile(k_gain, heads).reshape(1, lanes)
    _, d, f = w_up.shape
    steps = bsz * nblk
    assert d % steps == 0 and f % steps == 0

    def col(seg):
        return pl.BlockSpec((1, seq, lanes), lambda b, j: (b, 0, base + seg * nblk + j))

    return pl.pallas_call(
        _dil_kernel,
        grid=(bsz, nblk),
        in_specs=[
            col(0), col(1), col(2),
            pl.BlockSpec((1, lanes), lambda b, j: (0, 0)),
            pl.BlockSpec((1, lanes), lambda b, j: (0, 0)),
            pl.BlockSpec((len(DIL_PAIRS), heads, BLK, 2 * BLK), lambda b, j: (0, j, 0, 0)),
            pl.BlockSpec((1, d // steps, f), lambda b, j: (layer, b * nblk + j, 0)),
            pl.BlockSpec((1, f // steps, d), lambda b, j: (layer, b * nblk + j, 0)),
        ],
        out_specs=[
            pl.BlockSpec((1, seq, lanes), lambda b, j: (b, 0, j)),
            pl.BlockSpec((d // steps, f), lambda b, j: (b * nblk + j, 0)),
            pl.BlockSpec((f // steps, d), lambda b, j: (b * nblk + j, 0)),
        ],
        out_shape=[
            jax.ShapeDtypeStruct((bsz, seq, GROUP_WIDTH), BF16),
            jax.ShapeDtypeStruct((d, f), BF16),
            jax.ShapeDtypeStruct((f, d), BF16),
        ],
        scratch_shapes=[
            pltpu.VMEM((seq, lanes), F32),
            pltpu.VMEM((seq, lanes), F32),
            pltpu.VMEM((seq, lanes), F32),
            pltpu.VMEM((len(DIL_PAIRS), seq, lanes), F32),
            pltpu.VMEM((len(DIL_PAIRS), seq, lanes), F32),
            pltpu.VMEM((len(DIL_PAIRS), seq, lanes), F32),
            pltpu.VMEM((len(DIL_PAIRS) * heads * 2, BLK, 2 * BLK), F32),
        ],
        compiler_params=_params("parallel", "parallel"),
        name="dil_mixer",
    )(proj, proj, proj, gq, gk, bias, w_up, w_down)


def _rwkv_maps_kernel(*refs, has_vres):
    if has_vres:
        (p_ref, pp_ref, mu_ref, w0_ref, w2_ref, a0_ref, a2_ref, g2_ref, kk_ref, ka_ref, rk_ref,
         lng_ref, lnb_ref, vf_ref, v0_ref, v1_ref, v2_ref, o_ref, *scratch) = refs
    else:
        (p_ref, pp_ref, mu_ref, w0_ref, w2_ref, a0_ref, a2_ref, g2_ref, kk_ref, ka_ref, rk_ref,
         lng_ref, lnb_ref, o_ref, v_out, *scratch) = refs
    *scratch, q_s, y1_s, m_s, gm_s, st_ref = scratch

    @pl.when(pl.program_id(1) == 0)
    def _():
        st_ref[...] = jnp.zeros(st_ref.shape, F32)

    w = GROUP_WIDTH
    p = p_ref[0].astype(F32)
    row = lax.broadcasted_iota(jnp.int32, p.shape, 0)
    last_prev = pp_ref[0, PREV_ROWS - 1:PREV_ROWS, :].astype(F32)
    last_prev = jnp.where(pl.program_id(1) > 0, last_prev, 0.0)
    prev = jnp.where(row >= 1, pltpu.roll(p, 1, axis=0), last_prev)
    xs = p + (prev - p) * mu_ref[...]
    grows = WKV_GROUP * WKV_CHUNK
    for grp in range(p.shape[0] // grows):
        rs = slice(grp * grows, (grp + 1) * grows)
        x = xs[rs]
        r = x[:, 0:w]
        k = x[:, w:2 * w]
        v = x[:, 2 * w:3 * w]
        o = 3 * w
        wd = x[:, o:o + DECAY_LORA]
        ad = x[:, o + DECAY_LORA:o + DECAY_LORA + ICLR_LORA]
        gd = x[:, o + DECAY_LORA + ICLR_LORA:]

        z = -(w0_ref[...] + _mm(jnp.tanh(wd), w2_ref[...]))
        softplus = jnp.maximum(z, 0.0) + jnp.log(1.0 + jnp.exp(-jnp.abs(z)))
        logw = -softplus - 0.5
        lw = -jnp.exp(logw)
        a = _sigmoid(a0_ref[...] + _mm(ad, a2_ref[...]))
        gate = _mm(_sigmoid(gd), g2_ref[...])
        if has_vres:
            mix = _sigmoid(v0_ref[...] + _mm(_mm(v, v1_ref[...]), v2_ref[...]))
            v = v + (vf_ref[0, rs] - v) * mix
        else:
            v_out[0, rs] = v
        kk = k * kk_ref[...]
        ss = _head_sums(kk * kk, split=False)
        kk = kk * lax.rsqrt(jnp.maximum(ss, 1e-24))
        k = k * (1.0 + (a - 1.0) * ka_ref[...])
        bonus = _head_sums(r * k * rk_ref[...]) * v
        _chunk_maps(grp, r, lw, k, v, kk, kk * a, q_s, y1_s, m_s, gm_s, *scratch)
        y = _walk_chunks(range(grp * WKV_GROUP, (grp + 1) * WKV_GROUP), q_s, y1_s, m_s, gm_s, st_ref)
        mean = _head_sums(y) * (1.0 / HEAD_DIM)
        yc = y - mean
        var = _head_sums(yc * yc) * (1.0 / HEAD_DIM)
        yn = yc * lax.rsqrt(var + LN_X_EPS) * lng_ref[...] + lnb_ref[...]
        o_ref[0, rs] = ((yn + bonus) * gate).astype(o_ref.dtype)


def _walk_chunks(chunks, q_ref, y1_ref, m_ref, gm_ref, st_ref):
    c = WKV_CHUNK
    w2 = 2 * HEAD_DIM
    pairs = range(N_HEADS // 2)
    state = [st_ref[p] for p in pairs]
    entering = []
    for j in chunks:
        rs = slice(j * w2, (j + 1) * w2)
        s_in = [s.astype(BF16) for s in state]
        entering.append(s_in)
        state = [_mm(m_ref[rs, p * w2:(p + 1) * w2], s_in[p]) + gm_ref[rs, p * w2:(p + 1) * w2]
                 for p in pairs]
    for p in pairs:
        st_ref[p] = state[p]
    ys = [jnp.concatenate([_mm(q_ref[j * c:(j + 1) * c, p * w2:(p + 1) * w2], entering[ci][p])
                           for p in pairs], axis=-1) + y1_ref[j * c:(j + 1) * c, :]
          for ci, j in enumerate(chunks)]
    return jnp.concatenate(ys, axis=0)


def _rwkv_maps(proj, mu, w0, w2, a0, a2, g2, k_k, k_a, r_k, ln_g, ln_b, v_first, vres, ts=512):
    bsz, seq, _ = proj.shape
    w = GROUP_WIDTH
    nt = seq // ts
    has_vres = vres is not None

    def full(shape):
        return pl.BlockSpec(shape, lambda b, i: (0,) * len(shape))

    row = lambda a: a.reshape(1, -1)
    tile = pl.BlockSpec((1, ts, w), lambda b, i: (b, i, 0))
    in_specs = [
        pl.BlockSpec((pl.Element(1), pl.Element(ts), pl.Element(RWKV_IN_WIDTH)),
                     lambda b, i: (b, i * ts, OFF_RW)),
        pl.BlockSpec((pl.Element(1), pl.Element(PREV_ROWS), pl.Element(RWKV_IN_WIDTH)),
                     lambda b, i: (b, jnp.maximum(i * (ts // PREV_ROWS) - 1, 0) * PREV_ROWS, OFF_RW)),
        full((1, RWKV_IN_WIDTH)), full((1, w)), full((DECAY_LORA, w)), full((1, w)),
        full((ICLR_LORA, w)), full((GATE_LORA, w)), full((1, w)), full((1, w)), full((1, w)),
        full((1, w)), full((1, w)),
    ]
    args = [proj, proj, row(mu), row(w0), w2, row(a0), a2, g2, row(k_k), row(k_a), row(r_k),
            row(ln_g), row(ln_b)]
    if has_vres:
        v0, v1, v2 = vres
        in_specs += [tile, full((1, w)), full(v1.shape), full(v2.shape)]
        args += [v_first, row(v0), v1, v2]
    f32 = jax.ShapeDtypeStruct((bsz, seq, w), F32)
    bf16 = jax.ShapeDtypeStruct((bsz, seq, w), BF16)
    out_shape = [bf16] + ([] if has_vres else [f32])
    return pl.pallas_call(
        functools.partial(_rwkv_maps_kernel, has_vres=has_vres),
        grid=(bsz, nt),
        in_specs=in_specs,
        out_specs=[tile] * len(out_shape),
        out_shape=out_shape,
        scratch_shapes=[pltpu.VMEM((ts, w), dt) for dt in (BF16, F32, BF16, BF16, F32, BF16, BF16,
                                                           BF16, F32)]
        + [pltpu.VMEM((2 * ts, w), BF16), pltpu.VMEM((2 * ts, w), F32),
           pltpu.VMEM((N_HEADS // 2, 2 * HEAD_DIM, 2 * HEAD_DIM), F32)],
        compiler_params=_params("parallel", "arbitrary"),
        name="rwkv_chunk_maps",
    )(*args)


def _chunk_maps(grp, r, lw, k, v, kk, b, q_out, y1_out, m_out, g_out,
                at_ref, rt_ref, bt_ref, kt_ref, dec_ref, vb_ref, rb_ref):
    c = WKV_CHUNK
    n = HEAD_DIM
    rows = r.shape[0]
    rs = slice(grp * rows, (grp + 1) * rows)

    row = lax.broadcasted_iota(jnp.int32, (rows, rows), 0)
    col = lax.broadcasted_iota(jnp.int32, (rows, rows), 1)
    tri = ((row >= col) & ((row // c) == (col // c))).astype(BF16)
    lw_hi, lw_lo = _split_bf16(lw)
    cum = (jnp.dot(tri, lw_hi, preferred_element_type=F32)
           + jnp.dot(tri, lw_lo, preferred_element_type=F32))
    e_pos = jnp.exp(cum)
    e_neg = jnp.exp(-cum)
    r_t = r * e_pos
    at_ref[rs] = (-kk * jnp.exp(cum - lw)).astype(BF16)
    rt_ref[rs] = r_t
    rb_ref[rs] = r_t.astype(BF16)
    bt_ref[rs] = (b * e_neg).astype(BF16)
    kt_ref[rs] = (k * e_neg).astype(BF16)
    dec_ref[rs] = e_pos
    vb_ref[rs] = v.astype(BF16)

    _chunk_group_pairs(range(grp * WKV_GROUP, (grp + 1) * WKV_GROUP), at_ref, rt_ref, bt_ref,
                       kt_ref, dec_ref, vb_ref, rb_ref, q_out, y1_out, m_out, g_out)


def _chunk_group_pairs(chunks, at_ref, rt_ref, bt_ref, kt_ref, dec_ref, vb_ref, rb_ref,
                       q_out, y1_out, m_out, g_out):
    c = WKV_CHUNK
    n = HEAD_DIM
    assert c == n
    w2 = 2 * n
    units = [(j, p) for j in chunks for p in range(N_HEADS // 2)]
    idx = range(len(units))

    def tile(ref, i):
        j, p = units[i]
        return ref[j * c:(j + 1) * c, p * w2:(p + 1) * w2]

    left = lax.broadcasted_iota(jnp.int32, (1, w2), 1) < n
    row2 = lax.broadcasted_iota(jnp.int32, (2 * c, 2 * c), 0)
    col2 = lax.broadcasted_iota(jnp.int32, (2 * c, 2 * c), 1)
    keep = (row2 % c > col2 % c) | ((row2 >= c) & (row2 % c == col2 % c))
    wrow = lax.broadcasted_iota(jnp.int32, (c, w2), 0)
    wcol = lax.broadcasted_iota(jnp.int32, (c, w2), 1)
    eye_left = (wcol == wrow).astype(F32)
    eye_right = (wcol == wrow + c).astype(F32)

    def only(mask, x):
        return jnp.where(mask, x, jnp.zeros_like(x))

    def rows_ba(x):
        return jnp.concatenate([only(~left, x), only(left, x)], axis=0)

    def rows_ab(x):
        return jnp.concatenate([only(left, x), only(~left, x)], axis=0)

    a = [tile(at_ref, i) for i in idx]
    b = [tile(bt_ref, i) for i in idx]
    k = [tile(kt_ref, i) for i in idx]
    v = [tile(vb_ref, i) for i in idx]
    ar = [jnp.concatenate([a[i], tile(rb_ref, i)], axis=0) for i in idx]
    bk = [jnp.concatenate([b[i], k[i]], axis=0) for i in idx]
    kb = [jnp.concatenate([k[i], b[i]], axis=0) for i in idx]
    aa_a = [jnp.where(keep, _mm_nt(only(left, ar[i]), bk[i]), 0.0) for i in idx]
    aa_b = [jnp.where(keep, _mm_nt(only(~left, ar[i]), kb[i]), 0.0) for i in idx]
    ps_a = [only(left, aa_a[i][:c]) + eye_right for i in idx]
    ps_b = [only(~left, aa_b[i][:c]) + eye_left for i in idx]
    zeros = jnp.zeros((c, w2), BF16)
    for _ in range(int(math.log2(c))):
        nxt_a, nxt_b = [], []
        for i in idx:
            pa, pb = ps_a[i].astype(BF16), ps_b[i].astype(BF16)
            rhs = jnp.concatenate([jnp.concatenate([pa, zeros], axis=1),
                                   jnp.concatenate([zeros, pb], axis=1)], axis=0)
            prod = jnp.dot(jnp.where(left, pa, pb), rhs, preferred_element_type=F32)
            nxt_a.append(prod[:, :w2] + only(~left, ps_a[i]))
            nxt_b.append(prod[:, w2:] + only(left, ps_b[i]))
        ps_a, ps_b = nxt_a, nxt_b
    t_ba = [jnp.where(left, ps_b[i], ps_a[i]).astype(BF16) for i in idx]
    av = [_mm(jnp.where(left, aa_b[i], aa_a[i]), rows_ba(v[i])) for i in idx]
    av_u = [av[i][:c].astype(BF16) for i in idx]
    wu = [_mm(t_ba[i], jnp.concatenate([rows_ba(a[i]), rows_ba(av_u[i])], axis=1)).astype(BF16)
          for i in idx]
    a_rb = [jnp.where(left, aa_a[i][c:], aa_b[i][c:]) for i in idx]
    aw = [_mm(a_rb[i], jnp.concatenate([rows_ab(wu[i][:, :w2]), rows_ab(wu[i][:, w2:])], axis=1))
          for i in idx]
    q = [tile(rt_ref, i) + aw[i][:, :w2] for i in idx]
    y1 = [aw[i][:, w2:] + av[i][c:] for i in idx]
    x1 = [_mm_tn(b[i], wu[i][:, :w2]) for i in idx]
    x2 = [_mm_tn(bk[i], jnp.concatenate([wu[i][:, w2:], v[i]], axis=0)) for i in idx]
    diag = row2 == col2
    same_head = (row2 // n) == (col2 // n)
    eye2 = diag.astype(F32)
    m, g = [], []
    for i in idx:
        j, p = units[i]
        d = dec_ref[(j + 1) * c - 1:(j + 1) * c, p * w2:(p + 1) * w2]
        d_col = jnp.sum(jnp.where(diag, d, 0.0), axis=1, keepdims=True)
        m.append((eye2 + jnp.where(same_head, x1[i], 0.0)) * d_col)
        g.append(jnp.where(same_head, x2[i], 0.0) * d_col)
    per = N_HEADS // 2
    for ci, j in enumerate(chunks):
        sel = slice(ci * per, (ci + 1) * per)
        q_out[j * c:(j + 1) * c, :] = jnp.concatenate(q[sel], axis=-1).astype(q_out.dtype)
        y1_out[j * c:(j + 1) * c, :] = jnp.concatenate(y1[sel], axis=-1)
        m_out[j * w2:(j + 1) * w2, :] = jnp.concatenate(m[sel], axis=-1).astype(m_out.dtype)
        g_out[j * w2:(j + 1) * w2, :] = jnp.concatenate(g[sel], axis=-1)


def _wout_kernel(x_ref, y0_ref, y1_ref, y2_ref, y3_ref, w_ref, o_ref):
    acc = x_ref[...]
    for idx, y_ref in enumerate((y0_ref, y1_ref, y2_ref, y3_ref)):
        w = w_ref[0, idx * GROUP_WIDTH:(idx + 1) * GROUP_WIDTH, :].astype(BF16)
        acc = acc + jnp.dot(y_ref[...], w, preferred_element_type=F32)
    o_ref[...] = acc


def _wout(x2d, ys, w_stack, layer, tm=512):
    m, d = x2d.shape
    ytile = pl.BlockSpec((tm, GROUP_WIDTH), lambda i: (i, 0))
    xtile = pl.BlockSpec((tm, d), lambda i: (i, 0))
    wspec = pl.BlockSpec((1,) + w_stack.shape[1:], lambda i: (layer, 0, 0))
    return pl.pallas_call(
        _wout_kernel,
        grid=(m // tm,),
        in_specs=[xtile] + [ytile] * 4 + [wspec],
        out_specs=xtile,
        out_shape=jax.ShapeDtypeStruct((m, d), F32),
        compiler_params=_params("parallel", vmem_limit=FFN_VMEM_LIMIT),
        name="wout_residual",
    )(x2d, *ys, w_stack)


def _ffn_kernel(x_ref, g_ref, wu_ref, wd_ref, o_ref, h_ref):
    def down(u):
        act = jnp.square(jnp.maximum(u, 0.0)).astype(BF16)
        return jnp.dot(act, wd_ref[0].astype(BF16), preferred_element_type=F32)

    @pl.when(pl.program_id(1) == 0)
    def _():
        u = None
        for cols, h in _norm_chunks(x_ref, g_ref, h_ref, _rms_scale(x_ref)):
            part = jnp.dot(h, wu_ref[0, cols, :].astype(BF16), preferred_element_type=F32)
            u = part if u is None else u + part
        o_ref[...] = x_ref[...] + down(u)

    @pl.when(pl.program_id(1) > 0)
    def _():
        u = jnp.dot(h_ref[...], wu_ref[0].astype(BF16), preferred_element_type=F32)
        o_ref[...] += down(u)


def _ffn(x2d, gains, layer, wu, wd, tm=1024, tf=1024):
    m, d = x2d.shape
    f = wu.shape[1]
    wu_stack = wu.reshape(1, d, f)
    wd_stack = wd.reshape(1, f, d)
    xtile = pl.BlockSpec((tm, d), lambda i, j: (i, 0))
    return pl.pallas_call(
        _ffn_kernel,
        grid=(m // tm, f // tf),
        in_specs=[
            xtile,
            pl.BlockSpec((1, 1, d), lambda i, j: (layer, 0, 0)),
            pl.BlockSpec((1, d, tf), lambda i, j: (0, 0, j)),
            pl.BlockSpec((1, tf, d), lambda i, j: (0, j, 0)),
        ],
        out_specs=xtile,
        out_shape=jax.ShapeDtypeStruct((m, d), F32),
        scratch_shapes=[pltpu.VMEM((tm, d), BF16)],
        compiler_params=_params("parallel", "arbitrary",
                                vmem_limit=V7X_VMEM_BYTES - 3 * 1024 * 1024),
        name="ffn",
    )(x2d, gains.reshape(-1, 1, d), wu_stack, wd_stack)


def kernel(x, norm_mix, w_in, conv_w, swa_q_norm, swa_k_norm, swa_sink, dil_q_norm, dil_k_norm,
           rwkv_mu, decay_w0, decay_w2, iclr_a0, iclr_a2, gate_g2, k_k, k_a, r_k, ln_x_g, ln_x_b,
           vres_v0, vres_v1, vres_v2, w_out, norm_ffn, w_up, w_down, rel_bias):
    bsz, seq, d = x.shape
    depth = w_in.shape[0]
    bias = _bias_tiles(rel_bias)
    swa_bias = bias[0, :N_HEADS]
    dil_bias = bias[:, N_HEADS:]
    x2d = x.reshape(bsz * seq, d)
    v_first = None
    for layer in range(depth):
        proj = _norm_matmul(x2d, norm_mix, w_in, layer).reshape(bsz, seq, IN_WIDTH)
        y_swa, y_conv = _swa_conv_mixers(proj, swa_q_norm[layer], swa_k_norm[layer],
                                         swa_sink[layer], swa_bias, conv_w[layer])
        y_dil, wu_bf16, wd_bf16 = _dil_mixer(proj, dil_q_norm[layer], dil_k_norm[layer], dil_bias,
                                             w_up, w_down, layer)
        vres = None if layer == 0 else (vres_v0[layer - 1], vres_v1[layer - 1], vres_v2[layer - 1])
        outs = _rwkv_maps(
            proj, rwkv_mu[layer], decay_w0[layer], decay_w2[layer], iclr_a0[layer], iclr_a2[layer],
            gate_g2[layer], k_k[layer], k_a[layer], r_k[layer], ln_x_g[layer], ln_x_b[layer],
            v_first, vres)
        y_rwkv = outs[0]
        if layer == 0:
            v_first = outs[1]
        ys = [y.reshape(bsz * seq, GROUP_WIDTH) for y in (y_conv, y_swa, y_dil, y_rwkv)]
        x2d = _wout(x2d, ys, w_out, layer)
        x2d = _ffn(x2d, norm_ffn, layer, wu_bf16, wd_bf16)
    return x2d.reshape(bsz, seq, d)
```

```python
import functools
import math

import jax
import jax.numpy as jnp
from jax import lax
from jax.experimental import pallas as pl
from jax.experimental.pallas import tpu as pltpu

F32 = jnp.float32
BF16 = jnp.bfloat16

HEAD_DIM = 64
GROUP_WIDTH = 512
N_HEADS = GROUP_WIDTH // HEAD_DIM
SWA_KV_HEADS = 2
SWA_GROUP = N_HEADS // SWA_KV_HEADS
SWA_WINDOW = 128
DIL_PAIRS = ((128, 1), (512, 4), (2048, 16))
DECAY_LORA = 64
ICLR_LORA = 64
GATE_LORA = 128
RWKV_IN_WIDTH = 3 * GROUP_WIDTH + DECAY_LORA + ICLR_LORA + GATE_LORA
BLK = 128
NUM_BUCKETS = 32
BUCKET_MAX_DIST = 128
RMS_EPS = 1e-6
LN_X_EPS = 64e-5
NEG = -1e30
LOG2E = math.log2(math.e)
WKV_CHUNK = 64
WKV_GROUP = 4
DIL_TILES = 4
PREV_ROWS = 16
NORM_CHUNKS = 4
PREP_UNROLL = 8
ATTN_UNROLL = 4

OFF_CONV = 0
OFF_SWA_Q = OFF_CONV + 3 * GROUP_WIDTH
OFF_SWA_K = OFF_SWA_Q + GROUP_WIDTH
OFF_SWA_V = OFF_SWA_K + SWA_KV_HEADS * HEAD_DIM
OFF_DIL = OFF_SWA_V + SWA_KV_HEADS * HEAD_DIM
OFF_RW = OFF_DIL + 3 * GROUP_WIDTH
IN_WIDTH = OFF_RW + RWKV_IN_WIDTH

V7X_VMEM_BYTES = 64 * 1024 * 1024
VMEM_LIMIT = 48 * 1024 * 1024
FFN_VMEM_LIMIT = V7X_VMEM_BYTES - 6 * 1024 * 1024


def _params(*sem, vmem_limit=VMEM_LIMIT):
    return pltpu.CompilerParams(dimension_semantics=sem, vmem_limit_bytes=vmem_limit)


def _mm(a, b):
    return jnp.dot(a.astype(BF16), b.astype(BF16), preferred_element_type=F32)


def _mm_nt(a, b):
    return lax.dot_general(a.astype(BF16), b.astype(BF16), (((1,), (1,)), ((), ())),
                           preferred_element_type=F32)


def _mm_tn(a, b):
    return lax.dot_general(a.astype(BF16), b.astype(BF16), (((0,), (0,)), ((), ())),
                           preferred_element_type=F32)


def _split_bf16(x):
    hi = x.astype(BF16)
    lo = (x - hi.astype(F32)).astype(BF16)
    return hi, lo


def _head_sums(x, split=True):
    lanes = 128
    r = lax.broadcasted_iota(jnp.int32, (lanes, lanes), 0) // HEAD_DIM
    c = lax.broadcasted_iota(jnp.int32, (lanes, lanes), 1) // HEAD_DIM
    bd = (r == c).astype(BF16)
    parts = _split_bf16(x) if split else (x.astype(BF16),)
    cols = []
    for j in range(x.shape[-1] // lanes):
        sl = slice(j * lanes, (j + 1) * lanes)
        cols.append(sum(jnp.dot(part[:, sl], bd, preferred_element_type=F32) for part in parts))
    return cols[0] if len(cols) == 1 else jnp.concatenate(cols, axis=-1)


def _sigmoid(z):
    return 1.0 / (1.0 + jnp.exp(-z))


def _rms_scale(x_ref):
    x = x_ref[...]
    return lax.rsqrt(jnp.mean(x * x, axis=-1, keepdims=True) + RMS_EPS)


def _norm_chunks(x_ref, g_ref, h_ref, scale):
    k = x_ref.shape[1]
    kc = k // NORM_CHUNKS
    for c in range(NORM_CHUNKS):
        cols = slice(c * kc, (c + 1) * kc)
        h = (x_ref[:, cols] * scale * g_ref[0, :, cols]).astype(BF16)
        h_ref[:, cols] = h
        yield cols, h


def _norm_matmul_kernel(x_ref, g_ref, w_ref, o_ref, h_ref):
    @pl.when(pl.program_id(1) == 0)
    def _():
        for _ in _norm_chunks(x_ref, g_ref, h_ref, _rms_scale(x_ref)):
            pass

    o_ref[...] = jnp.dot(h_ref[...], w_ref[0].astype(BF16),
                         preferred_element_type=F32).astype(o_ref.dtype)


def _norm_matmul(x2d, gains, w_stack, layer, tm=2048, tn=512):
    m, k = x2d.shape
    n = w_stack.shape[2]
    assert n % tn == 0 and m % tm == 0
    return pl.pallas_call(
        _norm_matmul_kernel,
        grid=(m // tm, n // tn),
        in_specs=[
            pl.BlockSpec((tm, k), lambda i, j: (i, 0)),
            pl.BlockSpec((1, 1, k), lambda i, j: (layer, 0, 0)),
            pl.BlockSpec((1, k, tn), lambda i, j: (layer, 0, j)),
        ],
        out_specs=pl.BlockSpec((tm, tn), lambda i, j: (i, j)),
        out_shape=jax.ShapeDtypeStruct((m, n), BF16),
        scratch_shapes=[pltpu.VMEM((tm, k), BF16)],
        compiler_params=_params("parallel", "arbitrary", vmem_limit=FFN_VMEM_LIMIT),
        name="norm_matmul",
    )(x2d, gains.reshape(-1, 1, k), w_stack)


def _t5_bucket(dist):
    dist = jnp.maximum(dist, 0)
    max_exact = NUM_BUCKETS // 2
    scaled = (jnp.log(jnp.maximum(dist, 1).astype(F32) / max_exact)
              / math.log(BUCKET_MAX_DIST / max_exact))
    large = max_exact + (scaled * (NUM_BUCKETS - max_exact)).astype(jnp.int32)
    large = jnp.minimum(large, NUM_BUCKETS - 1)
    return jnp.where(dist < max_exact, dist, large)


def _bias_kernel(bucket_ref, table_ref, o_ref):
    bucket = bucket_ref[0]

    def head(h, carry):
        g = jnp.zeros(bucket.shape, F32)
        for b in range(NUM_BUCKETS):
            g = jnp.where(bucket == b, table_ref[b, h], g)
        rows = jnp.broadcast_to(g[0:1], (BLK, 2 * BLK))
        o_ref[0, h] = pltpu.roll(rows, 0, axis=1, stride=1, stride_axis=0)
        return carry

    lax.fori_loop(0, o_ref.shape[1], head, 0)


def _bias_tiles(rel_bias):
    lag_dist = BLK - jnp.arange(2 * BLK)
    buckets = jnp.stack([_t5_bucket(lag_dist * r) for _, r in DIL_PAIRS]).astype(jnp.int32)
    buckets = jnp.broadcast_to(buckets[:, None, :], (len(DIL_PAIRS), 8, 2 * BLK))
    nh = rel_bias.shape[1]
    return pl.pallas_call(
        _bias_kernel,
        grid=(len(DIL_PAIRS),),
        in_specs=[
            pl.BlockSpec((1, 8, 2 * BLK), lambda s: (s, 0, 0)),
            pl.BlockSpec(memory_space=pltpu.SMEM),
        ],
        out_specs=pl.BlockSpec((1, nh, BLK, 2 * BLK), lambda s: (s, 0, 0, 0)),
        out_shape=jax.ShapeDtypeStruct((len(DIL_PAIRS), nh, BLK, 2 * BLK), F32),
        compiler_params=_params("arbitrary"),
        name="bias_tiles",
    )(buckets, rel_bias)


def _short_conv(b_ref, c_ref, u_ref, w_ref, o_ref):
    lanes = 128
    for j in range(b_ref.shape[2] // lanes):
        cols = slice(j * lanes, (j + 1) * lanes)
        z = c_ref[0, :, cols].astype(F32) * u_ref[0, :, cols].astype(F32)
        row = lax.broadcasted_iota(jnp.int32, z.shape, 0)
        z1 = jnp.where(row >= 1, pltpu.roll(z, 1, axis=0), 0.0)
        z2 = jnp.where(row >= 2, pltpu.roll(z, 2, axis=0), 0.0)
        w = w_ref[:, cols]
        y = z2 * w[0:1, :] + z1 * w[1:2, :] + z * w[2:3, :]
        o_ref[0, :, cols] = (b_ref[0, :, cols].astype(F32) * y).astype(o_ref.dtype)


def _band_mask(max_dist):
    a = lax.broadcasted_iota(jnp.int32, (BLK, 2 * BLK), 0)
    b = lax.broadcasted_iota(jnp.int32, (BLK, 2 * BLK), 1)
    dist = BLK + a - b
    return (dist >= 0) & (dist <= max_dist), b


def _attend(qs, kws, vws, biases, normalize=True):
    idx = range(len(qs))
    s = [lax.dot_general(qs[i], kws[i], (((1,), (1,)), ((), ())), preferred_element_type=F32)
         + biases[i] for i in idx]
    m = [jnp.max(s[i], axis=-1, keepdims=True) for i in idx]
    p = [jnp.exp2(s[i] - m[i]) for i in idx]
    den = [jnp.sum(p[i], axis=-1, keepdims=True) for i in idx]
    o = [jnp.dot(p[i].astype(BF16), vws[i], preferred_element_type=F32) for i in idx]
    if normalize:
        o = [o[i] / den[i] for i in idx]
    return o, m, den


def _head_rms(x, gain):
    ms = _head_sums(x * x, split=False) * (1.0 / HEAD_DIM)
    return x * lax.rsqrt(ms + RMS_EPS) * gain


def _swa_kernel(q_ref, k_ref, v_ref, qg_ref, kg_ref, sink_ref, bias_ref,
                cb_ref, cc_ref, cu_ref, cw_ref, o_ref, oc_ref,
                qn_ref, kn_ref, vb_ref, bm_ref):
    seq = q_ref.shape[1]
    nb = seq // BLK
    kvw = SWA_KV_HEADS * HEAD_DIM
    scale = HEAD_DIM ** -0.5 * LOG2E

    _short_conv(cb_ref, cc_ref, cu_ref, cw_ref, oc_ref)

    kn_ref[0:BLK, :] = jnp.zeros((BLK, 2 * kvw), BF16)
    vb_ref[0:BLK, :] = jnp.zeros((BLK, 2 * kvw), BF16)
    lane_half = lax.broadcasted_iota(jnp.int32, (1, kvw), 1) // HEAD_DIM

    def both_halves(x):
        xr = pltpu.roll(x, HEAD_DIM, axis=1)
        return jnp.concatenate([jnp.where(lane_half == 0, x, xr),
                                jnp.where(lane_half == 0, xr, x)], axis=-1)

    def prep(i, carry):
        r0 = pl.multiple_of(i * BLK, BLK)
        q = q_ref[0, pl.ds(r0, BLK), :].astype(F32)
        qn_ref[pl.ds(r0, BLK), :] = (_head_rms(q, qg_ref[...]) * scale).astype(BF16)
        k = _head_rms(k_ref[0, pl.ds(r0, BLK), :].astype(F32), kg_ref[...])
        kn_ref[pl.ds(r0 + BLK, BLK), :] = both_halves(k).astype(BF16)
        v = v_ref[0, pl.ds(r0, BLK), :].astype(F32)
        vb_ref[pl.ds(r0 + BLK, BLK), :] = both_halves(v).astype(BF16)
        return carry

    lax.fori_loop(0, nb, prep, 0, unroll=PREP_UNROLL)

    assert SWA_WINDOW <= BLK
    band, kcol = _band_mask(SWA_WINDOW - 1)
    for h in range(N_HEADS):
        sink = sink_ref[h] * LOG2E
        bias = bias_ref[h] * LOG2E
        bm_ref[h] = jnp.where(kcol == 0, sink, jnp.where(band & (kcol >= BLK), bias, NEG))
        bm_ref[N_HEADS + h] = jnp.where(kcol == 0, sink, jnp.where(band, bias, NEG))
    first_key = lax.broadcasted_iota(jnp.int32, (2 * BLK, kvw), 0) == 0

    def block(i, carry):
        r0 = pl.multiple_of(i * BLK, BLK)
        later = jnp.minimum(i, 1) * N_HEADS
        qs, kws, vws, bms = [], [], [], []
        for hk in range(SWA_KV_HEADS):
            kw = kn_ref[pl.ds(r0, 2 * BLK), hk * kvw:(hk + 1) * kvw]
            vw = vb_ref[pl.ds(r0, 2 * BLK), hk * kvw:(hk + 1) * kvw]
            kw = jnp.where(first_key, jnp.zeros_like(kw), kw)
            vw = jnp.where(first_key, jnp.zeros_like(vw), vw)
            for g in range(SWA_GROUP):
                h = hk * SWA_GROUP + g
                q2 = qn_ref[pl.ds(r0, BLK), (h // 2) * kvw:(h // 2 + 1) * kvw]
                qs.append(jnp.where(lane_half == h % 2, q2, jnp.zeros_like(q2)))
                kws.append(kw)
                vws.append(vw)
                bms.append(bm_ref[later + h])
        outs, _, _ = _attend(qs, kws, vws, bms)
        pairs = [jnp.where(lane_half == 0, outs[h], outs[h + 1]) for h in range(0, N_HEADS, 2)]
        o_ref[0, pl.ds(r0, BLK), :] = jnp.concatenate(pairs, axis=-1).astype(o_ref.dtype)
        return carry

    lax.fori_loop(0, nb, block, 0, unroll=2 * ATTN_UNROLL)


def _swa_conv_mixers(proj, q_gain, k_gain, sink, bias, conv_w):
    bsz, seq, _ = proj.shape
    kvw = SWA_KV_HEADS * HEAD_DIM
    q_gain_t = jnp.tile(q_gain, N_HEADS).reshape(1, GROUP_WIDTH)
    k_gain_t = jnp.tile(k_gain, SWA_KV_HEADS).reshape(1, kvw)
    wide = lambda off: pl.BlockSpec((1, seq, GROUP_WIDTH), lambda b: (b, 0, off // GROUP_WIDTH))
    out = jax.ShapeDtypeStruct((bsz, seq, GROUP_WIDTH), BF16)
    return pl.pallas_call(
        _swa_kernel,
        grid=(bsz,),
        in_specs=[
            wide(OFF_SWA_Q),
            pl.BlockSpec((1, seq, kvw), lambda b: (b, 0, OFF_SWA_K // kvw)),
            pl.BlockSpec((1, seq, kvw), lambda b: (b, 0, OFF_SWA_V // kvw)),
            pl.BlockSpec((1, GROUP_WIDTH), lambda b: (0, 0)),
            pl.BlockSpec((1, kvw), lambda b: (0, 0)),
            pl.BlockSpec(memory_space=pltpu.SMEM),
            pl.BlockSpec((N_HEADS, BLK, 2 * BLK), lambda b: (0, 0, 0)),
            wide(OFF_CONV), wide(OFF_CONV + GROUP_WIDTH), wide(OFF_CONV + 2 * GROUP_WIDTH),
            pl.BlockSpec((3, GROUP_WIDTH), lambda b: (0, 0)),
        ],
        out_specs=[pl.BlockSpec((1, seq, GROUP_WIDTH), lambda b: (b, 0, 0))] * 2,
        out_shape=[out, out],
        scratch_shapes=[
            pltpu.VMEM((seq, GROUP_WIDTH), BF16),
            pltpu.VMEM((seq + BLK, 2 * kvw), BF16),
            pltpu.VMEM((seq + BLK, 2 * kvw), BF16),
            pltpu.VMEM((2 * N_HEADS, BLK, 2 * BLK), F32),
        ],
        compiler_params=_params("parallel"),
        name="swa_conv_mixers",
    )(proj, proj, proj, q_gain_t, k_gain_t, sink, bias, proj, proj, proj, conv_w)


def _dil_kernel(q_ref, k_ref, v_ref, qg_ref, kg_ref, bias_ref, wu_ref, wd_ref,
                o_ref, wu_out, wd_out,
                qn_ref, kn_ref, vn_ref, ob_ref, mb_ref, db_ref, bm_ref):
    wu_out[...] = wu_ref[0].astype(BF16)
    wd_out[...] = wd_ref[0].astype(BF16)
    seq = q_ref.shape[1]
    lanes = q_ref.shape[2]
    heads = lanes // HEAD_DIM
    scale = HEAD_DIM ** -0.5 * LOG2E

    def prep(i, carry):
        r0 = pl.multiple_of(i * BLK, BLK)
        q = q_ref[0, pl.ds(r0, BLK), :].astype(F32)
        k = k_ref[0, pl.ds(r0, BLK), :].astype(F32)
        qn_ref[pl.ds(r0, BLK), :] = _head_rms(q, qg_ref[...]) * scale
        kn_ref[pl.ds(r0, BLK), :] = _head_rms(k, kg_ref[...])
        vn_ref[pl.ds(r0, BLK), :] = v_ref[0, pl.ds(r0, BLK), :].astype(F32)
        return carry

    lax.fori_loop(0, seq // BLK, prep, 0, unroll=PREP_UNROLL)

    for br, (window, r) in enumerate(DIL_PAIRS):
        band, kcol = _band_mask(window // r)
        for h in range(heads):
            bias = bias_ref[br, h] * LOG2E
            bm_ref[(br * heads + h) * 2] = jnp.where(band & (kcol >= BLK), bias, NEG)
            bm_ref[(br * heads + h) * 2 + 1] = jnp.where(band, bias, NEG)

    lane_head = lax.broadcasted_iota(jnp.int32, (1, lanes), 1) // HEAD_DIM

    for br, (window, r) in enumerate(DIL_PAIRS):
        nb = seq // r // BLK

        def blocks(it, carry, br=br, r=r, nb=nb):
            qs, kws, vws, bms, curs = [], [], [], [], []
            for u in range(DIL_TILES):
                t = it * DIL_TILES + u
                c = t // nb
                i = t - c * nb
                cur = c + i * (BLK * r)
                prev = jnp.maximum(cur - BLK * r, c)
                later = jnp.minimum(i, 1)

                def rows(ref, start):
                    if r == 1:
                        return ref[pl.ds(start, BLK), :]
                    return ref[pl.ds(start, BLK, stride=r), :]

                q = rows(qn_ref, cur).astype(BF16)
                if nb == 1:
                    kw = rows(kn_ref, cur).astype(BF16)
                    vw = rows(vn_ref, cur).astype(BF16)
                else:
                    kw = jnp.concatenate([rows(kn_ref, prev), rows(kn_ref, cur)],
                                         axis=0).astype(BF16)
                    vw = jnp.concatenate([rows(vn_ref, prev), rows(vn_ref, cur)],
                                         axis=0).astype(BF16)
                curs.append(cur)
                for h in range(heads):
                    qs.append(jnp.where(lane_head == h, q, jnp.zeros_like(q)))
                    kws.append(kw)
                    vws.append(vw)
                    if nb == 1:
                        bms.append(bm_ref[(br * heads + h) * 2, :, BLK:])
                    else:
                        bms.append(bm_ref[(br * heads + h) * 2 + later])
            outs, ms, dens = _attend(qs, kws, vws, bms, normalize=False)
            for u in range(DIL_TILES):
                o_all, m_all, d_all = outs[u * heads], ms[u * heads], dens[u * heads]
                for h in range(1, heads):
                    o_all = jnp.where(lane_head == h, outs[u * heads + h], o_all)
                    m_all = jnp.where(lane_head == h, ms[u * heads + h], m_all)
                    d_all = jnp.where(lane_head == h, dens[u * heads + h], d_all)
                if r == 1:
                    rows = pl.ds(curs[u], BLK)
                else:
                    rows = pl.ds(curs[u], BLK, stride=r)
                ob_ref[br, rows, :] = o_all
                mb_ref[br, rows, :] = m_all
                db_ref[br, rows, :] = d_all
            return carry

        lax.fori_loop(0, r * nb // DIL_TILES, blocks, 0, unroll=ATTN_UNROLL)

    def combine(i, carry):
        r0 = pl.multiple_of(i * BLK, BLK)
        rows = pl.ds(r0, BLK)
        m0, m1, m2 = mb_ref[0, rows, :], mb_ref[1, rows, :], mb_ref[2, rows, :]
        m = jnp.maximum(jnp.maximum(m0, m1), m2)
        e0, e1, e2 = jnp.exp2(m0 - m), jnp.exp2(m1 - m), jnp.exp2(m2 - m)
        num = e0 * ob_ref[0, rows, :] + e1 * ob_ref[1, rows, :] + e2 * ob_ref[2, rows, :]
        den = e0 * db_ref[0, rows, :] + e1 * db_ref[1, rows, :] + e2 * db_ref[2, rows, :]
        o_ref[0, rows, :] = (num / den).astype(o_ref.dtype)
        return carry

    lax.fori_loop(0, seq // BLK, combine, 0, unroll=PREP_UNROLL)


def _dil_mixer(proj, q_gain, k_gain, bias, w_up, w_down, layer):
    bsz, seq, _ = proj.shape
    lanes = 128
    heads = lanes // HEAD_DIM
    nblk = GROUP_WIDTH // lanes
    base = OFF_DIL // lanes
    gq = jnp.tile(q_gain, heads).reshape(1, lanes)
    gk = jnp.tile(k_gain, heads).reshape(1, lanes)
    _, d, f = w_up.shape
    steps = bsz * nblk
    assert d % steps == 0 and f % steps == 0

    def col(seg):
        return pl.BlockSpec((1, seq, lanes), lambda b, j: (b, 0, base + seg * nblk + j))

    return pl.pallas_call(
        _dil_kernel,
        grid=(bsz, nblk),
        in_specs=[
            col(0), col(1), col(2),
            pl.BlockSpec((1, lanes), lambda b, j: (0, 0)),
            pl.BlockSpec((1, lanes), lambda b, j: (0, 0)),
            pl.BlockSpec((len(DIL_PAIRS), heads, BLK, 2 * BLK), lambda b, j: (0, j, 0, 0)),
            pl.BlockSpec((1, d // steps, f), lambda b, j: (layer, b * nblk + j, 0)),
            pl.BlockSpec((1, f // steps, d), lambda b, j: (layer, b * nblk + j, 0)),
        ],
        out_specs=[
            pl.BlockSpec((1, seq, lanes), lambda b, j: (b, 0, j)),
            pl.BlockSpec((d // steps, f), lambda b, j: (b * nblk + j, 0)),
            pl.BlockSpec((f // steps, d), lambda b, j: (b * nblk + j, 0)),
        ],
        out_shape=[
            jax.ShapeDtypeStruct((bsz, seq, GROUP_WIDTH), BF16),
            jax.ShapeDtypeStruct((d, f), BF16),
            jax.ShapeDtypeStruct((f, d), BF16),
        ],
        scratch_shapes=[
            pltpu.VMEM((seq, lanes), F32),
            pltpu.VMEM((seq, lanes), F32),
            pltpu.VMEM((seq, lanes), F32),
            pltpu.VMEM((len(DIL_PAIRS), seq, lanes), F32),
            pltpu.VMEM((len(DIL_PAIRS), seq, lanes), F32),
            pltpu.VMEM((len(DIL_PAIRS), seq, lanes), F32),
            pltpu.VMEM((len(DIL_PAIRS) * heads * 2, BLK, 2 * BLK), F32),
        ],
        compiler_params=_params("parallel", "parallel"),
        name="dil_mixer",
    )(proj, proj, proj, gq, gk, bias, w_up, w_down)


def _rwkv_maps_kernel(*refs, has_vres):
    if has_vres:
        (p_ref, pp_ref, mu_ref, w0_ref, w2_ref, a0_ref, a2_ref, g2_ref, kk_ref, ka_ref, rk_ref,
         lng_ref, lnb_ref, vf_ref, v0_ref, v1_ref, v2_ref, o_ref, *scratch) = refs
    else:
        (p_ref, pp_ref, mu_ref, w0_ref, w2_ref, a0_ref, a2_ref, g2_ref, kk_ref, ka_ref, rk_ref,
         lng_ref, lnb_ref, o_ref, v_out, *scratch) = refs
    *scratch, q_s, y1_s, m_s, gm_s, st_ref = scratch

    @pl.when(pl.program_id(1) == 0)
    def _():
        st_ref[...] = jnp.zeros(st_ref.shape, F32)

    w = GROUP_WIDTH
    p = p_ref[0].astype(F32)
    row = lax.broadcasted_iota(jnp.int32, p.shape, 0)
    last_prev = pp_ref[0, PREV_ROWS - 1:PREV_ROWS, :].astype(F32)
    last_prev = jnp.where(pl.program_id(1) > 0, last_prev, 0.0)
    prev = jnp.where(row >= 1, pltpu.roll(p, 1, axis=0), last_prev)
    xs = p + (prev - p) * mu_ref[...]
    grows = WKV_GROUP * WKV_CHUNK
    for grp in range(p.shape[0] // grows):
        rs = slice(grp * grows, (grp + 1) * grows)
        x = xs[rs]
        r = x[:, 0:w]
        k = x[:, w:2 * w]
        v = x[:, 2 * w:3 * w]
        o = 3 * w
        wd = x[:, o:o + DECAY_LORA]
        ad = x[:, o + DECAY_LORA:o + DECAY_LORA + ICLR_LORA]
        gd = x[:, o + DECAY_LORA + ICLR_LORA:]

        z = -(w0_ref[...] + _mm(jnp.tanh(wd), w2_ref[...]))
        softplus = jnp.maximum(z, 0.0) + jnp.log(1.0 + jnp.exp(-jnp.abs(z)))
        logw = -softplus - 0.5
        lw = -jnp.exp(logw)
        a = _sigmoid(a0_ref[...] + _mm(ad, a2_ref[...]))
        gate = _mm(_sigmoid(gd), g2_ref[...])
        if has_vres:
            mix = _sigmoid(v0_ref[...] + _mm(_mm(v, v1_ref[...]), v2_ref[...]))
            v = v + (vf_ref[0, rs] - v) * mix
        else:
            v_out[0, rs] = v
        kk = k * kk_ref[...]
        ss = _head_sums(kk * kk, split=False)
        kk = kk * lax.rsqrt(jnp.maximum(ss, 1e-24))
        k = k * (1.0 + (a - 1.0) * ka_ref[...])
        bonus = _head_sums(r * k * rk_ref[...]) * v
        _chunk_maps(grp, r, lw, k, v, kk, kk * a, q_s, y1_s, m_s, gm_s, *scratch)
        y = _walk_chunks(range(grp * WKV_GROUP, (grp + 1) * WKV_GROUP), q_s, y1_s, m_s, gm_s, st_ref)
        mean = _head_sums(y) * (1.0 / HEAD_DIM)
        yc = y - mean
        var = _head_sums(yc * yc) * (1.0 / HEAD_DIM)
        yn = yc * lax.rsqrt(var + LN_X_EPS) * lng_ref[...] + lnb_ref[...]
        o_ref[0, rs] = ((yn + bonus) * gate).astype(o_ref.dtype)


def _walk_chunks(chunks, q_ref, y1_ref, m_ref, gm_ref, st_ref):
    c = WKV_CHUNK
    w2 = 2 * HEAD_DIM
    pairs = range(N_HEADS // 2)
    state = [st_ref[p] for p in pairs]
    entering = []
    for j in chunks:
        rs = slice(j * w2, (j + 1) * w2)
        s_in = [s.astype(BF16) for s in state]
        entering.append(s_in)
        state = [_mm(m_ref[rs, p * w2:(p + 1) * w2], s_in[p]) + gm_ref[rs, p * w2:(p + 1) * w2]
                 for p in pairs]
    for p in pairs:
        st_ref[p] = state[p]
    ys = [jnp.concatenate([_mm(q_ref[j * c:(j + 1) * c, p * w2:(p + 1) * w2], entering[ci][p])
                           for p in pairs], axis=-1) + y1_ref[j * c:(j + 1) * c, :]
          for ci, j in enumerate(chunks)]
    return jnp.concatenate(ys, axis=0)


def _rwkv_maps(proj, mu, w0, w2, a0, a2, g2, k_k, k_a, r_k, ln_g, ln_b, v_first, vres, ts=512):
    bsz, seq, _ = proj.shape
    w = GROUP_WIDTH
    nt = seq // ts
    has_vres = vres is not None

    def full(shape):
        return pl.BlockSpec(shape, lambda b, i: (0,) * len(shape))

    row = lambda a: a.reshape(1, -1)
    tile = pl.BlockSpec((1, ts, w), lambda b, i: (b, i, 0))
    in_specs = [
        pl.BlockSpec((pl.Element(1), pl.Element(ts), pl.Element(RWKV_IN_WIDTH)),
                     lambda b, i: (b, i * ts, OFF_RW)),
        pl.BlockSpec((pl.Element(1), pl.Element(PREV_ROWS), pl.Element(RWKV_IN_WIDTH)),
                     lambda b, i: (b, jnp.maximum(i * (ts // PREV_ROWS) - 1, 0) * PREV_ROWS, OFF_RW)),
        full((1, RWKV_IN_WIDTH)), full((1, w)), full((DECAY_LORA, w)), full((1, w)),
        full((ICLR_LORA, w)), full((GATE_LORA, w)), full((1, w)), full((1, w)), full((1, w)),
        full((1, w)), full((1, w)),
    ]
    args = [proj, proj, row(mu), row(w0), w2, row(a0), a2, g2, row(k_k), row(k_a), row(r_k),
            row(ln_g), row(ln_b)]
    if has_vres:
        v0, v1, v2 = vres
        in_specs += [tile, full((1, w)), full(v1.shape), full(v2.shape)]
        args += [v_first, row(v0), v1, v2]
    f32 = jax.ShapeDtypeStruct((bsz, seq, w), F32)
    bf16 = jax.ShapeDtypeStruct((bsz, seq, w), BF16)
    out_shape = [bf16] + ([] if has_vres else [f32])
    return pl.pallas_call(
        functools.partial(_rwkv_maps_kernel, has_vres=has_vres),
        grid=(bsz, nt),
        in_specs=in_specs,
        out_specs=[tile] * len(out_shape),
        out_shape=out_shape,
        scratch_shapes=[pltpu.VMEM((ts, w), dt) for dt in (BF16, F32, BF16, BF16, F32, BF16, BF16,
                                                           BF16, F32)]
        + [pltpu.VMEM((2 * ts, w), BF16), pltpu.VMEM((2 * ts, w), F32),
           pltpu.VMEM((N_HEADS // 2, 2 * HEAD_DIM, 2 * HEAD_DIM), F32)],
        compiler_params=_params("parallel", "arbitrary"),
        name="rwkv_chunk_maps",
    )(*args)


def _chunk_maps(grp, r, lw, k, v, kk, b, q_out, y1_out, m_out, g_out,
                at_ref, rt_ref, bt_ref, kt_ref, dec_ref, vb_ref, rb_ref):
    c = WKV_CHUNK
    n = HEAD_DIM
    rows = r.shape[0]
    rs = slice(grp * rows, (grp + 1) * rows)

    row = lax.broadcasted_iota(jnp.int32, (rows, rows), 0)
    col = lax.broadcasted_iota(jnp.int32, (rows, rows), 1)
    tri = ((row >= col) & ((row // c) == (col // c))).astype(BF16)
    lw_hi, lw_lo = _split_bf16(lw)
    cum = (jnp.dot(tri, lw_hi, preferred_element_type=F32)
           + jnp.dot(tri, lw_lo, preferred_element_type=F32))
    e_pos = jnp.exp(cum)
    e_neg = jnp.exp(-cum)
    r_t = r * e_pos
    at_ref[rs] = (-kk * jnp.exp(cum - lw)).astype(BF16)
    rt_ref[rs] = r_t
    rb_ref[rs] = r_t.astype(BF16)
    bt_ref[rs] = (b * e_neg).astype(BF16)
    kt_ref[rs] = (k * e_neg).astype(BF16)
    dec_ref[rs] = e_pos
    vb_ref[rs] = v.astype(BF16)

    _chunk_group_pairs(range(grp * WKV_GROUP, (grp + 1) * WKV_GROUP), at_ref, rt_ref, bt_ref,
                       kt_ref, dec_ref, vb_ref, rb_ref, q_out, y1_out, m_out, g_out)


def _chunk_group_pairs(chunks, at_ref, rt_ref, bt_ref, kt_ref, dec_ref, vb_ref, rb_ref,
                       q_out, y1_out, m_out, g_out):
    c = WKV_CHUNK
    n = HEAD_DIM
    assert c == n
    w2 = 2 * n
    units = [(j, p) for j in chunks for p in range(N_HEADS // 2)]
    idx = range(len(units))

    def tile(ref, i):
        j, p = units[i]
        return ref[j * c:(j + 1) * c, p * w2:(p + 1) * w2]

    left = lax.broadcasted_iota(jnp.int32, (1, w2), 1) < n
    row2 = lax.broadcasted_iota(jnp.int32, (2 * c, 2 * c), 0)
    col2 = lax.broadcasted_iota(jnp.int32, (2 * c, 2 * c), 1)
    keep = (row2 % c > col2 % c) | ((row2 >= c) & (row2 % c == col2 % c))
    wrow = lax.broadcasted_iota(jnp.int32, (c, w2), 0)
    wcol = lax.broadcasted_iota(jnp.int32, (c, w2), 1)
    eye_left = (wcol == wrow).astype(F32)
    eye_right = (wcol == wrow + c).astype(F32)

    def only(mask, x):
        return jnp.where(mask, x, jnp.zeros_like(x))

    def rows_ba(x):
        return jnp.concatenate([only(~left, x), only(left, x)], axis=0)

    def rows_ab(x):
        return jnp.concatenate([only(left, x), only(~left, x)], axis=0)

    a = [tile(at_ref, i) for i in idx]
    b = [tile(bt_ref, i) for i in idx]
    k = [tile(kt_ref, i) for i in idx]
    v = [tile(vb_ref, i) for i in idx]
    ar = [jnp.concatenate([a[i], tile(rb_ref, i)], axis=0) for i in idx]
    bk = [jnp.concatenate([b[i], k[i]], axis=0) for i in idx]
    kb = [jnp.concatenate([k[i], b[i]], axis=0) for i in idx]
    aa_a = [jnp.where(keep, _mm_nt(only(left, ar[i]), bk[i]), 0.0) for i in idx]
    aa_b = [jnp.where(keep, _mm_nt(only(~left, ar[i]), kb[i]), 0.0) for i in idx]
    ps_a = [only(left, aa_a[i][:c]) + eye_right for i in idx]
    ps_b = [only(~left, aa_b[i][:c]) + eye_left for i in idx]
    zeros = jnp.zeros((c, w2), BF16)
    for _ in range(int(math.log2(c))):
        nxt_a, nxt_b = [], []
        for i in idx:
            pa, pb = ps_a[i].astype(BF16), ps_b[i].astype(BF16)
            rhs = jnp.concatenate([jnp.concatenate([pa, zeros], axis=1),
                                   jnp.concatenate([zeros, pb], axis=1)], axis=0)
            prod = jnp.dot(jnp.where(left, pa, pb), rhs, preferred_element_type=F32)
            nxt_a.append(prod[:, :w2] + only(~left, ps_a[i]))
            nxt_b.append(prod[:, w2:] + only(left, ps_b[i]))
        ps_a, ps_b = nxt_a, nxt_b
    t_ba = [jnp.where(left, ps_b[i], ps_a[i]).astype(BF16) for i in idx]
    av = [_mm(jnp.where(left, aa_b[i], aa_a[i]), rows_ba(v[i])) for i in idx]
    av_u = [av[i][:c].astype(BF16) for i in idx]
    wu = [_mm(t_ba[i], jnp.concatenate([rows_ba(a[i]), rows_ba(av_u[i])], axis=1)).astype(BF16)
          for i in idx]
    a_rb = [jnp.where(left, aa_a[i][c:], aa_b[i][c:]) for i in idx]
    aw = [_mm(a_rb[i], jnp.concatenate([rows_ab(wu[i][:, :w2]), rows_ab(wu[i][:, w2:])], axis=1))
          for i in idx]
    q = [tile(rt_ref, i) + aw[i][:, :w2] for i in idx]
    y1 = [aw[i][:, w2:] + av[i][c:] for i in idx]
    x1 = [_mm_tn(b[i], wu[i][:, :w2]) for i in idx]
    x2 = [_mm_tn(bk[i], jnp.concatenate([wu[i][:, w2:], v[i]], axis=0)) for i in idx]
    diag = row2 == col2
    same_head = (row2 // n) == (col2 // n)
    eye2 = diag.astype(F32)
    m, g = [], []
    for i in idx:
        j, p = units[i]
        d = dec_ref[(j + 1) * c - 1:(j + 1) * c, p * w2:(p + 1) * w2]
        d_col = jnp.sum(jnp.where(diag, d, 0.0), axis=1, keepdims=True)
        m.append((eye2 + jnp.where(same_head, x1[i], 0.0)) * d_col)
        g.append(jnp.where(same_head, x2[i], 0.0) * d_col)
    per = N_HEADS // 2
    for ci, j in enumerate(chunks):
        sel = slice(ci * per, (ci + 1) * per)
        q_out[j * c:(j + 1) * c, :] = jnp.concatenate(q[sel], axis=-1).astype(q_out.dtype)
        y1_out[j * c:(j + 1) * c, :] = jnp.concatenate(y1[sel], axis=-1)
        m_out[j * w2:(j + 1) * w2, :] = jnp.concatenate(m[sel], axis=-1).astype(m_out.dtype)
        g_out[j * w2:(j + 1) * w2, :] = jnp.concatenate(g[sel], axis=-1)


def _wout_kernel(x_ref, y0_ref, y1_ref, y2_ref, y3_ref, w_ref, o_ref):
    acc = x_ref[...]
    for idx, y_ref in enumerate((y0_ref, y1_ref, y2_ref, y3_ref)):
        w = w_ref[0, idx * GROUP_WIDTH:(idx + 1) * GROUP_WIDTH, :].astype(BF16)
        acc = acc + jnp.dot(y_ref[...], w, preferred_element_type=F32)
    o_ref[...] = acc


def _wout(x2d, ys, w_stack, layer, tm=512):
    m, d = x2d.shape
    ytile = pl.BlockSpec((tm, GROUP_WIDTH), lambda i: (i, 0))
    xtile = pl.BlockSpec((tm, d), lambda i: (i, 0))
    wspec = pl.BlockSpec((1,) + w_stack.shape[1:], lambda i: (layer, 0, 0))
    return pl.pallas_call(
        _wout_kernel,
        grid=(m // tm,),
        in_specs=[xtile] + [ytile] * 4 + [wspec],
        out_specs=xtile,
        out_shape=jax.ShapeDtypeStruct((m, d), F32),
        compiler_params=_params("parallel", vmem_limit=FFN_VMEM_LIMIT),
        name="wout_residual",
    )(x2d, *ys, w_stack)


def _ffn_kernel(x_ref, g_ref, wu_ref, wd_ref, o_ref, h_ref):
    def down(u):
        act = jnp.square(jnp.maximum(u, 0.0)).astype(BF16)
        return jnp.dot(act, wd_ref[0].astype(BF16), preferred_element_type=F32)

    @pl.when(pl.program_id(1) == 0)
    def _():
        u = None
        for cols, h in _norm_chunks(x_ref, g_ref, h_ref, _rms_scale(x_ref)):
            part = jnp.dot(h, wu_ref[0, cols, :].astype(BF16), preferred_element_type=F32)
            u = part if u is None else u + part
        o_ref[...] = x_ref[...] + down(u)

    @pl.when(pl.program_id(1) > 0)
    def _():
        u = jnp.dot(h_ref[...], wu_ref[0].astype(BF16), preferred_element_type=F32)
        o_ref[...] += down(u)


def _ffn(x2d, gains, layer, wu, wd, tm=1024, tf=1024):
    m, d = x2d.shape
    f = wu.shape[1]
    wu_stack = wu.reshape(1, d, f)
    wd_stack = wd.reshape(1, f, d)
    xtile = pl.BlockSpec((tm, d), lambda i, j: (i, 0))
    return pl.pallas_call(
        _ffn_kernel,
        grid=(m // tm, f // tf),
        in_specs=[
            xtile,
            pl.BlockSpec((1, 1, d), lambda i, j: (layer, 0, 0)),
            pl.BlockSpec((1, d, tf), lambda i, j: (0, 0, j)),
            pl.BlockSpec((1, tf, d), lambda i, j: (0, j, 0)),
        ],
        out_specs=xtile,
        out_shape=jax.ShapeDtypeStruct((m, d), F32),
        scratch_shapes=[pltpu.VMEM((tm, d), BF16)],
        compiler_params=_params("parallel", "arbitrary",
                                vmem_limit=V7X_VMEM_BYTES - 3 * 1024 * 1024),
        name="ffn",
    )(x2d, gains.reshape(-1, 1, d), wu_stack, wd_stack)


def kernel(x, norm_mix, w_in, conv_w, swa_q_norm, swa_k_norm, swa_sink, dil_q_norm, dil_k_norm,
           rwkv_mu, decay_w0, decay_w2, iclr_a0, iclr_a2, gate_g2, k_k, k_a, r_k, ln_x_g, ln_x_b,
           vres_v0, vres_v1, vres_v2, w_out, norm_ffn, w_up, w_down, rel_bias):
    bsz, seq, d = x.shape
    depth = w_in.shape[0]
    bias = _bias_tiles(rel_bias)
    swa_bias = bias[0, :N_HEADS]
    dil_bias = bias[:, N_HEADS:]
    x2d = x.reshape(bsz * seq, d)
    v_first = None
    for layer in range(depth):
        proj = _norm_matmul(x2d, norm_mix, w_in, layer).reshape(bsz, seq, IN_WIDTH)
        y_swa, y_conv = _swa_conv_mixers(proj, swa_q_norm[layer], swa_k_norm[layer],
                                         swa_sink[layer], swa_bias, conv_w[layer])
        y_dil, wu_bf16, wd_bf16 = _dil_mixer(proj, dil_q_norm[layer], dil_k_norm[layer], dil_bias,
                                             w_up, w_down, layer)
        vres = None if layer == 0 else (vres_v0[layer - 1], vres_v1[layer - 1], vres_v2[layer - 1])
        outs = _rwkv_maps(
            proj, rwkv_mu[layer], decay_w0[layer], decay_w2[layer], iclr_a0[layer], iclr_a2[layer],
            gate_g2[layer], k_k[layer], k_a[layer], r_k[layer], ln_x_g[layer], ln_x_b[layer],
            v_first, vres)
        y_rwkv = outs[0]
        if layer == 0:
            v_first = outs[1]
        ys = [y.reshape(bsz * seq, GROUP_WIDTH) for y in (y_conv, y_swa, y_dil, y_rwkv)]
        x2d = _wout(x2d, ys, w_out, layer)
        x2d = _ffn(x2d, norm_ffn, layer, wu_bf16, wd_bf16)
    return x2d.reshape(bsz, seq, d)
```
